```python
import jax
import jax.numpy as jnp
from jax import lax
import numpy as np

D_MODEL = 1024
BATCH = 4
SEQ = 4096
DEPTH = 2

A_HEAD_DIM = 64
A_WIDTH = D_MODEL // 2
A_HEADS = A_WIDTH // A_HEAD_DIM
MOBA_BLOCK = 256
MOBA_TOPK = 3
MOBA_Q_CHUNK = 32
B_WIDTH = D_MODEL // 2
B_HEADS = 4
B_HEAD_DIM = B_WIDTH // B_HEADS
B_QKV_BLOCK = 4
B_CONV = 4
MLSTM_CHUNK = 64
C_WIDTH = D_MODEL
C_GROUPS = 8
C_CHUNK = 128
IN_WIDTH = 4 * A_WIDTH + 2 * B_WIDTH
MIX_WIDTH = A_WIDTH + B_WIDTH
N_EVEN = (DEPTH + 1) // 2
N_ODD = DEPTH // 2
NEG = -1e30

kernel_name = "hybrid_moba_mlstm_gmlp_block"


def _rms_norm(x, g, eps=1e-6):
    xf = x.astype(jnp.float32)
    y = xf * lax.rsqrt(jnp.mean(xf * xf, axis=-1, keepdims=True) + eps)
    return (y * g.astype(jnp.float32)).astype(x.dtype)


def _layer_norm(x, eps=1e-5):
    xf = x.astype(jnp.float32)
    mu = jnp.mean(xf, axis=-1, keepdims=True)
    var = jnp.mean(jnp.square(xf - mu), axis=-1, keepdims=True)
    return (xf - mu) * lax.rsqrt(var + eps)


def _heads(t, n):
    b, s, _ = t.shape
    return t.reshape(b, s, n, -1).transpose(0, 2, 1, 3)


def _merge(t):
    b, h, s, d = t.shape
    return t.transpose(0, 2, 1, 3).reshape(b, s, h * d)


def _moba(q, k, v):
    b, h, s, dh = q.shape
    nb = -(-s // MOBA_BLOCK)
    pad = nb * MOBA_BLOCK - s
    kp = jnp.pad(k, ((0, 0), (0, 0), (0, pad), (0, 0)))
    vp = jnp.pad(v, ((0, 0), (0, 0), (0, pad), (0, 0)))
    kb = kp.reshape(b, h, nb, MOBA_BLOCK, dh)
    vb = vp.reshape(b, h, nb, MOBA_BLOCK, dh)
    kmean = jnp.mean(kb.astype(jnp.float32), axis=3)
    qblk = jnp.arange(s) // MOBA_BLOCK
    score = jnp.einsum('bhsd,bhnd->bhsn', q.astype(jnp.float32), kmean)
    past = jnp.arange(nb)[None, :] < qblk[:, None]
    score = jnp.where(past, score, NEG)
    topk = min(MOBA_TOPK, nb)
    _, idx = lax.top_k(score, topk)
    valid = idx < qblk[:, None]
    nq = s // MOBA_Q_CHUNK

    def chunks(t):
        return jnp.moveaxis(t.reshape(b, h, nq, MOBA_Q_CHUNK, *t.shape[3:]), 2, 0)

    gather = jax.vmap(jax.vmap(lambda blocks, ix: blocks[ix]))
    scale = dh ** -0.5

    def step(args):
        qc, ic, vmask, cid = args
        q0 = cid * MOBA_Q_CHUNK
        j = q0 // MOBA_BLOCK
        qpos = q0 + jnp.arange(MOBA_Q_CHUNK)
        k_sel = gather(kb, ic)
        v_sel = gather(vb, ic)
        k_own = lax.dynamic_slice_in_dim(kp, j * MOBA_BLOCK, MOBA_BLOCK, axis=2)
        v_own = lax.dynamic_slice_in_dim(vp, j * MOBA_BLOCK, MOBA_BLOCK, axis=2)
        l_sel = jnp.einsum('bhqd,bhqtkd->bhqtk', qc, k_sel).astype(jnp.float32) * scale
        l_sel = jnp.where(vmask[..., None], l_sel, NEG).reshape(b, h, MOBA_Q_CHUNK, topk * MOBA_BLOCK)
        kpos = j * MOBA_BLOCK + jnp.arange(MOBA_BLOCK)
        l_own = jnp.einsum('bhqd,bhkd->bhqk', qc, k_own).astype(jnp.float32) * scale
        l_own = jnp.where(kpos[None, :] <= qpos[:, None], l_own, NEG)
        p = jax.nn.softmax(jnp.concatenate([l_sel, l_own], axis=-1), axis=-1).astype(v.dtype)
        p_sel = p[..., :topk * MOBA_BLOCK].reshape(b, h, MOBA_Q_CHUNK, topk, MOBA_BLOCK)
        p_own = p[..., topk * MOBA_BLOCK:]
        return (jnp.einsum('bhqtk,bhqtkd->bhqd', p_sel, v_sel)
                + jnp.einsum('bhqk,bhkd->bhqd', p_own, v_own))

    out = lax.map(step, (chunks(q), chunks(idx), chunks(valid), jnp.arange(nq)))
    return jnp.moveaxis(out, 0, 2).reshape(b, h, s, dh)


def _mlstm(q, k, v, ig, fg):
    dtype = q.dtype
    b, h, s, dh = q.shape
    L = MLSTM_CHUNK
    nc = s // L
    q = q.astype(jnp.float32)
    k = k.astype(jnp.float32) * (dh ** -0.5)
    v = v.astype(jnp.float32)
    ig = ig.astype(jnp.float32)
    lf = jax.nn.log_sigmoid(fg.astype(jnp.float32))

    def chunks(t):
        return jnp.moveaxis(t.reshape(b, h, nc, L, *t.shape[3:]), 2, 0)

    tri = jnp.tril(jnp.ones((L, L), dtype=bool))

    def step(carry, xs):
        C, n, m = carry
        qc, kc, vc, ic, lc = xs
        cum = jnp.cumsum(lc, axis=-1)
        tot = cum[..., -1]
        dmat = jnp.where(tri, cum[..., :, None] - cum[..., None, :] + ic[..., None, :], NEG)
        g = cum + m[..., None]
        m_t = jnp.maximum(g, jnp.max(dmat, axis=-1))
        w_intra = jnp.einsum('bhtd,bhsd->bhts', qc, kc) * jnp.exp(dmat - m_t[..., None])
        w_inter = jnp.exp(g - m_t)
        num = (w_inter[..., None] * jnp.einsum('bhvk,bhtk->bhtv', C, qc)
               + jnp.einsum('bhts,bhsv->bhtv', w_intra, vc))
        den = w_inter * jnp.einsum('bhk,bhtk->bht', n, qc) + jnp.sum(w_intra, axis=-1)
        hc = num / jnp.maximum(jnp.abs(den), jnp.exp(-m_t))[..., None]
        w_state = tot[..., None] - cum + ic
        m_new = jnp.maximum(tot + m, jnp.max(w_state, axis=-1))
        decay = jnp.exp(tot + m - m_new)
        ws = jnp.exp(w_state - m_new[..., None])
        C = decay[..., None, None] * C + jnp.einsum('bhs,bhsv,bhsk->bhvk', ws, vc, kc)
        n = decay[..., None] * n + jnp.einsum('bhs,bhsk->bhk', ws, kc)
        return (C, n, m_new), hc

    init = (jnp.zeros((b, h, dh, dh), jnp.float32), jnp.zeros((b, h, dh), jnp.float32),
            jnp.zeros((b, h), jnp.float32))
    _, hs = lax.scan(step, init, (chunks(q), chunks(k), chunks(v), chunks(ig), chunks(lf)))
    return jnp.moveaxis(hs, 0, 2).reshape(b, h, s, dh).astype(dtype)


def _causal_conv(x, w, bias):
    ch = x.shape[-1]
    y = lax.conv_general_dilated(x, w[:, None, :].astype(x.dtype), window_strides=(1,),
                                 padding=[(B_CONV - 1, 0)], dimension_numbers=('NWC', 'WIO', 'NWC'),
                                 feature_group_count=ch)
    return y + bias


def _headwise(x, w):
    b, s, d = x.shape
    nblk, blk, _ = w.shape
    return jnp.einsum('bsgi,gio->bsgo', x.reshape(b, s, nblk, blk), w).reshape(b, s, d)


def _even_mixer(h, w_in, w_out, q_g, k_g, conv_w, conv_b, wq, wk, wv, w_gates, b_gates, out_g, skip):
    proj = h @ w_in
    aq, ak, av, az, bx, bz = jnp.split(
        proj, [A_WIDTH, 2 * A_WIDTH, 3 * A_WIDTH, 4 * A_WIDTH, 4 * A_WIDTH + B_WIDTH], axis=-1)
    q = _rms_norm(_heads(aq, A_HEADS), q_g)
    k = _rms_norm(_heads(ak, A_HEADS), k_g)
    ya = _merge(_moba(q, k, _heads(av, A_HEADS))) * jax.nn.silu(az)
    xc = jax.nn.silu(_causal_conv(bx, conv_w, conv_b))
    bq = _headwise(xc, wq)
    bk = _headwise(xc, wk)
    bv = _headwise(bx, wv)
    gates = jnp.concatenate([bq, bk, bv], axis=-1) @ w_gates + b_gates
    ig = gates[..., :B_HEADS].transpose(0, 2, 1)
    fg = gates[..., B_HEADS:].transpose(0, 2, 1)
    hb = _mlstm(_heads(bq, B_HEADS), _heads(bk, B_HEADS), _heads(bv, B_HEADS), ig, fg)
    hb = _merge(_layer_norm(hb).astype(h.dtype)) * out_g
    yb = (hb + skip * xc) * jax.nn.silu(bz)
    return jnp.concatenate([ya, yb], axis=-1) @ w_out


def _odd_mixer(h, w_in, w_out, ln_g, ln_b, ws, bs):
    b, s, _ = h.shape
    u, v, z = jnp.split(h @ w_in, [C_WIDTH, 2 * C_WIDTH], axis=-1)
    u = jax.nn.gelu(u)
    v = (_layer_norm(jax.nn.gelu(v)) * ln_g + ln_b).astype(h.dtype)
    nch = s // C_CHUNK
    vg = v.reshape(b, nch, C_CHUNK, C_GROUPS, C_WIDTH // C_GROUPS)
    wm = jnp.where(jnp.tril(jnp.ones((C_CHUNK, C_CHUNK), dtype=bool)), ws, 0.0)
    sg = jnp.einsum('gts,bnsgc->bntgc', wm, vg) + bs.T[:, :, None]
    y = u * sg.reshape(b, s, C_WIDTH) * jax.nn.silu(z)
    return y @ w_out


def setup_inputs(seed: int = 0) -> dict:
    key = jax.random.key(seed)
    ks = jax.random.split(key, 24)

    def nrm(k, shape, sc):
        return jax.random.normal(k, shape, jnp.float32) * sc

    nblk = B_WIDTH // B_QKV_BLOCK
    i_bias = nrm(ks[15], (N_EVEN, B_HEADS), 0.1)
    f_bias = jnp.linspace(3.0, 6.0, B_HEADS, dtype=jnp.float32)[None, :] + nrm(ks[16], (N_EVEN, B_HEADS), 0.1)
    return {
        "x": nrm(ks[0], (BATCH, SEQ, D_MODEL), 1.0),
        "c": nrm(ks[1], (BATCH, D_MODEL), 1.0),
        "ln_g": 1.0 + nrm(ks[2], (DEPTH, D_MODEL), 0.1),
        "ada_w": nrm(ks[3], (DEPTH, D_MODEL, 3 * D_MODEL), 0.5 * D_MODEL ** -0.5),
        "ada_b": nrm(ks[4], (DEPTH, 3 * D_MODEL), 0.02),
        "w_in": nrm(ks[5], (DEPTH, D_MODEL, IN_WIDTH), D_MODEL ** -0.5),
        "w_out": nrm(ks[6], (DEPTH, MIX_WIDTH, D_MODEL), MIX_WIDTH ** -0.5),
        "a_q_g": 1.0 + nrm(ks[7], (N_EVEN, A_HEAD_DIM), 0.1),
        "a_k_g": 1.0 + nrm(ks[8], (N_EVEN, A_HEAD_DIM), 0.1),
        "b_conv_w": nrm(ks[9], (N_EVEN, B_CONV, B_WIDTH), B_CONV ** -0.5),
        "b_conv_b": nrm(ks[10], (N_EVEN, B_WIDTH), 0.02),
        "b_wq": nrm(ks[11], (N_EVEN, nblk, B_QKV_BLOCK, B_QKV_BLOCK), B_QKV_BLOCK ** -0.5),
        "b_wk": nrm(ks[12], (N_EVEN, nblk, B_QKV_BLOCK, B_QKV_BLOCK), B_QKV_BLOCK ** -0.5),
        "b_wv": nrm(ks[13], (N_EVEN, nblk, B_QKV_BLOCK, B_QKV_BLOCK), B_QKV_BLOCK ** -0.5),
        "b_w_gates": nrm(ks[14], (N_EVEN, 3 * B_WIDTH, 2 * B_HEADS), (3 * B_WIDTH) ** -0.5),
        "b_b_gates": jnp.concatenate([i_bias, f_bias], axis=-1),
        "b_out_g": 1.0 + nrm(ks[17], (N_EVEN, B_WIDTH), 0.1),
        "b_skip": 1.0 + nrm(ks[18], (N_EVEN, B_WIDTH), 0.1),
        "c_ln_g": 1.0 + nrm(ks[19], (N_ODD, C_WIDTH), 0.1),
        "c_ln_b": nrm(ks[20], (N_ODD, C_WIDTH), 0.02),
        "c_ws": nrm(ks[21], (N_ODD, C_GROUPS, C_CHUNK, C_CHUNK), C_CHUNK ** -0.5),
        "c_bs": 1.0 + nrm(ks[22], (N_ODD, C_GROUPS, C_CHUNK), 0.1),
    }


def reference(x, c, ln_g, ada_w, ada_b, w_in, w_out, a_q_g, a_k_g, b_conv_w, b_conv_b, b_wq, b_wk,
              b_wv, b_w_gates, b_b_gates, b_out_g, b_skip, c_ln_g, c_ln_b, c_ws, c_bs):
    cs = jax.nn.silu(c)
    for layer in range(DEPTH):
        mod = cs @ ada_w[layer] + ada_b[layer]
        shift, scale, gate = jnp.split(mod, 3, axis=-1)
        h = _rms_norm(x, ln_g[layer]) * (1.0 + scale[:, None, :]) + shift[:, None, :]
        if layer % 2 == 0:
            e = layer // 2
            y = _even_mixer(h, w_in[layer], w_out[layer], a_q_g[e], a_k_g[e], b_conv_w[e], b_conv_b[e],
                            b_wq[e], b_wk[e], b_wv[e], b_w_gates[e], b_b_gates[e], b_out_g[e], b_skip[e])
        else:
            o = layer // 2
            y = _odd_mixer(h, w_in[layer], w_out[layer], c_ln_g[o], c_ln_b[o], c_ws[o], c_bs[o])
        x = x + gate[:, None, :] * y
    return x
```

```python
import functools

import jax
import jax.numpy as jnp
from jax import lax
from jax.experimental import pallas as pl
from jax.experimental.pallas import tpu as pltpu

F32 = jnp.float32
BF16 = jnp.bfloat16

D_MODEL = 1024
A_HEADS = 8
A_HEAD_DIM = 64
A_WIDTH = A_HEADS * A_HEAD_DIM
MOBA_BLOCK = 256
MOBA_TOPK = 3
B_HEADS = 4
B_HEAD_DIM = 128
B_WIDTH = B_HEADS * B_HEAD_DIM
B_CONV = 4
C_GROUPS = 8
C_CHUNK = 128
C_WIDTH = D_MODEL
NEG = -1e30

MLSTM_L = 256
ROW_TILE = 256
CONV_HALO = 8
VMEM_LIMIT = 48 * 1024 * 1024


def _silu(x):
    return x * jax.nn.sigmoid(x)


def _gelu_tanh(x):
    return 0.5 * x * (1.0 + jnp.tanh(0.7978845608028654 * (x + 0.044715 * (x * x * x))))


def _log_sigmoid(x):
    return jnp.minimum(x, 0.0) - jnp.log(1.0 + jnp.exp(-jnp.abs(x)))


def _split_bf16(x):
    hi = x.astype(BF16)
    lo = (x - hi.astype(F32)).astype(BF16)
    return hi, lo


def _dot(a, b):
    return jnp.dot(a, b, preferred_element_type=F32)


def _dot_nt(a, b):
    return lax.dot_general(a, b, (((1,), (1,)), ((), ())), preferred_element_type=F32)


def _adaln_rmsnorm(x, ln_g, scale, shift):
    y = x * lax.rsqrt(jnp.mean(x * x, axis=-1, keepdims=True) + 1e-6)
    return (y * ln_g) * (1.0 + scale) + shift


def _ada_kernel(c_ref, w_ref, b_ref, o_ref):
    c = c_ref[...]
    cs = _silu(c)
    o_ref[0] = jnp.dot(cs, w_ref[0], preferred_element_type=F32,
                       precision=lax.Precision.HIGHEST) + b_ref[0]


def _ada_mods(c, ada_w, ada_b):
    depth, d, d3 = ada_w.shape
    b = c.shape[0]
    bp = 8
    cp = jnp.zeros((bp, d), F32).at[:b].set(c)
    nt = d3 // d
    out = pl.pallas_call(
        _ada_kernel,
        grid=(depth, nt),
        in_specs=[
            pl.BlockSpec((bp, d), lambda l, n: (0, 0)),
            pl.BlockSpec((1, d, d), lambda l, n: (l, 0, n)),
            pl.BlockSpec((1, 1, d), lambda l, n: (l, 0, n)),
        ],
        out_specs=pl.BlockSpec((1, bp, d), lambda l, n: (l, 0, n)),
        out_shape=jax.ShapeDtypeStruct((depth, bp, d3), F32),
        name="ada_mods",
    )(cp, ada_w, ada_b.reshape(depth, 1, d3))
    return out[:, :b].reshape(depth, b, 3, d)


def _inproj0_kernel(x_ref, mod_ref, lng_ref, wt_ref, w_ref, qg_ref, kg_ref, pmat_ref,
                    qT_ref, vT_ref, gT_ref, k_ref, bx_ref, sbz_ref):
    x = x_ref[0]
    mod = mod_ref[0]
    h = _adaln_rmsnorm(x, lng_ref[...], mod[1:2], mod[0:1]).astype(BF16)
    pt = _dot_nt(wt_ref[...], h)
    qg = qg_ref[...]
    scale = A_HEAD_DIM ** -0.5
    for hd in range(A_HEADS):
        lo = hd * A_HEAD_DIM
        q = pt[lo:lo + A_HEAD_DIM]
        r = lax.rsqrt(jnp.mean(q * q, axis=0, keepdims=True) + 1e-6)
        qT_ref[0, 0, lo:lo + A_HEAD_DIM, :] = ((q * r) * (qg * scale)).astype(BF16)
    vT_ref[0, 0] = pt[A_WIDTH:2 * A_WIDTH].astype(BF16)
    gT_ref[0, 0] = _silu(pt[2 * A_WIDTH:]).astype(BF16)
    p = _dot(h, w_ref[...])
    k = p[:, :A_WIDTH]
    ksq_hi, ksq_lo = _split_bf16(k * k)
    pm = pmat_ref[...]
    ms = _dot(ksq_hi, pm) + _dot(ksq_lo, pm)
    k_ref[0] = ((k * lax.rsqrt(ms + 1e-6)) * kg_ref[...]).astype(BF16)
    bx_ref[0] = p[:, A_WIDTH:A_WIDTH + B_WIDTH].astype(BF16)
    sbz_ref[0] = _silu(p[:, A_WIDTH + B_WIDTH:]).astype(BF16)


def _inproj0(x, mod, ln_g, w_in, q_g, k_g):
    b, s, d = x.shape
    tm = ROW_TILE
    nt = s // tm
    aw, bw = A_WIDTH, B_WIDTH
    wq, wk, wv, wz = (w_in[:, i * aw:(i + 1) * aw] for i in range(4))
    wbx = w_in[:, 4 * aw:4 * aw + bw]
    wbz = w_in[:, 4 * aw + bw:]
    wt = jnp.concatenate([wq, wv, wz], axis=1).T.astype(BF16)
    wn = jnp.concatenate([wk, wbx, wbz], axis=1).astype(BF16)
    head_of = jnp.arange(aw) // A_HEAD_DIM
    pmat = (head_of[:, None] == head_of[None, :]).astype(BF16) * (1.0 / A_HEAD_DIM)
    pmat = pmat.astype(BF16)
    t_shape = jax.ShapeDtypeStruct((b, nt, aw, tm), BF16)
    n_shape = jax.ShapeDtypeStruct((b, s, aw), BF16)
    t_spec = pl.BlockSpec((1, 1, aw, tm), lambda i, j: (i, j, 0, 0))
    n_spec = pl.BlockSpec((1, tm, aw), lambda i, j: (i, j, 0))
    const = lambda shape: pl.BlockSpec(shape, lambda i, j: (0,) * len(shape))
    return pl.pallas_call(
        _inproj0_kernel,
        grid=(b, nt),
        in_specs=[
            pl.BlockSpec((1, tm, d), lambda i, j: (i, j, 0)),
            pl.BlockSpec((1, 3, d), lambda i, j: (i, 0, 0)),
            const((1, d)),
            const((3 * aw, d)),
            const((d, aw + 2 * bw)),
            const((A_HEAD_DIM, 1)),
            const((1, aw)),
            const((aw, aw)),
        ],
        out_specs=[t_spec, t_spec, t_spec, n_spec, n_spec, n_spec],
        out_shape=[t_shape, t_shape, t_shape, n_shape, n_shape, n_shape],
        compiler_params=pltpu.CompilerParams(
            dimension_semantics=("parallel", "parallel"), vmem_limit_bytes=VMEM_LIMIT),
        name="inproj0",
    )(x, mod, ln_g.reshape(1, d), wt, wn, q_g.reshape(A_HEAD_DIM, 1),
      jnp.tile(k_g, A_HEADS).reshape(1, aw), pmat)


def _moba_kernel(qT_ref, k_ref, vT_ref, gT_ref, o_ref, kmean_ref, bias_ref, *, nb):
    blk = MOBA_BLOCK
    dh = A_HEAD_DIM
    j = pl.program_id(2)

    @pl.when(j == 0)
    def _():
        for n in range(nb):
            kb = k_ref[0, n * blk:(n + 1) * blk, :].astype(F32)
            kmean_ref[n:n + 1, :] = jnp.mean(kb, axis=0, keepdims=True)

    qT = qT_ref[0, 0]
    d_iota = lax.broadcasted_iota(jnp.int32, qT.shape, 0)
    zero = jnp.zeros_like(qT)
    qTs = (jnp.where(d_iota < dh, qT, zero), jnp.where(d_iota >= dh, qT, zero))

    km_hi, km_lo = _split_bf16(kmean_ref[...])
    n_iota = lax.broadcasted_iota(jnp.int32, (nb, blk), 0)
    past = n_iota < j
    for hd in range(2):
        sc = _dot(km_hi, qTs[hd]) + _dot(km_lo, qTs[hd])
        sc = jnp.where(past, sc, NEG)
        rank = jnp.zeros((nb, blk), jnp.int32)
        for n2 in range(nb):
            r = sc[n2:n2 + 1, :]
            beats = jnp.where(r > sc, 1, jnp.where(r == sc, (n_iota > n2).astype(jnp.int32), 0))
            rank = rank + beats
        sel = jnp.where(past, rank, MOBA_TOPK) < MOBA_TOPK
        bias_ref[hd] = jnp.where(sel, 0.0, NEG)

    k_iota = lax.broadcasted_iota(jnp.int32, (blk, blk), 0)
    q_iota = lax.broadcasted_iota(jnp.int32, (blk, blk), 1)
    causal = k_iota <= q_iota

    def tile(n, carry, own):
        row0 = pl.multiple_of(n * blk, blk)
        kt = k_ref[0, pl.ds(row0, blk), :]
        vt = vT_ref[0, n]
        new = []
        for hd in range(2):
            s = _dot(kt, qTs[hd])
            if own:
                s = jnp.where(causal, s, NEG)
            else:
                s = s + bias_ref[hd, pl.ds(n, 1), :]
            smax = jnp.max(s, axis=0, keepdims=True)
            if own:
                m_new = smax
                p = jnp.exp(s - m_new)
                l_new = jnp.sum(p, axis=0, keepdims=True)
                acc_new = _dot(vt[hd * dh:(hd + 1) * dh], p.astype(BF16))
            else:
                m, l, acc = carry[hd]
                m_new = jnp.maximum(m, smax)
                alpha = jnp.exp(m - m_new)
                p = jnp.exp(s - m_new)
                l_new = alpha * l + jnp.sum(p, axis=0, keepdims=True)
                acc_new = alpha * acc + _dot(vt[hd * dh:(hd + 1) * dh], p.astype(BF16))
            new.append((m_new, l_new, acc_new))
        return tuple(new)

    carry = tile(j, None, True)
    carry = lax.fori_loop(0, j, lambda n, c: tile(n, c, False), carry)
    oT = jnp.concatenate([carry[0][2] / carry[0][1], carry[1][2] / carry[1][1]], axis=0)
    oT = oT * gT_ref[0, 0].astype(F32)
    o_ref[0] = oT.T.astype(BF16)


def _moba(qT, k, vT, gT):
    b, nb, aw, blk = qT.shape
    s = nb * blk
    pairs = aw // (2 * A_HEAD_DIM)
    pw = 2 * A_HEAD_DIM
    return pl.pallas_call(
        functools.partial(_moba_kernel, nb=nb),
        grid=(b, pairs, nb),
        in_specs=[
            pl.BlockSpec((1, 1, pw, blk), lambda i, p, j: (i, j, p, 0)),
            pl.BlockSpec((1, s, pw), lambda i, p, j: (i, 0, p)),
            pl.BlockSpec((1, nb, pw, blk), lambda i, p, j: (i, 0, p, 0)),
            pl.BlockSpec((1, 1, pw, blk), lambda i, p, j: (i, j, p, 0)),
        ],
        out_specs=pl.BlockSpec((1, blk, pw), lambda i, p, j: (i, j, p)),
        out_shape=jax.ShapeDtypeStruct((b, s, aw), BF16),
        scratch_shapes=[pltpu.VMEM((nb, pw), F32), pltpu.VMEM((2, nb, blk), F32)],
        compiler_params=pltpu.CompilerParams(
            dimension_semantics=("parallel", "parallel", "arbitrary"),
            vmem_limit_bytes=VMEM_LIMIT),
        name="moba",
    )(qT, k, vT, gT)


def _mlstm_kernel(bx_ref, sbz_ref, cw_ref, cb_ref, wq_ref, wk_ref, wkt_ref, wv_ref, wg_ref, bg_ref,
                  og_ref, skip_ref, o_ref, xbuf_ref, state_ref, m_ref):
    L = MLSTM_L
    dh = B_HEAD_DIM
    c = pl.program_id(1)

    @pl.when(c == 0)
    def _():
        xbuf_ref[0:CONV_HALO, :] = jnp.zeros((CONV_HALO, B_WIDTH), F32)
        state_ref[...] = jnp.zeros_like(state_ref)
        m_ref[...] = jnp.zeros_like(m_ref)

    bx_b = bx_ref[0]
    xbuf_ref[CONV_HALO:CONV_HALO + L, :] = bx_b.astype(F32)
    cw = cw_ref[...]
    conv = cb_ref[...]
    for i in range(B_CONV):
        off = CONV_HALO - (B_CONV - 1) + i
        conv = conv + cw[i:i + 1, :] * xbuf_ref[off:off + L, :]
    tail = xbuf_ref[L:L + CONV_HALO, :]
    xbuf_ref[0:CONV_HALO, :] = tail
    xc = _silu(conv)
    xc_b = xc.astype(BF16)

    bq = _dot(xc_b, wq_ref[...])
    bk = _dot(xc_b, wk_ref[...])
    bkT = _dot_nt(wkt_ref[...], xc_b)
    bv = _dot(bx_b, wv_ref[...])
    bq_b = bq.astype(BF16)
    wg = wg_ref[...]
    gates = (_dot(bq_b, wg[:B_WIDTH]) + _dot(bk.astype(BF16), wg[B_WIDTH:2 * B_WIDTH])
             + _dot(bv.astype(BF16), wg[2 * B_WIDTH:]) + bg_ref[...])
    ig = gates[:, :128]
    lf = _log_sigmoid(gates[:, 128:])
    t_iota = lax.broadcasted_iota(jnp.int32, (L, L), 0)
    s_iota = lax.broadcasted_iota(jnp.int32, (L, L), 1)
    tril = s_iota <= t_iota
    tri_b = jnp.where(tril, 1.0, 0.0).astype(BF16)
    lf_hi, lf_lo = _split_bf16(lf)
    cum = _dot(tri_b, lf_hi) + _dot(tri_b, lf_lo)
    a = ig - cum
    aT = a.T
    ones = jnp.ones((L, dh), F32)
    kscale = dh ** -0.5

    outs = []
    for hd in range(B_HEADS):
        lo = hd * dh
        q_h = bq_b[:, lo:lo + dh]
        kT_h = (bkT[lo:lo + dh, :] * kscale).astype(BF16)
        v_aug = jnp.concatenate([bv[:, lo:lo + dh], ones], axis=1)
        a_row = aT[hd:hd + 1, :]
        a_col = a[:, hd:hd + 1]
        cum_col = cum[:, hd:hd + 1]
        m_prev = m_ref[hd][0:1, 0:1]
        amax = jnp.max(jnp.where(tril, a_row, NEG), axis=1, keepdims=True)
        b_col = jnp.maximum(m_prev, amax)
        dmat = jnp.exp(jnp.where(tril, a_row - b_col, NEG))
        w_intra = (_dot(q_h, kT_h) * dmat).astype(BF16)
        intra = _dot(w_intra, v_aug.astype(BF16))
        state = state_ref[hd]
        inter = _dot(q_h, state.astype(BF16))
        w_inter = jnp.exp(m_prev - b_col)
        num = w_inter * inter[:, :dh] + intra[:, :dh]
        den = w_inter * inter[:, dh:dh + 1] + intra[:, dh:dh + 1]
        hc = num / jnp.maximum(jnp.abs(den), jnp.exp(-(cum_col + b_col)))
        mu = jnp.mean(hc, axis=-1, keepdims=True)
        hcc = hc - mu
        var = jnp.mean(hcc * hcc, axis=-1, keepdims=True)
        outs.append(hcc * lax.rsqrt(var + 1e-5))
        b_end = b_col[L - 1:L, :]
        ws = jnp.exp(a_col - b_end)
        decay = jnp.exp(m_prev - b_end)
        state_ref[hd] = decay * state + _dot(kT_h, (ws * v_aug).astype(BF16))
        m_ref[hd] = jnp.broadcast_to(cum_col[L - 1:L, :] + b_end, m_ref.shape[1:])

    hb = jnp.concatenate(outs, axis=1) * og_ref[...]
    yb = (hb + skip_ref[...] * xc) * sbz_ref[0].astype(F32)
    o_ref[0] = yb.astype(BF16)


def _block_diag_dense(w):
    nblk, blk, _ = w.shape
    eye = jnp.eye(nblk, dtype=w.dtype)
    return (eye[:, None, :, None] * w[:, :, None, :]).reshape(nblk * blk, nblk * blk)


def _mlstm(bx, sbz, conv_w, conv_b, wq, wk, wv, w_gates, b_gates, out_g, skip):
    b, s, bw = bx.shape
    L = MLSTM_L
    nc = s // L
    wq_d = _block_diag_dense(wq).astype(BF16)
    wk_d = _block_diag_dense(wk)
    wkt_d = wk_d.T.astype(BF16)
    wk_d = wk_d.astype(BF16)
    wv_d = _block_diag_dense(wv).astype(BF16)
    wg = jnp.zeros((3 * bw, 256), F32)
    wg = wg.at[:, :B_HEADS].set(w_gates[:, :B_HEADS]).at[:, 128:128 + B_HEADS].set(w_gates[:, B_HEADS:])
    bg = jnp.zeros((1, 256), F32)
    bg = bg.at[0, :B_HEADS].set(b_gates[:B_HEADS]).at[0, 128:128 + B_HEADS].set(b_gates[B_HEADS:])
    const = lambda shape: pl.BlockSpec(shape, lambda i, j: (0,) * len(shape))
    tok = pl.BlockSpec((1, L, bw), lambda i, j: (i, j, 0))
    return pl.pallas_call(
        _mlstm_kernel,
        grid=(b, nc),
        in_specs=[tok, tok, const((B_CONV, bw)), const((1, bw)), const((bw, bw)), const((bw, bw)),
                  const((bw, bw)), const((bw, bw)), const((3 * bw, 256)), const((1, 256)),
                  const((1, bw)), const((1, bw))],
        out_specs=tok,
        out_shape=jax.ShapeDtypeStruct((b, s, bw), BF16),
        scratch_shapes=[pltpu.VMEM((L + CONV_HALO, bw), F32),
                        pltpu.VMEM((B_HEADS, B_HEAD_DIM, 2 * B_HEAD_DIM), F32),
                        pltpu.VMEM((B_HEADS, 8, 128), F32)],
        compiler_params=pltpu.CompilerParams(
            dimension_semantics=("parallel", "arbitrary"), vmem_limit_bytes=VMEM_LIMIT),
        name="mlstm",
    )(bx, sbz, conv_w, conv_b.reshape(1, bw), wq_d, wk_d, wkt_d, wv_d, wg.astype(BF16), bg,
      out_g.reshape(1, bw), skip.reshape(1, bw))


def _tail_kernel(x_ref, ya_ref, yb_ref, mod0_ref, mod1_ref, lng_ref, wo0_ref, wi1_ref, clg_ref,
                 clb_ref, ws_ref, bst_ref, wo1_ref, o_ref):
    tm = x_ref.shape[1]
    aw = A_WIDTH
    wo0 = wo0_ref
    y0 = _dot(ya_ref[0], wo0[0:aw, :]) + _dot(yb_ref[0], wo0[aw:, :])
    x1 = x_ref[0] + mod0_ref[0][2:3] * y0
    mod1 = mod1_ref[0]
    h = _adaln_rmsnorm(x1, lng_ref[...], mod1[1:2], mod1[0:1]).astype(BF16)
    p = _dot(h, wi1_ref[...])
    u = _gelu_tanh(p[:, :C_WIDTH])
    v = _gelu_tanh(p[:, C_WIDTH:2 * C_WIDTH])
    mu = jnp.mean(v, axis=-1, keepdims=True)
    vc = v - mu
    var = jnp.mean(vc * vc, axis=-1, keepdims=True)
    vn = ((vc * lax.rsqrt(var + 1e-5)) * clg_ref[...] + clb_ref[...]).astype(BF16)
    gate = u * _silu(p[:, 2 * C_WIDTH:])
    t_iota = lax.broadcasted_iota(jnp.int32, (C_CHUNK, C_CHUNK), 0)
    s_iota = lax.broadcasted_iota(jnp.int32, (C_CHUNK, C_CHUNK), 1)
    tril = s_iota <= t_iota
    gw = C_WIDTH // C_GROUPS
    bst = bst_ref[...]
    cols = []
    for g in range(C_GROUPS):
        wm = jnp.where(tril, ws_ref[g], 0.0).astype(BF16)
        rows = []
        for ch in range(tm // C_CHUNK):
            vg = vn[ch * C_CHUNK:(ch + 1) * C_CHUNK, g * gw:(g + 1) * gw]
            rows.append(_dot(wm, vg) + bst[:, g:g + 1])
        cols.append(jnp.concatenate(rows, axis=0))
    sg = jnp.concatenate(cols, axis=1)
    y1 = (gate * sg).astype(BF16)
    o_ref[0] = x1 + mod1[2:3] * _dot(y1, wo1_ref[...])


def _tail(x, ya, yb, mod0, mod1, ln_g1, w_out0, w_in1, w_out1, c_ln_g, c_ln_b, c_ws, c_bs):
    b, s, d = x.shape
    tm = ROW_TILE
    nt = s // tm
    const = lambda shape: pl.BlockSpec(shape, lambda i, j: (0,) * len(shape))
    half = pl.BlockSpec((1, tm, A_WIDTH), lambda i, j: (i, j, 0))
    full = pl.BlockSpec((1, tm, d), lambda i, j: (i, j, 0))
    modspec = pl.BlockSpec((1, 3, d), lambda i, j: (i, 0, 0))
    return pl.pallas_call(
        _tail_kernel,
        grid=(b, nt),
        in_specs=[full, half, half, modspec, modspec, const((1, d)), const((d, d)),
                  const((d, 3 * C_WIDTH)), const((1, d)), const((1, d)),
                  const((C_GROUPS, C_CHUNK, C_CHUNK)), const((C_CHUNK, C_GROUPS)), const((d, d))],
        out_specs=full,
        out_shape=jax.ShapeDtypeStruct((b, s, d), F32),
        compiler_params=pltpu.CompilerParams(
            dimension_semantics=("parallel", "parallel"), vmem_limit_bytes=VMEM_LIMIT),
        name="tail",
    )(x, ya, yb, mod0, mod1, ln_g1.reshape(1, d), w_out0.astype(BF16), w_in1.astype(BF16),
      c_ln_g.reshape(1, d), c_ln_b.reshape(1, d), c_ws, c_bs.T, w_out1.astype(BF16))


def kernel(x, c, ln_g, ada_w, ada_b, w_in, w_out, a_q_g, a_k_g, b_conv_w, b_conv_b, b_wq, b_wk, b_wv,
           b_w_gates, b_b_gates, b_out_g, b_skip, c_ln_g, c_ln_b, c_ws, c_bs):
    mods = _ada_mods(c, ada_w, ada_b)
    qT, vT, gT, k, bx, sbz = _inproj0(x, mods[0], ln_g[0], w_in[0], a_q_g[0], a_k_g[0])
    ya = _moba(qT, k, vT, gT)
    yb = _mlstm(bx, sbz, b_conv_w[0], b_conv_b[0], b_wq[0], b_wk[0], b_wv[0], b_w_gates[0],
                b_b_gates[0], b_out_g[0], b_skip[0])
    return _tail(x, ya, yb, mods[0], mods[1], ln_g[1], w_out[0], w_in[1], w_out[1],
                 c_ln_g[0], c_ln_b[0], c_ws[0], c_bs[0])
```

```python
import functools

import jax
import jax.numpy as jnp
from jax import lax
from jax.experimental import pallas as pl
from jax.experimental.pallas import tpu as pltpu

F32 = jnp.float32
BF16 = jnp.bfloat16

D_MODEL = 1024
A_HEADS = 8
A_HEAD_DIM = 64
A_WIDTH = A_HEADS * A_HEAD_DIM
MOBA_BLOCK = 256
MOBA_TOPK = 3
B_HEADS = 4
B_HEAD_DIM = 128
B_WIDTH = B_HEADS * B_HEAD_DIM
B_CONV = 4
C_GROUPS = 8
C_CHUNK = 128
C_WIDTH = D_MODEL
NEG = -1e30
LOG2E = 1.4426950408889634

MLSTM_L = 256
ROW_TILE = 256
CONV_HALO = 8
VMEM_LIMIT = 48 * 1024 * 1024


def _silu(x):
    return x * jax.nn.sigmoid(x)


def _gelu_tanh(x):
    return 0.5 * x * (1.0 + jnp.tanh(0.7978845608028654 * (x + 0.044715 * (x * x * x))))


def _log_sigmoid(x):
    return jnp.minimum(x, 0.0) - jnp.log(1.0 + jnp.exp(-jnp.abs(x)))


def _split_bf16(x):
    hi = x.astype(BF16)
    lo = (x - hi.astype(F32)).astype(BF16)
    return hi, lo


def _dot(a, b):
    return jnp.dot(a, b, preferred_element_type=F32)


def _dot_nt(a, b):
    return lax.dot_general(a, b, (((1,), (1,)), ((), ())), preferred_element_type=F32)


def _adaln_rmsnorm(x, ln_g, scale, shift):
    y = x * lax.rsqrt(jnp.mean(x * x, axis=-1, keepdims=True) + 1e-6)
    return (y * ln_g) * (1.0 + scale) + shift


def _ada_kernel(c_ref, w_ref, b_ref, o_ref):
    c = c_ref[...]
    cs = _silu(c)
    o_ref[0] = jnp.dot(cs, w_ref[0], preferred_element_type=F32,
                       precision=lax.Precision.HIGHEST) + b_ref[0]


def _ada_mods(c, ada_w, ada_b):
    depth, d, d3 = ada_w.shape
    b = c.shape[0]
    bp = 8
    cp = jnp.zeros((bp, d), F32).at[:b].set(c)
    nt = d3 // d
    out = pl.pallas_call(
        _ada_kernel,
        grid=(depth, nt),
        in_specs=[
            pl.BlockSpec((bp, d), lambda l, n: (0, 0)),
            pl.BlockSpec((1, d, d), lambda l, n: (l, 0, n)),
            pl.BlockSpec((1, 1, d), lambda l, n: (l, 0, n)),
        ],
        out_specs=pl.BlockSpec((1, bp, d), lambda l, n: (l, 0, n)),
        out_shape=jax.ShapeDtypeStruct((depth, bp, d3), F32),
        name="ada_mods",
    )(cp, ada_w, ada_b.reshape(depth, 1, d3))
    return out[:, :b].reshape(depth, b, 3, d)


def _inproj0_kernel(x_ref, mod_ref, lng_ref, wt_ref, w_ref, qg_ref, kg_ref, pmat_ref,
                    qT_ref, vT_ref, gT_ref, k_ref, bx_ref, sbz_ref):
    x = x_ref[0]
    mod = mod_ref[0]
    h = _adaln_rmsnorm(x, lng_ref[...], mod[1:2], mod[0:1]).astype(BF16)
    pt = _dot_nt(wt_ref[...], h)
    qg = qg_ref[...]
    scale = A_HEAD_DIM ** -0.5 * LOG2E
    for hd in range(A_HEADS):
        lo = hd * A_HEAD_DIM
        q = pt[lo:lo + A_HEAD_DIM]
        r = lax.rsqrt(jnp.mean(q * q, axis=0, keepdims=True) + 1e-6)
        qT_ref[0, 0, lo:lo + A_HEAD_DIM, :] = ((q * r) * (qg * scale)).astype(BF16)
    vT_ref[0, 0] = pt[A_WIDTH:2 * A_WIDTH].astype(BF16)
    gT_ref[0, 0] = _silu(pt[2 * A_WIDTH:]).astype(BF16)
    p = _dot(h, w_ref[...])
    k = p[:, :A_WIDTH]
    ksq_hi, ksq_lo = _split_bf16(k * k)
    pm = pmat_ref[...]
    ms = _dot(ksq_hi, pm) + _dot(ksq_lo, pm)
    k_ref[0] = ((k * lax.rsqrt(ms + 1e-6)) * kg_ref[...]).astype(BF16)
    bx_ref[0] = p[:, A_WIDTH:A_WIDTH + B_WIDTH].astype(BF16)
    sbz_ref[0] = _silu(p[:, A_WIDTH + B_WIDTH:]).astype(BF16)


def _inproj0(x, mod, ln_g, w_in, q_g, k_g):
    b, s, d = x.shape
    tm = ROW_TILE
    nt = s // tm
    aw, bw = A_WIDTH, B_WIDTH
    wq, wk, wv, wz = (w_in[:, i * aw:(i + 1) * aw] for i in range(4))
    wbx = w_in[:, 4 * aw:4 * aw + bw]
    wbz = w_in[:, 4 * aw + bw:]
    wt = jnp.concatenate([wq, wv, wz], axis=1).T.astype(BF16)
    wn = jnp.concatenate([wk, wbx, wbz], axis=1).astype(BF16)
    head_of = jnp.arange(aw) // A_HEAD_DIM
    pmat = (head_of[:, None] == head_of[None, :]).astype(BF16) * (1.0 / A_HEAD_DIM)
    pmat = pmat.astype(BF16)
    t_shape = jax.ShapeDtypeStruct((b, nt, aw, tm), BF16)
    n_shape = jax.ShapeDtypeStruct((b, s, aw), BF16)
    t_spec = pl.BlockSpec((1, 1, aw, tm), lambda i, j: (i, j, 0, 0))
    n_spec = pl.BlockSpec((1, tm, aw), lambda i, j: (i, j, 0))
    const = lambda shape: pl.BlockSpec(shape, lambda i, j: (0,) * len(shape))
    return pl.pallas_call(
        _inproj0_kernel,
        grid=(b, nt),
        in_specs=[
            pl.BlockSpec((1, tm, d), lambda i, j: (i, j, 0)),
            pl.BlockSpec((1, 3, d), lambda i, j: (i, 0, 0)),
            const((1, d)),
            const((3 * aw, d)),
            const((d, aw + 2 * bw)),
            const((A_HEAD_DIM, 1)),
            const((1, aw)),
            const((aw, aw)),
        ],
        out_specs=[t_spec, t_spec, t_spec, n_spec, n_spec, n_spec],
        out_shape=[t_shape, t_shape, t_shape, n_shape, n_shape, n_shape],
        compiler_params=pltpu.CompilerParams(
            dimension_semantics=("parallel", "parallel"), vmem_limit_bytes=VMEM_LIMIT),
        name="inproj0",
    )(x, mod, ln_g.reshape(1, d), wt, wn, q_g.reshape(A_HEAD_DIM, 1),
      jnp.tile(k_g, A_HEADS).reshape(1, aw), pmat)


def _moba_kernel(qT_ref, k_ref, vT_ref, gT_ref, o_ref, kmean_ref, bias_ref, qm_ref, s_ref, p_ref,
                 m_ref, l_ref, alpha_ref, acc_ref, *, nb):
    blk = MOBA_BLOCK
    dh = A_HEAD_DIM
    pw = 2 * dh
    j = pl.program_id(1)

    @pl.when(j == 0)
    def _():
        for n in range(nb):
            kb = k_ref[0, n * blk:(n + 1) * blk, :].astype(F32)
            kmean_ref[n:n + 1, :] = jnp.mean(kb, axis=0, keepdims=True)

    d_iota = lax.broadcasted_iota(jnp.int32, (pw, blk), 0)
    for hd in range(A_HEADS):
        pr = hd // 2
        q2 = qT_ref[0, 0, pr * pw:(pr + 1) * pw, :]
        keep = (d_iota < dh) if hd % 2 == 0 else (d_iota >= dh)
        qm_ref[hd] = jnp.where(keep, q2, jnp.zeros_like(q2))

    km_hi, km_lo = _split_bf16(kmean_ref[...])
    n_iota = lax.broadcasted_iota(jnp.int32, (nb, blk), 0)
    past = n_iota < j
    for hd in range(A_HEADS):
        pr = hd // 2
        qm = qm_ref[hd]
        sc = (_dot(km_hi[:, pr * pw:(pr + 1) * pw], qm)
              + _dot(km_lo[:, pr * pw:(pr + 1) * pw], qm))
        sc = jnp.where(past, sc, NEG)
        rank = jnp.zeros((nb, blk), jnp.int32)
        for n2 in range(nb):
            r = sc[n2:n2 + 1, :]
            beats = jnp.where(r > sc, 1, jnp.where(r == sc, (n_iota > n2).astype(jnp.int32), 0))
            rank = rank + beats
        sel = jnp.where(past, rank, MOBA_TOPK) < MOBA_TOPK
        bias_ref[hd] = jnp.where(sel, 0.0, NEG)

    k_iota = lax.broadcasted_iota(jnp.int32, (blk, blk), 0)
    q_iota = lax.broadcasted_iota(jnp.int32, (blk, blk), 1)
    causal = k_iota <= q_iota

    def tile(n, own):
        row0 = pl.multiple_of(n * blk, blk)
        for hd in range(A_HEADS):
            pr = hd // 2
            kt = k_ref[0, pl.ds(row0, blk), pr * pw:(pr + 1) * pw]
            s_ref[hd] = _dot(kt, qm_ref[hd])
        for hd in range(A_HEADS):
            s = s_ref[hd]
            if own:
                s = jnp.where(causal, s, NEG)
                m_new = jnp.max(s, axis=0, keepdims=True)
                p = jnp.exp2(s - m_new)
                l_ref[hd] = jnp.sum(p, axis=0, keepdims=True)
            else:
                s = s + bias_ref[hd, pl.ds(n, 1), :]
                m_old = m_ref[hd]
                m_new = jnp.maximum(m_old, jnp.max(s, axis=0, keepdims=True))
                alpha = jnp.exp2(m_old - m_new)
                p = jnp.exp2(s - m_new)
                l_ref[hd] = alpha * l_ref[hd] + jnp.sum(p, axis=0, keepdims=True)
                alpha_ref[hd] = alpha
            m_ref[hd] = m_new
            p_ref[hd] = p.astype(BF16)
        for hd in range(A_HEADS):
            pv = _dot(vT_ref[0, n, hd * dh:(hd + 1) * dh, :], p_ref[hd])
            if own:
                acc_ref[hd] = pv
            else:
                acc_ref[hd] = alpha_ref[hd] * acc_ref[hd] + pv

    tile(j, True)

    def body(n, carry):
        tile(n, False)
        return carry

    lax.fori_loop(0, j, body, 0)
    for pr in range(A_HEADS // 2):
        oT = jnp.concatenate([acc_ref[2 * pr] / l_ref[2 * pr],
                              acc_ref[2 * pr + 1] / l_ref[2 * pr + 1]], axis=0)
        oT = oT * gT_ref[0, 0, pr * pw:(pr + 1) * pw, :].astype(F32)
        o_ref[0, :, pr * pw:(pr + 1) * pw] = oT.T.astype(BF16)


def _moba(qT, k, vT, gT):
    b, nb, aw, blk = qT.shape
    s = nb * blk
    pw = 2 * A_HEAD_DIM
    q_spec = pl.BlockSpec((1, 1, aw, blk), lambda i, j: (i, j, 0, 0))
    row = lambda: pltpu.VMEM((A_HEADS, 1, blk), F32)
    return pl.pallas_call(
        functools.partial(_moba_kernel, nb=nb),
        grid=(b, nb),
        in_specs=[
            q_spec,
            pl.BlockSpec((1, s, aw), lambda i, j: (i, 0, 0)),
            pl.BlockSpec((1, nb, aw, blk), lambda i, j: (i, 0, 0, 0)),
            q_spec,
        ],
        out_specs=pl.BlockSpec((1, blk, aw), lambda i, j: (i, j, 0)),
        out_shape=jax.ShapeDtypeStruct((b, s, aw), BF16),
        scratch_shapes=[
            pltpu.VMEM((nb, aw), F32),
            pltpu.VMEM((A_HEADS, nb, blk), F32),
            pltpu.VMEM((A_HEADS, pw, blk), BF16),
            pltpu.VMEM((A_HEADS, blk, blk), F32),
            pltpu.VMEM((A_HEADS, blk, blk), BF16),
            row(), row(), row(),
            pltpu.VMEM((A_HEADS, A_HEAD_DIM, blk), F32),
        ],
        compiler_params=pltpu.CompilerParams(
            dimension_semantics=("parallel", "arbitrary"), vmem_limit_bytes=VMEM_LIMIT),
        name="moba",
    )(qT, k, vT, gT)


def _mlstm_kernel(bx_ref, sbz_ref, cw_ref, cb_ref, wq_ref, wk_ref, wkt_ref, wv_ref, wg_ref, bg_ref,
                  og_ref, skip_ref, o_ref, xbuf_ref, state_ref, m_ref):
    L = MLSTM_L
    dh = B_HEAD_DIM
    c = pl.program_id(1)

    @pl.when(c == 0)
    def _():
        xbuf_ref[0:CONV_HALO, :] = jnp.zeros((CONV_HALO, B_WIDTH), F32)
        state_ref[...] = jnp.zeros_like(state_ref)
        m_ref[...] = jnp.zeros_like(m_ref)

    bx_b = bx_ref[0]
    xbuf_ref[CONV_HALO:CONV_HALO + L, :] = bx_b.astype(F32)
    cw = cw_ref[...]
    conv = cb_ref[...]
    for i in range(B_CONV):
        off = CONV_HALO - (B_CONV - 1) + i
        conv = conv + cw[i:i + 1, :] * xbuf_ref[off:off + L, :]
    tail = xbuf_ref[L:L + CONV_HALO, :]
    xbuf_ref[0:CONV_HALO, :] = tail
    xc = _silu(conv)
    xc_b = xc.astype(BF16)

    bq = _dot(xc_b, wq_ref[...])
    bk = _dot(xc_b, wk_ref[...])
    bkT = _dot_nt(wkt_ref[...], xc_b)
    bv = _dot(bx_b, wv_ref[...])
    bq_b = bq.astype(BF16)
    wg = wg_ref[...]
    gates = (_dot(bq_b, wg[:B_WIDTH]) + _dot(bk.astype(BF16), wg[B_WIDTH:2 * B_WIDTH])
             + _dot(bv.astype(BF16), wg[2 * B_WIDTH:]) + bg_ref[...])
    ig = gates[:, :128]
    lf = _log_sigmoid(gates[:, 128:])
    t_iota = lax.broadcasted_iota(jnp.int32, (L, L), 0)
    s_iota = lax.broadcasted_iota(jnp.int32, (L, L), 1)
    tril = s_iota <= t_iota
    tri_b = jnp.where(tril, 1.0, 0.0).astype(BF16)
    lf_hi, lf_lo = _split_bf16(lf)
    cum = _dot(tri_b, lf_hi) + _dot(tri_b, lf_lo)
    a = ig - cum
    aT = a.T
    ones = jnp.ones((L, dh), F32)
    kscale = dh ** -0.5

    outs = []
    for hd in range(B_HEADS):
        lo = hd * dh
        q_h = bq_b[:, lo:lo + dh]
        kT_h = (bkT[lo:lo + dh, :] * kscale).astype(BF16)
        v_aug = jnp.concatenate([bv[:, lo:lo + dh], ones], axis=1)
        a_row = aT[hd:hd + 1, :]
        a_col = a[:, hd:hd + 1]
        cum_col = cum[:, hd:hd + 1]
        m_prev = m_ref[hd][0:1, 0:1]
        amax = jnp.max(jnp.where(tril, a_row, NEG), axis=1, keepdims=True)
        b_col = jnp.maximum(m_prev, amax)
        dmat = jnp.exp(jnp.where(tril, a_row - b_col, NEG))
        w_intra = (_dot(q_h, kT_h) * dmat).astype(BF16)
        intra = _dot(w_intra, v_aug.astype(BF16))
        state = state_ref[hd]
        inter = _dot(q_h, state.astype(BF16))
        w_inter = jnp.exp(m_prev - b_col)
        num = w_inter * inter[:, :dh] + intra[:, :dh]
        den = w_inter * inter[:, dh:dh + 1] + intra[:, dh:dh + 1]
        hc = num / jnp.maximum(jnp.abs(den), jnp.exp(-(cum_col + b_col)))
        mu = jnp.mean(hc, axis=-1, keepdims=True)
        hcc = hc - mu
        var = jnp.mean(hcc * hcc, axis=-1, keepdims=True)
        outs.append(hcc * lax.rsqrt(var + 1e-5))
        b_end = b_col[L - 1:L, :]
        ws = jnp.exp(a_col - b_end)
        decay = jnp.exp(m_prev - b_end)
        state_ref[hd] = decay * state + _dot(kT_h, (ws * v_aug).astype(BF16))
        m_ref[hd] = jnp.broadcast_to(cum_col[L - 1:L, :] + b_end, m_ref.shape[1:])

    hb = jnp.concatenate(outs, axis=1) * og_ref[...]
    yb = (hb + skip_ref[...] * xc) * sbz_ref[0].astype(F32)
    o_ref[0] = yb.astype(BF16)


def _block_diag_dense(w):
    nblk, blk, _ = w.shape
    n = nblk * blk
    idx = jnp.arange(n) // blk
    return jnp.where(idx[:, None] == idx[None, :], jnp.tile(w.reshape(n, blk), (1, nblk)), 0.0)


def _mlstm(bx, sbz, conv_w, conv_b, wq, wk, wv, w_gates, b_gates, out_g, skip):
    b, s, bw = bx.shape
    L = MLSTM_L
    nc = s // L
    wq_d = _block_diag_dense(wq).astype(BF16)
    wk_d = _block_diag_dense(wk)
    wkt_d = wk_d.T.astype(BF16)
    wk_d = wk_d.astype(BF16)
    wv_d = _block_diag_dense(wv).astype(BF16)
    wg = jnp.zeros((3 * bw, 256), F32)
    wg = wg.at[:, :B_HEADS].set(w_gates[:, :B_HEADS]).at[:, 128:128 + B_HEADS].set(w_gates[:, B_HEADS:])
    bg = jnp.zeros((1, 256), F32)
    bg = bg.at[0, :B_HEADS].set(b_gates[:B_HEADS]).at[0, 128:128 + B_HEADS].set(b_gates[B_HEADS:])
    const = lambda shape: pl.BlockSpec(shape, lambda i, j: (0,) * len(shape))
    tok = pl.BlockSpec((1, L, bw), lambda i, j: (i, j, 0))
    return pl.pallas_call(
        _mlstm_kernel,
        grid=(b, nc),
        in_specs=[tok, tok, const((B_CONV, bw)), const((1, bw)), const((bw, bw)), const((bw, bw)),
                  const((bw, bw)), const((bw, bw)), const((3 * bw, 256)), const((1, 256)),
                  const((1, bw)), const((1, bw))],
        out_specs=tok,
        out_shape=jax.ShapeDtypeStruct((b, s, bw), BF16),
        scratch_shapes=[pltpu.VMEM((L + CONV_HALO, bw), F32),
                        pltpu.VMEM((B_HEADS, B_HEAD_DIM, 2 * B_HEAD_DIM), F32),
                        pltpu.VMEM((B_HEADS, 8, 128), F32)],
        compiler_params=pltpu.CompilerParams(
            dimension_semantics=("parallel", "arbitrary"), vmem_limit_bytes=VMEM_LIMIT),
        name="mlstm",
    )(bx, sbz, conv_w, conv_b.reshape(1, bw), wq_d, wk_d, wkt_d, wv_d, wg.astype(BF16), bg,
      out_g.reshape(1, bw), skip.reshape(1, bw))


def _tail_kernel(x_ref, ya_ref, yb_ref, mod0_ref, mod1_ref, lng_ref, wo0_ref, wi1_ref, clg_ref,
                 clb_ref, ws_ref, bst_ref, wo1_ref, o_ref):
    tm = x_ref.shape[1]
    aw = A_WIDTH
    wo0 = wo0_ref
    y0 = _dot(ya_ref[0], wo0[0:aw, :]) + _dot(yb_ref[0], wo0[aw:, :])
    x1 = x_ref[0] + mod0_ref[0][2:3] * y0
    mod1 = mod1_ref[0]
    h = _adaln_rmsnorm(x1, lng_ref[...], mod1[1:2], mod1[0:1]).astype(BF16)
    p = _dot(h, wi1_ref[...])
    u = _gelu_tanh(p[:, :C_WIDTH])
    v = _gelu_tanh(p[:, C_WIDTH:2 * C_WIDTH])
    mu = jnp.mean(v, axis=-1, keepdims=True)
    vc = v - mu
    var = jnp.mean(vc * vc, axis=-1, keepdims=True)
    vn = ((vc * lax.rsqrt(var + 1e-5)) * clg_ref[...] + clb_ref[...]).astype(BF16)
    gate = u * _silu(p[:, 2 * C_WIDTH:])
    t_iota = lax.broadcasted_iota(jnp.int32, (C_CHUNK, C_CHUNK), 0)
    s_iota = lax.broadcasted_iota(jnp.int32, (C_CHUNK, C_CHUNK), 1)
    tril = s_iota <= t_iota
    gw = C_WIDTH // C_GROUPS
    bst = bst_ref[...]
    cols = []
    for g in range(C_GROUPS):
        wm = jnp.where(tril, ws_ref[g], 0.0).astype(BF16)
        rows = []
        for ch in range(tm // C_CHUNK):
            vg = vn[ch * C_CHUNK:(ch + 1) * C_CHUNK, g * gw:(g + 1) * gw]
            rows.append(_dot(wm, vg) + bst[:, g:g + 1])
        cols.append(jnp.concatenate(rows, axis=0))
    sg = jnp.concatenate(cols, axis=1)
    y1 = (gate * sg).astype(BF16)
    o_ref[0] = x1 + mod1[2:3] * _dot(y1, wo1_ref[...])


def _tail(x, ya, yb, mod0, mod1, ln_g1, w_out0, w_in1, w_out1, c_ln_g, c_ln_b, c_ws, c_bs):
    b, s, d = x.shape
    tm = ROW_TILE
    nt = s // tm
    const = lambda shape: pl.BlockSpec(shape, lambda i, j: (0,) * len(shape))
    half = pl.BlockSpec((1, tm, A_WIDTH), lambda i, j: (i, j, 0))
    full = pl.BlockSpec((1, tm, d), lambda i, j: (i, j, 0))
    modspec = pl.BlockSpec((1, 3, d), lambda i, j: (i, 0, 0))
    return pl.pallas_call(
        _tail_kernel,
        grid=(b, nt),
        in_specs=[full, half, half, modspec, modspec, const((1, d)), const((d, d)),
                  const((d, 3 * C_WIDTH)), const((1, d)), const((1, d)),
                  const((C_GROUPS, C_CHUNK, C_CHUNK)), const((C_CHUNK, C_GROUPS)), const((d, d))],
        out_specs=full,
        out_shape=jax.ShapeDtypeStruct((b, s, d), F32),
        compiler_params=pltpu.CompilerParams(
            dimension_semantics=("parallel", "parallel"), vmem_limit_bytes=VMEM_LIMIT),
        name="tail",
    )(x, ya, yb, mod0, mod1, ln_g1.reshape(1, d), w_out0.astype(BF16), w_in1.astype(BF16),
      c_ln_g.reshape(1, d), c_ln_b.reshape(1, d), c_ws, c_bs.T, w_out1.astype(BF16))


def kernel(x, c, ln_g, ada_w, ada_b, w_in, w_out, a_q_g, a_k_g, b_conv_w, b_conv_b, b_wq, b_wk, b_wv,
           b_w_gates, b_b_gates, b_out_g, b_skip, c_ln_g, c_ln_b, c_ws, c_bs):
    mods = _ada_mods(c, ada_w, ada_b)
    qT, vT, gT, k, bx, sbz = _inproj0(x, mods[0], ln_g[0], w_in[0], a_q_g[0], a_k_g[0])
    ya = _moba(qT, k, vT, gT)
    yb = _mlstm(bx, sbz, b_conv_w[0], b_conv_b[0], b_wq[0], b_wk[0], b_wv[0], b_w_gates[0],
                b_b_gates[0], b_out_g[0], b_skip[0])
    return _tail(x, ya, yb, mods[0], mods[1], ln_g[1], w_out[0], w_in[1], w_out[1],
                 c_ln_g[0], c_ln_b[0], c_ws[0], c_bs[0])
```

```python
import functools

import jax
import jax.numpy as jnp
from jax import lax
from jax.experimental import pallas as pl
from jax.experimental.pallas import tpu as pltpu

F32 = jnp.float32
BF16 = jnp.bfloat16

D_MODEL = 1024
A_HEADS = 8
A_HEAD_DIM = 64
A_WIDTH = A_HEADS * A_HEAD_DIM
MOBA_BLOCK = 256
MOBA_TOPK = 3
B_HEADS = 4
B_HEAD_DIM = 128
B_WIDTH = B_HEADS * B_HEAD_DIM
B_CONV = 4
C_GROUPS = 8
C_CHUNK = 128
C_WIDTH = D_MODEL
NEG = -1e30
LOG2E = 1.4426950408889634

MLSTM_L = 256
ROW_TILE = 256
CONV_HALO = 8
PV_ONES_ROWS = 16
VMEM_LIMIT = 48 * 1024 * 1024


def _silu(x):
    return x * jax.nn.sigmoid(x)


def _gelu_tanh(x):
    return 0.5 * x * (1.0 + jnp.tanh(0.7978845608028654 * (x + 0.044715 * (x * x * x))))


def _log_sigmoid(x):
    return jnp.minimum(x, 0.0) - jnp.log(1.0 + jnp.exp(-jnp.abs(x)))


def _split_bf16(x):
    hi = x.astype(BF16)
    lo = (x - hi.astype(F32)).astype(BF16)
    return hi, lo


def _dot(a, b):
    return jnp.dot(a, b, preferred_element_type=F32)


def _dot_nt(a, b):
    return lax.dot_general(a, b, (((1,), (1,)), ((), ())), preferred_element_type=F32)


def _adaln_rmsnorm(x, ln_g, scale, shift):
    y = x * lax.rsqrt(jnp.mean(x * x, axis=-1, keepdims=True) + 1e-6)
    return (y * ln_g) * (1.0 + scale) + shift


def _ada_kernel(c_ref, w_ref, b_ref, o_ref):
    c = c_ref[...]
    cs = _silu(c)
    o_ref[0] = jnp.dot(cs, w_ref[0], preferred_element_type=F32,
                       precision=lax.Precision.HIGHEST) + b_ref[0]


def _ada_mods(c, ada_w, ada_b):
    depth, d, d3 = ada_w.shape
    b = c.shape[0]
    bp = 8
    cp = jnp.zeros((bp, d), F32).at[:b].set(c)
    nt = d3 // d
    out = pl.pallas_call(
        _ada_kernel,
        grid=(depth, nt),
        in_specs=[
            pl.BlockSpec((bp, d), lambda l, n: (0, 0)),
            pl.BlockSpec((1, d, d), lambda l, n: (l, 0, n)),
            pl.BlockSpec((1, 1, d), lambda l, n: (l, 0, n)),
        ],
        out_specs=pl.BlockSpec((1, bp, d), lambda l, n: (l, 0, n)),
        out_shape=jax.ShapeDtypeStruct((depth, bp, d3), F32),
        name="ada_mods",
    )(cp, ada_w, ada_b.reshape(depth, 1, d3))
    return out[:, :b].reshape(depth, b, 3, d)


def _inproj0_kernel(x_ref, mod_ref, lng_ref, wt_ref, w_ref, qg_ref, kg_ref,
                    qT_ref, vT_ref, gT_ref, k_ref, bx_ref, sbz_ref):
    x = x_ref[0]
    mod = mod_ref[0]
    h = _adaln_rmsnorm(x, lng_ref[...], mod[1:2], mod[0:1]).astype(BF16)
    pt = _dot_nt(wt_ref[...], h)
    aw = A_WIDTH
    qg = qg_ref[...] * (A_HEAD_DIM ** -0.5 * LOG2E)
    kg = kg_ref[...]
    kn = []
    for hd in range(A_HEADS):
        lo = hd * A_HEAD_DIM
        q = pt[lo:lo + A_HEAD_DIM]
        r = lax.rsqrt(jnp.mean(q * q, axis=0, keepdims=True) + 1e-6)
        qT_ref[0, 0, lo:lo + A_HEAD_DIM, :] = ((q * r) * qg).astype(BF16)
        k = pt[aw + lo:aw + lo + A_HEAD_DIM]
        r = lax.rsqrt(jnp.mean(k * k, axis=0, keepdims=True) + 1e-6)
        kn.append((k * r) * kg)
    k_ref[0] = jnp.concatenate(kn, axis=0).T.astype(BF16)
    vT_ref[0, 0] = pt[2 * aw:3 * aw].astype(BF16)
    gT_ref[0, 0] = _silu(pt[3 * aw:]).astype(BF16)
    p = _dot(h, w_ref[...])
    bx_ref[0] = p[:, :B_WIDTH].astype(BF16)
    sbz_ref[0] = _silu(p[:, B_WIDTH:]).astype(BF16)


def _inproj0(x, mod, ln_g, w_in, q_g, k_g):
    b, s, d = x.shape
    tm = ROW_TILE
    nt = s // tm
    aw, bw = A_WIDTH, B_WIDTH
    wq, wk, wv, wz = (w_in[:, i * aw:(i + 1) * aw] for i in range(4))
    wbx = w_in[:, 4 * aw:4 * aw + bw]
    wbz = w_in[:, 4 * aw + bw:]
    wt = jnp.concatenate([wq, wk, wv, wz], axis=1).T.astype(BF16)
    wn = jnp.concatenate([wbx, wbz], axis=1).astype(BF16)
    t_shape = jax.ShapeDtypeStruct((b, nt, aw, tm), BF16)
    n_shape = jax.ShapeDtypeStruct((b, s, aw), BF16)
    t_spec = pl.BlockSpec((1, 1, aw, tm), lambda i, j: (i, j, 0, 0))
    n_spec = pl.BlockSpec((1, tm, aw), lambda i, j: (i, j, 0))
    const = lambda shape: pl.BlockSpec(shape, lambda i, j: (0,) * len(shape))
    return pl.pallas_call(
        _inproj0_kernel,
        grid=(b, nt),
        in_specs=[
            pl.BlockSpec((1, tm, d), lambda i, j: (i, j, 0)),
            pl.BlockSpec((1, 3, d), lambda i, j: (i, 0, 0)),
            const((1, d)),
            const((4 * aw, d)),
            const((d, 2 * bw)),
            const((A_HEAD_DIM, 1)),
            const((A_HEAD_DIM, 1)),
        ],
        out_specs=[t_spec, t_spec, t_spec, n_spec, n_spec, n_spec],
        out_shape=[t_shape, t_shape, t_shape, n_shape, n_shape, n_shape],
        compiler_params=pltpu.CompilerParams(
            dimension_semantics=("parallel", "parallel"), vmem_limit_bytes=VMEM_LIMIT),
        name="inproj0",
    )(x, mod, ln_g.reshape(1, d), wt, wn, q_g.reshape(A_HEAD_DIM, 1), k_g.reshape(A_HEAD_DIM, 1))


def _moba_kernel(qT_ref, k_ref, vT_ref, gT_ref, o_ref, kmean_ref, sel_ref, qm_ref, raw_ref, p_ref,
                 m_ref, alpha_ref, acc_ref, *, nb):
    blk = MOBA_BLOCK
    dh = A_HEAD_DIM
    pw = 2 * dh
    j = pl.program_id(1)

    @pl.when(j == 0)
    def _():
        for n in range(nb):
            kb = k_ref[0, n * blk:(n + 1) * blk, :].astype(F32)
            kmean_ref[n:n + 1, :] = jnp.mean(kb, axis=0, keepdims=True)

    d_iota = lax.broadcasted_iota(jnp.int32, (pw, blk), 0)
    for hd in range(A_HEADS):
        pr = hd // 2
        q2 = qT_ref[0, 0, pr * pw:(pr + 1) * pw, :]
        keep = (d_iota < dh) if hd % 2 == 0 else (d_iota >= dh)
        qm_ref[hd] = jnp.where(keep, q2, jnp.zeros_like(q2))

    def score_dots(n, slot):
        row0 = pl.multiple_of(n * blk, blk)
        for hd in range(A_HEADS):
            pr = hd // 2
            kt = k_ref[0, pl.ds(row0, blk), pr * pw:(pr + 1) * pw]
            raw_ref[slot, hd] = _dot(kt, qm_ref[hd])

    def value_dots(n, slot):
        ones = jnp.ones((PV_ONES_ROWS, blk), BF16)
        return [_dot(jnp.concatenate([vT_ref[0, n, hd * dh:(hd + 1) * dh, :], ones], axis=0),
                     p_ref[slot, hd]) for hd in range(A_HEADS)]

    score_dots(j, 0)
    score_dots(0, 1)

    km_hi, km_lo = _split_bf16(kmean_ref[...])
    n_iota = lax.broadcasted_iota(jnp.int32, (nb, blk), 0)
    past = n_iota < j
    for hd in range(A_HEADS):
        pr = hd // 2
        qm = qm_ref[hd]
        sc = (_dot(km_hi[:, pr * pw:(pr + 1) * pw], qm)
              + _dot(km_lo[:, pr * pw:(pr + 1) * pw], qm))
        sc = jnp.where(past, sc, NEG)
        rank = jnp.zeros((nb, blk), jnp.int32)
        for n2 in range(nb):
            r = sc[n2:n2 + 1, :]
            beats = jnp.where(r > sc, 1, jnp.where(r == sc, (n_iota > n2).astype(jnp.int32), 0))
            rank = rank + beats
        sel = jnp.where(past, rank, MOBA_TOPK) < MOBA_TOPK
        sel_ref[hd, 0:nb, :] = jnp.where(sel, 1.0, 0.0)
        sel_ref[hd, nb:nb + 1, :] = jnp.ones((1, blk), F32)

    k_iota = lax.broadcasted_iota(jnp.int32, (blk, blk), 0)
    q_iota = lax.broadcasted_iota(jnp.int32, (blk, blk), 1)
    causal = k_iota <= q_iota
    for hd in range(A_HEADS):
        s = jnp.where(causal, raw_ref[0, hd], NEG)
        m_new = jnp.max(s, axis=0, keepdims=True)
        p_ref[0, hd] = jnp.exp2(s - m_new).astype(BF16)
        m_ref[hd] = m_new
        acc_ref[hd] = jnp.zeros(acc_ref.shape[1:], F32)

    def step(i, cur):
        nxt = 1 - cur
        n = i - 1
        prev = jnp.where(i == 1, j, i - 2)
        prev_sel = jnp.where(i == 1, nb, i - 2)
        score_dots(jnp.minimum(i, nb - 1), nxt)
        pvs = value_dots(prev, nxt)
        for hd in range(A_HEADS):
            s = raw_ref[cur, hd]
            selb = sel_ref[hd, pl.ds(n, 1), :] > 0.5
            smax = jnp.where(selb, jnp.max(s, axis=0, keepdims=True), NEG)
            m_old = m_ref[hd]
            m_new = jnp.maximum(m_old, smax)
            alpha_ref[hd] = jnp.exp2(m_old - m_new)
            m_ref[hd] = m_new
            p_ref[cur, hd] = jnp.exp2(s - m_new).astype(BF16)
        for hd in range(A_HEADS):
            keep = sel_ref[hd, pl.ds(prev_sel, 1), :] > 0.5
            acc_ref[hd] = alpha_ref[hd] * (acc_ref[hd] + jnp.where(keep, pvs[hd], 0.0))

    def body(ii, carry):
        step(2 * ii + 1, 1)
        step(2 * ii + 2, 0)
        return carry

    trips = (j + 1) // 2
    lax.fori_loop(0, trips, body, 0)

    last = jnp.where(j == 0, j, 2 * trips - 1)
    last_sel = jnp.where(j == 0, nb, 2 * trips - 1)
    pvs = value_dots(last, 0)
    for pr in range(A_HEADS // 2):
        halves = []
        for hd in (2 * pr, 2 * pr + 1):
            keep = sel_ref[hd, pl.ds(last_sel, 1), :] > 0.5
            acc = acc_ref[hd] + jnp.where(keep, pvs[hd], 0.0)
            halves.append(acc[0:dh] / acc[dh:dh + 1])
        oT = jnp.concatenate(halves, axis=0)
        oT = oT * gT_ref[0, 0, pr * pw:(pr + 1) * pw, :].astype(F32)
        o_ref[0, :, pr * pw:(pr + 1) * pw] = oT.T.astype(BF16)


def _moba(qT, k, vT, gT):
    b, nb, aw, blk = qT.shape
    s = nb * blk
    pw = 2 * A_HEAD_DIM
    q_spec = pl.BlockSpec((1, 1, aw, blk), lambda i, j: (i, j, 0, 0))
    row = lambda: pltpu.VMEM((A_HEADS, 1, blk), F32)
    return pl.pallas_call(
        functools.partial(_moba_kernel, nb=nb),
        grid=(b, nb),
        in_specs=[
            q_spec,
            pl.BlockSpec((1, s, aw), lambda i, j: (i, 0, 0)),
            pl.BlockSpec((1, nb, aw, blk), lambda i, j: (i, 0, 0, 0)),
            q_spec,
        ],
        out_specs=pl.BlockSpec((1, blk, aw), lambda i, j: (i, j, 0)),
        out_shape=jax.ShapeDtypeStruct((b, s, aw), BF16),
        scratch_shapes=[
            pltpu.VMEM((nb, aw), F32),
            pltpu.VMEM((A_HEADS, nb + 8, blk), F32),
            pltpu.VMEM((A_HEADS, pw, blk), BF16),
            pltpu.VMEM((2, A_HEADS, blk, blk), F32),
            pltpu.VMEM((2, A_HEADS, blk, blk), BF16),
            row(), row(),
            pltpu.VMEM((A_HEADS, A_HEAD_DIM + PV_ONES_ROWS, blk), F32),
        ],
        compiler_params=pltpu.CompilerParams(
            dimension_semantics=("parallel", "arbitrary"), vmem_limit_bytes=VMEM_LIMIT),
        name="moba",
    )(qT, k, vT, gT)


def _mlstm_kernel(bx_ref, sbz_ref, cw_ref, cb_ref, wq_ref, wk_ref, wkt_ref, wv_ref, wg_ref, bg_ref,
                  og_ref, skip_ref, o_ref, xbuf_ref, state_ref, m_ref):
    L = MLSTM_L
    dh = B_HEAD_DIM
    c = pl.program_id(1)

    @pl.when(c == 0)
    def _():
        xbuf_ref[0:CONV_HALO, :] = jnp.zeros((CONV_HALO, B_WIDTH), F32)
        state_ref[...] = jnp.zeros_like(state_ref)
        m_ref[...] = jnp.zeros_like(m_ref)

    bx_b = bx_ref[0]
    xbuf_ref[CONV_HALO:CONV_HALO + L, :] = bx_b.astype(F32)
    cw = cw_ref[...]
    conv = cb_ref[...]
    for i in range(B_CONV):
        off = CONV_HALO - (B_CONV - 1) + i
        conv = conv + cw[i:i + 1, :] * xbuf_ref[off:off + L, :]
    tail = xbuf_ref[L:L + CONV_HALO, :]
    xbuf_ref[0:CONV_HALO, :] = tail
    xc = _silu(conv)
    xc_b = xc.astype(BF16)

    bq = _dot(xc_b, wq_ref[...])
    bk = _dot(xc_b, wk_ref[...])
    bkT = _dot_nt(wkt_ref[...], xc_b)
    bv = _dot(bx_b, wv_ref[...])
    bq_b = bq.astype(BF16)
    wg = wg_ref[...]
    gates = (_dot(bq_b, wg[:B_WIDTH]) + _dot(bk.astype(BF16), wg[B_WIDTH:2 * B_WIDTH])
             + _dot(bv.astype(BF16), wg[2 * B_WIDTH:]) + bg_ref[...])
    ig = gates[:, :128]
    lf = _log_sigmoid(gates[:, 128:])
    t_iota = lax.broadcasted_iota(jnp.int32, (L, L), 0)
    s_iota = lax.broadcasted_iota(jnp.int32, (L, L), 1)
    tril = s_iota <= t_iota
    tri_b = jnp.where(tril, 1.0, 0.0).astype(BF16)
    lf_hi, lf_lo = _split_bf16(lf)
    cum = _dot(tri_b, lf_hi) + _dot(tri_b, lf_lo)
    a = ig - cum
    aT = a.T
    ones = jnp.ones((L, dh), F32)
    kscale = dh ** -0.5

    outs = []
    for hd in range(B_HEADS):
        lo = hd * dh
        q_h = bq_b[:, lo:lo + dh]
        kT_h = (bkT[lo:lo + dh, :] * kscale).astype(BF16)
        v_aug = jnp.concatenate([bv[:, lo:lo + dh], ones], axis=1)
        a_row = aT[hd:hd + 1, :]
        a_col = a[:, hd:hd + 1]
        cum_col = cum[:, hd:hd + 1]
        m_prev = m_ref[hd][0:1, 0:1]
        amax = jnp.max(jnp.where(tril, a_row, NEG), axis=1, keepdims=True)
        b_col = jnp.maximum(m_prev, amax)
        dmat = jnp.exp(jnp.where(tril, a_row - b_col, NEG))
        w_intra = (_dot(q_h, kT_h) * dmat).astype(BF16)
        intra = _dot(w_intra, v_aug.astype(BF16))
        state = state_ref[hd]
        inter = _dot(q_h, state.astype(BF16))
        w_inter = jnp.exp(m_prev - b_col)
        num = w_inter * inter[:, :dh] + intra[:, :dh]
        den = w_inter * inter[:, dh:dh + 1] + intra[:, dh:dh + 1]
        hc = num / jnp.maximum(jnp.abs(den), jnp.exp(-(cum_col + b_col)))
        mu = jnp.mean(hc, axis=-1, keepdims=True)
        hcc = hc - mu
        var = jnp.mean(hcc * hcc, axis=-1, keepdims=True)
        outs.append(hcc * lax.rsqrt(var + 1e-5))
        b_end = b_col[L - 1:L, :]
        ws = jnp.exp(a_col - b_end)
        decay = jnp.exp(m_prev - b_end)
        state_ref[hd] = decay * state + _dot(kT_h, (ws * v_aug).astype(BF16))
        m_ref[hd] = jnp.broadcast_to(cum_col[L - 1:L, :] + b_end, m_ref.shape[1:])

    hb = jnp.concatenate(outs, axis=1) * og_ref[...]
    yb = (hb + skip_ref[...] * xc) * sbz_ref[0].astype(F32)
    o_ref[0] = yb.astype(BF16)


def _block_diag_dense(w):
    nblk, blk, _ = w.shape
    n = nblk * blk
    idx = jnp.arange(n) // blk
    return jnp.where(idx[:, None] == idx[None, :], jnp.tile(w.reshape(n, blk), (1, nblk)), 0.0)


def _mlstm(bx, sbz, conv_w, conv_b, wq, wk, wv, w_gates, b_gates, out_g, skip):
    b, s, bw = bx.shape
    L = MLSTM_L
    nc = s // L
    wq_d = _block_diag_dense(wq).astype(BF16)
    wk_d = _block_diag_dense(wk)
    wkt_d = wk_d.T.astype(BF16)
    wk_d = wk_d.astype(BF16)
    wv_d = _block_diag_dense(wv).astype(BF16)
    wg = jnp.zeros((3 * bw, 256), F32)
    wg = wg.at[:, :B_HEADS].set(w_gates[:, :B_HEADS]).at[:, 128:128 + B_HEADS].set(w_gates[:, B_HEADS:])
    bg = jnp.zeros((1, 256), F32)
    bg = bg.at[0, :B_HEADS].set(b_gates[:B_HEADS]).at[0, 128:128 + B_HEADS].set(b_gates[B_HEADS:])
    const = lambda shape: pl.BlockSpec(shape, lambda i, j: (0,) * len(shape))
    tok = pl.BlockSpec((1, L, bw), lambda i, j: (i, j, 0))
    return pl.pallas_call(
        _mlstm_kernel,
        grid=(b, nc),
        in_specs=[tok, tok, const((B_CONV, bw)), const((1, bw)), const((bw, bw)), const((bw, bw)),
                  const((bw, bw)), const((bw, bw)), const((3 * bw, 256)), const((1, 256)),
                  const((1, bw)), const((1, bw))],
        out_specs=tok,
        out_shape=jax.ShapeDtypeStruct((b, s, bw), BF16),
        scratch_shapes=[pltpu.VMEM((L + CONV_HALO, bw), F32),
                        pltpu.VMEM((B_HEADS, B_HEAD_DIM, 2 * B_HEAD_DIM), F32),
                        pltpu.VMEM((B_HEADS, 8, 128), F32)],
        compiler_params=pltpu.CompilerParams(
            dimension_semantics=("parallel", "arbitrary"), vmem_limit_bytes=VMEM_LIMIT),
        name="mlstm",
    )(bx, sbz, conv_w, conv_b.reshape(1, bw), wq_d, wk_d, wkt_d, wv_d, wg.astype(BF16), bg,
      out_g.reshape(1, bw), skip.reshape(1, bw))


def _tail_kernel(x_ref, ya_ref, yb_ref, mod0_ref, mod1_ref, lng_ref, wo0_ref, wi1_ref, clg_ref,
                 clb_ref, ws_ref, bst_ref, wo1_ref, o_ref):
    tm = x_ref.shape[1]
    aw = A_WIDTH
    wo0 = wo0_ref
    y0 = _dot(ya_ref[0], wo0[0:aw, :]) + _dot(yb_ref[0], wo0[aw:, :])
    x1 = x_ref[0] + mod0_ref[0][2:3] * y0
    mod1 = mod1_ref[0]
    h = _adaln_rmsnorm(x1, lng_ref[...], mod1[1:2], mod1[0:1]).astype(BF16)
    p = _dot(h, wi1_ref[...])
    u = _gelu_tanh(p[:, :C_WIDTH])
    v = _gelu_tanh(p[:, C_WIDTH:2 * C_WIDTH])
    mu = jnp.mean(v, axis=-1, keepdims=True)
    vc = v - mu
    var = jnp.mean(vc * vc, axis=-1, keepdims=True)
    vn = ((vc * lax.rsqrt(var + 1e-5)) * clg_ref[...] + clb_ref[...]).astype(BF16)
    gate = u * _silu(p[:, 2 * C_WIDTH:])
    t_iota = lax.broadcasted_iota(jnp.int32, (C_CHUNK, C_CHUNK), 0)
    s_iota = lax.broadcasted_iota(jnp.int32, (C_CHUNK, C_CHUNK), 1)
    tril = s_iota <= t_iota
    gw = C_WIDTH // C_GROUPS
    bst = bst_ref[...]
    cols = []
    for g in range(C_GROUPS):
        wm = jnp.where(tril, ws_ref[g], 0.0).astype(BF16)
        rows = []
        for ch in range(tm // C_CHUNK):
            vg = vn[ch * C_CHUNK:(ch + 1) * C_CHUNK, g * gw:(g + 1) * gw]
            rows.append(_dot(wm, vg) + bst[:, g:g + 1])
        cols.append(jnp.concatenate(rows, axis=0))
    sg = jnp.concatenate(cols, axis=1)
    y1 = (gate * sg).astype(BF16)
    o_ref[0] = x1 + mod1[2:3] * _dot(y1, wo1_ref[...])


def _tail(x, ya, yb, mod0, mod1, ln_g1, w_out0, w_in1, w_out1, c_ln_g, c_ln_b, c_ws, c_bs):
    b, s, d = x.shape
    tm = ROW_TILE
    nt = s // tm
    const = lambda shape: pl.BlockSpec(shape, lambda i, j: (0,) * len(shape))
    half = pl.BlockSpec((1, tm, A_WIDTH), lambda i, j: (i, j, 0))
    full = pl.BlockSpec((1, tm, d), lambda i, j: (i, j, 0))
    modspec = pl.BlockSpec((1, 3, d), lambda i, j: (i, 0, 0))
    return pl.pallas_call(
        _tail_kernel,
        grid=(b, nt),
        in_specs=[full, half, half, modspec, modspec, const((1, d)), const((d, d)),
                  const((d, 3 * C_WIDTH)), const((1, d)), const((1, d)),
                  const((C_GROUPS, C_CHUNK, C_CHUNK)), const((C_CHUNK, C_GROUPS)), const((d, d))],
        out_specs=full,
        out_shape=jax.ShapeDtypeStruct((b, s, d), F32),
        compiler_params=pltpu.CompilerParams(
            dimension_semantics=("parallel", "parallel"), vmem_limit_bytes=VMEM_LIMIT),
        name="tail",
    )(x, ya, yb, mod0, mod1, ln_g1.reshape(1, d), w_out0.astype(BF16), w_in1.astype(BF16),
      c_ln_g.reshape(1, d), c_ln_b.reshape(1, d), c_ws, c_bs.T, w_out1.astype(BF16))


def kernel(x, c, ln_g, ada_w, ada_b, w_in, w_out, a_q_g, a_k_g, b_conv_w, b_conv_b, b_wq, b_wk, b_wv,
           b_w_gates, b_b_gates, b_out_g, b_skip, c_ln_g, c_ln_b, c_ws, c_bs):
    mods = _ada_mods(c, ada_w, ada_b)
    qT, vT, gT, k, bx, sbz = _inproj0(x, mods[0], ln_g[0], w_in[0], a_q_g[0], a_k_g[0])
    ya = _moba(qT, k, vT, gT)
    yb = _mlstm(bx, sbz, b_conv_w[0], b_conv_b[0], b_wq[0], b_wk[0], b_wv[0], b_w_gates[0],
                b_b_gates[0], b_out_g[0], b_skip[0])
    return _tail(x, ya, yb, mods[0], mods[1], ln_g[1], w_out[0], w_in[1], w_out[1],
                 c_ln_g[0], c_ln_b[0], c_ws[0], c_bs[0])
```

```python
import functools

import jax
import jax.numpy as jnp
from jax import lax
from jax.experimental import pallas as pl
from jax.experimental.pallas import tpu as pltpu

F32 = jnp.float32
BF16 = jnp.bfloat16

D_MODEL = 1024
A_HEADS = 8
A_HEAD_DIM = 64
A_WIDTH = A_HEADS * A_HEAD_DIM
MOBA_BLOCK = 256
MOBA_TOPK = 3
B_HEADS = 4
B_HEAD_DIM = 128
B_WIDTH = B_HEADS * B_HEAD_DIM
B_CONV = 4
C_GROUPS = 8
C_CHUNK = 128
C_WIDTH = D_MODEL
NEG = -1e30
LOG2E = 1.4426950408889634

MLSTM_L = 256
ROW_TILE = 256
TAIL_ROW_TILE = 512
CONV_HALO = 8
PV_ONES_ROWS = 16
VMEM_LIMIT = 48 * 1024 * 1024


def _silu(x):
    return x * jax.nn.sigmoid(x)


def _gelu_tanh(x):
    return 0.5 * x * (1.0 + jnp.tanh(0.7978845608028654 * (x + 0.044715 * (x * x * x))))


def _log_sigmoid(x):
    return jnp.minimum(x, 0.0) - jnp.log(1.0 + jnp.exp(-jnp.abs(x)))


def _split_bf16(x):
    hi = x.astype(BF16)
    lo = (x - hi.astype(F32)).astype(BF16)
    return hi, lo


def _dot(a, b):
    return jnp.dot(a, b, preferred_element_type=F32)


def _dot_nt(a, b):
    return lax.dot_general(a, b, (((1,), (1,)), ((), ())), preferred_element_type=F32)


def _adaln_rmsnorm(x, ln_g, scale, shift):
    y = x * lax.rsqrt(jnp.mean(x * x, axis=-1, keepdims=True) + 1e-6)
    return (y * ln_g) * (1.0 + scale) + shift


def _ada_kernel(c_ref, w_ref, b_ref, o_ref):
    c = c_ref[...]
    cs = _silu(c)
    o_ref[0] = jnp.dot(cs, w_ref[0], preferred_element_type=F32,
                       precision=lax.Precision.HIGHEST) + b_ref[0]


def _ada_mods(c, ada_w, ada_b):
    depth, d, d3 = ada_w.shape
    b = c.shape[0]
    bp = 8
    cp = jnp.zeros((bp, d), F32).at[:b].set(c)
    nt = d3 // d
    out = pl.pallas_call(
        _ada_kernel,
        grid=(depth, nt),
        in_specs=[
            pl.BlockSpec((bp, d), lambda l, n: (0, 0)),
            pl.BlockSpec((1, d, d), lambda l, n: (l, 0, n)),
            pl.BlockSpec((1, 1, d), lambda l, n: (l, 0, n)),
        ],
        out_specs=pl.BlockSpec((1, bp, d), lambda l, n: (l, 0, n)),
        out_shape=jax.ShapeDtypeStruct((depth, bp, d3), F32),
        name="ada_mods",
    )(cp, ada_w, ada_b.reshape(depth, 1, d3))
    return out[:, :b].reshape(depth, b, 3, d)


def _inproj0_kernel(x_ref, mod_ref, lng_ref, wt_ref, w_ref, qg_ref, kg_ref,
                    qT_ref, vT_ref, gT_ref, k_ref, bx_ref, sbz_ref):
    x = x_ref[0]
    mod = mod_ref[0]
    h = _adaln_rmsnorm(x, lng_ref[...], mod[1:2], mod[0:1]).astype(BF16)
    pt = _dot_nt(wt_ref[...], h)
    aw = A_WIDTH
    qg = qg_ref[...] * (A_HEAD_DIM ** -0.5 * LOG2E)
    kg = kg_ref[...]
    kn = []
    for hd in range(A_HEADS):
        lo = hd * A_HEAD_DIM
        q = pt[lo:lo + A_HEAD_DIM]
        r = lax.rsqrt(jnp.mean(q * q, axis=0, keepdims=True) + 1e-6)
        qT_ref[0, 0, lo:lo + A_HEAD_DIM, :] = ((q * r) * qg).astype(BF16)
        k = pt[aw + lo:aw + lo + A_HEAD_DIM]
        r = lax.rsqrt(jnp.mean(k * k, axis=0, keepdims=True) + 1e-6)
        kn.append((k * r) * kg)
    k_ref[0] = jnp.concatenate(kn, axis=0).T.astype(BF16)
    vT_ref[0, 0] = pt[2 * aw:3 * aw].astype(BF16)
    gT_ref[0, 0] = _silu(pt[3 * aw:]).astype(BF16)
    p = _dot(h, w_ref[...])
    bx_ref[0] = p[:, :B_WIDTH].astype(BF16)
    sbz_ref[0] = _silu(p[:, B_WIDTH:]).astype(BF16)


def _inproj0(x, mod, ln_g, w_in, q_g, k_g):
    b, s, d = x.shape
    tm = ROW_TILE
    nt = s // tm
    aw, bw = A_WIDTH, B_WIDTH
    wq, wk, wv, wz = (w_in[:, i * aw:(i + 1) * aw] for i in range(4))
    wbx = w_in[:, 4 * aw:4 * aw + bw]
    wbz = w_in[:, 4 * aw + bw:]
    wt = jnp.concatenate([wq, wk, wv, wz], axis=1).T.astype(BF16)
    wn = jnp.concatenate([wbx, wbz], axis=1).astype(BF16)
    t_shape = jax.ShapeDtypeStruct((b, nt, aw, tm), BF16)
    n_shape = jax.ShapeDtypeStruct((b, s, aw), BF16)
    t_spec = pl.BlockSpec((1, 1, aw, tm), lambda i, j: (i, j, 0, 0))
    n_spec = pl.BlockSpec((1, tm, aw), lambda i, j: (i, j, 0))
    const = lambda shape: pl.BlockSpec(shape, lambda i, j: (0,) * len(shape))
    return pl.pallas_call(
        _inproj0_kernel,
        grid=(b, nt),
        in_specs=[
            pl.BlockSpec((1, tm, d), lambda i, j: (i, j, 0)),
            pl.BlockSpec((1, 3, d), lambda i, j: (i, 0, 0)),
            const((1, d)),
            const((4 * aw, d)),
            const((d, 2 * bw)),
            const((A_HEAD_DIM, 1)),
            const((A_HEAD_DIM, 1)),
        ],
        out_specs=[t_spec, t_spec, t_spec, n_spec, n_spec, n_spec],
        out_shape=[t_shape, t_shape, t_shape, n_shape, n_shape, n_shape],
        compiler_params=pltpu.CompilerParams(
            dimension_semantics=("parallel", "parallel"), vmem_limit_bytes=VMEM_LIMIT),
        name="inproj0",
    )(x, mod, ln_g.reshape(1, d), wt, wn, q_g.reshape(A_HEAD_DIM, 1), k_g.reshape(A_HEAD_DIM, 1))


def _moba_kernel(qT_ref, k_ref, vT_ref, gT_ref, o_ref, kmean_ref, sel_ref, qm_ref, raw_ref, p_ref,
                 m_ref, alpha_ref, acc_ref, *, nb):
    blk = MOBA_BLOCK
    dh = A_HEAD_DIM
    pw = 2 * dh
    j = pl.program_id(1)

    @pl.when(j == 0)
    def _():
        for n in range(nb):
            kb = k_ref[0, n * blk:(n + 1) * blk, :].astype(F32)
            kmean_ref[n:n + 1, :] = jnp.mean(kb, axis=0, keepdims=True)

    d_iota = lax.broadcasted_iota(jnp.int32, (pw, blk), 0)
    for hd in range(A_HEADS):
        pr = hd // 2
        q2 = qT_ref[0, 0, pr * pw:(pr + 1) * pw, :]
        keep = (d_iota < dh) if hd % 2 == 0 else (d_iota >= dh)
        qm_ref[hd] = jnp.where(keep, q2, jnp.zeros_like(q2))

    def score_dots(n, slot):
        row0 = pl.multiple_of(n * blk, blk)
        for hd in range(A_HEADS):
            pr = hd // 2
            kt = k_ref[0, pl.ds(row0, blk), pr * pw:(pr + 1) * pw]
            raw_ref[slot, hd] = _dot(kt, qm_ref[hd]).astype(BF16)

    def value_dots(n, slot):
        ones = jnp.ones((PV_ONES_ROWS, blk), BF16)
        return [_dot(jnp.concatenate([vT_ref[0, n, hd * dh:(hd + 1) * dh, :], ones], axis=0),
                     p_ref[slot, hd]) for hd in range(A_HEADS)]

    score_dots(j, 0)
    score_dots(0, 1)

    km_hi, km_lo = _split_bf16(kmean_ref[...])
    n_iota = lax.broadcasted_iota(jnp.int32, (nb, blk), 0)
    past = n_iota < j
    for hd in range(A_HEADS):
        pr = hd // 2
        qm = qm_ref[hd]
        sc = (_dot(km_hi[:, pr * pw:(pr + 1) * pw], qm)
              + _dot(km_lo[:, pr * pw:(pr + 1) * pw], qm))
        sc = jnp.where(past, sc, NEG)
        sel = jnp.zeros((nb, blk), F32)
        for _ in range(MOBA_TOPK):
            mx = jnp.max(sc, axis=0, keepdims=True)
            first = jnp.min(jnp.where(sc == mx, n_iota, nb), axis=0, keepdims=True)
            pick = n_iota == first
            sel = jnp.where(pick, 1.0, sel)
            sc = jnp.where(pick, -jnp.inf, sc)
        sel_ref[hd, 0:nb, :] = jnp.where(past, sel, 0.0)
        sel_ref[hd, nb:nb + 1, :] = jnp.ones((1, blk), F32)

    k_iota = lax.broadcasted_iota(jnp.int32, (blk, blk), 0)
    q_iota = lax.broadcasted_iota(jnp.int32, (blk, blk), 1)
    causal_bias = jnp.where(k_iota <= q_iota, 0.0, NEG).astype(BF16)
    for hd in range(A_HEADS):
        s = raw_ref[0, hd] + causal_bias
        m_new = jnp.max(s, axis=0, keepdims=True)
        p_ref[0, hd] = jnp.exp2(s - m_new)
        m_ref[hd] = m_new.astype(F32)
        acc_ref[hd] = jnp.zeros(acc_ref.shape[1:], F32)

    def step(i, cur):
        nxt = 1 - cur
        n = i - 1
        prev = jnp.where(i == 1, j, i - 2)
        prev_sel = jnp.where(i == 1, nb, i - 2)
        score_dots(jnp.minimum(i, nb - 1), nxt)
        pvs = value_dots(prev, nxt)
        for hd in range(A_HEADS):
            s = raw_ref[cur, hd]
            selb = sel_ref[hd, pl.ds(n, 1), :] > 0.5
            smax = jnp.where(selb, jnp.max(s, axis=0, keepdims=True).astype(F32), NEG)
            m_old = m_ref[hd]
            m_new = jnp.maximum(m_old, smax)
            alpha_ref[hd] = jnp.exp2(m_old - m_new)
            m_ref[hd] = m_new
            p_ref[cur, hd] = jnp.exp2(s - m_new.astype(BF16))
        for hd in range(A_HEADS):
            keep = sel_ref[hd, pl.ds(prev_sel, 1), :] > 0.5
            acc_ref[hd] = alpha_ref[hd] * (acc_ref[hd] + jnp.where(keep, pvs[hd], 0.0))

    def body(ii, carry):
        step(2 * ii + 1, 1)
        step(2 * ii + 2, 0)
        return carry

    trips = (j + 1) // 2
    lax.fori_loop(0, trips, body, 0)

    last = jnp.where(j == 0, j, 2 * trips - 1)
    last_sel = jnp.where(j == 0, nb, 2 * trips - 1)
    pvs = value_dots(last, 0)
    for pr in range(A_HEADS // 2):
        halves = []
        for hd in (2 * pr, 2 * pr + 1):
            keep = sel_ref[hd, pl.ds(last_sel, 1), :] > 0.5
            acc = acc_ref[hd] + jnp.where(keep, pvs[hd], 0.0)
            halves.append(acc[0:dh] / acc[dh:dh + 1])
        oT = jnp.concatenate(halves, axis=0)
        oT = oT * gT_ref[0, 0, pr * pw:(pr + 1) * pw, :].astype(F32)
        o_ref[0, :, pr * pw:(pr + 1) * pw] = oT.T.astype(BF16)


def _moba(qT, k, vT, gT):
    b, nb, aw, blk = qT.shape
    s = nb * blk
    pw = 2 * A_HEAD_DIM
    q_spec = pl.BlockSpec((1, 1, aw, blk), lambda i, j: (i, j, 0, 0))
    row = lambda: pltpu.VMEM((A_HEADS, 1, blk), F32)
    return pl.pallas_call(
        functools.partial(_moba_kernel, nb=nb),
        grid=(b, nb),
        in_specs=[
            q_spec,
            pl.BlockSpec((1, s, aw), lambda i, j: (i, 0, 0)),
            pl.BlockSpec((1, nb, aw, blk), lambda i, j: (i, 0, 0, 0)),
            q_spec,
        ],
        out_specs=pl.BlockSpec((1, blk, aw), lambda i, j: (i, j, 0)),
        out_shape=jax.ShapeDtypeStruct((b, s, aw), BF16),
        scratch_shapes=[
            pltpu.VMEM((nb, aw), F32),
            pltpu.VMEM((A_HEADS, nb + 8, blk), F32),
            pltpu.VMEM((A_HEADS, pw, blk), BF16),
            pltpu.VMEM((2, A_HEADS, blk, blk), BF16),
            pltpu.VMEM((2, A_HEADS, blk, blk), BF16),
            row(), row(),
            pltpu.VMEM((A_HEADS, A_HEAD_DIM + PV_ONES_ROWS, blk), F32),
        ],
        compiler_params=pltpu.CompilerParams(
            dimension_semantics=("parallel", "arbitrary"), vmem_limit_bytes=VMEM_LIMIT),
        name="moba",
    )(qT, k, vT, gT)


def _mlstm_kernel(bx_ref, sbz_ref, cw_ref, cb_ref, wq_ref, wk_ref, wkt_ref, wv_ref, wg_ref, bg_ref,
                  og_ref, skip_ref, o_ref, xbuf_ref, state_ref, m_ref):
    L = MLSTM_L
    dh = B_HEAD_DIM
    c = pl.program_id(1)

    @pl.when(c == 0)
    def _():
        xbuf_ref[0:CONV_HALO, :] = jnp.zeros((CONV_HALO, B_WIDTH), F32)
        state_ref[...] = jnp.zeros_like(state_ref)
        m_ref[...] = jnp.zeros_like(m_ref)

    bx_b = bx_ref[0]
    xbuf_ref[CONV_HALO:CONV_HALO + L, :] = bx_b.astype(F32)
    cw = cw_ref[...]
    conv = cb_ref[...]
    for i in range(B_CONV):
        off = CONV_HALO - (B_CONV - 1) + i
        conv = conv + cw[i:i + 1, :] * xbuf_ref[off:off + L, :]
    tail = xbuf_ref[L:L + CONV_HALO, :]
    xbuf_ref[0:CONV_HALO, :] = tail
    xc = _silu(conv)
    xc_b = xc.astype(BF16)

    bq = _dot(xc_b, wq_ref[...])
    bk = _dot(xc_b, wk_ref[...])
    bkT = _dot_nt(wkt_ref[...], xc_b)
    bv = _dot(bx_b, wv_ref[...])
    bq_b = bq.astype(BF16)
    wg = wg_ref[...]
    gates = (_dot(bq_b, wg[:B_WIDTH]) + _dot(bk.astype(BF16), wg[B_WIDTH:2 * B_WIDTH])
             + _dot(bv.astype(BF16), wg[2 * B_WIDTH:]) + bg_ref[...])
    ig = gates[:, :128]
    lf = _log_sigmoid(gates[:, 128:])
    t_iota = lax.broadcasted_iota(jnp.int32, (L, L), 0)
    s_iota = lax.broadcasted_iota(jnp.int32, (L, L), 1)
    tril = s_iota <= t_iota
    tri_b = jnp.where(tril, 1.0, 0.0).astype(BF16)
    lf_hi, lf_lo = _split_bf16(lf)
    cum = _dot(tri_b, lf_hi) + _dot(tri_b, lf_lo)
    a = ig - cum
    aT = a.T
    ones = jnp.ones((L, dh), F32)
    kscale = dh ** -0.5

    outs = []
    for hd in range(B_HEADS):
        lo = hd * dh
        q_h = bq_b[:, lo:lo + dh]
        kT_h = (bkT[lo:lo + dh, :] * kscale).astype(BF16)
        v_aug = jnp.concatenate([bv[:, lo:lo + dh], ones], axis=1)
        a_row = aT[hd:hd + 1, :]
        a_col = a[:, hd:hd + 1]
        cum_col = cum[:, hd:hd + 1]
        m_prev = m_ref[hd][0:1, 0:1]
        amax = jnp.max(jnp.where(tril, a_row, NEG), axis=1, keepdims=True)
        b_col = jnp.maximum(m_prev, amax)
        dmat = jnp.exp(jnp.where(tril, a_row - b_col, NEG))
        w_intra = (_dot(q_h, kT_h) * dmat).astype(BF16)
        intra = _dot(w_intra, v_aug.astype(BF16))
        state = state_ref[hd]
        inter = _dot(q_h, state.astype(BF16))
        w_inter = jnp.exp(m_prev - b_col)
        num = w_inter * inter[:, :dh] + intra[:, :dh]
        den = w_inter * inter[:, dh:dh + 1] + intra[:, dh:dh + 1]
        hc = num / jnp.maximum(jnp.abs(den), jnp.exp(-(cum_col + b_col)))
        mu = jnp.mean(hc, axis=-1, keepdims=True)
        hcc = hc - mu
        var = jnp.mean(hcc * hcc, axis=-1, keepdims=True)
        outs.append(hcc * lax.rsqrt(var + 1e-5))
        b_end = b_col[L - 1:L, :]
        ws = jnp.exp(a_col - b_end)
        decay = jnp.exp(m_prev - b_end)
        state_ref[hd] = decay * state + _dot(kT_h, (ws * v_aug).astype(BF16))
        m_ref[hd] = jnp.broadcast_to(cum_col[L - 1:L, :] + b_end, m_ref.shape[1:])

    hb = jnp.concatenate(outs, axis=1) * og_ref[...]
    yb = (hb + skip_ref[...] * xc) * sbz_ref[0].astype(F32)
    o_ref[0] = yb.astype(BF16)


def _block_diag_dense(w):
    nblk, blk, _ = w.shape
    n = nblk * blk
    idx = jnp.arange(n) // blk
    return jnp.where(idx[:, None] == idx[None, :], jnp.tile(w.reshape(n, blk), (1, nblk)), 0.0)


def _mlstm(bx, sbz, conv_w, conv_b, wq, wk, wv, w_gates, b_gates, out_g, skip):
    b, s, bw = bx.shape
    L = MLSTM_L
    nc = s // L
    wq_d = _block_diag_dense(wq).astype(BF16)
    wk_d = _block_diag_dense(wk)
    wkt_d = wk_d.T.astype(BF16)
    wk_d = wk_d.astype(BF16)
    wv_d = _block_diag_dense(wv).astype(BF16)
    wg = jnp.zeros((3 * bw, 256), F32)
    wg = wg.at[:, :B_HEADS].set(w_gates[:, :B_HEADS]).at[:, 128:128 + B_HEADS].set(w_gates[:, B_HEADS:])
    bg = jnp.zeros((1, 256), F32)
    bg = bg.at[0, :B_HEADS].set(b_gates[:B_HEADS]).at[0, 128:128 + B_HEADS].set(b_gates[B_HEADS:])
    const = lambda shape: pl.BlockSpec(shape, lambda i, j: (0,) * len(shape))
    tok = pl.BlockSpec((1, L, bw), lambda i, j: (i, j, 0))
    return pl.pallas_call(
        _mlstm_kernel,
        grid=(b, nc),
        in_specs=[tok, tok, const((B_CONV, bw)), const((1, bw)), const((bw, bw)), const((bw, bw)),
                  const((bw, bw)), const((bw, bw)), const((3 * bw, 256)), const((1, 256)),
                  const((1, bw)), const((1, bw))],
        out_specs=tok,
        out_shape=jax.ShapeDtypeStruct((b, s, bw), BF16),
        scratch_shapes=[pltpu.VMEM((L + CONV_HALO, bw), F32),
                        pltpu.VMEM((B_HEADS, B_HEAD_DIM, 2 * B_HEAD_DIM), F32),
                        pltpu.VMEM((B_HEADS, 8, 128), F32)],
        compiler_params=pltpu.CompilerParams(
            dimension_semantics=("parallel", "arbitrary"), vmem_limit_bytes=VMEM_LIMIT),
        name="mlstm",
    )(bx, sbz, conv_w, conv_b.reshape(1, bw), wq_d, wk_d, wkt_d, wv_d, wg.astype(BF16), bg,
      out_g.reshape(1, bw), skip.reshape(1, bw))


def _tail_kernel(x_ref, ya_ref, yb_ref, mod0_ref, mod1_ref, lng_ref, wo0_ref, wi1_ref, clg_ref,
                 clb_ref, ws_ref, bst_ref, wo1_ref, o_ref):
    tm = x_ref.shape[1]
    aw = A_WIDTH
    wo0 = wo0_ref
    y0 = _dot(ya_ref[0], wo0[0:aw, :]) + _dot(yb_ref[0], wo0[aw:, :])
    x1 = x_ref[0] + mod0_ref[0][2:3] * y0
    mod1 = mod1_ref[0]
    h = _adaln_rmsnorm(x1, lng_ref[...], mod1[1:2], mod1[0:1]).astype(BF16)
    p = _dot(h, wi1_ref[...])
    u = _gelu_tanh(p[:, :C_WIDTH])
    v = _gelu_tanh(p[:, C_WIDTH:2 * C_WIDTH])
    mu = jnp.mean(v, axis=-1, keepdims=True)
    vc = v - mu
    var = jnp.mean(vc * vc, axis=-1, keepdims=True)
    vn = ((vc * lax.rsqrt(var + 1e-5)) * clg_ref[...] + clb_ref[...]).astype(BF16)
    gate = u * _silu(p[:, 2 * C_WIDTH:])
    t_iota = lax.broadcasted_iota(jnp.int32, (C_CHUNK, C_CHUNK), 0)
    s_iota = lax.broadcasted_iota(jnp.int32, (C_CHUNK, C_CHUNK), 1)
    tril = s_iota <= t_iota
    gw = C_WIDTH // C_GROUPS
    bst = bst_ref[...]
    cols = []
    for g in range(C_GROUPS):
        wm = jnp.where(tril, ws_ref[g], 0.0).astype(BF16)
        rows = []
        for ch in range(tm // C_CHUNK):
            vg = vn[ch * C_CHUNK:(ch + 1) * C_CHUNK, g * gw:(g + 1) * gw]
            rows.append(_dot(wm, vg) + bst[:, g:g + 1])
        cols.append(jnp.concatenate(rows, axis=0))
    sg = jnp.concatenate(cols, axis=1)
    y1 = (gate * sg).astype(BF16)
    o_ref[0] = x1 + mod1[2:3] * _dot(y1, wo1_ref[...])


def _tail(x, ya, yb, mod0, mod1, ln_g1, w_out0, w_in1, w_out1, c_ln_g, c_ln_b, c_ws, c_bs):
    b, s, d = x.shape
    tm = TAIL_ROW_TILE
    nt = s // tm
    const = lambda shape: pl.BlockSpec(shape, lambda i, j: (0,) * len(shape))
    half =pl.BlockSpec((1, tm, A_WIDTH), lambda i, j: (i, j, 0))
    full = pl.BlockSpec((1, tm, d), lambda i, j: (i, j, 0))
    modspec = pl.BlockSpec((1, 3, d), lambda i, j: (i, 0, 0))
    return pl.pallas_call(
        _tail_kernel,
        grid=(b, nt),
        in_specs=[full, half, half, modspec, modspec, const((1, d)), const((d, d)),
                  const((d, 3 * C_WIDTH)), const((1, d)), const((1, d)),
                  const((C_GROUPS, C_CHUNK, C_CHUNK)), const((C_CHUNK, C_GROUPS)), const((d, d))],
        out_specs=full,
        out_shape=jax.ShapeDtypeStruct((b, s, d), F32),
        compiler_params=pltpu.CompilerParams(
            dimension_semantics=("parallel", "parallel"), vmem_limit_bytes=VMEM_LIMIT),
        name="tail",
    )(x, ya, yb, mod0, mod1, ln_g1.reshape(1, d), w_out0.astype(BF16), w_in1.astype(BF16),
      c_ln_g.reshape(1, d), c_ln_b.reshape(1, d), c_ws, c_bs.T, w_out1.astype(BF16))


def kernel(x, c, ln_g, ada_w, ada_b, w_in, w_out, a_q_g, a_k_g, b_conv_w, b_conv_b, b_wq, b_wk, b_wv,
           b_w_gates, b_b_gates, b_out_g, b_skip, c_ln_g, c_ln_b, c_ws, c_bs):
    mods = _ada_mods(c, ada_w, ada_b)
    qT, vT, gT, k, bx, sbz = _inproj0(x, mods[0], ln_g[0], w_in[0], a_q_g[0], a_k_g[0])
    ya = _moba(qT, k, vT, gT)
    yb = _mlstm(bx, sbz, b_conv_w[0], b_conv_b[0], b_wq[0], b_wk[0], b_wv[0], b_w_gates[0],
                b_b_gates[0], b_out_g[0], b_skip[0])
    return _tail(x, ya, yb, mods[0], mods[1], ln_g[1], w_out[0], w_in[1], w_out[1],
                 c_ln_g[0], c_ln_b[0], c_ws[0], c_bs[0])
```

```python
import functools

import jax
import jax.numpy as jnp
from jax import lax
from jax.experimental import pallas as pl
from jax.experimental.pallas import tpu as pltpu

F32 = jnp.float32
BF16 = jnp.bfloat16

D_MODEL = 1024
A_HEADS = 8
A_HEAD_DIM = 64
A_WIDTH = A_HEADS * A_HEAD_DIM
MOBA_BLOCK = 256
MOBA_TOPK = 3
B_HEADS = 4
B_HEAD_DIM = 128
B_WIDTH = B_HEADS * B_HEAD_DIM
B_CONV = 4
C_GROUPS = 8
C_CHUNK = 128
C_WIDTH = D_MODEL
NEG = -1e30
LOG2E = 1.4426950408889634

MLSTM_L = 256
MLSTM_NB = 2
ROW_TILE = 256
TAIL_ROW_TILE = 512
CONV_HALO = 8
GATE_ROWS = 2 * B_HEADS
PV_ONES_ROWS = 16
VMEM_LIMIT = 48 * 1024 * 1024


def _silu(x):
    return x * jax.nn.sigmoid(x)


def _gelu_tanh(x):
    return 0.5 * x * (1.0 + jnp.tanh(0.7978845608028654 * (x + 0.044715 * (x * x * x))))


def _log_sigmoid(x):
    return jnp.minimum(x, 0.0) - jnp.log(1.0 + jnp.exp(-jnp.abs(x)))


def _split_bf16(x):
    hi = x.astype(BF16)
    lo = (x - hi.astype(F32)).astype(BF16)
    return hi, lo


def _dot(a, b):
    return jnp.dot(a, b, preferred_element_type=F32)


def _dot_nt(a, b):
    return lax.dot_general(a, b, (((1,), (1,)), ((), ())), preferred_element_type=F32)


def _adaln_rmsnorm(x, ln_g, scale, shift):
    y = x * lax.rsqrt(jnp.mean(x * x, axis=-1, keepdims=True) + 1e-6)
    return (y * ln_g) * (1.0 + scale) + shift


def _ada_kernel(c_ref, w_ref, b_ref, o_ref):
    c = c_ref[...]
    cs = _silu(c)
    o_ref[0] = jnp.dot(cs, w_ref[0], preferred_element_type=F32,
                       precision=lax.Precision.HIGHEST) + b_ref[0]


def _ada_mods(c, ada_w, ada_b):
    depth, d, d3 = ada_w.shape
    b = c.shape[0]
    bp = 8
    cp = jnp.zeros((bp, d), F32).at[:b].set(c)
    nt = d3 // d
    out = pl.pallas_call(
        _ada_kernel,
        grid=(depth, nt),
        in_specs=[
            pl.BlockSpec((bp, d), lambda l, n: (0, 0)),
            pl.BlockSpec((1, d, d), lambda l, n: (l, 0, n)),
            pl.BlockSpec((1, 1, d), lambda l, n: (l, 0, n)),
        ],
        out_specs=pl.BlockSpec((1, bp, d), lambda l, n: (l, 0, n)),
        out_shape=jax.ShapeDtypeStruct((depth, bp, d3), F32),
        name="ada_mods",
    )(cp, ada_w, ada_b.reshape(depth, 1, d3))
    return out[:, :b].reshape(depth, b, 3, d)


def _inproj0_kernel(x_ref, mod_ref, lng_ref, wt_ref, w_ref, qg_ref, kg_ref,
                    qT_ref, vT_ref, gT_ref, k_ref, bx_ref, sbz_ref):
    x = x_ref[0]
    mod = mod_ref[0]
    h = _adaln_rmsnorm(x, lng_ref[...], mod[1:2], mod[0:1]).astype(BF16)
    pt = _dot_nt(wt_ref[...], h)
    aw = A_WIDTH
    qg = qg_ref[...] * (A_HEAD_DIM ** -0.5 * LOG2E)
    kg = kg_ref[...]
    kn = []
    for hd in range(A_HEADS):
        lo = hd * A_HEAD_DIM
        q = pt[lo:lo + A_HEAD_DIM]
        r = lax.rsqrt(jnp.mean(q * q, axis=0, keepdims=True) + 1e-6)
        qT_ref[0, 0, lo:lo + A_HEAD_DIM, :] = ((q * r) * qg).astype(BF16)
        k = pt[aw + lo:aw + lo + A_HEAD_DIM]
        r = lax.rsqrt(jnp.mean(k * k, axis=0, keepdims=True) + 1e-6)
        kn.append((k * r) * kg)
    k_ref[0] = jnp.concatenate(kn, axis=0).T.astype(BF16)
    vT_ref[0, 0] = pt[2 * aw:3 * aw].astype(BF16)
    gT_ref[0, 0] = _silu(pt[3 * aw:]).astype(BF16)
    p = _dot(h, w_ref[...])
    bx_ref[0] = p[:, :B_WIDTH].astype(BF16)
    sbz_ref[0] = _silu(p[:, B_WIDTH:]).astype(BF16)


def _inproj0(x, mod, ln_g, w_in, q_g, k_g):
    b, s, d = x.shape
    tm = ROW_TILE
    nt = s // tm
    aw, bw = A_WIDTH, B_WIDTH
    wq, wk, wv, wz = (w_in[:, i * aw:(i + 1) * aw] for i in range(4))
    wbx = w_in[:, 4 * aw:4 * aw + bw]
    wbz = w_in[:, 4 * aw + bw:]
    wt = jnp.concatenate([wq, wk, wv, wz], axis=1).T.astype(BF16)
    wn = jnp.concatenate([wbx, wbz], axis=1).astype(BF16)
    t_shape = jax.ShapeDtypeStruct((b, nt, aw, tm), BF16)
    n_shape = jax.ShapeDtypeStruct((b, s, aw), BF16)
    t_spec = pl.BlockSpec((1, 1, aw, tm), lambda i, j: (i, j, 0, 0))
    n_spec = pl.BlockSpec((1, tm, aw), lambda i, j: (i, j, 0))
    const = lambda shape: pl.BlockSpec(shape, lambda i, j: (0,) * len(shape))
    return pl.pallas_call(
        _inproj0_kernel,
        grid=(b, nt),
        in_specs=[
            pl.BlockSpec((1, tm, d), lambda i, j: (i, j, 0)),
            pl.BlockSpec((1, 3, d), lambda i, j: (i, 0, 0)),
            const((1, d)),
            const((4 * aw, d)),
            const((d, 2 * bw)),
            const((A_HEAD_DIM, 1)),
            const((A_HEAD_DIM, 1)),
        ],
        out_specs=[t_spec, t_spec, t_spec, n_spec, n_spec, n_spec],
        out_shape=[t_shape, t_shape, t_shape, n_shape, n_shape, n_shape],
        compiler_params=pltpu.CompilerParams(
            dimension_semantics=("parallel", "parallel"), vmem_limit_bytes=VMEM_LIMIT),
        name="inproj0",
    )(x, mod, ln_g.reshape(1, d), wt, wn, q_g.reshape(A_HEAD_DIM, 1), k_g.reshape(A_HEAD_DIM, 1))


def _moba_kernel(qT_ref, k_ref, vT_ref, gT_ref, o_ref, kmean_ref, sel_ref, qm_ref, raw_ref, p_ref,
                 m_ref, alpha_ref, acc_ref, *, nb):
    blk = MOBA_BLOCK
    dh = A_HEAD_DIM
    pw = 2 * dh
    j = pl.program_id(1)

    @pl.when(j == 0)
    def _():
        for n in range(nb):
            kb = k_ref[0, n * blk:(n + 1) * blk, :].astype(F32)
            kmean_ref[n:n + 1, :] = jnp.mean(kb, axis=0, keepdims=True)

    d_iota = lax.broadcasted_iota(jnp.int32, (pw, blk), 0)
    for hd in range(A_HEADS):
        pr = hd // 2
        q2 = qT_ref[0, 0, pr * pw:(pr + 1) * pw, :]
        keep = (d_iota < dh) if hd % 2 == 0 else (d_iota >= dh)
        qm_ref[hd] = jnp.where(keep, q2, jnp.zeros_like(q2))

    def score_dots(n, slot):
        row0 = pl.multiple_of(n * blk, blk)
        for hd in range(A_HEADS):
            pr = hd // 2
            kt = k_ref[0, pl.ds(row0, blk), pr * pw:(pr + 1) * pw]
            raw_ref[slot, hd] = _dot(kt, qm_ref[hd]).astype(BF16)

    def value_dots(n, slot):
        ones = jnp.ones((PV_ONES_ROWS, blk), BF16)
        return [_dot(jnp.concatenate([vT_ref[0, n, hd * dh:(hd + 1) * dh, :], ones], axis=0),
                     p_ref[slot, hd]) for hd in range(A_HEADS)]

    score_dots(j, 0)
    score_dots(0, 1)

    km_hi, km_lo = _split_bf16(kmean_ref[...])
    n_iota = lax.broadcasted_iota(jnp.int32, (nb, blk), 0)
    past = n_iota < j
    for hd in range(A_HEADS):
        pr = hd // 2
        qm = qm_ref[hd]
        sc = (_dot(km_hi[:, pr * pw:(pr + 1) * pw], qm)
              + _dot(km_lo[:, pr * pw:(pr + 1) * pw], qm))
        sc = jnp.where(past, sc, NEG)
        sel = jnp.zeros((nb, blk), F32)
        for _ in range(MOBA_TOPK):
            mx = jnp.max(sc, axis=0, keepdims=True)
            first = jnp.min(jnp.where(sc == mx, n_iota, nb), axis=0, keepdims=True)
            pick = n_iota == first
            sel = jnp.where(pick, 1.0, sel)
            sc = jnp.where(pick, -jnp.inf, sc)
        sel_ref[hd, 0:nb, :] = jnp.where(past, sel, 0.0)
        sel_ref[hd, nb:nb + 1, :] = jnp.ones((1, blk), F32)

    k_iota = lax.broadcasted_iota(jnp.int32, (blk, blk), 0)
    q_iota = lax.broadcasted_iota(jnp.int32, (blk, blk), 1)
    causal_bias = jnp.where(k_iota <= q_iota, 0.0, NEG).astype(BF16)
    for hd in range(A_HEADS):
        s = raw_ref[0, hd] + causal_bias
        m_new = jnp.max(s, axis=0, keepdims=True)
        p_ref[0, hd] = jnp.exp2(s - m_new)
        m_ref[hd] = m_new.astype(F32)
        acc_ref[hd] = jnp.zeros(acc_ref.shape[1:], F32)

    def step(i, cur):
        nxt = 1 - cur
        n = i - 1
        prev = jnp.where(i == 1, j, i - 2)
        prev_sel = jnp.where(i == 1, nb, i - 2)
        score_dots(jnp.minimum(i, nb - 1), nxt)
        pvs = value_dots(prev, nxt)
        for hd in range(A_HEADS):
            s = raw_ref[cur, hd]
            selb = sel_ref[hd, pl.ds(n, 1), :] > 0.5
            smax = jnp.where(selb, jnp.max(s, axis=0, keepdims=True).astype(F32), NEG)
            m_old = m_ref[hd]
            m_new = jnp.maximum(m_old, smax)
            alpha_ref[hd] = jnp.exp2(m_old - m_new)
            m_ref[hd] = m_new
            p_ref[cur, hd] = jnp.exp2(s - m_new.astype(BF16))
        for hd in range(A_HEADS):
            keep = sel_ref[hd, pl.ds(prev_sel, 1), :] > 0.5
            acc_ref[hd] = alpha_ref[hd] * (acc_ref[hd] + jnp.where(keep, pvs[hd], 0.0))

    def body(ii, carry):
        step(2 * ii + 1, 1)
        step(2 * ii + 2, 0)
        return carry

    trips = (j + 1) // 2
    lax.fori_loop(0, trips, body, 0)

    last = jnp.where(j == 0, j, 2 * trips - 1)
    last_sel = jnp.where(j == 0, nb, 2 * trips - 1)
    pvs = value_dots(last, 0)
    for pr in range(A_HEADS // 2):
        halves = []
        for hd in (2 * pr, 2 * pr + 1):
            keep = sel_ref[hd, pl.ds(last_sel, 1), :] > 0.5
            acc = acc_ref[hd] + jnp.where(keep, pvs[hd], 0.0)
            halves.append(acc[0:dh] / acc[dh:dh + 1])
        oT = jnp.concatenate(halves, axis=0)
        oT = oT * gT_ref[0, 0, pr * pw:(pr + 1) * pw, :].astype(F32)
        o_ref[0, :, pr * pw:(pr + 1) * pw] = oT.T.astype(BF16)


def _moba(qT, k, vT, gT):
    b, nb, aw, blk = qT.shape
    s = nb * blk
    pw = 2 * A_HEAD_DIM
    q_spec = pl.BlockSpec((1, 1, aw, blk), lambda i, j: (i, j, 0, 0))
    row = lambda: pltpu.VMEM((A_HEADS, 1, blk), F32)
    return pl.pallas_call(
        functools.partial(_moba_kernel, nb=nb),
        grid=(b, nb),
        in_specs=[
            q_spec,
            pl.BlockSpec((1, s, aw), lambda i, j: (i, 0, 0)),
            pl.BlockSpec((1, nb, aw, blk), lambda i, j: (i, 0, 0, 0)),
            q_spec,
        ],
        out_specs=pl.BlockSpec((1, blk, aw), lambda i, j: (i, j, 0)),
        out_shape=jax.ShapeDtypeStruct((b, s, aw), BF16),
        scratch_shapes=[
            pltpu.VMEM((nb, aw), F32),
            pltpu.VMEM((A_HEADS, nb + 8, blk), F32),
            pltpu.VMEM((A_HEADS, pw, blk), BF16),
            pltpu.VMEM((2, A_HEADS, blk, blk), BF16),
            pltpu.VMEM((2, A_HEADS, blk, blk), BF16),
            row(), row(),
            pltpu.VMEM((A_HEADS, A_HEAD_DIM + PV_ONES_ROWS, blk), F32),
        ],
        compiler_params=pltpu.CompilerParams(
            dimension_semantics=("parallel", "arbitrary"), vmem_limit_bytes=VMEM_LIMIT),
        name="moba",
    )(qT, k, vT, gT)


def _mlstm_kernel(bx_ref, sbz_ref, cw_ref, cb_ref, wqt_ref, wk_ref, wvt_ref, wgq_ref, wgk_ref,
                  wgv_ref, bg_ref, og_ref, skip_ref, o_ref, xbuf_ref, state_ref, m_ref):
    L = MLSTM_L
    dh = B_HEAD_DIM
    c = pl.program_id(1)

    @pl.when(c == 0)
    def _():
        xbuf_ref[:, 0:CONV_HALO, :] = jnp.zeros((MLSTM_NB, CONV_HALO, B_WIDTH), F32)
        state_ref[...] = jnp.zeros_like(state_ref)
        m_ref[...] = jnp.zeros_like(m_ref)

    cw = cw_ref[...]
    cb = cb_ref[...]
    kscale = dh ** -0.5
    s_iota = lax.broadcasted_iota(jnp.int32, (L, L), 0)
    t_iota = lax.broadcasted_iota(jnp.int32, (L, L), 1)
    tri = s_iota <= t_iota
    tri_b = jnp.where(tri, 1.0, 0.0).astype(BF16)
    ones = jnp.ones((dh, L), F32)

    def project(bi):
        bx_b = bx_ref[bi]
        xbuf_ref[bi, CONV_HALO:CONV_HALO + L, :] = bx_b.astype(F32)
        st = dict(xc=[], ks=[], vT=[], scores=[], inter=[], state=[])
        gT = bg_ref[...]
        for hd in range(B_HEADS):
            lo = hd * dh
            conv = cb[:, lo:lo + dh]
            for i in range(B_CONV):
                off = CONV_HALO - (B_CONV - 1) + i
                conv = conv + cw[i:i + 1, lo:lo + dh] * xbuf_ref[bi, off:off + L, lo:lo + dh]
            xc = _silu(conv)
            xc_b = xc.astype(BF16)
            qT_h = _dot_nt(wqt_ref[hd], xc_b).astype(BF16)
            vT_h = _dot_nt(wvt_ref[hd], bx_b[:, lo:lo + dh])
            k_h = _dot(xc_b, wk_ref[hd])
            ks_h = (k_h * kscale).astype(BF16)
            state = state_ref[bi, hd]
            st["scores"].append(_dot(ks_h, qT_h))
            st["inter"].append(_dot(state.astype(BF16), qT_h))
            gT = gT + (_dot(wgq_ref[:, lo:lo + dh], qT_h)
                       + _dot_nt(wgk_ref[:, lo:lo + dh], k_h.astype(BF16))
                       + _dot(wgv_ref[:, lo:lo + dh], vT_h.astype(BF16)))
            st["xc"].append(xc); st["ks"].append(ks_h); st["vT"].append(vT_h); st["state"].append(state)
        tail = xbuf_ref[bi, L:L + CONV_HALO, :]
        xbuf_ref[bi, 0:CONV_HALO, :] = tail
        st["gT"] = gT
        return st

    def gate_chain(st):
        gT = st["gT"]
        lf_hi, lf_lo = _split_bf16(_log_sigmoid(gT))
        cum = (_dot(lf_hi, tri_b) + _dot(lf_lo, tri_b))[B_HEADS:2 * B_HEADS]
        a = gT[0:B_HEADS] - cum
        st["cum"], st["a"] = cum, a
        st["a_cols"] = jnp.concatenate([a, jnp.zeros((128 - B_HEADS, L), F32)], axis=0).T

    def recur(bi, st):
        a, cum, a_cols = st["a"], st["cum"], st["a_cols"]
        for hd in range(B_HEADS):
            lo = hd * dh
            ks_h, state, inter = st["ks"][hd], st["state"][hd], st["inter"][hd]
            vT_aug = jnp.concatenate([st["vT"][hd], ones], axis=0)
            a_row = a[hd:hd + 1]
            a_col = a_cols[:, hd:hd + 1]
            cum_row = cum[hd:hd + 1]
            m_prev = m_ref[bi, hd][0:1, 0:1]
            amax = jnp.max(jnp.where(tri, a_col, NEG), axis=0, keepdims=True)
            b_row = jnp.maximum(m_prev, amax)
            dmat = jnp.exp(jnp.where(tri, a_col - b_row, NEG))
            w_intra = (st["scores"][hd] * dmat).astype(BF16)
            intra = _dot(vT_aug.astype(BF16), w_intra)
            w_inter = jnp.exp(m_prev - b_row)
            num = w_inter * inter[0:dh] + intra[0:dh]
            den = w_inter * inter[dh:dh + 1] + intra[dh:dh + 1]
            hc = num * (1.0 / jnp.maximum(jnp.abs(den), jnp.exp(-(cum_row + b_row))))
            mu = jnp.mean(hc, axis=0, keepdims=True)
            hcc = hc - mu
            var = jnp.mean(hcc * hcc, axis=0, keepdims=True)
            hb = (hcc * lax.rsqrt(var + 1e-5)).T * og_ref[:, lo:lo + dh]
            yb = (hb + skip_ref[:, lo:lo + dh] * st["xc"][hd]) * sbz_ref[bi, :, lo:lo + dh].astype(F32)
            o_ref[bi, :, lo:lo + dh] = yb.astype(BF16)
            b_end = b_row[:, L - 1:L]
            ws = jnp.exp(a_row - b_end)
            decay = jnp.exp(m_prev - b_end)
            state_ref[bi, hd] = decay * state + _dot((vT_aug * ws).astype(BF16), ks_h)
            m_ref[bi, hd] = jnp.broadcast_to(cum_row[:, L - 1:L] + b_end, m_ref.shape[2:])

    sts = []
    for bi in range(MLSTM_NB):
        sts.append(project(bi))
        gate_chain(sts[bi])
    for bi in range(MLSTM_NB):
        recur(bi, sts[bi])


def _block_diag_dense(w):
    nblk, blk, _ = w.shape
    n = nblk * blk
    idx = jnp.arange(n) // blk
    return jnp.where(idx[:, None] == idx[None, :], jnp.tile(w.reshape(n, blk), (1, nblk)), 0.0)


def _mlstm(bx, sbz, conv_w, conv_b, wq, wk, wv, w_gates, b_gates, out_g, skip):
    b, s, bw = bx.shape
    L = MLSTM_L
    nc = s // L
    per_head = wq.shape[0] // B_HEADS

    def head_blocks(w, transpose):
        blocks = [_block_diag_dense(w[h * per_head:(h + 1) * per_head]) for h in range(B_HEADS)]
        return jnp.stack([blk.T if transpose else blk for blk in blocks]).astype(BF16)

    wqt_d = head_blocks(wq, True)
    wk_d = head_blocks(wk, False)
    wvt_d = head_blocks(wv, True)
    hblk = (B_HEADS, B_HEAD_DIM, B_HEAD_DIM)
    wgq = w_gates[:bw].T.astype(BF16)
    wgv = w_gates[2 * bw:].T.astype(BF16)
    wgk = w_gates[bw:2 * bw].T.astype(BF16)
    const = lambda shape: pl.BlockSpec(shape, lambda i, j: (0,) * len(shape))
    nbb = MLSTM_NB
    tok = pl.BlockSpec((nbb, L, bw), lambda i, j: (i, j, 0))
    return pl.pallas_call(
        _mlstm_kernel,
        grid=(b // nbb, nc),
        in_specs=[tok, tok, const((B_CONV, bw)), const((1, bw)),
                  const(hblk), const(hblk), const(hblk),
                  const((GATE_ROWS, bw)), const((GATE_ROWS, bw)), const((GATE_ROWS, bw)),
                  const((GATE_ROWS, 1)), const((1, bw)), const((1, bw))],
        out_specs=tok,
        out_shape=jax.ShapeDtypeStruct((b, s, bw), BF16),
        scratch_shapes=[pltpu.VMEM((nbb, L + CONV_HALO, bw), F32),
                        pltpu.VMEM((nbb, B_HEADS, 2 * B_HEAD_DIM, B_HEAD_DIM), F32),
                        pltpu.VMEM((nbb, B_HEADS, 8, 128), F32)],
        compiler_params=pltpu.CompilerParams(
            dimension_semantics=("parallel", "arbitrary"), vmem_limit_bytes=VMEM_LIMIT),
        name="mlstm",
    )(bx, sbz, conv_w, conv_b.reshape(1, bw), wqt_d, wk_d, wvt_d, wgq, wgk, wgv,
      b_gates.reshape(GATE_ROWS, 1), out_g.reshape(1, bw), skip.reshape(1, bw))


def _tail_kernel(x_ref, ya_ref, yb_ref, mod0_ref, mod1_ref, lng_ref, wo0_ref, wi1_ref, clg_ref,
                 clb_ref, ws_ref, bst_ref, wo1_ref, o_ref):
    tm = x_ref.shape[1]
    aw = A_WIDTH
    wo0 = wo0_ref
    y0 = _dot(ya_ref[0], wo0[0:aw, :]) + _dot(yb_ref[0], wo0[aw:, :])
    x1 = x_ref[0] + mod0_ref[0][2:3] * y0
    mod1 = mod1_ref[0]
    h = _adaln_rmsnorm(x1, lng_ref[...], mod1[1:2], mod1[0:1]).astype(BF16)
    p = _dot(h, wi1_ref[...])
    u = _gelu_tanh(p[:, :C_WIDTH])
    v = _gelu_tanh(p[:, C_WIDTH:2 * C_WIDTH])
    mu = jnp.mean(v, axis=-1, keepdims=True)
    vc = v - mu
    var = jnp.mean(vc * vc, axis=-1, keepdims=True)
    vn = ((vc * lax.rsqrt(var + 1e-5)) * clg_ref[...] + clb_ref[...]).astype(BF16)
    gate = u * _silu(p[:, 2 * C_WIDTH:])
    t_iota = lax.broadcasted_iota(jnp.int32, (C_CHUNK, C_CHUNK), 0)
    s_iota = lax.broadcasted_iota(jnp.int32, (C_CHUNK, C_CHUNK), 1)
    tril = s_iota <= t_iota
    gw = C_WIDTH // C_GROUPS
    bst = bst_ref[...]
    cols = []
    for g in range(C_GROUPS):
        wm = jnp.where(tril, ws_ref[g], 0.0).astype(BF16)
        rows = []
        for ch in range(tm // C_CHUNK):
            vg = vn[ch * C_CHUNK:(ch + 1) * C_CHUNK, g * gw:(g + 1) * gw]
            rows.append(_dot(wm, vg) + bst[:, g:g + 1])
        cols.append(jnp.concatenate(rows, axis=0))
    sg = jnp.concatenate(cols, axis=1)
    y1 = (gate * sg).astype(BF16)
    o_ref[0] = x1 + mod1[2:3] * _dot(y1, wo1_ref[...])


def _tail(x, ya, yb, mod0, mod1, ln_g1, w_out0, w_in1, w_out1, c_ln_g, c_ln_b, c_ws, c_bs):
    b, s, d = x.shape
    tm = TAIL_ROW_TILE
    nt = s // tm
    const = lambda shape: pl.BlockSpec(shape, lambda i, j: (0,) * len(shape))
    half =pl.BlockSpec((1, tm, A_WIDTH), lambda i, j: (i, j, 0))
    full = pl.BlockSpec((1, tm, d), lambda i, j: (i, j, 0))
    modspec = pl.BlockSpec((1, 3, d), lambda i, j: (i, 0, 0))
    return pl.pallas_call(
        _tail_kernel,
        grid=(b, nt),
        in_specs=[full, half, half, modspec, modspec, const((1, d)), const((d, d)),
                  const((d, 3 * C_WIDTH)), const((1, d)), const((1, d)),
                  const((C_GROUPS, C_CHUNK, C_CHUNK)), const((C_CHUNK, C_GROUPS)), const((d, d))],
        out_specs=full,
        out_shape=jax.ShapeDtypeStruct((b, s, d), F32),
        compiler_params=pltpu.CompilerParams(
            dimension_semantics=("parallel", "parallel"), vmem_limit_bytes=VMEM_LIMIT),
        name="tail",
    )(x, ya, yb, mod0, mod1, ln_g1.reshape(1, d), w_out0.astype(BF16), w_in1.astype(BF16),
      c_ln_g.reshape(1, d), c_ln_b.reshape(1, d), c_ws, c_bs.T, w_out1.astype(BF16))


def kernel(x, c, ln_g, ada_w, ada_b, w_in, w_out, a_q_g, a_k_g, b_conv_w, b_conv_b, b_wq, b_wk, b_wv,
           b_w_gates, b_b_gates, b_out_g, b_skip, c_ln_g, c_ln_b, c_ws, c_bs):
    mods = _ada_mods(c, ada_w, ada_b)
    qT, vT, gT, k, bx, sbz = _inproj0(x, mods[0], ln_g[0], w_in[0], a_q_g[0], a_k_g[0])
    ya = _moba(qT, k, vT, gT)
    yb = _mlstm(bx, sbz, b_conv_w[0], b_conv_b[0], b_wq[0], b_wk[0], b_wv[0], b_w_gates[0],
                b_b_gates[0], b_out_g[0], b_skip[0])
    return _tail(x, ya, yb, mods[0], mods[1], ln_g[1], w_out[0], w_in[1], w_out[1],
                 c_ln_g[0], c_ln_b[0], c_ws[0], c_bs[0])
```

```python
import functools

import jax
import jax.numpy as jnp
from jax import lax
from jax.experimental import pallas as pl
from jax.experimental.pallas import tpu as pltpu

F32 = jnp.float32
BF16 = jnp.bfloat16

D_MODEL = 1024
A_HEADS = 8
A_HEAD_DIM = 64
A_WIDTH = A_HEADS * A_HEAD_DIM
MOBA_BLOCK = 256
MOBA_TOPK = 3
B_HEADS = 4
B_HEAD_DIM = 128
B_WIDTH = B_HEADS * B_HEAD_DIM
B_CONV = 4
C_GROUPS = 8
C_CHUNK = 128
C_WIDTH = D_MODEL
NEG = -1e30
LOG2E = 1.4426950408889634

MLSTM_L = 256
MLSTM_NB = 2
INPROJ_SUBTILES = 2
TAIL_ROW_TILE = 512
CONV_HALO = 8
GATE_ROWS = 2 * B_HEADS
PV_ONES_ROWS = 16
VMEM_LIMIT = 48 * 1024 * 1024


def _silu(x):
    return x * jax.nn.sigmoid(x)


def _gelu_tanh(x):
    return 0.5 * x * (1.0 + jnp.tanh(0.7978845608028654 * (x + 0.044715 * (x * x * x))))


def _log_sigmoid(x):
    return jnp.minimum(x, 0.0) - jnp.log(1.0 + jnp.exp(-jnp.abs(x)))


def _split_bf16(x):
    hi = x.astype(BF16)
    lo = (x - hi.astype(F32)).astype(BF16)
    return hi, lo


def _dot(a, b):
    return jnp.dot(a, b, preferred_element_type=F32)


def _dot_nt(a, b):
    return lax.dot_general(a, b, (((1,), (1,)), ((), ())), preferred_element_type=F32)


def _adaln_rmsnorm(x, ln_g, scale, shift):
    y = x * lax.rsqrt(jnp.mean(x * x, axis=-1, keepdims=True) + 1e-6)
    return (y * ln_g) * (1.0 + scale) + shift


def _ada_kernel(c_ref, w_ref, b_ref, o_ref):
    cs_hi, cs_lo = _split_bf16(_silu(c_ref[...]))
    w_hi, w_lo = _split_bf16(w_ref[0])
    o_ref[0] = _dot(cs_hi, w_hi) + _dot(cs_lo, w_hi) + _dot(cs_hi, w_lo) + b_ref[0]


def _ada_mods(c, ada_w, ada_b):
    depth, d, d3 = ada_w.shape
    b = c.shape[0]
    bp = 8
    cp = jnp.zeros((bp, d), F32).at[:b].set(c)
    nt = d3 // d
    out = pl.pallas_call(
        _ada_kernel,
        grid=(depth, nt),
        in_specs=[
            pl.BlockSpec((bp, d), lambda l, n: (0, 0)),
            pl.BlockSpec((1, d, d), lambda l, n: (l, 0, n)),
            pl.BlockSpec((1, 1, d), lambda l, n: (l, 0, n)),
        ],
        out_specs=pl.BlockSpec((1, bp, d), lambda l, n: (l, 0, n)),
        out_shape=jax.ShapeDtypeStruct((depth, bp, d3), F32),
        name="ada_mods",
    )(cp, ada_w, ada_b.reshape(depth, 1, d3))
    return out[:, :b].reshape(depth, b, 3, d)


def _inproj0_kernel(x_ref, mod_ref, lng_ref, wt_ref, w_ref, qg_ref, kg_ref,
                    qT_ref, vT_ref, gT_ref, k_ref, bx_ref, sbz_ref):
    tm = MOBA_BLOCK
    aw = A_WIDTH
    mod = mod_ref[0]
    qg = qg_ref[...] * (A_HEAD_DIM ** -0.5 * LOG2E)
    kg = kg_ref[...]
    hs = [_adaln_rmsnorm(x_ref[0, r * tm:(r + 1) * tm], lng_ref[...], mod[1:2], mod[0:1]).astype(BF16)
          for r in range(INPROJ_SUBTILES)]
    for r in range(INPROJ_SUBTILES):
        h = hs[r]
        rows = slice(r * tm, (r + 1) * tm)
        pt = _dot_nt(wt_ref[...], h)
        p = _dot(h, w_ref[...])
        kn = []
        for hd in range(A_HEADS):
            lo = hd * A_HEAD_DIM
            q = pt[lo:lo + A_HEAD_DIM]
            rq = lax.rsqrt(jnp.mean(q * q, axis=0, keepdims=True) + 1e-6)
            qT_ref[0, r, lo:lo + A_HEAD_DIM, :] = ((q * rq) * qg).astype(BF16)
            k = pt[aw + lo:aw + lo + A_HEAD_DIM]
            rk = lax.rsqrt(jnp.mean(k * k, axis=0, keepdims=True) + 1e-6)
            kn.append((k * rk) * kg)
        k_ref[0, rows, :] = jnp.concatenate(kn, axis=0).T.astype(BF16)
        vT_ref[0, r] = pt[2 * aw:3 * aw].astype(BF16)
        gT_ref[0, r] = _silu(pt[3 * aw:]).astype(BF16)
        bx_ref[0, rows, :] = p[:, :B_WIDTH].astype(BF16)
        sbz_ref[0, rows, :] = _silu(p[:, B_WIDTH:]).astype(BF16)


def _inproj0(x, mod, ln_g, w_in, q_g, k_g):
    b, s, d = x.shape
    blk = MOBA_BLOCK
    sub = INPROJ_SUBTILES
    tm = sub * blk
    nt = s // tm
    aw, bw = A_WIDTH, B_WIDTH
    wq, wk, wv, wz = (w_in[:, i * aw:(i + 1) * aw] for i in range(4))
    wbx = w_in[:, 4 * aw:4 * aw + bw]
    wbz = w_in[:, 4 * aw + bw:]
    wt = jnp.concatenate([wq, wk, wv, wz], axis=1).T.astype(BF16)
    wn = jnp.concatenate([wbx, wbz], axis=1).astype(BF16)
    t_shape = jax.ShapeDtypeStruct((b, s // blk, aw, blk), BF16)
    n_shape = jax.ShapeDtypeStruct((b, s, aw), BF16)
    t_spec = pl.BlockSpec((1, sub, aw, blk), lambda i, j: (i, j, 0, 0))
    n_spec = pl.BlockSpec((1, tm, aw), lambda i, j: (i, j, 0))
    const = lambda shape: pl.BlockSpec(shape, lambda i, j: (0,) * len(shape))
    return pl.pallas_call(
        _inproj0_kernel,
        grid=(b, nt),
        in_specs=[
            pl.BlockSpec((1, tm, d), lambda i, j: (i, j, 0)),
            pl.BlockSpec((1, 3, d), lambda i, j: (i, 0, 0)),
            const((1, d)),
            const((4 * aw, d)),
            const((d, 2 * bw)),
            const((A_HEAD_DIM, 1)),
            const((A_HEAD_DIM, 1)),
        ],
        out_specs=[t_spec, t_spec, t_spec, n_spec, n_spec, n_spec],
        out_shape=[t_shape, t_shape, t_shape, n_shape, n_shape, n_shape],
        compiler_params=pltpu.CompilerParams(
            dimension_semantics=("parallel", "parallel"), vmem_limit_bytes=VMEM_LIMIT),
        name="inproj0",
    )(x, mod, ln_g.reshape(1, d), wt, wn, q_g.reshape(A_HEAD_DIM, 1), k_g.reshape(A_HEAD_DIM, 1))


def _moba_kernel(qT_ref, k_ref, vT_ref, gT_ref, o_ref, kmean_ref, sel_ref, qm_ref, raw_ref, p_ref,
                 m_ref, alpha_ref, acc_ref, *, nb):
    blk = MOBA_BLOCK
    dh = A_HEAD_DIM
    pw = 2 * dh
    j = pl.program_id(1)

    @pl.when(j == 0)
    def _():
        for n in range(nb):
            kb = k_ref[0, n * blk:(n + 1) * blk, :].astype(F32)
            kmean_ref[n:n + 1, :] = jnp.mean(kb, axis=0, keepdims=True)

    d_iota = lax.broadcasted_iota(jnp.int32, (pw, blk), 0)
    for hd in range(A_HEADS):
        pr = hd // 2
        q2 = qT_ref[0, 0, pr * pw:(pr + 1) * pw, :]
        keep = (d_iota < dh) if hd % 2 == 0 else (d_iota >= dh)
        qm_ref[hd] = jnp.where(keep, q2, jnp.zeros_like(q2))

    def score_dots(n, slot):
        row0 = pl.multiple_of(n * blk, blk)
        for hd in range(A_HEADS):
            pr = hd // 2
            kt = k_ref[0, pl.ds(row0, blk), pr * pw:(pr + 1) * pw]
            raw_ref[slot, hd] = _dot(kt, qm_ref[hd]).astype(BF16)

    def value_dots(n, slot):
        ones = jnp.ones((PV_ONES_ROWS, blk), BF16)
        return [_dot(jnp.concatenate([vT_ref[0, n, hd * dh:(hd + 1) * dh, :], ones], axis=0),
                     p_ref[slot, hd]) for hd in range(A_HEADS)]

    km_hi, km_lo = _split_bf16(kmean_ref[...])
    km2 = jnp.concatenate([km_hi, km_lo], axis=0)
    sel_scores = []
    for hd in range(A_HEADS):
        pr = hd // 2
        r2 = _dot(km2[:, pr * pw:(pr + 1) * pw], qm_ref[hd])
        sel_scores.append(r2[0:nb] + r2[nb:2 * nb])

    score_dots(j, 0)
    score_dots(0, 1)

    n_iota = lax.broadcasted_iota(jnp.int32, (nb, blk), 0)
    past = n_iota < j
    for hd in range(A_HEADS):
        sc = jnp.where(past, sel_scores[hd], NEG)
        sel = jnp.zeros((nb, blk), F32)
        for _ in range(MOBA_TOPK):
            mx = jnp.max(sc, axis=0, keepdims=True)
            first = jnp.min(jnp.where(sc == mx, n_iota, nb), axis=0, keepdims=True)
            pick = n_iota == first
            sel = jnp.where(pick, 1.0, sel)
            sc = jnp.where(pick, -jnp.inf, sc)
        sel_ref[hd, 0:nb, :] = jnp.where(past, sel, 0.0)
        sel_ref[hd, nb:nb + 1, :] = jnp.ones((1, blk), F32)

    k_iota = lax.broadcasted_iota(jnp.int32, (blk, blk), 0)
    q_iota = lax.broadcasted_iota(jnp.int32, (blk, blk), 1)
    causal_bias = jnp.where(k_iota <= q_iota, 0.0, NEG).astype(BF16)
    for hd in range(A_HEADS):
        s = raw_ref[0, hd] + causal_bias
        m_new = jnp.max(s, axis=0, keepdims=True)
        p_ref[0, hd] = jnp.exp2(s - m_new)
        m_ref[hd] = m_new.astype(F32)
        acc_ref[hd] = jnp.zeros(acc_ref.shape[1:], F32)

    def step(i, cur):
        nxt = 1 - cur
        n = i - 1
        prev = jnp.where(i == 1, j, i - 2)
        prev_sel = jnp.where(i == 1, nb, i - 2)
        row0 = pl.multiple_of(jnp.minimum(i, nb - 1) * blk, blk)
        ones = jnp.ones((PV_ONES_ROWS, blk), BF16)
        pvs = []
        for hd in range(A_HEADS):
            pr = hd // 2
            kt = k_ref[0, pl.ds(row0, blk), pr * pw:(pr + 1) * pw]
            raw_ref[nxt, hd] = _dot(kt, qm_ref[hd]).astype(BF16)
            pvs.append(_dot(jnp.concatenate([vT_ref[0, prev, hd * dh:(hd + 1) * dh, :], ones], axis=0),
                            p_ref[nxt, hd]))
            s = raw_ref[cur, hd]
            selb = sel_ref[hd, pl.ds(n, 1), :] > 0.5
            smax = jnp.where(selb, jnp.max(s, axis=0, keepdims=True).astype(F32), NEG)
            m_old = m_ref[hd]
            m_new = jnp.maximum(m_old, smax)
            alpha_ref[hd] = jnp.exp2(m_old - m_new)
            m_ref[hd] = m_new
            p_ref[cur, hd] = jnp.exp2(s - m_new.astype(BF16))
        for hd in range(A_HEADS):
            keep = sel_ref[hd, pl.ds(prev_sel, 1), :] > 0.5
            acc_ref[hd] = alpha_ref[hd] * (acc_ref[hd] + jnp.where(keep, pvs[hd], 0.0))

    def body(ii, carry):
        step(2 * ii + 1, 1)
        step(2 * ii + 2, 0)
        return carry

    trips = (j + 1) // 2
    lax.fori_loop(0, trips, body, 0)

    last = jnp.where(j == 0, j, 2 * trips - 1)
    last_sel = jnp.where(j == 0, nb, 2 * trips - 1)
    pvs = value_dots(last, 0)
    for pr in range(A_HEADS // 2):
        halves = []
        for hd in (2 * pr, 2 * pr + 1):
            keep = sel_ref[hd, pl.ds(last_sel, 1), :] > 0.5
            acc = acc_ref[hd] + jnp.where(keep, pvs[hd], 0.0)
            halves.append(acc[0:dh] / acc[dh:dh + 1])
        oT = jnp.concatenate(halves, axis=0)
        oT = oT * gT_ref[0, 0, pr * pw:(pr + 1) * pw, :].astype(F32)
        o_ref[0, :, pr * pw:(pr + 1) * pw] = oT.T.astype(BF16)


def _moba(qT, k, vT, gT):
    b, nb, aw, blk = qT.shape
    s = nb * blk
    pw = 2 * A_HEAD_DIM
    q_spec = pl.BlockSpec((1, 1, aw, blk), lambda i, j: (i, j, 0, 0))
    row = lambda: pltpu.VMEM((A_HEADS, 1, blk), F32)
    return pl.pallas_call(
        functools.partial(_moba_kernel, nb=nb),
        grid=(b, nb),
        in_specs=[
            q_spec,
            pl.BlockSpec((1, s, aw), lambda i, j: (i, 0, 0)),
            pl.BlockSpec((1, nb, aw, blk), lambda i, j: (i, 0, 0, 0)),
            q_spec,
        ],
        out_specs=pl.BlockSpec((1, blk, aw), lambda i, j: (i, j, 0)),
        out_shape=jax.ShapeDtypeStruct((b, s, aw), BF16),
        scratch_shapes=[
            pltpu.VMEM((nb, aw), F32),
            pltpu.VMEM((A_HEADS, nb + 8, blk), F32),
            pltpu.VMEM((A_HEADS, pw, blk), BF16),
            pltpu.VMEM((2, A_HEADS, blk, blk), BF16),
            pltpu.VMEM((2, A_HEADS, blk, blk), BF16),
            row(), row(),
            pltpu.VMEM((A_HEADS, A_HEAD_DIM + PV_ONES_ROWS, blk), F32),
        ],
        compiler_params=pltpu.CompilerParams(
            dimension_semantics=("parallel", "arbitrary"), vmem_limit_bytes=VMEM_LIMIT),
        name="moba",
    )(qT, k, vT, gT)


def _mlstm_kernel(bx_ref, sbz_ref, cw_ref, cb_ref, wqt_ref, wk_ref, wvt_ref, wgq_ref, wgk_ref,
                  wgv_ref, bg_ref, og_ref, skip_ref, o_ref, xbuf_ref, state_ref, m_ref):
    L = MLSTM_L
    dh = B_HEAD_DIM
    c = pl.program_id(1)

    @pl.when(c == 0)
    def _():
        xbuf_ref[:, 0:CONV_HALO, :] = jnp.zeros((MLSTM_NB, CONV_HALO, B_WIDTH), F32)
        state_ref[...] = jnp.zeros_like(state_ref)
        m_ref[...] = jnp.zeros_like(m_ref)

    cw = cw_ref[...]
    cb = cb_ref[...]
    kscale = dh ** -0.5
    s_iota = lax.broadcasted_iota(jnp.int32, (L, L), 0)
    t_iota = lax.broadcasted_iota(jnp.int32, (L, L), 1)
    tri = s_iota <= t_iota
    tri_b = jnp.where(tri, 1.0, 0.0).astype(BF16)
    ones = jnp.ones((dh, L), F32)

    def project(bi):
        bx_b = bx_ref[bi]
        xbuf_ref[bi, CONV_HALO:CONV_HALO + L, :] = bx_b.astype(F32)
        st = dict(xc=[], ks=[], vT=[], scores=[], inter=[], state=[])
        gT = bg_ref[...]
        for hd in range(B_HEADS):
            lo = hd * dh
            conv = cb[:, lo:lo + dh]
            for i in range(B_CONV):
                off = CONV_HALO - (B_CONV - 1) + i
                conv = conv + cw[i:i + 1, lo:lo + dh] * xbuf_ref[bi, off:off + L, lo:lo + dh]
            xc = _silu(conv)
            xc_b = xc.astype(BF16)
            qT_h = _dot_nt(wqt_ref[hd], xc_b).astype(BF16)
            vT_h = _dot_nt(wvt_ref[hd], bx_b[:, lo:lo + dh])
            k_h = _dot(xc_b, wk_ref[hd])
            ks_h = (k_h * kscale).astype(BF16)
            state = state_ref[bi, hd]
            st["scores"].append(_dot(ks_h, qT_h))
            st["inter"].append(_dot(state.astype(BF16), qT_h))
            gT = gT + (_dot(wgq_ref[:, lo:lo + dh], qT_h)
                       + _dot_nt(wgk_ref[:, lo:lo + dh], k_h.astype(BF16))
                       + _dot(wgv_ref[:, lo:lo + dh], vT_h.astype(BF16)))
            st["xc"].append(xc); st["ks"].append(ks_h); st["vT"].append(vT_h); st["state"].append(state)
        tail = xbuf_ref[bi, L:L + CONV_HALO, :]
        xbuf_ref[bi, 0:CONV_HALO, :] = tail
        st["gT"] = gT
        return st

    def gate_chain(st):
        gT = st["gT"]
        lf_hi, lf_lo = _split_bf16(_log_sigmoid(gT))
        cum = (_dot(lf_hi, tri_b) + _dot(lf_lo, tri_b))[B_HEADS:2 * B_HEADS]
        a = gT[0:B_HEADS] - cum
        st["cum"], st["a"] = cum, a
        st["a_cols"] = jnp.concatenate([a, jnp.zeros((128 - B_HEADS, L), F32)], axis=0).T

    def recur(bi, st):
        a, cum, a_cols = st["a"], st["cum"], st["a_cols"]
        for hd in range(B_HEADS):
            lo = hd * dh
            ks_h, state, inter = st["ks"][hd], st["state"][hd], st["inter"][hd]
            vT_aug = jnp.concatenate([st["vT"][hd], ones], axis=0)
            a_row = a[hd:hd + 1]
            a_col = a_cols[:, hd:hd + 1]
            cum_row = cum[hd:hd + 1]
            m_prev = m_ref[bi, hd][0:1, 0:1]
            amax = jnp.max(jnp.where(tri, a_col, NEG), axis=0, keepdims=True)
            b_row = jnp.maximum(m_prev, amax)
            dmat = jnp.exp(jnp.where(tri, a_col - b_row, NEG))
            w_intra = (st["scores"][hd] * dmat).astype(BF16)
            intra = _dot(vT_aug.astype(BF16), w_intra)
            w_inter = jnp.exp(m_prev - b_row)
            num = w_inter * inter[0:dh] + intra[0:dh]
            den = w_inter * inter[dh:dh + 1] + intra[dh:dh + 1]
            hc = num * (1.0 / jnp.maximum(jnp.abs(den), jnp.exp(-(cum_row + b_row))))
            mu = jnp.mean(hc, axis=0, keepdims=True)
            hcc = hc - mu
            var = jnp.mean(hcc * hcc, axis=0, keepdims=True)
            hb = (hcc * lax.rsqrt(var + 1e-5)).T * og_ref[:, lo:lo + dh]
            yb = (hb + skip_ref[:, lo:lo + dh] * st["xc"][hd]) * sbz_ref[bi, :, lo:lo + dh].astype(F32)
            o_ref[bi, :, lo:lo + dh] = yb.astype(BF16)
            b_end = b_row[:, L - 1:L]
            ws = jnp.exp(a_row - b_end)
            decay = jnp.exp(m_prev - b_end)
            state_ref[bi, hd] = decay * state + _dot((vT_aug * ws).astype(BF16), ks_h)
            m_ref[bi, hd] = jnp.broadcast_to(cum_row[:, L - 1:L] + b_end, m_ref.shape[2:])

    sts = []
    for bi in range(MLSTM_NB):
        sts.append(project(bi))
        gate_chain(sts[bi])
    for bi in range(MLSTM_NB):
        recur(bi, sts[bi])


def _block_diag_dense(w):
    nblk, blk, _ = w.shape
    n = nblk * blk
    idx = jnp.arange(n) // blk
    return jnp.where(idx[:, None] == idx[None, :], jnp.tile(w.reshape(n, blk), (1, nblk)), 0.0)


def _mlstm(bx, sbz, conv_w, conv_b, wq, wk, wv, w_gates, b_gates, out_g, skip):
    b, s, bw = bx.shape
    L = MLSTM_L
    nc = s // L
    nbb = MLSTM_NB
    assert b % nbb == 0 and s % L == 0
    per_head = wq.shape[0] // B_HEADS

    def head_blocks(w, transpose):
        blocks = [_block_diag_dense(w[h * per_head:(h + 1) * per_head]) for h in range(B_HEADS)]
        return jnp.stack([blk.T if transpose else blk for blk in blocks]).astype(BF16)

    wqt_d = head_blocks(wq, True)
    wk_d = head_blocks(wk, False)
    wvt_d = head_blocks(wv, True)
    hblk = (B_HEADS, B_HEAD_DIM, B_HEAD_DIM)
    wgq = w_gates[:bw].T.astype(BF16)
    wgv = w_gates[2 * bw:].T.astype(BF16)
    wgk = w_gates[bw:2 * bw].T.astype(BF16)
    const = lambda shape: pl.BlockSpec(shape, lambda i, j: (0,) * len(shape))
    tok = pl.BlockSpec((nbb, L, bw), lambda i, j: (i, j, 0))
    return pl.pallas_call(
        _mlstm_kernel,
        grid=(b // nbb, nc),
        in_specs=[tok, tok, const((B_CONV, bw)), const((1, bw)),
                  const(hblk), const(hblk), const(hblk),
                  const((GATE_ROWS, bw)), const((GATE_ROWS, bw)), const((GATE_ROWS, bw)),
                  const((GATE_ROWS, 1)), const((1, bw)), const((1, bw))],
        out_specs=tok,
        out_shape=jax.ShapeDtypeStruct((b, s, bw), BF16),
        scratch_shapes=[pltpu.VMEM((nbb, L + CONV_HALO, bw), F32),
                        pltpu.VMEM((nbb, B_HEADS, 2 * B_HEAD_DIM, B_HEAD_DIM), F32),
                        pltpu.VMEM((nbb, B_HEADS, 8, 128), F32)],
        compiler_params=pltpu.CompilerParams(
            dimension_semantics=("parallel", "arbitrary"), vmem_limit_bytes=VMEM_LIMIT),
        name="mlstm",
    )(bx, sbz, conv_w, conv_b.reshape(1, bw), wqt_d, wk_d, wvt_d, wgq, wgk, wgv,
      b_gates.reshape(GATE_ROWS, 1), out_g.reshape(1, bw), skip.reshape(1, bw))


def _tail_kernel(x_ref, ya_ref, yb_ref, mod0_ref, mod1_ref, lng_ref, wo0_ref, wi1_ref, clg_ref,
                 clb_ref, ws_ref, bst_ref, wo1_ref, o_ref):
    tm = x_ref.shape[1]
    aw = A_WIDTH
    wo0 = wo0_ref
    y0 = _dot(ya_ref[0], wo0[0:aw, :]) + _dot(yb_ref[0], wo0[aw:, :])
    x1 = x_ref[0] + mod0_ref[0][2:3] * y0
    mod1 = mod1_ref[0]
    h = _adaln_rmsnorm(x1, lng_ref[...], mod1[1:2], mod1[0:1]).astype(BF16)
    p = _dot(h, wi1_ref[...])
    u = _gelu_tanh(p[:, :C_WIDTH])
    v = _gelu_tanh(p[:, C_WIDTH:2 * C_WIDTH])
    mu = jnp.mean(v, axis=-1, keepdims=True)
    vc = v - mu
    var = jnp.mean(vc * vc, axis=-1, keepdims=True)
    vn = ((vc * lax.rsqrt(var + 1e-5)) * clg_ref[...] + clb_ref[...]).astype(BF16)
    gate = u * _silu(p[:, 2 * C_WIDTH:])
    t_iota = lax.broadcasted_iota(jnp.int32, (C_CHUNK, C_CHUNK), 0)
    s_iota = lax.broadcasted_iota(jnp.int32, (C_CHUNK, C_CHUNK), 1)
    tril = s_iota <= t_iota
    gw = C_WIDTH // C_GROUPS
    bst = bst_ref[...]
    cols = []
    for g in range(C_GROUPS):
        wm = jnp.where(tril, ws_ref[g], 0.0).astype(BF16)
        rows = []
        for ch in range(tm // C_CHUNK):
            vg = vn[ch * C_CHUNK:(ch + 1) * C_CHUNK, g * gw:(g + 1) * gw]
            rows.append(_dot(wm, vg) + bst[:, g:g + 1])
        cols.append(jnp.concatenate(rows, axis=0))
    sg = jnp.concatenate(cols, axis=1)
    y1 = (gate * sg).astype(BF16)
    o_ref[0] = x1 + mod1[2:3] * _dot(y1, wo1_ref[...])


def _tail(x, ya, yb, mod0, mod1, ln_g1, w_out0, w_in1, w_out1, c_ln_g, c_ln_b, c_ws, c_bs):
    b, s, d = x.shape
    tm = TAIL_ROW_TILE
    nt = s // tm
    const = lambda shape: pl.BlockSpec(shape, lambda i, j: (0,) * len(shape))
    half = pl.BlockSpec((1, tm, A_WIDTH), lambda i, j: (i, j, 0))
    full = pl.BlockSpec((1, tm, d), lambda i, j: (i, j, 0))
    modspec = pl.BlockSpec((1, 3, d), lambda i, j: (i, 0, 0))
    return pl.pallas_call(
        _tail_kernel,
        grid=(b, nt),
        in_specs=[full, half, half, modspec, modspec, const((1, d)), const((d, d)),
                  const((d, 3 * C_WIDTH)), const((1, d)), const((1, d)),
                  const((C_GROUPS, C_CHUNK, C_CHUNK)), const((C_CHUNK, C_GROUPS)), const((d, d))],
        out_specs=full,
        out_shape=jax.ShapeDtypeStruct((b, s, d), F32),
        compiler_params=pltpu.CompilerParams(
            dimension_semantics=("parallel", "parallel"), vmem_limit_bytes=VMEM_LIMIT),
        name="tail",
    )(x, ya, yb, mod0, mod1, ln_g1.reshape(1, d), w_out0.astype(BF16), w_in1.astype(BF16),
      c_ln_g.reshape(1, d), c_ln_b.reshape(1, d), c_ws, c_bs.T, w_out1.astype(BF16))


def kernel(x, c, ln_g, ada_w, ada_b, w_in, w_out, a_q_g, a_k_g, b_conv_w, b_conv_b, b_wq, b_wk, b_wv,
           b_w_gates, b_b_gates, b_out_g, b_skip, c_ln_g, c_ln_b, c_ws, c_bs):
    mods = _ada_mods(c, ada_w, ada_b)
    qT, vT, gT, k, bx, sbz = _inproj0(x, mods[0], ln_g[0], w_in[0], a_q_g[0], a_k_g[0])
    ya = _moba(qT, k, vT, gT)
    yb = _mlstm(bx, sbz, b_conv_w[0], b_conv_b[0], b_wq[0], b_wk[0], b_wv[0], b_w_gates[0],
                b_b_gates[0], b_out_g[0], b_skip[0])
    return _tail(x, ya, yb, mods[0], mods[1], ln_g[1], w_out[0], w_in[1], w_out[1],
                 c_ln_g[0], c_ln_b[0], c_ws[0], c_bs[0])
```

```python
import functools

import jax
import jax.numpy as jnp
from jax import lax
from jax.experimental import pallas as pl
from jax.experimental.pallas import tpu as pltpu

F32 = jnp.float32
BF16 = jnp.bfloat16

D_MODEL = 1024
A_HEADS = 8
A_HEAD_DIM = 64
A_WIDTH = A_HEADS * A_HEAD_DIM
MOBA_BLOCK = 256
MOBA_TOPK = 3
B_HEADS = 4
B_HEAD_DIM = 128
B_WIDTH = B_HEADS * B_HEAD_DIM
B_CONV = 4
C_GROUPS = 8
C_CHUNK = 128
C_WIDTH = D_MODEL
NEG = -1e30
LOG2E = 1.4426950408889634

MLSTM_L = 256
MLSTM_NB = 4
INPROJ_SUBTILES = 2
TAIL_ROW_TILE = 512
TAIL_SUB_ROWS = 256
CONV_HALO = 8
GATE_ROWS = 2 * B_HEADS
PV_ONES_ROWS = 16
VMEM_LIMIT = 48 * 1024 * 1024


def _silu(x):
    return x * jax.nn.sigmoid(x)


def _gelu_tanh(x):
    return 0.5 * x * (1.0 + jnp.tanh(0.7978845608028654 * (x + 0.044715 * (x * x * x))))


def _log_sigmoid(x):
    return jnp.minimum(x, 0.0) - jnp.log(1.0 + jnp.exp(-jnp.abs(x)))


def _split_bf16(x):
    hi = x.astype(BF16)
    lo = (x - hi.astype(F32)).astype(BF16)
    return hi, lo


def _dot(a, b):
    return jnp.dot(a, b, preferred_element_type=F32)


def _dot_nt(a, b):
    return lax.dot_general(a, b, (((1,), (1,)), ((), ())), preferred_element_type=F32)


def _adaln_rmsnorm(x, ln_g, scale, shift):
    y = x * lax.rsqrt(jnp.mean(x * x, axis=-1, keepdims=True) + 1e-6)
    return (y * ln_g) * (1.0 + scale) + shift


def _ada_kernel(c_ref, w_ref, b_ref, o_ref):
    cs_hi, cs_lo = _split_bf16(_silu(c_ref[...]))
    w_hi, w_lo = _split_bf16(w_ref[0])
    o_ref[0] = _dot(cs_hi, w_hi) + _dot(cs_lo, w_hi) + _dot(cs_hi, w_lo) + b_ref[0]


def _ada_mods(c, ada_w, ada_b):
    depth, d, d3 = ada_w.shape
    b = c.shape[0]
    bp = 8
    cp = jnp.zeros((bp, d), F32).at[:b].set(c)
    nt = d3 // d
    out = pl.pallas_call(
        _ada_kernel,
        grid=(depth, nt),
        in_specs=[
            pl.BlockSpec((bp, d), lambda l, n: (0, 0)),
            pl.BlockSpec((1, d, d), lambda l, n: (l, 0, n)),
            pl.BlockSpec((1, 1, d), lambda l, n: (l, 0, n)),
        ],
        out_specs=pl.BlockSpec((1, bp, d), lambda l, n: (l, 0, n)),
        out_shape=jax.ShapeDtypeStruct((depth, bp, d3), F32),
        name="ada_mods",
    )(cp, ada_w, ada_b.reshape(depth, 1, d3))
    return out[:, :b].reshape(depth, b, 3, d)


def _inproj0_kernel(x_ref, mod_ref, lng_ref, wt_ref, w_ref, qg_ref, kg_ref,
                    qT_ref, vT_ref, gT_ref, k_ref, bx_ref, sbz_ref):
    tm = MOBA_BLOCK
    aw = A_WIDTH
    mod = mod_ref[0]
    qg = qg_ref[...] * (A_HEAD_DIM ** -0.5 * LOG2E)
    kg = kg_ref[...]
    hs = [_adaln_rmsnorm(x_ref[0, r * tm:(r + 1) * tm], lng_ref[...], mod[1:2], mod[0:1]).astype(BF16)
          for r in range(INPROJ_SUBTILES)]
    for r in range(INPROJ_SUBTILES):
        h = hs[r]
        rows = slice(r * tm, (r + 1) * tm)
        pt = _dot_nt(wt_ref[...], h)
        p = _dot(h, w_ref[...])
        kn = []
        for hd in range(A_HEADS):
            lo = hd * A_HEAD_DIM
            q = pt[lo:lo + A_HEAD_DIM]
            rq = lax.rsqrt(jnp.mean(q * q, axis=0, keepdims=True) + 1e-6)
            qT_ref[0, r, lo:lo + A_HEAD_DIM, :] = ((q * rq) * qg).astype(BF16)
            k = pt[aw + lo:aw + lo + A_HEAD_DIM]
            rk = lax.rsqrt(jnp.mean(k * k, axis=0, keepdims=True) + 1e-6)
            kn.append((k * rk) * kg)
        k_ref[0, rows, :] = jnp.concatenate(kn, axis=0).T.astype(BF16)
        vT_ref[0, r] = pt[2 * aw:3 * aw].astype(BF16)
        gT_ref[0, r] = _silu(pt[3 * aw:]).astype(BF16)
        bx_ref[0, rows, :] = p[:, :B_WIDTH].astype(BF16)
        sbz_ref[0, rows, :] = _silu(p[:, B_WIDTH:]).astype(BF16)


def _inproj0(x, mod, ln_g, w_in, q_g, k_g):
    b, s, d = x.shape
    blk = MOBA_BLOCK
    sub = INPROJ_SUBTILES
    tm = sub * blk
    nt = s // tm
    aw, bw = A_WIDTH, B_WIDTH
    wq, wk, wv, wz = (w_in[:, i * aw:(i + 1) * aw] for i in range(4))
    wbx = w_in[:, 4 * aw:4 * aw + bw]
    wbz = w_in[:, 4 * aw + bw:]
    wt = jnp.concatenate([wq, wk, wv, wz], axis=1).T.astype(BF16)
    wn = jnp.concatenate([wbx, wbz], axis=1).astype(BF16)
    t_shape = jax.ShapeDtypeStruct((b, s // blk, aw, blk), BF16)
    n_shape = jax.ShapeDtypeStruct((b, s, aw), BF16)
    t_spec = pl.BlockSpec((1, sub, aw, blk), lambda i, j: (i, j, 0, 0))
    n_spec = pl.BlockSpec((1, tm, aw), lambda i, j: (i, j, 0))
    const = lambda shape: pl.BlockSpec(shape, lambda i, j: (0,) * len(shape))
    return pl.pallas_call(
        _inproj0_kernel,
        grid=(b, nt),
        in_specs=[
            pl.BlockSpec((1, tm, d), lambda i, j: (i, j, 0)),
            pl.BlockSpec((1, 3, d), lambda i, j: (i, 0, 0)),
            const((1, d)),
            const((4 * aw, d)),
            const((d, 2 * bw)),
            const((A_HEAD_DIM, 1)),
            const((A_HEAD_DIM, 1)),
        ],
        out_specs=[t_spec, t_spec, t_spec, n_spec, n_spec, n_spec],
        out_shape=[t_shape, t_shape, t_shape, n_shape, n_shape, n_shape],
        compiler_params=pltpu.CompilerParams(
            dimension_semantics=("parallel", "parallel"), vmem_limit_bytes=VMEM_LIMIT),
        name="inproj0",
    )(x, mod, ln_g.reshape(1, d), wt, wn, q_g.reshape(A_HEAD_DIM, 1), k_g.reshape(A_HEAD_DIM, 1))


def _moba_kernel(qT_ref, k_ref, vT_ref, gT_ref, o_ref, kmean_ref, sel_ref, qm_ref, raw_ref, p_ref,
                 m_ref, alpha_ref, acc_ref, *, nb):
    blk = MOBA_BLOCK
    dh = A_HEAD_DIM
    pw = 2 * dh
    j = pl.program_id(1)

    @pl.when(j == 0)
    def _():
        for n in range(nb):
            kb = k_ref[0, n * blk:(n + 1) * blk, :].astype(F32)
            kmean_ref[n:n + 1, :] = jnp.mean(kb, axis=0, keepdims=True)

    d_iota = lax.broadcasted_iota(jnp.int32, (pw, blk), 0)
    for hd in range(A_HEADS):
        pr = hd // 2
        q2 = qT_ref[0, 0, pr * pw:(pr + 1) * pw, :]
        keep = (d_iota < dh) if hd % 2 == 0 else (d_iota >= dh)
        qm_ref[hd] = jnp.where(keep, q2, jnp.zeros_like(q2))

    def score_dots(n, slot):
        row0 = pl.multiple_of(n * blk, blk)
        for hd in range(A_HEADS):
            pr = hd // 2
            kt = k_ref[0, pl.ds(row0, blk), pr * pw:(pr + 1) * pw]
            raw_ref[slot, hd] = _dot(kt, qm_ref[hd]).astype(BF16)

    def value_dots(n, slot):
        ones = jnp.ones((PV_ONES_ROWS, blk), BF16)
        return [_dot(jnp.concatenate([vT_ref[0, n, hd * dh:(hd + 1) * dh, :], ones], axis=0),
                     p_ref[slot, hd]) for hd in range(A_HEADS)]

    km_hi, km_lo = _split_bf16(kmean_ref[...])
    km2 = jnp.concatenate([km_hi, km_lo], axis=0)
    sel_scores = []
    for hd in range(A_HEADS):
        pr = hd // 2
        r2 = _dot(km2[:, pr * pw:(pr + 1) * pw], qm_ref[hd])
        sel_scores.append(r2[0:nb] + r2[nb:2 * nb])

    score_dots(j, 0)
    score_dots(0, 1)

    n_iota = lax.broadcasted_iota(jnp.int32, (nb, blk), 0)
    past = n_iota < j
    for hd in range(A_HEADS):
        sc = jnp.where(past, sel_scores[hd], NEG)
        sel = jnp.zeros((nb, blk), F32)
        for _ in range(MOBA_TOPK):
            mx = jnp.max(sc, axis=0, keepdims=True)
            first = jnp.min(jnp.where(sc == mx, n_iota, nb), axis=0, keepdims=True)
            pick = n_iota == first
            sel = jnp.where(pick, 1.0, sel)
            sc = jnp.where(pick, -jnp.inf, sc)
        sel_ref[hd, 0:nb, :] = jnp.where(past, sel, 0.0)
        sel_ref[hd, nb:nb + 1, :] = jnp.ones((1, blk), F32)

    k_iota = lax.broadcasted_iota(jnp.int32, (blk, blk), 0)
    q_iota = lax.broadcasted_iota(jnp.int32, (blk, blk), 1)
    causal_bias = jnp.where(k_iota <= q_iota, 0.0, NEG).astype(BF16)
    for hd in range(A_HEADS):
        s = raw_ref[0, hd] + causal_bias
        m_new = jnp.max(s, axis=0, keepdims=True)
        p_ref[0, hd] = jnp.exp2(s - m_new)
        m_ref[hd] = m_new.astype(F32)
        acc_ref[hd] = jnp.zeros(acc_ref.shape[1:], F32)

    def step(i, cur):
        nxt = 1 - cur
        n = i - 1
        prev = jnp.where(i == 1, j, i - 2)
        prev_sel = jnp.where(i == 1, nb, i - 2)
        row0 = pl.multiple_of(jnp.minimum(i, nb - 1) * blk, blk)
        ones = jnp.ones((PV_ONES_ROWS, blk), BF16)
        pvs = []
        for hd in range(A_HEADS):
            pr = hd // 2
            kt = k_ref[0, pl.ds(row0, blk), pr * pw:(pr + 1) * pw]
            raw_ref[nxt, hd] = _dot(kt, qm_ref[hd]).astype(BF16)
            pvs.append(_dot(jnp.concatenate([vT_ref[0, prev, hd * dh:(hd + 1) * dh, :], ones], axis=0),
                            p_ref[nxt, hd]))
            s = raw_ref[cur, hd]
            selb = sel_ref[hd, pl.ds(n, 1), :] > 0.5
            smax = jnp.where(selb, jnp.max(s, axis=0, keepdims=True).astype(F32), NEG)
            m_old = m_ref[hd]
            m_new = jnp.maximum(m_old, smax)
            alpha_ref[hd] = jnp.exp2(m_old - m_new)
            m_ref[hd] = m_new
            p_ref[cur, hd] = jnp.exp2(s - m_new.astype(BF16))
        for hd in range(A_HEADS):
            keep = sel_ref[hd, pl.ds(prev_sel, 1), :] > 0.5
            acc_ref[hd] = alpha_ref[hd] * (acc_ref[hd] + jnp.where(keep, pvs[hd], 0.0))

    def body(ii, carry):
        step(2 * ii + 1, 1)
        step(2 * ii + 2, 0)
        return carry

    trips = (j + 1) // 2
    lax.fori_loop(0, trips, body, 0)

    last = jnp.where(j == 0, j, 2 * trips - 1)
    last_sel = jnp.where(j == 0, nb, 2 * trips - 1)
    pvs = value_dots(last, 0)
    for pr in range(A_HEADS // 2):
        halves = []
        for hd in (2 * pr, 2 * pr + 1):
            keep = sel_ref[hd, pl.ds(last_sel, 1), :] > 0.5
            acc = acc_ref[hd] + jnp.where(keep, pvs[hd], 0.0)
            halves.append(acc[0:dh] / acc[dh:dh + 1])
        oT = jnp.concatenate(halves, axis=0)
        oT = oT * gT_ref[0, 0, pr * pw:(pr + 1) * pw, :].astype(F32)
        o_ref[0, :, pr * pw:(pr + 1) * pw] = oT.T.astype(BF16)


def _moba(qT, k, vT, gT):
    b, nb, aw, blk = qT.shape
    s = nb * blk
    pw = 2 * A_HEAD_DIM
    q_spec = pl.BlockSpec((1, 1, aw, blk), lambda i, j: (i, j, 0, 0))
    row = lambda: pltpu.VMEM((A_HEADS, 1, blk), F32)
    return pl.pallas_call(
        functools.partial(_moba_kernel, nb=nb),
        grid=(b, nb),
        in_specs=[
            q_spec,
            pl.BlockSpec((1, s, aw), lambda i, j: (i, 0, 0)),
            pl.BlockSpec((1, nb, aw, blk), lambda i, j: (i, 0, 0, 0)),
            q_spec,
        ],
        out_specs=pl.BlockSpec((1, blk, aw), lambda i, j: (i, j, 0)),
        out_shape=jax.ShapeDtypeStruct((b, s, aw), BF16),
        scratch_shapes=[
            pltpu.VMEM((nb, aw), F32),
            pltpu.VMEM((A_HEADS, nb + 8, blk), F32),
            pltpu.VMEM((A_HEADS, pw, blk), BF16),
            pltpu.VMEM((2, A_HEADS, blk, blk), BF16),
            pltpu.VMEM((2, A_HEADS, blk, blk), BF16),
            row(), row(),
            pltpu.VMEM((A_HEADS, A_HEAD_DIM + PV_ONES_ROWS, blk), F32),
        ],
        compiler_params=pltpu.CompilerParams(
            dimension_semantics=("parallel", "arbitrary"), vmem_limit_bytes=VMEM_LIMIT),
        name="moba",
    )(qT, k, vT, gT)


def _mlstm_kernel(bx_ref, sbz_ref, cw_ref, cb_ref, wqt_ref, wk_ref, wvt_ref, wgq_ref, wgk_ref,
                  wgv_ref, bg_ref, og_ref, skip_ref, o_ref, xbuf_ref, state_ref, m_ref):
    L = MLSTM_L
    dh = B_HEAD_DIM
    c = pl.program_id(1)

    @pl.when(c == 0)
    def _():
        xbuf_ref[:, 0:CONV_HALO, :] = jnp.zeros((MLSTM_NB, CONV_HALO, B_WIDTH), F32)
        state_ref[...] = jnp.zeros_like(state_ref)
        m_ref[...] = jnp.zeros_like(m_ref)

    cw = cw_ref[...]
    cb = cb_ref[...]
    kscale = dh ** -0.5
    s_iota = lax.broadcasted_iota(jnp.int32, (L, L), 0)
    t_iota = lax.broadcasted_iota(jnp.int32, (L, L), 1)
    tri = s_iota <= t_iota
    tri_b = jnp.where(tri, 1.0, 0.0).astype(BF16)
    ones = jnp.ones((dh, L), F32)

    def project(bi):
        bx_b = bx_ref[bi]
        xbuf_ref[bi, CONV_HALO:CONV_HALO + L, :] = bx_b.astype(F32)
        st = dict(xc=[], ks=[], vT=[], scores=[], inter=[], state=[])
        gT = bg_ref[...]
        for hd in range(B_HEADS):
            lo = hd * dh
            conv = cb[:, lo:lo + dh]
            for i in range(B_CONV):
                off = CONV_HALO - (B_CONV - 1) + i
                conv = conv + cw[i:i + 1, lo:lo + dh] * xbuf_ref[bi, off:off + L, lo:lo + dh]
            xc = _silu(conv)
            xc_b = xc.astype(BF16)
            qT_h = _dot_nt(wqt_ref[hd], xc_b).astype(BF16)
            vT_h = _dot_nt(wvt_ref[hd], bx_b[:, lo:lo + dh])
            k_h = _dot(xc_b, wk_ref[hd])
            ks_h = (k_h * kscale).astype(BF16)
            state = state_ref[bi, hd]
            st["scores"].append(_dot(ks_h, qT_h))
            st["inter"].append(_dot(state.astype(BF16), qT_h))
            gT = gT + (_dot(wgq_ref[:, lo:lo + dh], qT_h)
                       + _dot_nt(wgk_ref[:, lo:lo + dh], k_h.astype(BF16))
                       + _dot(wgv_ref[:, lo:lo + dh], vT_h.astype(BF16)))
            st["xc"].append(xc); st["ks"].append(ks_h); st["vT"].append(vT_h); st["state"].append(state)
        tail = xbuf_ref[bi, L:L + CONV_HALO, :]
        xbuf_ref[bi, 0:CONV_HALO, :] = tail
        st["gT"] = gT
        return st

    def gate_chain(st):
        gT = st["gT"]
        lf_hi, lf_lo = _split_bf16(_log_sigmoid(gT))
        cum = (_dot(lf_hi, tri_b) + _dot(lf_lo, tri_b))[B_HEADS:2 * B_HEADS]
        a = gT[0:B_HEADS] - cum
        st["cum"], st["a"] = cum, a
        st["a_cols"] = jnp.concatenate([a, jnp.zeros((128 - B_HEADS, L), F32)], axis=0).T

    def recur(bi, st):
        a, cum, a_cols = st["a"], st["cum"], st["a_cols"]
        for hd in range(B_HEADS):
            lo = hd * dh
            ks_h, state, inter = st["ks"][hd], st["state"][hd], st["inter"][hd]
            vT_aug = jnp.concatenate([st["vT"][hd], ones], axis=0)
            a_row = a[hd:hd + 1]
            a_col = a_cols[:, hd:hd + 1]
            cum_row = cum[hd:hd + 1]
            m_prev = m_ref[bi, hd][0:1, 0:1]
            amax = jnp.max(jnp.where(tri, a_col, NEG), axis=0, keepdims=True)
            b_row = jnp.maximum(m_prev, amax)
            dmat = jnp.exp(jnp.where(tri, a_col - b_row, NEG))
            w_intra = (st["scores"][hd] * dmat).astype(BF16)
            intra = _dot(vT_aug.astype(BF16), w_intra)
            w_inter = jnp.exp(m_prev - b_row)
            num = w_inter * inter[0:dh] + intra[0:dh]
            den = w_inter * inter[dh:dh + 1] + intra[dh:dh + 1]
            hc = num * (1.0 / jnp.maximum(jnp.abs(den), jnp.exp(-(cum_row + b_row))))
            mu = jnp.mean(hc, axis=0, keepdims=True)
            hcc = hc - mu
            var = jnp.mean(hcc * hcc, axis=0, keepdims=True)
            hb = (hcc * lax.rsqrt(var + 1e-5)).T * og_ref[:, lo:lo + dh]
            yb = (hb + skip_ref[:, lo:lo + dh] * st["xc"][hd]) * sbz_ref[bi, :, lo:lo + dh].astype(F32)
            o_ref[bi, :, lo:lo + dh] = yb.astype(BF16)
            b_end = b_row[:, L - 1:L]
            ws = jnp.exp(a_row - b_end)
            decay = jnp.exp(m_prev - b_end)
            state_ref[bi, hd] = decay * state + _dot((vT_aug * ws).astype(BF16), ks_h)
            m_ref[bi, hd] = jnp.broadcast_to(cum_row[:, L - 1:L] + b_end, m_ref.shape[2:])

    sts = []
    for bi in range(MLSTM_NB):
        sts.append(project(bi))
        gate_chain(sts[bi])
    for bi in range(MLSTM_NB):
        recur(bi, sts[bi])


def _block_diag_dense(w):
    nblk, blk, _ = w.shape
    n = nblk * blk
    idx = jnp.arange(n) // blk
    return jnp.where(idx[:, None] == idx[None, :], jnp.tile(w.reshape(n, blk), (1, nblk)), 0.0)


def _mlstm(bx, sbz, conv_w, conv_b, wq, wk, wv, w_gates, b_gates, out_g, skip):
    b, s, bw = bx.shape
    L = MLSTM_L
    nc = s // L
    nbb = MLSTM_NB
    assert b % nbb == 0 and s % L == 0
    per_head = wq.shape[0] // B_HEADS

    def head_blocks(w, transpose):
        blocks = [_block_diag_dense(w[h * per_head:(h + 1) * per_head]) for h in range(B_HEADS)]
        return jnp.stack([blk.T if transpose else blk for blk in blocks]).astype(BF16)

    wqt_d = head_blocks(wq, True)
    wk_d = head_blocks(wk, False)
    wvt_d = head_blocks(wv, True)
    hblk = (B_HEADS, B_HEAD_DIM, B_HEAD_DIM)
    wgq = w_gates[:bw].T.astype(BF16)
    wgv = w_gates[2 * bw:].T.astype(BF16)
    wgk = w_gates[bw:2 * bw].T.astype(BF16)
    const = lambda shape: pl.BlockSpec(shape, lambda i, j: (0,) * len(shape))
    tok = pl.BlockSpec((nbb, L, bw), lambda i, j: (i, j, 0))
    return pl.pallas_call(
        _mlstm_kernel,
        grid=(b // nbb, nc),
        in_specs=[tok, tok, const((B_CONV, bw)), const((1, bw)),
                  const(hblk), const(hblk), const(hblk),
                  const((GATE_ROWS, bw)), const((GATE_ROWS, bw)), const((GATE_ROWS, bw)),
                  const((GATE_ROWS, 1)), const((1, bw)), const((1, bw))],
        out_specs=tok,
        out_shape=jax.ShapeDtypeStruct((b, s, bw), BF16),
        scratch_shapes=[pltpu.VMEM((nbb, L + CONV_HALO, bw), F32),
                        pltpu.VMEM((nbb, B_HEADS, 2 * B_HEAD_DIM, B_HEAD_DIM), F32),
                        pltpu.VMEM((nbb, B_HEADS, 8, 128), F32)],
        compiler_params=pltpu.CompilerParams(
            dimension_semantics=("parallel", "arbitrary"), vmem_limit_bytes=VMEM_LIMIT),
        name="mlstm",
    )(bx, sbz, conv_w, conv_b.reshape(1, bw), wqt_d, wk_d, wvt_d, wgq, wgk, wgv,
      b_gates.reshape(GATE_ROWS, 1), out_g.reshape(1, bw), skip.reshape(1, bw))


def _tail_kernel(x_ref, ya_ref, yb_ref, mod0_ref, mod1_ref, lng_ref, wo0_ref, wi1_ref, clg_ref,
                 clb_ref, ws_ref, bst_ref, wo1_ref, o_ref):
    tm = x_ref.shape[1]
    sub = TAIL_SUB_ROWS
    n_sub = tm // sub
    aw = A_WIDTH
    mod1 = mod1_ref[0]
    gate0 = mod0_ref[0][2:3]
    t_iota = lax.broadcasted_iota(jnp.int32, (C_CHUNK, C_CHUNK), 0)
    s_iota = lax.broadcasted_iota(jnp.int32, (C_CHUNK, C_CHUNK), 1)
    tril = s_iota <= t_iota
    gw = C_WIDTH // C_GROUPS
    bst = bst_ref[...]
    wms = [jnp.where(tril, ws_ref[g], 0.0).astype(BF16) for g in range(C_GROUPS)]

    def out_proj0(r):
        rows = slice(r * sub, (r + 1) * sub)
        y0 = _dot(ya_ref[0, rows, :], wo0_ref[0:aw, :]) + _dot(yb_ref[0, rows, :], wo0_ref[aw:, :])
        x1 = x_ref[0, rows, :] + gate0 * y0
        h = _adaln_rmsnorm(x1, lng_ref[...], mod1[1:2], mod1[0:1]).astype(BF16)
        return x1, h

    def mix_inputs(p):
        u = _gelu_tanh(p[:, :C_WIDTH])
        v = _gelu_tanh(p[:, C_WIDTH:2 * C_WIDTH])
        mu = jnp.mean(v, axis=-1, keepdims=True)
        vc = v - mu
        var = jnp.mean(vc * vc, axis=-1, keepdims=True)
        vn = ((vc * lax.rsqrt(var + 1e-5)) * clg_ref[...] + clb_ref[...]).astype(BF16)
        return vn, u * _silu(p[:, 2 * C_WIDTH:])

    def spatial_gate(vn, gate):
        cols = []
        for g in range(C_GROUPS):
            rows = []
            for ch in range(sub // C_CHUNK):
                vg = vn[ch * C_CHUNK:(ch + 1) * C_CHUNK, g * gw:(g + 1) * gw]
                rows.append(_dot(wms[g], vg) + bst[:, g:g + 1])
            cols.append(jnp.concatenate(rows, axis=0))
        return (gate * jnp.concatenate(cols, axis=1)).astype(BF16)

    def out_proj1(r, x1, y1):
        o_ref[0, r * sub:(r + 1) * sub, :] = x1 + mod1[2:3] * _dot(y1, wo1_ref[...])

    heads = [out_proj0(r) for r in range(n_sub)]
    projs = [_dot(heads[r][1], wi1_ref[...]) for r in range(n_sub)]
    y1_prev = None
    for r in range(n_sub):
        vn, gate = mix_inputs(projs[r])
        if y1_prev is not None:
            out_proj1(r - 1, heads[r - 1][0], y1_prev)
        y1_prev = spatial_gate(vn, gate)
    out_proj1(n_sub - 1, heads[n_sub - 1][0], y1_prev)


def _tail(x, ya, yb, mod0, mod1, ln_g1, w_out0, w_in1, w_out1, c_ln_g, c_ln_b, c_ws, c_bs):
    b, s, d = x.shape
    tm = TAIL_ROW_TILE
    nt = s // tm
    const = lambda shape: pl.BlockSpec(shape, lambda i, j: (0,) * len(shape))
    half = pl.BlockSpec((1, tm, A_WIDTH), lambda i, j: (i, j, 0))
    full = pl.BlockSpec((1, tm, d), lambda i, j: (i, j, 0))
    modspec = pl.BlockSpec((1, 3, d), lambda i, j: (i, 0, 0))
    return pl.pallas_call(
        _tail_kernel,
        grid=(b, nt),
        in_specs=[full, half, half, modspec, modspec, const((1, d)), const((d, d)),
                  const((d, 3 * C_WIDTH)), const((1, d)), const((1, d)),
                  const((C_GROUPS, C_CHUNK, C_CHUNK)), const((C_CHUNK, C_GROUPS)), const((d, d))],
        out_specs=full,
        out_shape=jax.ShapeDtypeStruct((b, s, d), F32),
        compiler_params=pltpu.CompilerParams(
            dimension_semantics=("parallel", "parallel"), vmem_limit_bytes=VMEM_LIMIT),
        name="tail",
    )(x, ya, yb, mod0, mod1, ln_g1.reshape(1, d), w_out0.astype(BF16), w_in1.astype(BF16),
      c_ln_g.reshape(1, d), c_ln_b.reshape(1, d), c_ws, c_bs.T, w_out1.astype(BF16))


def kernel(x, c, ln_g, ada_w, ada_b, w_in, w_out, a_q_g, a_k_g, b_conv_w, b_conv_b, b_wq, b_wk, b_wv,
           b_w_gates, b_b_gates, b_out_g, b_skip, c_ln_g, c_ln_b, c_ws, c_bs):
    mods = _ada_mods(c, ada_w, ada_b)
    qT, vT, gT, k, bx, sbz = _inproj0(x, mods[0], ln_g[0], w_in[0], a_q_g[0], a_k_g[0])
    ya = _moba(qT, k, vT, gT)
    yb = _mlstm(bx, sbz, b_conv_w[0], b_conv_b[0], b_wq[0], b_wk[0], b_wv[0], b_w_gates[0],
                b_b_gates[0], b_out_g[0], b_skip[0])
    return _tail(x, ya, yb, mods[0], mods[1], ln_g[1], w_out[0], w_in[1], w_out[1],
                 c_ln_g[0], c_ln_b[0], c_ws[0], c_bs[0])
```

```python
import functools

import jax
import jax.numpy as jnp
from jax import lax
from jax.experimental import pallas as pl
from jax.experimental.pallas import tpu as pltpu

F32 = jnp.float32
BF16 = jnp.bfloat16

D_MODEL = 1024
A_HEADS = 8
A_HEAD_DIM = 64
A_WIDTH = A_HEADS * A_HEAD_DIM
MOBA_BLOCK = 256
MOBA_TOPK = 3
B_HEADS = 4
B_HEAD_DIM = 128
B_WIDTH = B_HEADS * B_HEAD_DIM
B_CONV = 4
C_GROUPS = 8
C_CHUNK = 128
C_WIDTH = D_MODEL
NEG = -1e30
LOG2E = 1.4426950408889634

MLSTM_L = 256
MLSTM_NB = 4
INPROJ_SUBTILES = 2
TAIL_ROW_TILE = 512
TAIL_SUB_ROWS = 256
CONV_HALO = 8
GATE_ROWS = 2 * B_HEADS
PV_ONES_ROWS = 16
MOBA_NB = 2
VMEM_LIMIT = 48 * 1024 * 1024
MOBA_VMEM_LIMIT = 56 * 1024 * 1024


def _silu(x):
    return x * jax.nn.sigmoid(x)


def _gelu_tanh(x):
    return 0.5 * x * (1.0 + jnp.tanh(0.7978845608028654 * (x + 0.044715 * (x * x * x))))


def _log_sigmoid(x):
    return jnp.minimum(x, 0.0) - jnp.log(1.0 + jnp.exp(-jnp.abs(x)))


def _split_bf16(x):
    hi = x.astype(BF16)
    lo = (x - hi.astype(F32)).astype(BF16)
    return hi, lo


def _dot(a, b):
    return jnp.dot(a, b, preferred_element_type=F32)


def _dot_nt(a, b):
    return lax.dot_general(a, b, (((1,), (1,)), ((), ())), preferred_element_type=F32)


def _adaln_rmsnorm(x, ln_g, scale, shift):
    y = x * lax.rsqrt(jnp.mean(x * x, axis=-1, keepdims=True) + 1e-6)
    return (y * ln_g) * (1.0 + scale) + shift


def _ada_kernel(c_ref, w_ref, b_ref, o_ref):
    cs_hi, cs_lo = _split_bf16(_silu(c_ref[...]))
    w_hi, w_lo = _split_bf16(w_ref[0])
    o_ref[0] = _dot(cs_hi, w_hi) + _dot(cs_lo, w_hi) + _dot(cs_hi, w_lo) + b_ref[0]


def _ada_mods(c, ada_w, ada_b):
    depth, d, d3 = ada_w.shape
    b = c.shape[0]
    bp = 8
    cp = jnp.zeros((bp, d), F32).at[:b].set(c)
    nt = d3 // d
    out = pl.pallas_call(
        _ada_kernel,
        grid=(depth, nt),
        in_specs=[
            pl.BlockSpec((bp, d), lambda l, n: (0, 0)),
            pl.BlockSpec((1, d, d), lambda l, n: (l, 0, n)),
            pl.BlockSpec((1, 1, d), lambda l, n: (l, 0, n)),
        ],
        out_specs=pl.BlockSpec((1, bp, d), lambda l, n: (l, 0, n)),
        out_shape=jax.ShapeDtypeStruct((depth, bp, d3), F32),
        name="ada_mods",
    )(cp, ada_w, ada_b.reshape(depth, 1, d3))
    return out[:, :b].reshape(depth, b, 3, d)


def _inproj0_kernel(x_ref, mod_ref, lng_ref, wt_ref, w_ref, qg_ref, kg_ref,
                    qT_ref, vT_ref, gT_ref, k_ref, bx_ref, sbz_ref):
    tm = MOBA_BLOCK
    aw = A_WIDTH
    mod = mod_ref[0]
    qg = qg_ref[...] * (A_HEAD_DIM ** -0.5 * LOG2E)
    kg = kg_ref[...]
    hs = [_adaln_rmsnorm(x_ref[0, r * tm:(r + 1) * tm], lng_ref[...], mod[1:2], mod[0:1]).astype(BF16)
          for r in range(INPROJ_SUBTILES)]
    for r in range(INPROJ_SUBTILES):
        h = hs[r]
        rows = slice(r * tm, (r + 1) * tm)
        pt = _dot_nt(wt_ref[...], h)
        p = _dot(h, w_ref[...])
        kn = []
        for hd in range(A_HEADS):
            lo = hd * A_HEAD_DIM
            q = pt[lo:lo + A_HEAD_DIM]
            rq = lax.rsqrt(jnp.mean(q * q, axis=0, keepdims=True) + 1e-6)
            qT_ref[0, r, lo:lo + A_HEAD_DIM, :] = ((q * rq) * qg).astype(BF16)
            k = pt[aw + lo:aw + lo + A_HEAD_DIM]
            rk = lax.rsqrt(jnp.mean(k * k, axis=0, keepdims=True) + 1e-6)
            kn.append((k * rk) * kg)
        k_ref[0, rows, :] = jnp.concatenate(kn, axis=0).T.astype(BF16)
        vT_ref[0, r] = pt[2 * aw:3 * aw].astype(BF16)
        gT_ref[0, r] = _silu(pt[3 * aw:]).astype(BF16)
        bx_ref[0, rows, :] = p[:, :B_WIDTH].astype(BF16)
        sbz_ref[0, rows, :] = _silu(p[:, B_WIDTH:]).astype(BF16)


def _inproj0(x, mod, ln_g, w_in, q_g, k_g):
    b, s, d = x.shape
    blk = MOBA_BLOCK
    sub = INPROJ_SUBTILES
    tm = sub * blk
    nt = s // tm
    aw, bw = A_WIDTH, B_WIDTH
    wq, wk, wv, wz = (w_in[:, i * aw:(i + 1) * aw] for i in range(4))
    wbx = w_in[:, 4 * aw:4 * aw + bw]
    wbz = w_in[:, 4 * aw + bw:]
    wt = jnp.concatenate([wq, wk, wv, wz], axis=1).T.astype(BF16)
    wn = jnp.concatenate([wbx, wbz], axis=1).astype(BF16)
    t_shape = jax.ShapeDtypeStruct((b, s // blk, aw, blk), BF16)
    n_shape = jax.ShapeDtypeStruct((b, s, aw), BF16)
    t_spec = pl.BlockSpec((1, sub, aw, blk), lambda i, j: (i, j, 0, 0))
    n_spec = pl.BlockSpec((1, tm, aw), lambda i, j: (i, j, 0))
    const = lambda shape: pl.BlockSpec(shape, lambda i, j: (0,) * len(shape))
    return pl.pallas_call(
        _inproj0_kernel,
        grid=(b, nt),
        in_specs=[
            pl.BlockSpec((1, tm, d), lambda i, j: (i, j, 0)),
            pl.BlockSpec((1, 3, d), lambda i, j: (i, 0, 0)),
            const((1, d)),
            const((4 * aw, d)),
            const((d, 2 * bw)),
            const((A_HEAD_DIM, 1)),
            const((A_HEAD_DIM, 1)),
        ],
        out_specs=[t_spec, t_spec, t_spec, n_spec, n_spec, n_spec],
        out_shape=[t_shape, t_shape, t_shape, n_shape, n_shape, n_shape],
        compiler_params=pltpu.CompilerParams(
            dimension_semantics=("parallel", "parallel"), vmem_limit_bytes=VMEM_LIMIT),
        name="inproj0",
    )(x, mod, ln_g.reshape(1, d), wt, wn, q_g.reshape(A_HEAD_DIM, 1), k_g.reshape(A_HEAD_DIM, 1))


def _moba_kernel(qT_ref, k_ref, vT_ref, gT_ref, o_ref, kmean_ref, sel_ref, qm_ref, raw_ref, p_ref,
                 m_ref, alpha_ref, acc_ref, *, nb):
    blk = MOBA_BLOCK
    dh = A_HEAD_DIM
    pw = 2 * dh
    streams = range(MOBA_NB)
    j = pl.program_id(1)

    @pl.when(j == 0)
    def _():
        for z in streams:
            for n in range(nb):
                kb = k_ref[z, n * blk:(n + 1) * blk, :].astype(F32)
                kmean_ref[z, n:n + 1, :] = jnp.mean(kb, axis=0, keepdims=True)

    d_iota = lax.broadcasted_iota(jnp.int32, (pw, blk), 0)
    for z in streams:
        for hd in range(A_HEADS):
            pr = hd // 2
            q2 = qT_ref[z, 0, pr * pw:(pr + 1) * pw, :]
            keep = (d_iota < dh) if hd % 2 == 0 else (d_iota >= dh)
            qm_ref[z, hd] = jnp.where(keep, q2, jnp.zeros_like(q2))

    def score_dots(z, n, slot):
        row0 = pl.multiple_of(n * blk, blk)
        for hd in range(A_HEADS):
            pr = hd // 2
            kt = k_ref[z, pl.ds(row0, blk), pr * pw:(pr + 1) * pw]
            raw_ref[z, slot, hd] = _dot(kt, qm_ref[z, hd]).astype(BF16)

    def value_dots(z, n, slot):
        ones = jnp.ones((PV_ONES_ROWS, blk), BF16)
        return [_dot(jnp.concatenate([vT_ref[z, n, hd * dh:(hd + 1) * dh, :], ones], axis=0),
                     p_ref[z, slot, hd]) for hd in range(A_HEADS)]

    sel_scores = []
    for z in streams:
        km_hi, km_lo = _split_bf16(kmean_ref[z])
        km2 = jnp.concatenate([km_hi, km_lo], axis=0)
        per_head = []
        for hd in range(A_HEADS):
            pr = hd // 2
            r2 = _dot(km2[:, pr * pw:(pr + 1) * pw], qm_ref[z, hd])
            per_head.append(r2[0:nb] + r2[nb:2 * nb])
        sel_scores.append(per_head)

    for z in streams:
        score_dots(z, j, 0)
    for z in streams:
        score_dots(z, 0, 1)

    n_iota = lax.broadcasted_iota(jnp.int32, (nb, blk), 0)
    past = n_iota < j
    for z in streams:
        for hd in range(A_HEADS):
            sc = jnp.where(past, sel_scores[z][hd], NEG)
            sel = jnp.zeros((nb, blk), F32)
            for _ in range(MOBA_TOPK):
                mx = jnp.max(sc, axis=0, keepdims=True)
                first = jnp.min(jnp.where(sc == mx, n_iota, nb), axis=0, keepdims=True)
                pick = n_iota == first
                sel = jnp.where(pick, 1.0, sel)
                sc = jnp.where(pick, -jnp.inf, sc)
            sel_ref[z, hd, 0:nb, :] = jnp.where(past, sel, 0.0)
            sel_ref[z, hd, nb:nb + 1, :] = jnp.ones((1, blk), F32)

    k_iota = lax.broadcasted_iota(jnp.int32, (blk, blk), 0)
    q_iota = lax.broadcasted_iota(jnp.int32, (blk, blk), 1)
    causal_bias = jnp.where(k_iota <= q_iota, 0.0, NEG).astype(BF16)
    for z in streams:
        for hd in range(A_HEADS):
            s = raw_ref[z, 0, hd] + causal_bias
            m_new = jnp.max(s, axis=0, keepdims=True)
            p_ref[z, 0, hd] = jnp.exp2(s - m_new)
            m_ref[z, hd] = m_new.astype(F32)
            acc_ref[z, hd] = jnp.zeros(acc_ref.shape[2:], F32)

    def step(z, i, cur):
        nxt = 1 - cur
        n = i - 1
        prev = jnp.where(i == 1, j, i - 2)
        prev_sel = jnp.where(i == 1, nb, i - 2)
        row0 = pl.multiple_of(jnp.minimum(i, nb - 1) * blk, blk)
        ones = jnp.ones((PV_ONES_ROWS, blk), BF16)
        pvs = []
        for hd in range(A_HEADS):
            pr = hd // 2
            kt = k_ref[z, pl.ds(row0, blk), pr * pw:(pr + 1) * pw]
            raw_ref[z, nxt, hd] = _dot(kt, qm_ref[z, hd]).astype(BF16)
            pvs.append(_dot(jnp.concatenate([vT_ref[z, prev, hd * dh:(hd + 1) * dh, :], ones], axis=0),
                            p_ref[z, nxt, hd]))
            s = raw_ref[z, cur, hd]
            selb = sel_ref[z, hd, pl.ds(n, 1), :] > 0.5
            smax = jnp.where(selb, jnp.max(s, axis=0, keepdims=True).astype(F32), NEG)
            m_old = m_ref[z, hd]
            m_new = jnp.maximum(m_old, smax)
            alpha_ref[z, hd] = jnp.exp2(m_old - m_new)
            m_ref[z, hd] = m_new
            p_ref[z, cur, hd] = jnp.exp2(s - m_new.astype(BF16))
        for hd in range(A_HEADS):
            keep = sel_ref[z, hd, pl.ds(prev_sel, 1), :] > 0.5
            acc_ref[z, hd] = alpha_ref[z, hd] * (acc_ref[z, hd] + jnp.where(keep, pvs[hd], 0.0))

    def body(ii, carry):
        for z in streams:
            step(z, 2 * ii + 1, 1)
        for z in streams:
            step(z, 2 * ii + 2, 0)
        return carry

    trips = (j + 1) // 2
    lax.fori_loop(0, trips, body, 0)

    last = jnp.where(j == 0, j, 2 * trips - 1)
    last_sel = jnp.where(j == 0, nb, 2 * trips - 1)
    last_pvs = [value_dots(z, last, 0) for z in streams]
    for z in streams:
        for pr in range(A_HEADS // 2):
            halves = []
            for hd in (2 * pr, 2 * pr + 1):
                keep = sel_ref[z, hd, pl.ds(last_sel, 1), :] > 0.5
                acc = acc_ref[z, hd] + jnp.where(keep, last_pvs[z][hd], 0.0)
                halves.append(acc[0:dh] / acc[dh:dh + 1])
            oT = jnp.concatenate(halves, axis=0)
            oT = oT * gT_ref[z, 0, pr * pw:(pr + 1) * pw, :].astype(F32)
            o_ref[z, :, pr * pw:(pr + 1) * pw] = oT.T.astype(BF16)


def _moba(qT, k, vT, gT):
    b, nb, aw, blk = qT.shape
    s = nb * blk
    pw = 2 * A_HEAD_DIM
    z = MOBA_NB
    assert b % z == 0
    q_spec = pl.BlockSpec((z, 1, aw, blk), lambda i, j: (i, j, 0, 0))
    row = lambda: pltpu.VMEM((z, A_HEADS, 1, blk), F32)
    return pl.pallas_call(
        functools.partial(_moba_kernel, nb=nb),
        grid=(b // z, nb),
        in_specs=[
            q_spec,
            pl.BlockSpec((z, s, aw), lambda i, j: (i, 0, 0)),
            pl.BlockSpec((z, nb, aw, blk), lambda i, j: (i, 0, 0, 0)),
            q_spec,
        ],
        out_specs=pl.BlockSpec((z, blk, aw), lambda i, j: (i, j, 0)),
        out_shape=jax.ShapeDtypeStruct((b, s, aw), BF16),
        scratch_shapes=[
            pltpu.VMEM((z, nb, aw), F32),
            pltpu.VMEM((z, A_HEADS, nb + 8, blk), F32),
            pltpu.VMEM((z, A_HEADS, pw, blk), BF16),
            pltpu.VMEM((z, 2, A_HEADS, blk, blk), BF16),
            pltpu.VMEM((z, 2, A_HEADS, blk, blk), BF16),
            row(), row(),
            pltpu.VMEM((z, A_HEADS, A_HEAD_DIM + PV_ONES_ROWS, blk), F32),
        ],
        compiler_params=pltpu.CompilerParams(
            dimension_semantics=("parallel", "arbitrary"), vmem_limit_bytes=MOBA_VMEM_LIMIT),
        name="moba",
    )(qT, k, vT, gT)


def _mlstm_kernel(bx_ref, sbz_ref, cw_ref, cb_ref, wqt_ref, wk_ref, wvt_ref, wgq_ref, wgk_ref,
                  wgv_ref, bg_ref, og_ref, skip_ref, o_ref, xbuf_ref, state_ref, m_ref):
    L = MLSTM_L
    dh = B_HEAD_DIM
    c = pl.program_id(1)

    @pl.when(c == 0)
    def _():
        xbuf_ref[:, 0:CONV_HALO, :] = jnp.zeros((MLSTM_NB, CONV_HALO, B_WIDTH), F32)
        state_ref[...] = jnp.zeros_like(state_ref)
        m_ref[...] = jnp.zeros_like(m_ref)

    cw = cw_ref[...]
    cb = cb_ref[...]
    kscale = dh ** -0.5
    s_iota = lax.broadcasted_iota(jnp.int32, (L, L), 0)
    t_iota = lax.broadcasted_iota(jnp.int32, (L, L), 1)
    tri = s_iota <= t_iota
    tri_b = jnp.where(tri, 1.0, 0.0).astype(BF16)
    ones = jnp.ones((dh, L), F32)

    def project(bi):
        bx_b = bx_ref[bi]
        xbuf_ref[bi, CONV_HALO:CONV_HALO + L, :] = bx_b.astype(F32)
        st = dict(xc=[], ks=[], vT=[], scores=[], inter=[], state=[])
        gT = bg_ref[...]
        for hd in range(B_HEADS):
            lo = hd * dh
            conv = cb[:, lo:lo + dh]
            for i in range(B_CONV):
                off = CONV_HALO - (B_CONV - 1) + i
                conv = conv + cw[i:i + 1, lo:lo + dh] * xbuf_ref[bi, off:off + L, lo:lo + dh]
            xc = _silu(conv)
            xc_b = xc.astype(BF16)
            qT_h = _dot_nt(wqt_ref[hd], xc_b).astype(BF16)
            vT_h = _dot_nt(wvt_ref[hd], bx_b[:, lo:lo + dh])
            k_h = _dot(xc_b, wk_ref[hd])
            ks_h = (k_h * kscale).astype(BF16)
            state = state_ref[bi, hd]
            st["scores"].append(_dot(ks_h, qT_h))
            st["inter"].append(_dot(state.astype(BF16), qT_h))
            gT = gT + (_dot(wgq_ref[:, lo:lo + dh], qT_h)
                       + _dot_nt(wgk_ref[:, lo:lo + dh], k_h.astype(BF16))
                       + _dot(wgv_ref[:, lo:lo + dh], vT_h.astype(BF16)))
            st["xc"].append(xc); st["ks"].append(ks_h); st["vT"].append(vT_h); st["state"].append(state)
        tail = xbuf_ref[bi, L:L + CONV_HALO, :]
        xbuf_ref[bi, 0:CONV_HALO, :] = tail
        st["gT"] = gT
        return st

    def gate_chain(st):
        gT = st["gT"]
        lf_hi, lf_lo = _split_bf16(_log_sigmoid(gT))
        cum = (_dot(lf_hi, tri_b) + _dot(lf_lo, tri_b))[B_HEADS:2 * B_HEADS]
        a = gT[0:B_HEADS] - cum
        st["cum"], st["a"] = cum, a
        st["a_cols"] = jnp.concatenate([a, jnp.zeros((128 - B_HEADS, L), F32)], axis=0).T

    def recur(bi, st):
        a, cum, a_cols = st["a"], st["cum"], st["a_cols"]
        for hd in range(B_HEADS):
            lo = hd * dh
            ks_h, state, inter = st["ks"][hd], st["state"][hd], st["inter"][hd]
            vT_aug = jnp.concatenate([st["vT"][hd], ones], axis=0)
            a_row = a[hd:hd + 1]
            a_col = a_cols[:, hd:hd + 1]
            cum_row = cum[hd:hd + 1]
            m_prev = m_ref[bi, hd][0:1, 0:1]
            amax = jnp.max(jnp.where(tri, a_col, NEG), axis=0, keepdims=True)
            b_row = jnp.maximum(m_prev, amax)
            dmat = jnp.exp(jnp.where(tri, a_col - b_row, NEG))
            w_intra = (st["scores"][hd] * dmat).astype(BF16)
            intra = _dot(vT_aug.astype(BF16), w_intra)
            w_inter = jnp.exp(m_prev - b_row)
            num = w_inter * inter[0:dh] + intra[0:dh]
            den = w_inter * inter[dh:dh + 1] + intra[dh:dh + 1]
            hc = num * (1.0 / jnp.maximum(jnp.abs(den), jnp.exp(-(cum_row + b_row))))
            mu = jnp.mean(hc, axis=0, keepdims=True)
            hcc = hc - mu
            var = jnp.mean(hcc * hcc, axis=0, keepdims=True)
            hb = (hcc * lax.rsqrt(var + 1e-5)).T * og_ref[:, lo:lo + dh]
            yb = (hb + skip_ref[:, lo:lo + dh] * st["xc"][hd]) * sbz_ref[bi, :, lo:lo + dh].astype(F32)
            o_ref[bi, :, lo:lo + dh] = yb.astype(BF16)
            b_end = b_row[:, L - 1:L]
            ws = jnp.exp(a_row - b_end)
            decay = jnp.exp(m_prev - b_end)
            state_ref[bi, hd] = decay * state + _dot((vT_aug * ws).astype(BF16), ks_h)
            m_ref[bi, hd] = jnp.broadcast_to(cum_row[:, L - 1:L] + b_end, m_ref.shape[2:])

    sts = []
    for bi in range(MLSTM_NB):
        sts.append(project(bi))
        gate_chain(sts[bi])
    for bi in range(MLSTM_NB):
        recur(bi, sts[bi])


def _block_diag_dense(w):
    nblk, blk, _ = w.shape
    n = nblk * blk
    idx = jnp.arange(n) // blk
    return jnp.where(idx[:, None] == idx[None, :], jnp.tile(w.reshape(n, blk), (1, nblk)), 0.0)


def _mlstm(bx, sbz, conv_w, conv_b, wq, wk, wv, w_gates, b_gates, out_g, skip):
    b, s, bw = bx.shape
    L = MLSTM_L
    nc = s // L
    nbb = MLSTM_NB
    assert b % nbb == 0 and s % L == 0
    per_head = wq.shape[0] // B_HEADS

    def head_blocks(w, transpose):
        blocks = [_block_diag_dense(w[h * per_head:(h + 1) * per_head]) for h in range(B_HEADS)]
        return jnp.stack([blk.T if transpose else blk for blk in blocks]).astype(BF16)

    wqt_d = head_blocks(wq, True)
    wk_d = head_blocks(wk, False)
    wvt_d = head_blocks(wv, True)
    hblk = (B_HEADS, B_HEAD_DIM, B_HEAD_DIM)
    wgq = w_gates[:bw].T.astype(BF16)
    wgv = w_gates[2 * bw:].T.astype(BF16)
    wgk = w_gates[bw:2 * bw].T.astype(BF16)
    const = lambda shape: pl.BlockSpec(shape, lambda i, j: (0,) * len(shape))
    tok = pl.BlockSpec((nbb, L, bw), lambda i, j: (i, j, 0))
    return pl.pallas_call(
        _mlstm_kernel,
        grid=(b // nbb, nc),
        in_specs=[tok, tok, const((B_CONV, bw)), const((1, bw)),
                  const(hblk), const(hblk), const(hblk),
                  const((GATE_ROWS, bw)), const((GATE_ROWS, bw)), const((GATE_ROWS, bw)),
                  const((GATE_ROWS, 1)), const((1, bw)), const((1, bw))],
        out_specs=tok,
        out_shape=jax.ShapeDtypeStruct((b, s, bw), BF16),
        scratch_shapes=[pltpu.VMEM((nbb, L + CONV_HALO, bw), F32),
                        pltpu.VMEM((nbb, B_HEADS, 2 * B_HEAD_DIM, B_HEAD_DIM), F32),
                        pltpu.VMEM((nbb, B_HEADS, 8, 128), F32)],
        compiler_params=pltpu.CompilerParams(
            dimension_semantics=("parallel", "arbitrary"), vmem_limit_bytes=VMEM_LIMIT),
        name="mlstm",
    )(bx, sbz, conv_w, conv_b.reshape(1, bw), wqt_d, wk_d, wvt_d, wgq, wgk, wgv,
      b_gates.reshape(GATE_ROWS, 1), out_g.reshape(1, bw), skip.reshape(1, bw))


def _tail_kernel(x_ref, ya_ref, yb_ref, mod0_ref, mod1_ref, lng_ref, wo0_ref, wi1_ref, clg_ref,
                 clb_ref, ws_ref, bst_ref, wo1_ref, o_ref):
    tm = x_ref.shape[1]
    sub = TAIL_SUB_ROWS
    n_sub = tm // sub
    aw = A_WIDTH
    mod1 = mod1_ref[0]
    gate0 = mod0_ref[0][2:3]
    t_iota = lax.broadcasted_iota(jnp.int32, (C_CHUNK, C_CHUNK), 0)
    s_iota = lax.broadcasted_iota(jnp.int32, (C_CHUNK, C_CHUNK), 1)
    tril = s_iota <= t_iota
    gw = C_WIDTH // C_GROUPS
    bst = bst_ref[...]
    wms = [jnp.where(tril, ws_ref[g], 0.0).astype(BF16) for g in range(C_GROUPS)]

    def out_proj0(r):
        rows = slice(r * sub, (r + 1) * sub)
        y0 = _dot(ya_ref[0, rows, :], wo0_ref[0:aw, :]) + _dot(yb_ref[0, rows, :], wo0_ref[aw:, :])
        x1 = x_ref[0, rows, :] + gate0 * y0
        h = _adaln_rmsnorm(x1, lng_ref[...], mod1[1:2], mod1[0:1]).astype(BF16)
        return x1, h

    def mix_inputs(p):
        u = _gelu_tanh(p[:, :C_WIDTH])
        v = _gelu_tanh(p[:, C_WIDTH:2 * C_WIDTH])
        mu = jnp.mean(v, axis=-1, keepdims=True)
        vc = v - mu
        var = jnp.mean(vc * vc, axis=-1, keepdims=True)
        vn = ((vc * lax.rsqrt(var + 1e-5)) * clg_ref[...] + clb_ref[...]).astype(BF16)
        return vn, u * _silu(p[:, 2 * C_WIDTH:])

    def spatial_gate(vn, gate):
        cols = []
        for g in range(C_GROUPS):
            rows = []
            for ch in range(sub // C_CHUNK):
                vg = vn[ch * C_CHUNK:(ch + 1) * C_CHUNK, g * gw:(g + 1) * gw]
                rows.append(_dot(wms[g], vg) + bst[:, g:g + 1])
            cols.append(jnp.concatenate(rows, axis=0))
        return (gate * jnp.concatenate(cols, axis=1)).astype(BF16)

    def out_proj1(r, x1, y1):
        o_ref[0, r * sub:(r + 1) * sub, :] = x1 + mod1[2:3] * _dot(y1, wo1_ref[...])

    heads = [out_proj0(r) for r in range(n_sub)]
    projs = [_dot(heads[r][1], wi1_ref[...]) for r in range(n_sub)]
    y1_prev = None
    for r in range(n_sub):
        vn, gate = mix_inputs(projs[r])
        if y1_prev is not None:
            out_proj1(r - 1, heads[r - 1][0], y1_prev)
        y1_prev = spatial_gate(vn, gate)
    out_proj1(n_sub - 1, heads[n_sub - 1][0], y1_prev)


def _tail(x, ya, yb, mod0, mod1, ln_g1, w_out0, w_in1, w_out1, c_ln_g, c_ln_b, c_ws, c_bs):
    b, s, d = x.shape
    tm = TAIL_ROW_TILE
    nt = s // tm
    const = lambda shape: pl.BlockSpec(shape, lambda i, j: (0,) * len(shape))
    half = pl.BlockSpec((1, tm, A_WIDTH), lambda i, j: (i, j, 0))
    full = pl.BlockSpec((1, tm, d), lambda i, j: (i, j, 0))
    modspec = pl.BlockSpec((1, 3, d), lambda i, j: (i, 0, 0))
    return pl.pallas_call(
        _tail_kernel,
        grid=(b, nt),
        in_specs=[full, half, half, modspec, modspec, const((1, d)), const((d, d)),
                  const((d, 3 * C_WIDTH)), const((1, d)), const((1, d)),
                  const((C_GROUPS, C_CHUNK, C_CHUNK)), const((C_CHUNK, C_GROUPS)), const((d, d))],
        out_specs=full,
        out_shape=jax.ShapeDtypeStruct((b, s, d), F32),
        compiler_params=pltpu.CompilerParams(
            dimension_semantics=("parallel", "parallel"), vmem_limit_bytes=VMEM_LIMIT),
        name="tail",
    )(x, ya, yb, mod0, mod1, ln_g1.reshape(1, d), w_out0.astype(BF16), w_in1.astype(BF16),
      c_ln_g.reshape(1, d), c_ln_b.reshape(1, d), c_ws, c_bs.T, w_out1.astype(BF16))


def kernel(x, c, ln_g, ada_w, ada_b, w_in, w_out, a_q_g, a_k_g, b_conv_w, b_conv_b, b_wq, b_wk, b_wv,
           b_w_gates, b_b_gates, b_out_g, b_skip, c_ln_g, c_ln_b, c_ws, c_bs):
    mods = _ada_mods(c, ada_w, ada_b)
    qT, vT, gT, k, bx, sbz = _inproj0(x, mods[0], ln_g[0], w_in[0], a_q_g[0], a_k_g[0])
    ya = _moba(qT, k, vT, gT)
    yb = _mlstm(bx, sbz, b_conv_w[0], b_conv_b[0], b_wq[0], b_wk[0], b_wv[0], b_w_gates[0],
                b_b_gates[0], b_out_g[0], b_skip[0])
    return _tail(x, ya, yb, mods[0], mods[1], ln_g[1], w_out[0], w_in[1], w_out[1],
                 c_ln_g[0], c_ln_b[0], c_ws[0], c_bs[0])
```

```python
import functools

import jax
import jax.numpy as jnp
from jax import lax
from jax.experimental import pallas as pl
from jax.experimental.pallas import tpu as pltpu

F32 = jnp.float32
BF16 = jnp.bfloat16

D_MODEL = 1024
A_HEADS = 8
A_HEAD_DIM = 64
A_WIDTH = A_HEADS * A_HEAD_DIM
MOBA_BLOCK = 256
MOBA_TOPK = 3
B_HEADS = 4
B_HEAD_DIM = 128
B_WIDTH = B_HEADS * B_HEAD_DIM
B_CONV = 4
C_GROUPS = 8
C_CHUNK = 128
C_WIDTH = D_MODEL
NEG = -1e30
LOG2E = 1.4426950408889634

MLSTM_L = 256
MLSTM_NB = 4
INPROJ_SUBTILES = 4
TAIL_ROW_TILE = 1024
TAIL_SUB_ROWS = 256
CONV_HALO = 8
GATE_ROWS = 2 * B_HEADS
PV_ONES_ROWS = 16
MOBA_NB = 2
VMEM_LIMIT = 48 * 1024 * 1024
MOBA_VMEM_LIMIT = 56 * 1024 * 1024


def _silu(x):
    return x * jax.nn.sigmoid(x)


def _gelu_tanh(x):
    return 0.5 * x * (1.0 + jnp.tanh(0.7978845608028654 * (x + 0.044715 * (x * x * x))))


def _log_sigmoid(x):
    return jnp.minimum(x, 0.0) - jnp.log(1.0 + jnp.exp(-jnp.abs(x)))


def _split_bf16(x):
    hi = x.astype(BF16)
    lo = (x - hi.astype(F32)).astype(BF16)
    return hi, lo


def _dot(a, b):
    return jnp.dot(a, b, preferred_element_type=F32)


def _dot_nt(a, b):
    return lax.dot_general(a, b, (((1,), (1,)), ((), ())), preferred_element_type=F32)


def _adaln_rmsnorm(x, ln_g, scale, shift):
    y = x * lax.rsqrt(jnp.mean(x * x, axis=-1, keepdims=True) + 1e-6)
    return (y * ln_g) * (1.0 + scale) + shift


def _ada_kernel(c_ref, w_ref, b_ref, o_ref):
    cs_hi, cs_lo = _split_bf16(_silu(c_ref[...]))
    w_hi, w_lo = _split_bf16(w_ref[0])
    o_ref[0] = _dot(cs_hi, w_hi) + _dot(cs_lo, w_hi) + _dot(cs_hi, w_lo) + b_ref[0]


def _ada_mods(c, ada_w, ada_b):
    depth, d, d3 = ada_w.shape
    b = c.shape[0]
    bp = 8
    cp = jnp.zeros((bp, d), F32).at[:b].set(c)
    nt = d3 // d
    out = pl.pallas_call(
        _ada_kernel,
        grid=(depth, nt),
        in_specs=[
            pl.BlockSpec((bp, d), lambda l, n: (0, 0)),
            pl.BlockSpec((1, d, d), lambda l, n: (l, 0, n)),
            pl.BlockSpec((1, 1, d), lambda l, n: (l, 0, n)),
        ],
        out_specs=pl.BlockSpec((1, bp, d), lambda l, n: (l, 0, n)),
        out_shape=jax.ShapeDtypeStruct((depth, bp, d3), F32),
        name="ada_mods",
    )(cp, ada_w, ada_b.reshape(depth, 1, d3))
    return out[:, :b].reshape(depth, b, 3, d)


def _inproj0_kernel(x_ref, mod_ref, lng_ref, wt_ref, w_ref, qg_ref, kg_ref,
                    qT_ref, vT_ref, gT_ref, k_ref, bx_ref, sbz_ref):
    tm = MOBA_BLOCK
    aw = A_WIDTH
    mod = mod_ref[0]
    qg = qg_ref[...] * (A_HEAD_DIM ** -0.5 * LOG2E)
    kg = kg_ref[...]
    hs = [_adaln_rmsnorm(x_ref[0, r * tm:(r + 1) * tm], lng_ref[...], mod[1:2], mod[0:1]).astype(BF16)
          for r in range(INPROJ_SUBTILES)]
    for r in range(INPROJ_SUBTILES):
        h = hs[r]
        rows = slice(r * tm, (r + 1) * tm)
        pt = _dot_nt(wt_ref[...], h)
        p = _dot(h, w_ref[...])
        kn = []
        for hd in range(A_HEADS):
            lo = hd * A_HEAD_DIM
            q = pt[lo:lo + A_HEAD_DIM]
            rq = lax.rsqrt(jnp.mean(q * q, axis=0, keepdims=True) + 1e-6)
            qT_ref[0, r, lo:lo + A_HEAD_DIM, :] = ((q * rq) * qg).astype(BF16)
            k = pt[aw + lo:aw + lo + A_HEAD_DIM]
            rk = lax.rsqrt(jnp.mean(k * k, axis=0, keepdims=True) + 1e-6)
            kn.append((k * rk) * kg)
        k_ref[0, rows, :] = jnp.concatenate(kn, axis=0).T.astype(BF16)
        vT_ref[0, r] = pt[2 * aw:3 * aw].astype(BF16)
        gT_ref[0, r] = _silu(pt[3 * aw:]).astype(BF16)
        bx_ref[0, rows, :] = p[:, :B_WIDTH].astype(BF16)
        sbz_ref[0, rows, :] = _silu(p[:, B_WIDTH:]).astype(BF16)


def _inproj0(x, mod, ln_g, w_in, q_g, k_g):
    b, s, d = x.shape
    blk = MOBA_BLOCK
    sub = INPROJ_SUBTILES
    tm = sub * blk
    nt = s // tm
    aw, bw = A_WIDTH, B_WIDTH
    wq, wk, wv, wz = (w_in[:, i * aw:(i + 1) * aw] for i in range(4))
    wbx = w_in[:, 4 * aw:4 * aw + bw]
    wbz = w_in[:, 4 * aw + bw:]
    wt = jnp.concatenate([wq, wk, wv, wz], axis=1).T.astype(BF16)
    wn = jnp.concatenate([wbx, wbz], axis=1).astype(BF16)
    t_shape = jax.ShapeDtypeStruct((b, s // blk, aw, blk), BF16)
    n_shape = jax.ShapeDtypeStruct((b, s, aw), BF16)
    t_spec = pl.BlockSpec((1, sub, aw, blk), lambda i, j: (i, j, 0, 0))
    n_spec = pl.BlockSpec((1, tm, aw), lambda i, j: (i, j, 0))
    const = lambda shape: pl.BlockSpec(shape, lambda i, j: (0,) * len(shape),
                                       pipeline_mode=pl.Buffered(1))
    return pl.pallas_call(
        _inproj0_kernel,
        grid=(b, nt),
        in_specs=[
            pl.BlockSpec((1, tm, d), lambda i, j: (i, j, 0)),
            pl.BlockSpec((1, 3, d), lambda i, j: (i, 0, 0)),
            const((1, d)),
            const((4 * aw, d)),
            const((d, 2 * bw)),
            const((A_HEAD_DIM, 1)),
            const((A_HEAD_DIM, 1)),
        ],
        out_specs=[t_spec, t_spec, t_spec, n_spec, n_spec, n_spec],
        out_shape=[t_shape, t_shape, t_shape, n_shape, n_shape, n_shape],
        compiler_params=pltpu.CompilerParams(
            dimension_semantics=("parallel", "parallel"), vmem_limit_bytes=VMEM_LIMIT),
        name="inproj0",
    )(x, mod, ln_g.reshape(1, d), wt, wn, q_g.reshape(A_HEAD_DIM, 1), k_g.reshape(A_HEAD_DIM, 1))


def _moba_kernel(qT_ref, k_ref, vT_ref, gT_ref, o_ref, kmean_ref, sel_ref, qm_ref, raw_ref, p_ref,
                 m_ref, alpha_ref, acc_ref, *, nb):
    blk = MOBA_BLOCK
    dh = A_HEAD_DIM
    pw = 2 * dh
    streams = range(MOBA_NB)
    j = pl.program_id(1)

    @pl.when(j == 0)
    def _():
        for z in streams:
            for n in range(nb):
                kb = k_ref[z, n * blk:(n + 1) * blk, :].astype(F32)
                kmean_ref[z, n:n + 1, :] = jnp.mean(kb, axis=0, keepdims=True)

    d_iota = lax.broadcasted_iota(jnp.int32, (pw, blk), 0)
    for z in streams:
        for hd in range(A_HEADS):
            pr = hd // 2
            q2 = qT_ref[z, 0, pr * pw:(pr + 1) * pw, :]
            keep = (d_iota < dh) if hd % 2 == 0 else (d_iota >= dh)
            qm_ref[z, hd] = jnp.where(keep, q2, jnp.zeros_like(q2))

    def score_dots(z, n, slot):
        row0 = pl.multiple_of(n * blk, blk)
        for hd in range(A_HEADS):
            pr = hd // 2
            kt = k_ref[z, pl.ds(row0, blk), pr * pw:(pr + 1) * pw]
            raw_ref[z, slot, hd] = _dot(kt, qm_ref[z, hd]).astype(BF16)

    def value_dots(z, n, slot):
        ones = jnp.ones((PV_ONES_ROWS, blk), BF16)
        return [_dot(jnp.concatenate([vT_ref[z, n, hd * dh:(hd + 1) * dh, :], ones], axis=0),
                     p_ref[z, slot, hd]) for hd in range(A_HEADS)]

    sel_scores = []
    for z in streams:
        km_hi, km_lo = _split_bf16(kmean_ref[z])
        km2 = jnp.concatenate([km_hi, km_lo], axis=0)
        per_head = []
        for hd in range(A_HEADS):
            pr = hd // 2
            r2 = _dot(km2[:, pr * pw:(pr + 1) * pw], qm_ref[z, hd])
            per_head.append(r2[0:nb] + r2[nb:2 * nb])
        sel_scores.append(per_head)

    for z in streams:
        score_dots(z, j, 0)
    for z in streams:
        score_dots(z, 0, 1)

    n_iota = lax.broadcasted_iota(jnp.int32, (nb, blk), 0)
    past = n_iota < j
    for z in streams:
        for hd in range(A_HEADS):
            sc = jnp.where(past, sel_scores[z][hd], NEG)
            sel = jnp.zeros((nb, blk), F32)
            for _ in range(MOBA_TOPK):
                mx = jnp.max(sc, axis=0, keepdims=True)
                first = jnp.min(jnp.where(sc == mx, n_iota, nb), axis=0, keepdims=True)
                pick = n_iota == first
                sel = jnp.where(pick, 1.0, sel)
                sc = jnp.where(pick, -jnp.inf, sc)
            sel_ref[z, hd, 0:nb, :] = jnp.where(past, sel, 0.0)
            sel_ref[z, hd, nb:nb + 1, :] = jnp.ones((1, blk), F32)

    k_iota = lax.broadcasted_iota(jnp.int32, (blk, blk), 0)
    q_iota = lax.broadcasted_iota(jnp.int32, (blk, blk), 1)
    causal_bias = jnp.where(k_iota <= q_iota, 0.0, NEG).astype(BF16)
    for z in streams:
        for hd in range(A_HEADS):
            s = raw_ref[z, 0, hd] + causal_bias
            m_new = jnp.max(s, axis=0, keepdims=True)
            p_ref[z, 0, hd] = jnp.exp2(s - m_new)
            m_ref[z, hd] = m_new.astype(F32)
            acc_ref[z, hd] = jnp.zeros(acc_ref.shape[2:], F32)

    def step(z, i, cur):
        nxt = 1 - cur
        n = i - 1
        prev = jnp.where(i == 1, j, i - 2)
        prev_sel = jnp.where(i == 1, nb, i - 2)
        row0 = pl.multiple_of(jnp.minimum(i, nb - 1) * blk, blk)
        ones = jnp.ones((PV_ONES_ROWS, blk), BF16)
        pvs = []
        for hd in range(A_HEADS):
            pr = hd // 2
            kt = k_ref[z, pl.ds(row0, blk), pr * pw:(pr + 1) * pw]
            raw_ref[z, nxt, hd] = _dot(kt, qm_ref[z, hd]).astype(BF16)
            pvs.append(_dot(jnp.concatenate([vT_ref[z, prev, hd * dh:(hd + 1) * dh, :], ones], axis=0),
                            p_ref[z, nxt, hd]))
            s = raw_ref[z, cur, hd]
            selb = sel_ref[z, hd, pl.ds(n, 1), :] > 0.5
            smax = jnp.where(selb, jnp.max(s, axis=0, keepdims=True).astype(F32), NEG)
            m_old = m_ref[z, hd]
            m_new = jnp.maximum(m_old, smax)
            alpha_ref[z, hd] = jnp.exp2(m_old - m_new)
            m_ref[z, hd] = m_new
            p_ref[z, cur, hd] = jnp.exp2(s - m_new.astype(BF16))
        for hd in range(A_HEADS):
            keep = sel_ref[z, hd, pl.ds(prev_sel, 1), :] > 0.5
            acc_ref[z, hd] = alpha_ref[z, hd] * (acc_ref[z, hd] + jnp.where(keep, pvs[hd], 0.0))

    def body(ii, carry):
        for z in streams:
            step(z, 2 * ii + 1, 1)
        for z in streams:
            step(z, 2 * ii + 2, 0)
        return carry

    trips = (j + 1) // 2
    lax.fori_loop(0, trips, body, 0)

    last = jnp.where(j == 0, j, 2 * trips - 1)
    last_sel = jnp.where(j == 0, nb, 2 * trips - 1)
    last_pvs = [value_dots(z, last, 0) for z in streams]
    for z in streams:
        for pr in range(A_HEADS // 2):
            halves = []
            for hd in (2 * pr, 2 * pr + 1):
                keep = sel_ref[z, hd, pl.ds(last_sel, 1), :] > 0.5
                acc = acc_ref[z, hd] + jnp.where(keep, last_pvs[z][hd], 0.0)
                halves.append(acc[0:dh] / acc[dh:dh + 1])
            oT = jnp.concatenate(halves, axis=0)
            oT = oT * gT_ref[z, 0, pr * pw:(pr + 1) * pw, :].astype(F32)
            o_ref[z, :, pr * pw:(pr + 1) * pw] = oT.T.astype(BF16)


def _moba(qT, k, vT, gT):
    b, nb, aw, blk = qT.shape
    s = nb * blk
    pw = 2 * A_HEAD_DIM
    z = MOBA_NB
    assert b % z == 0
    q_spec = pl.BlockSpec((z, 1, aw, blk), lambda i, j: (i, j, 0, 0))
    row = lambda: pltpu.VMEM((z, A_HEADS, 1, blk), F32)
    return pl.pallas_call(
        functools.partial(_moba_kernel, nb=nb),
        grid=(b // z, nb),
        in_specs=[
            q_spec,
            pl.BlockSpec((z, s, aw), lambda i, j: (i, 0, 0)),
            pl.BlockSpec((z, nb, aw, blk), lambda i, j: (i, 0, 0, 0)),
            q_spec,
        ],
        out_specs=pl.BlockSpec((z, blk, aw), lambda i, j: (i, j, 0)),
        out_shape=jax.ShapeDtypeStruct((b, s, aw), BF16),
        scratch_shapes=[
            pltpu.VMEM((z, nb, aw), F32),
            pltpu.VMEM((z, A_HEADS, nb + 8, blk), F32),
            pltpu.VMEM((z, A_HEADS, pw, blk), BF16),
            pltpu.VMEM((z, 2, A_HEADS, blk, blk), BF16),
            pltpu.VMEM((z, 2, A_HEADS, blk, blk), BF16),
            row(), row(),
            pltpu.VMEM((z, A_HEADS, A_HEAD_DIM + PV_ONES_ROWS, blk), F32),
        ],
        compiler_params=pltpu.CompilerParams(
            dimension_semantics=("parallel", "arbitrary"), vmem_limit_bytes=MOBA_VMEM_LIMIT),
        name="moba",
    )(qT, k, vT, gT)


def _mlstm_kernel(bx_ref, sbz_ref, cw_ref, cb_ref, wqt_ref, wk_ref, wvt_ref, wgq_ref, wgk_ref,
                  wgv_ref, bg_ref, og_ref, skip_ref, o_ref, xbuf_ref, state_ref, m_ref):
    L = MLSTM_L
    dh = B_HEAD_DIM
    c = pl.program_id(1)

    @pl.when(c == 0)
    def _():
        xbuf_ref[:, 0:CONV_HALO, :] = jnp.zeros((MLSTM_NB, CONV_HALO, B_WIDTH), F32)
        state_ref[...] = jnp.zeros_like(state_ref)
        m_ref[...] = jnp.zeros_like(m_ref)

    cw = cw_ref[...]
    cb = cb_ref[...]
    kscale = dh ** -0.5
    s_iota = lax.broadcasted_iota(jnp.int32, (L, L), 0)
    t_iota = lax.broadcasted_iota(jnp.int32, (L, L), 1)
    tri = s_iota <= t_iota
    tri_b = jnp.where(tri, 1.0, 0.0).astype(BF16)
    ones = jnp.ones((dh, L), F32)

    def project(bi):
        bx_b = bx_ref[bi]
        xbuf_ref[bi, CONV_HALO:CONV_HALO + L, :] = bx_b.astype(F32)
        st = dict(xc=[], ks=[], vT=[], scores=[], inter=[], state=[])
        gT = bg_ref[...]
        for hd in range(B_HEADS):
            lo = hd * dh
            conv = cb[:, lo:lo + dh]
            for i in range(B_CONV):
                off = CONV_HALO - (B_CONV - 1) + i
                conv = conv + cw[i:i + 1, lo:lo + dh] * xbuf_ref[bi, off:off + L, lo:lo + dh]
            xc = _silu(conv)
            xc_b = xc.astype(BF16)
            qT_h = _dot_nt(wqt_ref[hd], xc_b).astype(BF16)
            vT_h = _dot_nt(wvt_ref[hd], bx_b[:, lo:lo + dh])
            k_h = _dot(xc_b, wk_ref[hd])
            ks_h = (k_h * kscale).astype(BF16)
            state = state_ref[bi, hd]
            st["scores"].append(_dot(ks_h, qT_h))
            st["inter"].append(_dot(state.astype(BF16), qT_h))
            gT = gT + (_dot(wgq_ref[:, lo:lo + dh], qT_h)
                       + _dot_nt(wgk_ref[:, lo:lo + dh], k_h.astype(BF16))
                       + _dot(wgv_ref[:, lo:lo + dh], vT_h.astype(BF16)))
            st["xc"].append(xc); st["ks"].append(ks_h); st["vT"].append(vT_h); st["state"].append(state)
        tail = xbuf_ref[bi, L:L + CONV_HALO, :]
        xbuf_ref[bi, 0:CONV_HALO, :] = tail
        st["gT"] = gT
        return st

    def gate_chain(st):
        gT = st["gT"]
        lf_hi, lf_lo = _split_bf16(_log_sigmoid(gT))
        cum = (_dot(lf_hi, tri_b) + _dot(lf_lo, tri_b))[B_HEADS:2 * B_HEADS]
        a = gT[0:B_HEADS] - cum
        st["cum"], st["a"] = cum, a
        st["a_cols"] = jnp.concatenate([a, jnp.zeros((128 - B_HEADS, L), F32)], axis=0).T

    def recur(bi, st):
        a, cum, a_cols = st["a"], st["cum"], st["a_cols"]
        for hd in range(B_HEADS):
            lo = hd * dh
            ks_h, state, inter = st["ks"][hd], st["state"][hd], st["inter"][hd]
            vT_aug = jnp.concatenate([st["vT"][hd], ones], axis=0)
            a_row = a[hd:hd + 1]
            a_col = a_cols[:, hd:hd + 1]
            cum_row = cum[hd:hd + 1]
            m_prev = m_ref[bi, hd][0:1, 0:1]
            amax = jnp.max(jnp.where(tri, a_col, NEG), axis=0, keepdims=True)
            b_row = jnp.maximum(m_prev, amax)
            dmat = jnp.exp(jnp.where(tri, a_col - b_row, NEG))
            w_intra = (st["scores"][hd] * dmat).astype(BF16)
            intra = _dot(vT_aug.astype(BF16), w_intra)
            w_inter = jnp.exp(m_prev - b_row)
            num = w_inter * inter[0:dh] + intra[0:dh]
            den = w_inter * inter[dh:dh + 1] + intra[dh:dh + 1]
            hc = num * (1.0 / jnp.maximum(jnp.abs(den), jnp.exp(-(cum_row + b_row))))
            mu = jnp.mean(hc, axis=0, keepdims=True)
            hcc = hc - mu
            var = jnp.mean(hcc * hcc, axis=0, keepdims=True)
            hb = (hcc * lax.rsqrt(var + 1e-5)).T * og_ref[:, lo:lo + dh]
            yb = (hb + skip_ref[:, lo:lo + dh] * st["xc"][hd]) * sbz_ref[bi, :, lo:lo + dh].astype(F32)
            o_ref[bi, :, lo:lo + dh] = yb.astype(BF16)
            b_end = b_row[:, L - 1:L]
            ws = jnp.exp(a_row - b_end)
            decay = jnp.exp(m_prev - b_end)
            state_ref[bi, hd] = decay * state + _dot((vT_aug * ws).astype(BF16), ks_h)
            m_ref[bi, hd] = jnp.broadcast_to(cum_row[:, L - 1:L] + b_end, m_ref.shape[2:])

    sts = []
    for bi in range(MLSTM_NB):
        sts.append(project(bi))
        gate_chain(sts[bi])
    for bi in range(MLSTM_NB):
        recur(bi, sts[bi])


def _block_diag_dense(w):
    nblk, blk, _ = w.shape
    n = nblk * blk
    idx = jnp.arange(n) // blk
    return jnp.where(idx[:, None] == idx[None, :], jnp.tile(w.reshape(n, blk), (1, nblk)), 0.0)


def _mlstm(bx, sbz, conv_w, conv_b, wq, wk, wv, w_gates, b_gates, out_g, skip):
    b, s, bw = bx.shape
    L = MLSTM_L
    nc = s // L
    nbb = MLSTM_NB
    assert b % nbb == 0 and s % L == 0
    per_head = wq.shape[0] // B_HEADS

    def head_blocks(w, transpose):
        blocks = [_block_diag_dense(w[h * per_head:(h + 1) * per_head]) for h in range(B_HEADS)]
        return jnp.stack([blk.T if transpose else blk for blk in blocks]).astype(BF16)

    wqt_d = head_blocks(wq, True)
    wk_d = head_blocks(wk, False)
    wvt_d = head_blocks(wv, True)
    hblk = (B_HEADS, B_HEAD_DIM, B_HEAD_DIM)
    wgq = w_gates[:bw].T.astype(BF16)
    wgv = w_gates[2 * bw:].T.astype(BF16)
    wgk = w_gates[bw:2 * bw].T.astype(BF16)
    const = lambda shape: pl.BlockSpec(shape, lambda i, j: (0,) * len(shape))
    tok = pl.BlockSpec((nbb, L, bw), lambda i, j: (i, j, 0))
    return pl.pallas_call(
        _mlstm_kernel,
        grid=(b // nbb, nc),
        in_specs=[tok, tok, const((B_CONV, bw)), const((1, bw)),
                  const(hblk), const(hblk), const(hblk),
                  const((GATE_ROWS, bw)), const((GATE_ROWS, bw)), const((GATE_ROWS, bw)),
                  const((GATE_ROWS, 1)), const((1, bw)), const((1, bw))],
        out_specs=tok,
        out_shape=jax.ShapeDtypeStruct((b, s, bw), BF16),
        scratch_shapes=[pltpu.VMEM((nbb, L + CONV_HALO, bw), F32),
                        pltpu.VMEM((nbb, B_HEADS, 2 * B_HEAD_DIM, B_HEAD_DIM), F32),
                        pltpu.VMEM((nbb, B_HEADS, 8, 128), F32)],
        compiler_params=pltpu.CompilerParams(
            dimension_semantics=("parallel", "arbitrary"), vmem_limit_bytes=VMEM_LIMIT),
        name="mlstm",
    )(bx, sbz, conv_w, conv_b.reshape(1, bw), wqt_d, wk_d, wvt_d, wgq, wgk, wgv,
      b_gates.reshape(GATE_ROWS, 1), out_g.reshape(1, bw), skip.reshape(1, bw))


def _tail_kernel(x_ref, ya_ref, yb_ref, mod0_ref, mod1_ref, lng_ref, wo0_ref, wi1_ref, clg_ref,
                 clb_ref, ws_ref, bst_ref, wo1_ref, o_ref):
    tm = x_ref.shape[1]
    sub = TAIL_SUB_ROWS
    n_sub = tm // sub
    aw = A_WIDTH
    mod1 = mod1_ref[0]
    gate0 = mod0_ref[0][2:3]
    t_iota = lax.broadcasted_iota(jnp.int32, (C_CHUNK, C_CHUNK), 0)
    s_iota = lax.broadcasted_iota(jnp.int32, (C_CHUNK, C_CHUNK), 1)
    tril = s_iota <= t_iota
    gw = C_WIDTH // C_GROUPS
    bst = bst_ref[...]
    wms = [jnp.where(tril, ws_ref[g], 0.0).astype(BF16) for g in range(C_GROUPS)]

    def out_proj0(r):
        rows = slice(r * sub, (r + 1) * sub)
        y0 = _dot(ya_ref[0, rows, :], wo0_ref[0:aw, :]) + _dot(yb_ref[0, rows, :], wo0_ref[aw:, :])
        x1 = x_ref[0, rows, :] + gate0 * y0
        h = _adaln_rmsnorm(x1, lng_ref[...], mod1[1:2], mod1[0:1]).astype(BF16)
        return x1, h

    def mix_inputs(p):
        u = _gelu_tanh(p[:, :C_WIDTH])
        v = _gelu_tanh(p[:, C_WIDTH:2 * C_WIDTH])
        mu = jnp.mean(v, axis=-1, keepdims=True)
        vc = v - mu
        var = jnp.mean(vc * vc, axis=-1, keepdims=True)
        vn = ((vc * lax.rsqrt(var + 1e-5)) * clg_ref[...] + clb_ref[...]).astype(BF16)
        return vn, u * _silu(p[:, 2 * C_WIDTH:])

    def spatial_gate(vn, gate):
        cols = []
        for g in range(C_GROUPS):
            rows = []
            for ch in range(sub // C_CHUNK):
                vg = vn[ch * C_CHUNK:(ch + 1) * C_CHUNK, g * gw:(g + 1) * gw]
                rows.append(_dot(wms[g], vg) + bst[:, g:g + 1])
            cols.append(jnp.concatenate(rows, axis=0))
        return (gate * jnp.concatenate(cols, axis=1)).astype(BF16)

    def out_proj1(r, x1, y1):
        o_ref[0, r * sub:(r + 1) * sub, :] = x1 + mod1[2:3] * _dot(y1, wo1_ref[...])

    heads = [out_proj0(r) for r in range(n_sub)]
    projs = [_dot(heads[r][1], wi1_ref[...]) for r in range(n_sub)]
    y1_prev = None
    for r in range(n_sub):
        vn, gate = mix_inputs(projs[r])
        if y1_prev is not None:
            out_proj1(r - 1, heads[r - 1][0], y1_prev)
        y1_prev = spatial_gate(vn, gate)
    out_proj1(n_sub - 1, heads[n_sub - 1][0], y1_prev)


def _tail(x, ya, yb, mod0, mod1, ln_g1, w_out0, w_in1, w_out1, c_ln_g, c_ln_b, c_ws, c_bs):
    b, s, d = x.shape
    tm = TAIL_ROW_TILE
    nt = s // tm
    const = lambda shape: pl.BlockSpec(shape, lambda i, j: (0,) * len(shape),
                                       pipeline_mode=pl.Buffered(1))
    half = pl.BlockSpec((1, tm, A_WIDTH), lambda i, j: (i, j, 0))
    full = pl.BlockSpec((1, tm, d), lambda i, j: (i, j, 0))
    modspec = pl.BlockSpec((1, 3, d), lambda i, j: (i, 0, 0))
    return pl.pallas_call(
        _tail_kernel,
        grid=(b, nt),
        in_specs=[full, half, half, modspec, modspec, const((1, d)), const((d, d)),
                  const((d, 3 * C_WIDTH)), const((1, d)), const((1, d)),
                  const((C_GROUPS, C_CHUNK, C_CHUNK)), const((C_CHUNK, C_GROUPS)), const((d, d))],
        out_specs=full,
        out_shape=jax.ShapeDtypeStruct((b, s, d), F32),
        compiler_params=pltpu.CompilerParams(
            dimension_semantics=("parallel", "parallel"), vmem_limit_bytes=VMEM_LIMIT),
        name="tail",
    )(x, ya, yb, mod0, mod1, ln_g1.reshape(1, d), w_out0.astype(BF16), w_in1.astype(BF16),
      c_ln_g.reshape(1, d), c_ln_b.reshape(1, d), c_ws, c_bs.T, w_out1.astype(BF16))


def kernel(x, c, ln_g, ada_w, ada_b, w_in, w_out, a_q_g, a_k_g, b_conv_w, b_conv_b, b_wq, b_wk, b_wv,
           b_w_gates, b_b_gates, b_out_g, b_skip, c_ln_g, c_ln_b, c_ws, c_bs):
    mods = _ada_mods(c, ada_w, ada_b)
    qT, vT, gT, k, bx, sbz = _inproj0(x, mods[0], ln_g[0], w_in[0], a_q_g[0], a_k_g[0])
    ya = _moba(qT, k, vT, gT)
    yb = _mlstm(bx, sbz, b_conv_w[0], b_conv_b[0], b_wq[0], b_wk[0], b_wv[0], b_w_gates[0],
                b_b_gates[0], b_out_g[0], b_skip[0])
    return _tail(x, ya, yb, mods[0], mods[1], ln_g[1], w_out[0], w_in[1], w_out[1],
                 c_ln_g[0], c_ln_b[0], c_ws[0], c_bs[0])
```

```python
import functools

import jax
import jax.numpy as jnp
from jax import lax
from jax.experimental import pallas as pl
from jax.experimental.pallas import tpu as pltpu

F32 = jnp.float32
BF16 = jnp.bfloat16

D_MODEL = 1024
A_HEADS = 8
A_HEAD_DIM = 64
A_WIDTH = A_HEADS * A_HEAD_DIM
MOBA_BLOCK = 256
MOBA_TOPK = 3
B_HEADS = 4
B_HEAD_DIM = 128
B_WIDTH = B_HEADS * B_HEAD_DIM
B_CONV = 4
C_GROUPS = 8
C_CHUNK = 128
C_WIDTH = D_MODEL
NEG = -1e30
LOG2E = 1.4426950408889634

MLSTM_L = 256
MLSTM_NB = 4
INPROJ_SUBTILES = 4
TAIL_ROW_TILE = 1024
TAIL_SUB_ROWS = 256
CONV_HALO = 8
GATE_ROWS = 2 * B_HEADS
PV_ONES_ROWS = 16
MOBA_NB = 2
VMEM_LIMIT = 48 * 1024 * 1024
MOBA_VMEM_LIMIT = 56 * 1024 * 1024


def _silu(x):
    return x * jax.nn.sigmoid(x)


def _gelu_tanh(x):
    return 0.5 * x * (1.0 + jnp.tanh(0.7978845608028654 * (x + 0.044715 * (x * x * x))))


def _log_sigmoid(x):
    return jnp.minimum(x, 0.0) - jnp.log(1.0 + jnp.exp(-jnp.abs(x)))


def _split_bf16(x):
    hi = x.astype(BF16)
    lo = (x - hi.astype(F32)).astype(BF16)
    return hi, lo


def _dot(a, b):
    return jnp.dot(a, b, preferred_element_type=F32)


def _dot_nt(a, b):
    return lax.dot_general(a, b, (((1,), (1,)), ((), ())), preferred_element_type=F32)


def _adaln_rmsnorm(x, ln_g, scale, shift):
    y = x * lax.rsqrt(jnp.mean(x * x, axis=-1, keepdims=True) + 1e-6)
    return (y * ln_g) * (1.0 + scale) + shift


def _ada_kernel(c_ref, w_ref, b_ref, o_ref):
    cs_hi, cs_lo = _split_bf16(_silu(c_ref[...]))
    w_hi, w_lo = _split_bf16(w_ref[0])
    o_ref[0] = _dot(cs_hi, w_hi) + _dot(cs_lo, w_hi) + _dot(cs_hi, w_lo) + b_ref[0]


def _ada_mods(c, ada_w, ada_b):
    depth, d, d3 = ada_w.shape
    b = c.shape[0]
    bp = 8
    cp = jnp.zeros((bp, d), F32).at[:b].set(c)
    nt = d3 // d
    out = pl.pallas_call(
        _ada_kernel,
        grid=(depth, nt),
        in_specs=[
            pl.BlockSpec((bp, d), lambda l, n: (0, 0)),
            pl.BlockSpec((1, d, d), lambda l, n: (l, 0, n)),
            pl.BlockSpec((1, 1, d), lambda l, n: (l, 0, n)),
        ],
        out_specs=pl.BlockSpec((1, bp, d), lambda l, n: (l, 0, n)),
        out_shape=jax.ShapeDtypeStruct((depth, bp, d3), F32),
        name="ada_mods",
    )(cp, ada_w, ada_b.reshape(depth, 1, d3))
    return out[:, :b].reshape(depth, b, 3, d)


def _inproj0_kernel(x_ref, mod_ref, lng_ref, win_ref, qg_ref, kg_ref, cw_ref, cb_ref,
                    qT_ref, vT_ref, gT_ref, k_ref, bx_ref, xc_ref, sbz_ref,
                    wt_ref, w_ref, xbuf_ref):
    tm = MOBA_BLOCK
    aw = A_WIDTH
    n_t = 4 * aw

    @pl.when((pl.program_id(0) == 0) & (pl.program_id(1) == 0))
    def _():
        for c0 in range(0, n_t, 128):
            wt_ref[c0:c0 + 128, :] = win_ref[:, c0:c0 + 128].T.astype(BF16)
        w_ref[...] = win_ref[:, n_t:].astype(BF16)

    @pl.when(pl.program_id(1) == 0)
    def _():
        xbuf_ref[0:CONV_HALO, :] = jnp.zeros((CONV_HALO, B_WIDTH), F32)

    mod = mod_ref[0]
    qg = qg_ref[...] * (A_HEAD_DIM ** -0.5 * LOG2E)
    kg = kg_ref[...]
    cw = cw_ref[...]
    hs = [_adaln_rmsnorm(x_ref[0, r * tm:(r + 1) * tm], lng_ref[...], mod[1:2], mod[0:1]).astype(BF16)
          for r in range(INPROJ_SUBTILES)]
    for r in range(INPROJ_SUBTILES):
        h = hs[r]
        rows = slice(r * tm, (r + 1) * tm)
        pt = _dot_nt(wt_ref[...], h)
        p = _dot(h, w_ref[...])
        base = CONV_HALO + r * tm
        xbuf_ref[base:base + tm, :] = p[:, :B_WIDTH]
        conv = cb_ref[...]
        for i in range(B_CONV):
            off = base - (B_CONV - 1) + i
            conv = conv + cw[i:i + 1, :] * xbuf_ref[off:off + tm, :]
        xc_ref[0, rows, :] = _silu(conv).astype(BF16)
        kn = []
        for hd in range(A_HEADS):
            lo = hd * A_HEAD_DIM
            q = pt[lo:lo + A_HEAD_DIM]
            rq = lax.rsqrt(jnp.mean(q * q, axis=0, keepdims=True) + 1e-6)
            qT_ref[0, r, lo:lo + A_HEAD_DIM, :] = ((q * rq) * qg).astype(BF16)
            k = pt[aw + lo:aw + lo + A_HEAD_DIM]
            rk = lax.rsqrt(jnp.mean(k * k, axis=0, keepdims=True) + 1e-6)
            kn.append((k * rk) * kg)
        k_ref[0, rows, :] = jnp.concatenate(kn, axis=0).T.astype(BF16)
        vT_ref[0, r] = pt[2 * aw:3 * aw].astype(BF16)
        gT_ref[0, r] = _silu(pt[3 * aw:]).astype(BF16)
        bx_ref[0, rows, :] = p[:, :B_WIDTH].astype(BF16)
        sbz_ref[0, rows, :] = _silu(p[:, B_WIDTH:]).astype(BF16)
    n_rows = INPROJ_SUBTILES * tm
    tail = xbuf_ref[n_rows:n_rows + CONV_HALO, :]
    xbuf_ref[0:CONV_HALO, :] = tail


def _inproj0(x, mod, ln_g, w_in, q_g, k_g, conv_w, conv_b):
    b, s, d = x.shape
    blk = MOBA_BLOCK
    sub = INPROJ_SUBTILES
    tm = sub * blk
    nt = s // tm
    aw, bw = A_WIDTH, B_WIDTH
    assert w_in.shape == (d, 4 * aw + 2 * bw)
    t_shape = jax.ShapeDtypeStruct((b, s // blk, aw, blk), BF16)
    n_shape = jax.ShapeDtypeStruct((b, s, aw), BF16)
    t_spec = pl.BlockSpec((1, sub, aw, blk), lambda i, j: (i, j, 0, 0))
    n_spec = pl.BlockSpec((1, tm, aw), lambda i, j: (i, j, 0))
    const = lambda shape: pl.BlockSpec(shape, lambda i, j: (0,) * len(shape),
                                       pipeline_mode=pl.Buffered(1))
    return pl.pallas_call(
        _inproj0_kernel,
        grid=(b, nt),
        in_specs=[
            pl.BlockSpec((1, tm, d), lambda i, j: (i, j, 0)),
            pl.BlockSpec((1, 3, d), lambda i, j: (i, 0, 0)),
            const((1, d)),
            const((d, 4 * aw + 2 * bw)),
            const((A_HEAD_DIM, 1)),
            const((A_HEAD_DIM, 1)),
            const((B_CONV, bw)),
            const((1, bw)),
        ],
        out_specs=[t_spec, t_spec, t_spec, n_spec, n_spec, n_spec, n_spec],
        out_shape=[t_shape, t_shape, t_shape, n_shape, n_shape, n_shape, n_shape],
        scratch_shapes=[pltpu.VMEM((4 * aw, d), BF16),
                        pltpu.VMEM((d, 2 * bw), BF16),
                        pltpu.VMEM((CONV_HALO + tm, bw), F32)],
        compiler_params=pltpu.CompilerParams(
            dimension_semantics=("arbitrary", "arbitrary"), vmem_limit_bytes=VMEM_LIMIT),
        name="inproj0",
    )(x, mod, ln_g.reshape(1, d), w_in, q_g.reshape(A_HEAD_DIM, 1), k_g.reshape(A_HEAD_DIM, 1),
      conv_w, conv_b.reshape(1, bw))


def _moba_kernel(qT_ref, k_ref, vT_ref, gT_ref, o_ref, kmean_ref, sel_ref, qm_ref, raw_ref, p_ref,
                 m_ref, alpha_ref, acc_ref, *, nb):
    blk = MOBA_BLOCK
    dh = A_HEAD_DIM
    pw = 2 * dh
    streams = range(MOBA_NB)
    j = pl.program_id(1)

    @pl.when(j == 0)
    def _():
        for z in streams:
            for n in range(nb):
                kb = k_ref[z, n * blk:(n + 1) * blk, :].astype(F32)
                kmean_ref[z, n:n + 1, :] = jnp.mean(kb, axis=0, keepdims=True)

    d_iota = lax.broadcasted_iota(jnp.int32, (pw, blk), 0)
    for z in streams:
        for hd in range(A_HEADS):
            pr = hd // 2
            q2 = qT_ref[z, 0, pr * pw:(pr + 1) * pw, :]
            keep = (d_iota < dh) if hd % 2 == 0 else (d_iota >= dh)
            qm_ref[z, hd] = jnp.where(keep, q2, jnp.zeros_like(q2))

    def score_dots(z, n, slot):
        row0 = pl.multiple_of(n * blk, blk)
        for hd in range(A_HEADS):
            pr = hd // 2
            kt = k_ref[z, pl.ds(row0, blk), pr * pw:(pr + 1) * pw]
            raw_ref[z, slot, hd] = _dot(kt, qm_ref[z, hd]).astype(BF16)

    def value_dots(z, n, slot):
        ones = jnp.ones((PV_ONES_ROWS, blk), BF16)
        return [_dot(jnp.concatenate([vT_ref[z, n, hd * dh:(hd + 1) * dh, :], ones], axis=0),
                     p_ref[z, slot, hd]) for hd in range(A_HEADS)]

    sel_scores = []
    for z in streams:
        km_hi, km_lo = _split_bf16(kmean_ref[z])
        km2 = jnp.concatenate([km_hi, km_lo], axis=0)
        per_head = []
        for hd in range(A_HEADS):
            pr = hd // 2
            r2 = _dot(km2[:, pr * pw:(pr + 1) * pw], qm_ref[z, hd])
            per_head.append(r2[0:nb] + r2[nb:2 * nb])
        sel_scores.append(per_head)

    for z in streams:
        score_dots(z, j, 0)
    for z in streams:
        score_dots(z, 0, 1)

    n_iota = lax.broadcasted_iota(jnp.int32, (nb, blk), 0)
    past = n_iota < j
    for z in streams:
        for hd in range(A_HEADS):
            sc = jnp.where(past, sel_scores[z][hd], NEG)
            sel = jnp.zeros((nb, blk), F32)
            for _ in range(MOBA_TOPK):
                mx = jnp.max(sc, axis=0, keepdims=True)
                first = jnp.min(jnp.where(sc == mx, n_iota, nb), axis=0, keepdims=True)
                pick = n_iota == first
                sel = jnp.where(pick, 1.0, sel)
                sc = jnp.where(pick, -jnp.inf, sc)
            sel_ref[z, hd, 0:nb, :] = jnp.where(past, sel, 0.0)
            sel_ref[z, hd, nb:nb + 1, :] = jnp.ones((1, blk), F32)

    k_iota = lax.broadcasted_iota(jnp.int32, (blk, blk), 0)
    q_iota = lax.broadcasted_iota(jnp.int32, (blk, blk), 1)
    causal_bias = jnp.where(k_iota <= q_iota, 0.0, NEG).astype(BF16)
    for z in streams:
        for hd in range(A_HEADS):
            s = raw_ref[z, 0, hd] + causal_bias
            m_new = jnp.max(s, axis=0, keepdims=True)
            p_ref[z, 0, hd] = jnp.exp2(s - m_new)
            m_ref[z, hd] = m_new.astype(F32)
            acc_ref[z, hd] = jnp.zeros(acc_ref.shape[2:], F32)

    def step(z, i, cur):
        nxt = 1 - cur
        n = i - 1
        prev = jnp.where(i == 1, j, i - 2)
        prev_sel = jnp.where(i == 1, nb, i - 2)
        row0 = pl.multiple_of(jnp.minimum(i, nb - 1) * blk, blk)
        ones = jnp.ones((PV_ONES_ROWS, blk), BF16)
        pvs = []
        for hd in range(A_HEADS):
            pr = hd // 2
            kt = k_ref[z, pl.ds(row0, blk), pr * pw:(pr + 1) * pw]
            raw_ref[z, nxt, hd] = _dot(kt, qm_ref[z, hd]).astype(BF16)
            pvs.append(_dot(jnp.concatenate([vT_ref[z, prev, hd * dh:(hd + 1) * dh, :], ones], axis=0),
                            p_ref[z, nxt, hd]))
            s = raw_ref[z, cur, hd]
            selb = sel_ref[z, hd, pl.ds(n, 1), :] > 0.5
            smax = jnp.where(selb, jnp.max(s, axis=0, keepdims=True).astype(F32), NEG)
            m_old = m_ref[z, hd]
            m_new = jnp.maximum(m_old, smax)
            alpha_ref[z, hd] = jnp.exp2(m_old - m_new)
            m_ref[z, hd] = m_new
            p_ref[z, cur, hd] = jnp.exp2(s - m_new.astype(BF16))
        for hd in range(A_HEADS):
            keep = sel_ref[z, hd, pl.ds(prev_sel, 1), :] > 0.5
            acc_ref[z, hd] = alpha_ref[z, hd] * (acc_ref[z, hd] + jnp.where(keep, pvs[hd], 0.0))

    def body(ii, carry):
        for z in streams:
            step(z, 2 * ii + 1, 1)
        for z in streams:
            step(z, 2 * ii + 2, 0)
        return carry

    trips = (j + 1) // 2
    lax.fori_loop(0, trips, body, 0)

    last = jnp.where(j == 0, j, 2 * trips - 1)
    last_sel = jnp.where(j == 0, nb, 2 * trips - 1)
    last_pvs = [value_dots(z, last, 0) for z in streams]
    for z in streams:
        for pr in range(A_HEADS // 2):
            halves = []
            for hd in (2 * pr, 2 * pr + 1):
                keep = sel_ref[z, hd, pl.ds(last_sel, 1), :] > 0.5
                acc = acc_ref[z, hd] + jnp.where(keep, last_pvs[z][hd], 0.0)
                halves.append(acc[0:dh] / acc[dh:dh + 1])
            oT = jnp.concatenate(halves, axis=0)
            oT = oT * gT_ref[z, 0, pr * pw:(pr + 1) * pw, :].astype(F32)
            o_ref[z, :, pr * pw:(pr + 1) * pw] = oT.T.astype(BF16)


def _moba(qT, k, vT, gT):
    b, nb, aw, blk = qT.shape
    s = nb * blk
    pw = 2 * A_HEAD_DIM
    z = MOBA_NB
    assert b % z == 0
    q_spec = pl.BlockSpec((z, 1, aw, blk), lambda i, j: (i, j, 0, 0))
    row = lambda: pltpu.VMEM((z, A_HEADS, 1, blk), F32)
    return pl.pallas_call(
        functools.partial(_moba_kernel, nb=nb),
        grid=(b // z, nb),
        in_specs=[
            q_spec,
            pl.BlockSpec((z, s, aw), lambda i, j: (i, 0, 0)),
            pl.BlockSpec((z, nb, aw, blk), lambda i, j: (i, 0, 0, 0)),
            q_spec,
        ],
        out_specs=pl.BlockSpec((z, blk, aw), lambda i, j: (i, j, 0)),
        out_shape=jax.ShapeDtypeStruct((b, s, aw), BF16),
        scratch_shapes=[
            pltpu.VMEM((z, nb, aw), F32),
            pltpu.VMEM((z, A_HEADS, nb + 8, blk), F32),
            pltpu.VMEM((z, A_HEADS, pw, blk), BF16),
            pltpu.VMEM((z, 2, A_HEADS, blk, blk), BF16),
            pltpu.VMEM((z, 2, A_HEADS, blk, blk), BF16),
            row(), row(),
            pltpu.VMEM((z, A_HEADS, A_HEAD_DIM + PV_ONES_ROWS, blk), F32),
        ],
        compiler_params=pltpu.CompilerParams(
            dimension_semantics=("parallel", "arbitrary"), vmem_limit_bytes=MOBA_VMEM_LIMIT),
        name="moba",
    )(qT, k, vT, gT)


def _mlstm_kernel(bx_ref, xc_ref, sbz_ref, wqt_ref, wk_ref, wvt_ref, wgq_ref, wgk_ref,
                  wgv_ref, bg_ref, og_ref, skip_ref, o_ref, state_ref, m_ref):
    L = MLSTM_L
    dh = B_HEAD_DIM
    c = pl.program_id(1)

    @pl.when(c == 0)
    def _():
        state_ref[...] = jnp.zeros_like(state_ref)
        m_ref[...] = jnp.zeros_like(m_ref)

    kscale = dh ** -0.5
    s_iota = lax.broadcasted_iota(jnp.int32, (L, L), 0)
    t_iota = lax.broadcasted_iota(jnp.int32, (L, L), 1)
    tri = s_iota <= t_iota
    tri_b = jnp.where(tri, 1.0, 0.0).astype(BF16)
    ones = jnp.ones((dh, L), F32)

    def project(bi):
        heads = range(B_HEADS)
        cols = [slice(hd * dh, (hd + 1) * dh) for hd in heads]
        xcs = [xc_ref[bi, :, cols[hd]] for hd in heads]
        qTs = [_dot_nt(wqt_ref[hd], xcs[hd]).astype(BF16) for hd in heads]
        ks = [_dot(xcs[hd], wk_ref[hd]) for hd in heads]
        vTs = [_dot_nt(wvt_ref[hd], bx_ref[bi, :, cols[hd]]) for hd in heads]
        kss = [(ks[hd] * kscale).astype(BF16) for hd in heads]
        states = [state_ref[bi, hd] for hd in heads]
        scores = [_dot(kss[hd], qTs[hd]) for hd in heads]
        inters = [_dot(states[hd].astype(BF16), qTs[hd]) for hd in heads]
        gT = bg_ref[...]
        for hd in heads:
            gT = gT + (_dot(wgq_ref[:, cols[hd]], qTs[hd])
                       + _dot_nt(wgk_ref[:, cols[hd]], ks[hd].astype(BF16))
                       + _dot(wgv_ref[:, cols[hd]], vTs[hd].astype(BF16)))
        return dict(ks=kss, vT=vTs, scores=scores, inter=inters, state=states, gT=gT)

    def gate_chain(st):
        gT = st["gT"]
        lf_hi, lf_lo = _split_bf16(_log_sigmoid(gT))
        cum = (_dot(lf_hi, tri_b) + _dot(lf_lo, tri_b))[B_HEADS:2 * B_HEADS]
        a = gT[0:B_HEADS] - cum
        st["cum"], st["a"] = cum, a
        st["a_cols"] = jnp.concatenate([a, jnp.zeros((128 - B_HEADS, L), F32)], axis=0).T

    def recur(bi, st):
        a, cum, a_cols = st["a"], st["cum"], st["a_cols"]
        for hd in range(B_HEADS):
            lo = hd * dh
            ks_h, state, inter = st["ks"][hd], st["state"][hd], st["inter"][hd]
            vT_aug = jnp.concatenate([st["vT"][hd], ones], axis=0)
            a_row = a[hd:hd + 1]
            a_col = a_cols[:, hd:hd + 1]
            cum_row = cum[hd:hd + 1]
            m_prev = m_ref[bi, hd][0:1, 0:1]
            amax = jnp.max(jnp.where(tri, a_col, NEG), axis=0, keepdims=True)
            b_row = jnp.maximum(m_prev, amax)
            dmat = jnp.exp(jnp.where(tri, a_col - b_row, NEG))
            w_intra = (st["scores"][hd] * dmat).astype(BF16)
            intra = _dot(vT_aug.astype(BF16), w_intra)
            w_inter = jnp.exp(m_prev - b_row)
            num = w_inter * inter[0:dh] + intra[0:dh]
            den = w_inter * inter[dh:dh + 1] + intra[dh:dh + 1]
            hc = num * (1.0 / jnp.maximum(jnp.abs(den), jnp.exp(-(cum_row + b_row))))
            mu = jnp.mean(hc, axis=0, keepdims=True)
            hcc = hc - mu
            var = jnp.mean(hcc * hcc, axis=0, keepdims=True)
            hb = (hcc * lax.rsqrt(var + 1e-5)).T * og_ref[:, lo:lo + dh]
            xc = xc_ref[bi, :, lo:lo + dh].astype(F32)
            yb = (hb + skip_ref[:, lo:lo + dh] * xc) * sbz_ref[bi, :, lo:lo + dh].astype(F32)
            o_ref[bi, :, lo:lo + dh] = yb.astype(BF16)
            b_end = b_row[:, L - 1:L]
            ws = jnp.exp(a_row - b_end)
            decay = jnp.exp(m_prev - b_end)
            state_ref[bi, hd] = decay * state + _dot((vT_aug * ws).astype(BF16), ks_h)
            m_ref[bi, hd] = jnp.broadcast_to(cum_row[:, L - 1:L] + b_end, m_ref.shape[2:])

    sts = []
    for bi in range(MLSTM_NB):
        sts.append(project(bi))
        gate_chain(sts[bi])
        if bi >= 1:
            recur(bi - 1, sts[bi - 1])
    recur(MLSTM_NB - 1, sts[MLSTM_NB - 1])


def _block_diag_dense(w):
    nblk, blk, _ = w.shape
    n = nblk * blk
    idx = jnp.arange(n) // blk
    return jnp.where(idx[:, None] == idx[None, :], jnp.tile(w.reshape(n, blk), (1, nblk)), 0.0)


def _mlstm(bx, xc, sbz, wq, wk, wv, w_gates, b_gates, out_g, skip):
    b, s, bw = bx.shape
    L = MLSTM_L
    nc = s // L
    nbb = MLSTM_NB
    assert b % nbb == 0 and s % L == 0
    per_head = wq.shape[0] // B_HEADS

    def head_blocks(w, transpose):
        blocks = [_block_diag_dense(w[h * per_head:(h + 1) * per_head]) for h in range(B_HEADS)]
        return jnp.stack([blk.T if transpose else blk for blk in blocks]).astype(BF16)

    wqt_d = head_blocks(wq, True)
    wk_d = head_blocks(wk, False)
    wvt_d = head_blocks(wv, True)
    hblk = (B_HEADS, B_HEAD_DIM, B_HEAD_DIM)
    wgq = w_gates[:bw].T.astype(BF16)
    wgv = w_gates[2 * bw:].T.astype(BF16)
    wgk = w_gates[bw:2 * bw].T.astype(BF16)
    const = lambda shape: pl.BlockSpec(shape, lambda i, j: (0,) * len(shape))
    tok = pl.BlockSpec((nbb, L, bw), lambda i, j: (i, j, 0))
    return pl.pallas_call(
        _mlstm_kernel,
        grid=(b // nbb, nc),
        in_specs=[tok, tok, tok,
                  const(hblk), const(hblk), const(hblk),
                  const((GATE_ROWS, bw)), const((GATE_ROWS, bw)), const((GATE_ROWS, bw)),
                  const((GATE_ROWS, 1)), const((1, bw)), const((1, bw))],
        out_specs=tok,
        out_shape=jax.ShapeDtypeStruct((b, s, bw), BF16),
        scratch_shapes=[pltpu.VMEM((nbb, B_HEADS, 2 * B_HEAD_DIM, B_HEAD_DIM), F32),
                        pltpu.VMEM((nbb, B_HEADS, 8, 128), F32)],
        compiler_params=pltpu.CompilerParams(
            dimension_semantics=("parallel", "arbitrary"), vmem_limit_bytes=VMEM_LIMIT),
        name="mlstm",
    )(bx, xc, sbz, wqt_d, wk_d, wvt_d, wgq, wgk, wgv,
      b_gates.reshape(GATE_ROWS, 1), out_g.reshape(1, bw), skip.reshape(1, bw))


def _tail_kernel(x_ref, ya_ref, yb_ref, mod0_ref, mod1_ref, lng_ref, wo0_ref, wi1_ref, clg_ref,
                 clb_ref, ws_ref, bst_ref, wo1_ref, o_ref):
    tm = x_ref.shape[1]
    sub = TAIL_SUB_ROWS
    n_sub = tm // sub
    aw = A_WIDTH
    mod1 = mod1_ref[0]
    gate0 = mod0_ref[0][2:3]
    t_iota = lax.broadcasted_iota(jnp.int32, (C_CHUNK, C_CHUNK), 0)
    s_iota = lax.broadcasted_iota(jnp.int32, (C_CHUNK, C_CHUNK), 1)
    tril = s_iota <= t_iota
    gw = C_WIDTH // C_GROUPS
    bst = bst_ref[...]
    wms = [jnp.where(tril, ws_ref[g], 0.0).astype(BF16) for g in range(C_GROUPS)]

    def out_proj0(r):
        rows = slice(r * sub, (r + 1) * sub)
        y0 = _dot(ya_ref[0, rows, :], wo0_ref[0:aw, :]) + _dot(yb_ref[0, rows, :], wo0_ref[aw:, :])
        x1 = x_ref[0, rows, :] + gate0 * y0
        h = _adaln_rmsnorm(x1, lng_ref[...], mod1[1:2], mod1[0:1]).astype(BF16)
        return x1, h

    def mix_inputs(p):
        u = _gelu_tanh(p[:, :C_WIDTH])
        v = _gelu_tanh(p[:, C_WIDTH:2 * C_WIDTH])
        mu = jnp.mean(v, axis=-1, keepdims=True)
        vc = v - mu
        var = jnp.mean(vc * vc, axis=-1, keepdims=True)
        vn = ((vc * lax.rsqrt(var + 1e-5)) * clg_ref[...] + clb_ref[...]).astype(BF16)
        return vn, u * _silu(p[:, 2 * C_WIDTH:])

    def spatial_gate(vn, gate):
        cols = []
        for g in range(C_GROUPS):
            rows = []
            for ch in range(sub // C_CHUNK):
                vg = vn[ch * C_CHUNK:(ch + 1) * C_CHUNK, g * gw:(g + 1) * gw]
                rows.append(_dot(wms[g], vg) + bst[:, g:g + 1])
            cols.append(jnp.concatenate(rows, axis=0))
        return (gate * jnp.concatenate(cols, axis=1)).astype(BF16)

    def out_proj1(r, x1, y1):
        o_ref[0, r * sub:(r + 1) * sub, :] = x1 + mod1[2:3] * _dot(y1, wo1_ref[...])

    heads = [out_proj0(r) for r in range(n_sub)]
    projs = [_dot(heads[r][1], wi1_ref[...]) for r in range(n_sub)]
    y1_prev = None
    for r in range(n_sub):
        vn, gate = mix_inputs(projs[r])
        if y1_prev is not None:
            out_proj1(r - 1, heads[r - 1][0], y1_prev)
        y1_prev = spatial_gate(vn, gate)
    out_proj1(n_sub - 1, heads[n_sub - 1][0], y1_prev)


def _tail(x, ya, yb, mod0, mod1, ln_g1, w_out0, w_in1, w_out1, c_ln_g, c_ln_b, c_ws, c_bs):
    b, s, d = x.shape
    tm = TAIL_ROW_TILE
    nt = s // tm
    const = lambda shape: pl.BlockSpec(shape, lambda i, j: (0,) * len(shape),
                                       pipeline_mode=pl.Buffered(1))
    half = pl.BlockSpec((1, tm, A_WIDTH), lambda i, j: (i, j, 0))
    full = pl.BlockSpec((1, tm, d), lambda i, j: (i, j, 0))
    modspec = pl.BlockSpec((1, 3, d), lambda i, j: (i, 0, 0))
    return pl.pallas_call(
        _tail_kernel,
        grid=(b, nt),
        in_specs=[full, half, half, modspec, modspec, const((1, d)), const((d, d)),
                  const((d, 3 * C_WIDTH)), const((1, d)), const((1, d)),
                  const((C_GROUPS, C_CHUNK, C_CHUNK)), const((C_CHUNK, C_GROUPS)), const((d, d))],
        out_specs=full,
        out_shape=jax.ShapeDtypeStruct((b, s, d), F32),
        compiler_params=pltpu.CompilerParams(
            dimension_semantics=("parallel", "parallel"), vmem_limit_bytes=VMEM_LIMIT),
        name="tail",
    )(x, ya, yb, mod0, mod1, ln_g1.reshape(1, d), w_out0.astype(BF16), w_in1.astype(BF16),
      c_ln_g.reshape(1, d), c_ln_b.reshape(1, d), c_ws, c_bs.T, w_out1.astype(BF16))


def kernel(x, c, ln_g, ada_w, ada_b, w_in, w_out, a_q_g, a_k_g, b_conv_w, b_conv_b, b_wq, b_wk, b_wv,
           b_w_gates, b_b_gates, b_out_g, b_skip, c_ln_g, c_ln_b, c_ws, c_bs):
    mods = _ada_mods(c, ada_w, ada_b)
    qT, vT, gT, k, bx, xc, sbz = _inproj0(x, mods[0], ln_g[0], w_in[0], a_q_g[0], a_k_g[0],
                                          b_conv_w[0], b_conv_b[0])
    ya = _moba(qT, k, vT, gT)
    yb = _mlstm(bx, xc, sbz, b_wq[0], b_wk[0], b_wv[0], b_w_gates[0], b_b_gates[0], b_out_g[0],
                b_skip[0])
    return _tail(x, ya, yb, mods[0], mods[1], ln_g[1], w_out[0], w_in[1], w_out[1],
                 c_ln_g[0], c_ln_b[0], c_ws[0], c_bs[0])
```

```python
import functools

import jax
import jax.numpy as jnp
from jax import lax
from jax.experimental import pallas as pl
from jax.experimental.pallas import tpu as pltpu

F32 = jnp.float32
BF16 = jnp.bfloat16

D_MODEL = 1024
A_HEADS = 8
A_HEAD_DIM = 64
A_WIDTH = A_HEADS * A_HEAD_DIM
MOBA_BLOCK = 256
MOBA_TOPK = 3
B_HEADS = 4
B_HEAD_DIM = 128
B_WIDTH = B_HEADS * B_HEAD_DIM
B_CONV = 4
C_GROUPS = 8
C_CHUNK = 128
C_WIDTH = D_MODEL
NEG = -1e30
LOG2E = 1.4426950408889634

MLSTM_L = 256
MLSTM_NB = 4
INPROJ_SUBTILES = 4
TAIL_ROW_TILE = 1024
TAIL_SUB_ROWS = 256
CONV_HALO = 8
GATE_ROWS = 2 * B_HEADS
PV_ONES_ROWS = 16
MOBA_NB = 2
VMEM_LIMIT = 48 * 1024 * 1024
MOBA_VMEM_LIMIT = 56 * 1024 * 1024


def _silu(x):
    return x * jax.nn.sigmoid(x)


def _gelu_tanh(x):
    return 0.5 * x * (1.0 + jnp.tanh(0.7978845608028654 * (x + 0.044715 * (x * x * x))))


def _log_sigmoid(x):
    return jnp.minimum(x, 0.0) - jnp.log(1.0 + jnp.exp(-jnp.abs(x)))


def _split_bf16(x):
    hi = x.astype(BF16)
    lo = (x - hi.astype(F32)).astype(BF16)
    return hi, lo


def _dot(a, b):
    return jnp.dot(a, b, preferred_element_type=F32)


def _dot_nt(a, b):
    return lax.dot_general(a, b, (((1,), (1,)), ((), ())), preferred_element_type=F32)


def _adaln_rmsnorm(x, ln_g, scale, shift):
    y = x * lax.rsqrt(jnp.mean(x * x, axis=-1, keepdims=True) + 1e-6)
    return (y * ln_g) * (1.0 + scale) + shift


def _ada_kernel(c_ref, w_ref, b_ref, o_ref):
    cs_hi, cs_lo = _split_bf16(_silu(c_ref[...]))
    w_hi, w_lo = _split_bf16(w_ref[0])
    o_ref[0] = _dot(cs_hi, w_hi) + _dot(cs_lo, w_hi) + _dot(cs_hi, w_lo) + b_ref[0]


def _ada_mods(c, ada_w, ada_b):
    depth, d, d3 = ada_w.shape
    b = c.shape[0]
    bp = 8
    cp = jnp.zeros((bp, d), F32).at[:b].set(c)
    nt = d3 // d
    out = pl.pallas_call(
        _ada_kernel,
        grid=(depth, nt),
        in_specs=[
            pl.BlockSpec((bp, d), lambda l, n: (0, 0)),
            pl.BlockSpec((1, d, d), lambda l, n: (l, 0, n)),
            pl.BlockSpec((1, 1, d), lambda l, n: (l, 0, n)),
        ],
        out_specs=pl.BlockSpec((1, bp, d), lambda l, n: (l, 0, n)),
        out_shape=jax.ShapeDtypeStruct((depth, bp, d3), F32),
        name="ada_mods",
    )(cp, ada_w, ada_b.reshape(depth, 1, d3))
    return out[:, :b].reshape(depth, b, 3, d)


def _inproj0_kernel(x_ref, mod_ref, lng_ref, wt_ref, w_ref, qg_ref, kg_ref,
                    qT_ref, vT_ref, gT_ref, k_ref, bx_ref, sbz_ref):
    tm = MOBA_BLOCK
    aw = A_WIDTH
    mod = mod_ref[0]
    qg = qg_ref[...] * (A_HEAD_DIM ** -0.5 * LOG2E)
    kg = kg_ref[...]
    hs = [_adaln_rmsnorm(x_ref[0, r * tm:(r + 1) * tm], lng_ref[...], mod[1:2], mod[0:1]).astype(BF16)
          for r in range(INPROJ_SUBTILES)]
    for r in range(INPROJ_SUBTILES):
        h = hs[r]
        rows = slice(r * tm, (r + 1) * tm)
        pt = _dot_nt(wt_ref[...], h)
        p = _dot(h, w_ref[...])
        kn = []
        for hd in range(A_HEADS):
            lo = hd * A_HEAD_DIM
            q = pt[lo:lo + A_HEAD_DIM]
            rq = lax.rsqrt(jnp.mean(q * q, axis=0, keepdims=True) + 1e-6)
            qT_ref[0, r, lo:lo + A_HEAD_DIM, :] = ((q * rq) * qg).astype(BF16)
            k = pt[aw + lo:aw + lo + A_HEAD_DIM]
            rk = lax.rsqrt(jnp.mean(k * k, axis=0, keepdims=True) + 1e-6)
            kn.append((k * rk) * kg)
        k_ref[0, rows, :] = jnp.concatenate(kn, axis=0).T.astype(BF16)
        vT_ref[0, r] = pt[2 * aw:3 * aw].astype(BF16)
        gT_ref[0, r] = _silu(pt[3 * aw:]).astype(BF16)
        bx_ref[0, rows, :] = p[:, :B_WIDTH].astype(BF16)
        sbz_ref[0, rows, :] = _silu(p[:, B_WIDTH:]).astype(BF16)


def _transpose_cast_kernel(w_ref, o_ref):
    o_ref[...] = w_ref[0].T.astype(BF16)


def _transposed_bf16(w_stack, layer, n_cols):
    _, d, _ = w_stack.shape
    cb = 2 * MOBA_BLOCK
    return pl.pallas_call(
        _transpose_cast_kernel,
        grid=(n_cols // cb,),
        in_specs=[pl.BlockSpec((1, d, cb), lambda c: (layer, 0, c))],
        out_specs=pl.BlockSpec((cb, d), lambda c: (c, 0)),
        out_shape=jax.ShapeDtypeStruct((n_cols, d), BF16),
        name="transpose_cast",
    )(w_stack)


def _inproj0(x, mod, ln_g, w_in_stack, q_g, k_g):
    w_in = w_in_stack[0]
    b, s, d = x.shape
    blk = MOBA_BLOCK
    sub = INPROJ_SUBTILES
    tm = sub * blk
    nt = s // tm
    aw, bw = A_WIDTH, B_WIDTH
    assert w_in.shape == (d, 4 * aw + 2 * bw)
    wt = _transposed_bf16(w_in_stack, 0, 4 * aw)
    wn = w_in[:, 4 * aw:].astype(BF16)
    t_shape = jax.ShapeDtypeStruct((b, s // blk, aw, blk), BF16)
    n_shape = jax.ShapeDtypeStruct((b, s, aw), BF16)
    t_spec = pl.BlockSpec((1, sub, aw, blk), lambda i, j: (i, j, 0, 0))
    n_spec = pl.BlockSpec((1, tm, aw), lambda i, j: (i, j, 0))
    const = lambda shape: pl.BlockSpec(shape, lambda i, j: (0,) * len(shape),
                                       pipeline_mode=pl.Buffered(1))
    return pl.pallas_call(
        _inproj0_kernel,
        grid=(b, nt),
        in_specs=[
            pl.BlockSpec((1, tm, d), lambda i, j: (i, j, 0)),
            pl.BlockSpec((1, 3, d), lambda i, j: (i, 0, 0)),
            const((1, d)),
            const((4 * aw, d)),
            const((d, 2 * bw)),
            const((A_HEAD_DIM, 1)),
            const((A_HEAD_DIM, 1)),
        ],
        out_specs=[t_spec, t_spec, t_spec, n_spec, n_spec, n_spec],
        out_shape=[t_shape, t_shape, t_shape, n_shape, n_shape, n_shape],
        compiler_params=pltpu.CompilerParams(
            dimension_semantics=("parallel", "parallel"), vmem_limit_bytes=VMEM_LIMIT),
        name="inproj0",
    )(x, mod, ln_g.reshape(1, d), wt, wn, q_g.reshape(A_HEAD_DIM, 1), k_g.reshape(A_HEAD_DIM, 1))


def _moba_kernel(qT_ref, k_ref, vT_ref, gT_ref, o_ref, kmean_ref, sel_ref, qm_ref, raw_ref, p_ref,
                 m_ref, alpha_ref, acc_ref, *, nb):
    blk = MOBA_BLOCK
    dh = A_HEAD_DIM
    pw = 2 * dh
    streams = range(MOBA_NB)
    j = pl.program_id(1)

    @pl.when(j == 0)
    def _():
        for z in streams:
            for n in range(nb):
                kb = k_ref[z, n * blk:(n + 1) * blk, :].astype(F32)
                kmean_ref[z, n:n + 1, :] = jnp.mean(kb, axis=0, keepdims=True)

    d_iota = lax.broadcasted_iota(jnp.int32, (pw, blk), 0)
    for z in streams:
        for hd in range(A_HEADS):
            pr = hd // 2
            q2 = qT_ref[z, 0, pr * pw:(pr + 1) * pw, :]
            keep = (d_iota < dh) if hd % 2 == 0 else (d_iota >= dh)
            qm_ref[z, hd] = jnp.where(keep, q2, jnp.zeros_like(q2))

    def score_dots(z, n, slot):
        row0 = pl.multiple_of(n * blk, blk)
        for hd in range(A_HEADS):
            pr = hd // 2
            kt = k_ref[z, pl.ds(row0, blk), pr * pw:(pr + 1) * pw]
            raw_ref[z, slot, hd] = _dot(kt, qm_ref[z, hd]).astype(BF16)

    def value_dots(z, n, slot):
        ones = jnp.ones((PV_ONES_ROWS, blk), BF16)
        return [_dot(jnp.concatenate([vT_ref[z, n, hd * dh:(hd + 1) * dh, :], ones], axis=0),
                     p_ref[z, slot, hd]) for hd in range(A_HEADS)]

    sel_scores = []
    for z in streams:
        km_hi, km_lo = _split_bf16(kmean_ref[z])
        km2 = jnp.concatenate([km_hi, km_lo], axis=0)
        per_head = []
        for hd in range(A_HEADS):
            pr = hd // 2
            r2 = _dot(km2[:, pr * pw:(pr + 1) * pw], qm_ref[z, hd])
            per_head.append(r2[0:nb] + r2[nb:2 * nb])
        sel_scores.append(per_head)

    for z in streams:
        score_dots(z, j, 0)
    for z in streams:
        score_dots(z, 0, 1)

    n_iota = lax.broadcasted_iota(jnp.int32, (nb, blk), 0)
    past = n_iota < j
    for z in streams:
        for hd in range(A_HEADS):
            sc = jnp.where(past, sel_scores[z][hd], NEG)
            sel = jnp.zeros((nb, blk), F32)
            for _ in range(MOBA_TOPK):
                mx = jnp.max(sc, axis=0, keepdims=True)
                first = jnp.min(jnp.where(sc == mx, n_iota, nb), axis=0, keepdims=True)
                pick = n_iota == first
                sel = jnp.where(pick, 1.0, sel)
                sc = jnp.where(pick, -jnp.inf, sc)
            sel_ref[z, hd, 0:nb, :] = jnp.where(past, sel, 0.0)
            sel_ref[z, hd, nb:nb + 1, :] = jnp.ones((1, blk), F32)

    k_iota = lax.broadcasted_iota(jnp.int32, (blk, blk), 0)
    q_iota = lax.broadcasted_iota(jnp.int32, (blk, blk), 1)
    causal_bias = jnp.where(k_iota <= q_iota, 0.0, NEG).astype(BF16)
    for z in streams:
        for hd in range(A_HEADS):
            s = raw_ref[z, 0, hd] + causal_bias
            m_new = jnp.max(s, axis=0, keepdims=True)
            p_ref[z, 0, hd] = jnp.exp2(s - m_new)
            m_ref[z, hd] = m_new.astype(F32)
            acc_ref[z, hd] = jnp.zeros(acc_ref.shape[2:], F32)

    def step(i, cur):
        nxt = 1 - cur
        n = i - 1
        prev = jnp.where(i == 1, j, i - 2)
        prev_sel = jnp.where(i == 1, nb, i - 2)
        row0 = pl.multiple_of(jnp.minimum(i, nb - 1) * blk, blk)
        ones = jnp.ones((PV_ONES_ROWS, blk), BF16)
        pvs = {}
        for hd in range(A_HEADS):
            pr = hd // 2
            for z in streams:
                kt = k_ref[z, pl.ds(row0, blk), pr * pw:(pr + 1) * pw]
                raw_ref[z, nxt, hd] = _dot(kt, qm_ref[z, hd]).astype(BF16)
                pvs[z, hd] = _dot(
                    jnp.concatenate([vT_ref[z, prev, hd * dh:(hd + 1) * dh, :], ones], axis=0),
                    p_ref[z, nxt, hd])
                s = raw_ref[z, cur, hd]
                selb = sel_ref[z, hd, pl.ds(n, 1), :] > 0.5
                smax = jnp.where(selb, jnp.max(s, axis=0, keepdims=True).astype(F32), NEG)
                m_old = m_ref[z, hd]
                m_new = jnp.maximum(m_old, smax)
                alpha_ref[z, hd] = jnp.exp2(m_old - m_new)
                m_ref[z, hd] = m_new
                p_ref[z, cur, hd] = jnp.exp2(s - m_new.astype(BF16))
        for hd in range(A_HEADS):
            for z in streams:
                keep = sel_ref[z, hd, pl.ds(prev_sel, 1), :] > 0.5
                acc_ref[z, hd] = alpha_ref[z, hd] * (acc_ref[z, hd] + jnp.where(keep, pvs[z, hd], 0.0))

    def body(ii, carry):
        step(2 * ii + 1, 1)
        step(2 * ii + 2, 0)
        return carry

    trips = (j + 1) // 2
    lax.fori_loop(0, trips, body, 0)

    last = jnp.where(j == 0, j, 2 * trips - 1)
    last_sel = jnp.where(j == 0, nb, 2 * trips - 1)
    last_pvs = [value_dots(z, last, 0) for z in streams]
    for z in streams:
        for pr in range(A_HEADS // 2):
            halves = []
            for hd in (2 * pr, 2 * pr + 1):
                keep = sel_ref[z, hd, pl.ds(last_sel, 1), :] > 0.5
                acc = acc_ref[z, hd] + jnp.where(keep, last_pvs[z][hd], 0.0)
                halves.append(acc[0:dh] / acc[dh:dh + 1])
            oT = jnp.concatenate(halves, axis=0)
            oT = oT * gT_ref[z, 0, pr * pw:(pr + 1) * pw, :].astype(F32)
            o_ref[z, :, pr * pw:(pr + 1) * pw] = oT.T.astype(BF16)


def _moba(qT, k, vT, gT):
    b, nb, aw, blk = qT.shape
    s = nb * blk
    pw = 2 * A_HEAD_DIM
    z = MOBA_NB
    assert b % z == 0
    q_spec = pl.BlockSpec((z, 1, aw, blk), lambda i, j: (i, j, 0, 0))
    row = lambda: pltpu.VMEM((z, A_HEADS, 1, blk), F32)
    return pl.pallas_call(
        functools.partial(_moba_kernel, nb=nb),
        grid=(b // z, nb),
        in_specs=[
            q_spec,
            pl.BlockSpec((z, s, aw), lambda i, j: (i, 0, 0)),
            pl.BlockSpec((z, nb, aw, blk), lambda i, j: (i, 0, 0, 0)),
            q_spec,
        ],
        out_specs=pl.BlockSpec((z, blk, aw), lambda i, j: (i, j, 0)),
        out_shape=jax.ShapeDtypeStruct((b, s, aw), BF16),
        scratch_shapes=[
            pltpu.VMEM((z, nb, aw), F32),
            pltpu.VMEM((z, A_HEADS, nb + 8, blk), F32),
            pltpu.VMEM((z, A_HEADS, pw, blk), BF16),
            pltpu.VMEM((z, 2, A_HEADS, blk, blk), BF16),
            pltpu.VMEM((z, 2, A_HEADS, blk, blk), BF16),
            row(), row(),
            pltpu.VMEM((z, A_HEADS, A_HEAD_DIM + PV_ONES_ROWS, blk), F32),
        ],
        compiler_params=pltpu.CompilerParams(
            dimension_semantics=("parallel", "arbitrary"), vmem_limit_bytes=MOBA_VMEM_LIMIT),
        name="moba",
    )(qT, k, vT, gT)


def _mlstm_kernel(bx_ref, sbz_ref, cw_ref, cb_ref, wqt_ref, wk_ref, wvt_ref, wgq_ref, wgk_ref,
                  wgv_ref, bg_ref, og_ref, skip_ref, o_ref, xbuf_ref, state_ref, m_ref):
    L = MLSTM_L
    dh = B_HEAD_DIM
    c = pl.program_id(1)

    @pl.when(c == 0)
    def _():
        xbuf_ref[:, 0:CONV_HALO, :] = jnp.zeros((MLSTM_NB, CONV_HALO, B_WIDTH), F32)
        state_ref[...] = jnp.zeros_like(state_ref)
        m_ref[...] = jnp.zeros_like(m_ref)

    cw = cw_ref[...]
    cb = cb_ref[...]
    kscale = dh ** -0.5
    s_iota = lax.broadcasted_iota(jnp.int32, (L, L), 0)
    t_iota = lax.broadcasted_iota(jnp.int32, (L, L), 1)
    tri = s_iota <= t_iota
    tri_b = jnp.where(tri, 1.0, 0.0).astype(BF16)
    ones = jnp.ones((dh, L), F32)

    def project(bi):
        bx_b = bx_ref[bi]
        xbuf_ref[bi, CONV_HALO:CONV_HALO + L, :] = bx_b.astype(F32)
        st = dict(xc=[], ks=[], vT=[], scores=[], inter=[], state=[])
        gT = bg_ref[...]
        for hd in range(B_HEADS):
            lo = hd * dh
            conv = cb[:, lo:lo + dh]
            for i in range(B_CONV):
                off = CONV_HALO - (B_CONV - 1) + i
                conv = conv + cw[i:i + 1, lo:lo + dh] * xbuf_ref[bi, off:off + L, lo:lo + dh]
            xc = _silu(conv)
            xc_b = xc.astype(BF16)
            qT_h = _dot_nt(wqt_ref[hd], xc_b).astype(BF16)
            vT_h = _dot_nt(wvt_ref[hd], bx_b[:, lo:lo + dh])
            k_h = _dot(xc_b, wk_ref[hd])
            ks_h = (k_h * kscale).astype(BF16)
            state = state_ref[bi, hd]
            st["scores"].append(_dot(ks_h, qT_h))
            st["inter"].append(_dot(state.astype(BF16), qT_h))
            gT = gT + (_dot(wgq_ref[:, lo:lo + dh], qT_h)
                       + _dot_nt(wgk_ref[:, lo:lo + dh], k_h.astype(BF16))
                       + _dot(wgv_ref[:, lo:lo + dh], vT_h.astype(BF16)))
            st["xc"].append(xc); st["ks"].append(ks_h); st["vT"].append(vT_h); st["state"].append(state)
        tail = xbuf_ref[bi, L:L + CONV_HALO, :]
        xbuf_ref[bi, 0:CONV_HALO, :] = tail
        st["gT"] = gT
        return st

    def gate_chain(st):
        gT = st["gT"]
        lf_hi, lf_lo = _split_bf16(_log_sigmoid(gT))
        cum = (_dot(lf_hi, tri_b) + _dot(lf_lo, tri_b))[B_HEADS:2 * B_HEADS]
        a = gT[0:B_HEADS] - cum
        st["cum"], st["a"] = cum, a
        st["a_cols"] = jnp.concatenate([a, jnp.zeros((128 - B_HEADS, L), F32)], axis=0).T

    def recur(bi, st):
        a, cum, a_cols = st["a"], st["cum"], st["a_cols"]
        for hd in range(B_HEADS):
            lo = hd * dh
            ks_h, state, inter = st["ks"][hd], st["state"][hd], st["inter"][hd]
            vT_aug = jnp.concatenate([st["vT"][hd], ones], axis=0)
            a_row = a[hd:hd + 1]
            a_col = a_cols[:, hd:hd + 1]
            cum_row = cum[hd:hd + 1]
            m_prev = m_ref[bi, hd][0:1, 0:1]
            amax = jnp.max(jnp.where(tri, a_col, NEG), axis=0, keepdims=True)
            b_row = jnp.maximum(m_prev, amax)
            dmat = jnp.exp(jnp.where(tri, a_col - b_row, NEG))
            w_intra = (st["scores"][hd] * dmat).astype(BF16)
            intra = _dot(vT_aug.astype(BF16), w_intra)
            w_inter = jnp.exp(m_prev - b_row)
            num = w_inter * inter[0:dh] + intra[0:dh]
            den = w_inter * inter[dh:dh + 1] + intra[dh:dh + 1]
            hc = num * (1.0 / jnp.maximum(jnp.abs(den), jnp.exp(-(cum_row + b_row))))
            mu = jnp.mean(hc, axis=0, keepdims=True)
            hcc = hc - mu
            var = jnp.mean(hcc * hcc, axis=0, keepdims=True)
            hb = (hcc * lax.rsqrt(var + 1e-5)).T * og_ref[:, lo:lo + dh]
            yb = (hb + skip_ref[:, lo:lo + dh] * st["xc"][hd]) * sbz_ref[bi, :, lo:lo + dh].astype(F32)
            o_ref[bi, :, lo:lo + dh] = yb.astype(BF16)
            b_end = b_row[:, L - 1:L]
            ws = jnp.exp(a_row - b_end)
            decay = jnp.exp(m_prev - b_end)
            state_ref[bi, hd] = decay * state + _dot((vT_aug * ws).astype(BF16), ks_h)
            m_ref[bi, hd] = jnp.broadcast_to(cum_row[:, L - 1:L] + b_end, m_ref.shape[2:])

    sts = []
    for bi in range(MLSTM_NB):
        sts.append(project(bi))
        gate_chain(sts[bi])
    for bi in range(MLSTM_NB):
        recur(bi, sts[bi])


def _block_diag_dense(w):
    nblk, blk, _ = w.shape
    n = nblk * blk
    idx = jnp.arange(n) // blk
    return jnp.where(idx[:, None] == idx[None, :], jnp.tile(w.reshape(n, blk), (1, nblk)), 0.0)


def _mlstm(bx, sbz, conv_w, conv_b, wq, wk, wv, w_gates, b_gates, out_g, skip):
    b, s, bw = bx.shape
    L = MLSTM_L
    nc = s // L
    nbb = MLSTM_NB
    assert b % nbb == 0 and s % L == 0
    per_head = wq.shape[0] // B_HEADS

    def head_blocks(w, transpose):
        blocks = [_block_diag_dense(w[h * per_head:(h + 1) * per_head]) for h in range(B_HEADS)]
        return jnp.stack([blk.T if transpose else blk for blk in blocks]).astype(BF16)

    wqt_d = head_blocks(wq, True)
    wk_d = head_blocks(wk, False)
    wvt_d = head_blocks(wv, True)
    hblk = (B_HEADS, B_HEAD_DIM, B_HEAD_DIM)
    wgq = w_gates[:bw].T.astype(BF16)
    wgv = w_gates[2 * bw:].T.astype(BF16)
    wgk = w_gates[bw:2 * bw].T.astype(BF16)
    const = lambda shape: pl.BlockSpec(shape, lambda i, j: (0,) * len(shape))
    tok = pl.BlockSpec((nbb, L, bw), lambda i, j: (i, j, 0))
    return pl.pallas_call(
        _mlstm_kernel,
        grid=(b // nbb, nc),
        in_specs=[tok, tok, const((B_CONV, bw)), const((1, bw)),
                  const(hblk), const(hblk), const(hblk),
                  const((GATE_ROWS, bw)), const((GATE_ROWS, bw)), const((GATE_ROWS, bw)),
                  const((GATE_ROWS, 1)), const((1, bw)), const((1, bw))],
        out_specs=tok,
        out_shape=jax.ShapeDtypeStruct((b, s, bw), BF16),
        scratch_shapes=[pltpu.VMEM((nbb, L + CONV_HALO, bw), F32),
                        pltpu.VMEM((nbb, B_HEADS, 2 * B_HEAD_DIM, B_HEAD_DIM), F32),
                        pltpu.VMEM((nbb, B_HEADS, 8, 128), F32)],
        compiler_params=pltpu.CompilerParams(
            dimension_semantics=("parallel", "arbitrary"), vmem_limit_bytes=VMEM_LIMIT),
        name="mlstm",
    )(bx, sbz, conv_w, conv_b.reshape(1, bw), wqt_d, wk_d, wvt_d, wgq, wgk, wgv,
      b_gates.reshape(GATE_ROWS, 1), out_g.reshape(1, bw), skip.reshape(1, bw))


def _tail_kernel(x_ref, ya_ref, yb_ref, mod0_ref, mod1_ref, lng_ref, wo0_ref, wi1_ref, clg_ref,
                 clb_ref, ws_ref, bst_ref, wo1_ref, o_ref):
    tm = x_ref.shape[1]
    sub = TAIL_SUB_ROWS
    n_sub = tm // sub
    aw = A_WIDTH
    mod1 = mod1_ref[0]
    gate0 = mod0_ref[0][2:3]
    t_iota = lax.broadcasted_iota(jnp.int32, (C_CHUNK, C_CHUNK), 0)
    s_iota = lax.broadcasted_iota(jnp.int32, (C_CHUNK, C_CHUNK), 1)
    tril = s_iota <= t_iota
    gw = C_WIDTH // C_GROUPS
    bst = bst_ref[...]
    wms = [jnp.where(tril, ws_ref[g], 0.0).astype(BF16) for g in range(C_GROUPS)]

    def out_proj0(r):
        rows = slice(r * sub, (r + 1) * sub)
        y0 = _dot(ya_ref[0, rows, :], wo0_ref[0:aw, :]) + _dot(yb_ref[0, rows, :], wo0_ref[aw:, :])
        x1 = x_ref[0, rows, :] + gate0 * y0
        h = _adaln_rmsnorm(x1, lng_ref[...], mod1[1:2], mod1[0:1]).astype(BF16)
        return x1, h

    def mix_inputs(p):
        u = _gelu_tanh(p[:, :C_WIDTH])
        v = _gelu_tanh(p[:, C_WIDTH:2 * C_WIDTH])
        mu = jnp.mean(v, axis=-1, keepdims=True)
        vc = v - mu
        var = jnp.mean(vc * vc, axis=-1, keepdims=True)
        vn = ((vc * lax.rsqrt(var + 1e-5)) * clg_ref[...] + clb_ref[...]).astype(BF16)
        return vn, u * _silu(p[:, 2 * C_WIDTH:])

    def spatial_gate(vn, gate):
        cols = []
        for g in range(C_GROUPS):
            rows = []
            for ch in range(sub // C_CHUNK):
                vg = vn[ch * C_CHUNK:(ch + 1) * C_CHUNK, g * gw:(g + 1) * gw]
                rows.append(_dot(wms[g], vg) + bst[:, g:g + 1])
            cols.append(jnp.concatenate(rows, axis=0))
        return (gate * jnp.concatenate(cols, axis=1)).astype(BF16)

    def out_proj1(r, x1, y1):
        o_ref[0, r * sub:(r + 1) * sub, :] = x1 + mod1[2:3] * _dot(y1, wo1_ref[...])

    heads = [out_proj0(r) for r in range(n_sub)]
    projs = [_dot(heads[r][1], wi1_ref[...]) for r in range(n_sub)]
    y1_prev = None
    for r in range(n_sub):
        vn, gate = mix_inputs(projs[r])
        if y1_prev is not None:
            out_proj1(r - 1, heads[r - 1][0], y1_prev)
        y1_prev = spatial_gate(vn, gate)
    out_proj1(n_sub - 1, heads[n_sub - 1][0], y1_prev)


def _tail(x, ya, yb, mod0, mod1, ln_g1, w_out0, w_in1, w_out1, c_ln_g, c_ln_b, c_ws, c_bs):
    b, s, d = x.shape
    tm = TAIL_ROW_TILE
    nt = s // tm
    const = lambda shape: pl.BlockSpec(shape, lambda i, j: (0,) * len(shape),
                                       pipeline_mode=pl.Buffered(1))
    half = pl.BlockSpec((1, tm, A_WIDTH), lambda i, j: (i, j, 0))
    full = pl.BlockSpec((1, tm, d), lambda i, j: (i, j, 0))
    modspec = pl.BlockSpec((1, 3, d), lambda i, j: (i, 0, 0))
    return pl.pallas_call(
        _tail_kernel,
        grid=(b, nt),
        in_specs=[full, half, half, modspec, modspec, const((1, d)), const((d, d)),
                  const((d, 3 * C_WIDTH)), const((1, d)), const((1, d)),
                  const((C_GROUPS, C_CHUNK, C_CHUNK)), const((C_CHUNK, C_GROUPS)), const((d, d))],
        out_specs=full,
        out_shape=jax.ShapeDtypeStruct((b, s, d), F32),
        compiler_params=pltpu.CompilerParams(
            dimension_semantics=("parallel", "parallel"), vmem_limit_bytes=VMEM_LIMIT),
        name="tail",
    )(x, ya, yb, mod0, mod1, ln_g1.reshape(1, d), w_out0.astype(BF16), w_in1.astype(BF16),
      c_ln_g.reshape(1, d), c_ln_b.reshape(1, d), c_ws, c_bs.T, w_out1.astype(BF16))


def kernel(x, c, ln_g, ada_w, ada_b, w_in, w_out, a_q_g, a_k_g, b_conv_w, b_conv_b, b_wq, b_wk, b_wv,
           b_w_gates, b_b_gates, b_out_g, b_skip, c_ln_g, c_ln_b, c_ws, c_bs):
    mods = _ada_mods(c, ada_w, ada_b)
    qT, vT, gT, k, bx, sbz = _inproj0(x, mods[0], ln_g[0], w_in, a_q_g[0], a_k_g[0])
    ya = _moba(qT, k, vT, gT)
    yb = _mlstm(bx, sbz, b_conv_w[0], b_conv_b[0], b_wq[0], b_wk[0], b_wv[0], b_w_gates[0],
                b_b_gates[0], b_out_g[0], b_skip[0])
    return _tail(x, ya, yb, mods[0], mods[1], ln_g[1], w_out[0], w_in[1], w_out[1],
                 c_ln_g[0], c_ln_b[0], c_ws[0], c_bs[0])
```

```python
import functools

import jax
import jax.numpy as jnp
from jax import lax
from jax.experimental import pallas as pl
from jax.experimental.pallas import tpu as pltpu

F32 = jnp.float32
BF16 = jnp.bfloat16

D_MODEL = 1024
A_HEADS = 8
A_HEAD_DIM = 64
A_WIDTH = A_HEADS * A_HEAD_DIM
MOBA_BLOCK = 256
MOBA_TOPK = 3
B_HEADS = 4
B_HEAD_DIM = 128
B_WIDTH = B_HEADS * B_HEAD_DIM
B_CONV = 4
C_GROUPS = 8
C_CHUNK = 128
C_WIDTH = D_MODEL
NEG = -1e30
LOG2E = 1.4426950408889634

V7X_SUBLANES = 8
V7X_LANES = 128

MLSTM_L = 256
MLSTM_NB = 4
INPROJ_SUBTILES = 4
TAIL_ROW_TILE = 1024
TAIL_SUB_ROWS = 256
CONV_HALO = 8
GATE_ROWS = 2 * B_HEADS
PV_ONES_ROWS = 16
MOBA_NB = 2
VMEM_LIMIT = 48 * 1024 * 1024
MOBA_VMEM_LIMIT = 56 * 1024 * 1024


def _silu(x):
    return x * jax.nn.sigmoid(x)


def _gelu_tanh(x):
    return 0.5 * x * (1.0 + jnp.tanh(0.7978845608028654 * (x + 0.044715 * (x * x * x))))


def _log_sigmoid(x):
    return jnp.minimum(x, 0.0) - jnp.log(1.0 + jnp.exp(-jnp.abs(x)))


def _split_bf16(x):
    hi = x.astype(BF16)
    lo = (x - hi.astype(F32)).astype(BF16)
    return hi, lo


def _dot(a, b):
    return jnp.dot(a, b, preferred_element_type=F32)


def _dot_nt(a, b):
    return lax.dot_general(a, b, (((1,), (1,)), ((), ())), preferred_element_type=F32)


def _adaln_rmsnorm(x, ln_g, scale, shift):
    y = x * lax.rsqrt(jnp.mean(x * x, axis=-1, keepdims=True) + 1e-6)
    return (y * ln_g) * (1.0 + scale) + shift


def _ada_kernel(c_ref, w_ref, b_ref, o_ref):
    cs_hi, cs_lo = _split_bf16(_silu(c_ref[...]))
    w_hi, w_lo = _split_bf16(w_ref[0])
    o_ref[0] = _dot(cs_hi, w_hi) + _dot(cs_lo, w_hi) + _dot(cs_hi, w_lo) + b_ref[0]


def _ada_mods(c, ada_w, ada_b):
    depth, d, d3 = ada_w.shape
    b = c.shape[0]
    bp = V7X_SUBLANES * pl.cdiv(b, V7X_SUBLANES)
    cp =jnp.zeros((bp, d), F32).at[:b].set(c)
    nt = d3 // d
    out = pl.pallas_call(
        _ada_kernel,
        grid=(depth, nt),
        in_specs=[
            pl.BlockSpec((bp, d), lambda l, n: (0, 0)),
            pl.BlockSpec((1, d, d), lambda l, n: (l, 0, n)),
            pl.BlockSpec((1, 1, d), lambda l, n: (l, 0, n)),
        ],
        out_specs=pl.BlockSpec((1, bp, d), lambda l, n: (l, 0, n)),
        out_shape=jax.ShapeDtypeStruct((depth, bp, d3), F32),
        name="ada_mods",
    )(cp, ada_w, ada_b.reshape(depth, 1, d3))
    return out[:, :b].reshape(depth, b, 3, d)


def _inproj0_kernel(x_ref, mod_ref, lng_ref, wt_ref, w_ref, qg_ref, kg_ref,
                    qT_ref, vT_ref, gT_ref, k_ref, bx_ref, sbz_ref):
    tm = MOBA_BLOCK
    aw = A_WIDTH
    mod = mod_ref[0]
    qg = qg_ref[...] * (A_HEAD_DIM ** -0.5 * LOG2E)
    kg = kg_ref[...]
    hs = [_adaln_rmsnorm(x_ref[0, r * tm:(r + 1) * tm], lng_ref[...], mod[1:2], mod[0:1]).astype(BF16)
          for r in range(INPROJ_SUBTILES)]
    for r in range(INPROJ_SUBTILES):
        h = hs[r]
        rows = slice(r * tm, (r + 1) * tm)
        pt = _dot_nt(wt_ref[...], h)
        p = _dot(h, w_ref[...])
        kn = []
        for hd in range(A_HEADS):
            lo = hd * A_HEAD_DIM
            q = pt[lo:lo + A_HEAD_DIM]
            rq = lax.rsqrt(jnp.mean(q * q, axis=0, keepdims=True) + 1e-6)
            qT_ref[0, r, lo:lo + A_HEAD_DIM, :] = ((q * rq) * qg).astype(BF16)
            k = pt[aw + lo:aw + lo + A_HEAD_DIM]
            rk = lax.rsqrt(jnp.mean(k * k, axis=0, keepdims=True) + 1e-6)
            kn.append((k * rk) * kg)
        k_ref[0, rows, :] = jnp.concatenate(kn, axis=0).T.astype(BF16)
        vT_ref[0, r] = pt[2 * aw:3 * aw].astype(BF16)
        gT_ref[0, r] = _silu(pt[3 * aw:]).astype(BF16)
        bx_ref[0, rows, :] = p[:, :B_WIDTH].astype(BF16)
        sbz_ref[0, rows, :] = _silu(p[:, B_WIDTH:]).astype(BF16)


def _inproj0(x, mod, ln_g, w_in, q_g, k_g):
    b, s, d = x.shape
    blk = MOBA_BLOCK
    sub = INPROJ_SUBTILES
    tm = sub * blk
    nt = s // tm
    aw, bw = A_WIDTH, B_WIDTH
    assert w_in.shape == (d, 4 * aw + 2 * bw)
    wt = w_in[:, :4 * aw].T.astype(BF16)
    wn = w_in[:, 4 * aw:].astype(BF16)
    t_shape = jax.ShapeDtypeStruct((b, s // blk, aw, blk), BF16)
    n_shape = jax.ShapeDtypeStruct((b, s, aw), BF16)
    t_spec = pl.BlockSpec((1, sub, aw, blk), lambda i, j: (i, j, 0, 0))
    n_spec = pl.BlockSpec((1, tm, aw), lambda i, j: (i, j, 0))
    const = lambda shape: pl.BlockSpec(shape, lambda i, j: (0,) * len(shape),
                                       pipeline_mode=pl.Buffered(1))
    return pl.pallas_call(
        _inproj0_kernel,
        grid=(b, nt),
        in_specs=[
            pl.BlockSpec((1, tm, d), lambda i, j: (i, j, 0)),
            pl.BlockSpec((1, 3, d), lambda i, j: (i, 0, 0)),
            const((1, d)),
            const((4 * aw, d)),
            const((d, 2 * bw)),
            const((A_HEAD_DIM, 1)),
            const((A_HEAD_DIM, 1)),
        ],
        out_specs=[t_spec, t_spec, t_spec, n_spec, n_spec, n_spec],
        out_shape=[t_shape, t_shape, t_shape, n_shape, n_shape, n_shape],
        compiler_params=pltpu.CompilerParams(
            dimension_semantics=("parallel", "parallel"), vmem_limit_bytes=VMEM_LIMIT),
        name="inproj0",
    )(x, mod, ln_g.reshape(1, d), wt, wn, q_g.reshape(A_HEAD_DIM, 1), k_g.reshape(A_HEAD_DIM, 1))


def _moba_kernel(qT_ref, k_ref, vT_ref, gT_ref, o_ref, kmean_ref, sel_ref, qm_ref, raw_ref, p_ref,
                 m_ref, alpha_ref, acc_ref, *, nb):
    blk = MOBA_BLOCK
    dh = A_HEAD_DIM
    pw = 2 * dh
    streams = range(MOBA_NB)
    j = pl.program_id(1)

    @pl.when(j == 0)
    def _():
        for z in streams:
            for n in range(nb):
                kb = k_ref[z, n * blk:(n + 1) * blk, :].astype(F32)
                kmean_ref[z, n:n + 1, :] = jnp.mean(kb, axis=0, keepdims=True)

    d_iota = lax.broadcasted_iota(jnp.int32, (pw, blk), 0)
    for z in streams:
        for hd in range(A_HEADS):
            pr = hd // 2
            q2 = qT_ref[z, 0, pr * pw:(pr + 1) * pw, :]
            keep = (d_iota < dh) if hd % 2 == 0 else (d_iota >= dh)
            qm_ref[z, hd] = jnp.where(keep, q2, jnp.zeros_like(q2))

    def score_dots(z, n, slot):
        row0 = pl.multiple_of(n * blk, blk)
        for hd in range(A_HEADS):
            pr = hd // 2
            kt = k_ref[z, pl.ds(row0, blk), pr * pw:(pr + 1) * pw]
            raw_ref[z, slot, hd] = _dot(kt, qm_ref[z, hd]).astype(BF16)

    def value_dots(z, n, slot):
        ones = jnp.ones((PV_ONES_ROWS, blk), BF16)
        return [_dot(jnp.concatenate([vT_ref[z, n, hd * dh:(hd + 1) * dh, :], ones], axis=0),
                     p_ref[z, slot, hd]) for hd in range(A_HEADS)]

    sel_scores = []
    for z in streams:
        km_hi, km_lo = _split_bf16(kmean_ref[z])
        km2 = jnp.concatenate([km_hi, km_lo], axis=0)
        per_head = []
        for hd in range(A_HEADS):
            pr = hd // 2
            r2 = _dot(km2[:, pr * pw:(pr + 1) * pw], qm_ref[z, hd])
            per_head.append(r2[0:nb] + r2[nb:2 * nb])
        sel_scores.append(per_head)

    for z in streams:
        score_dots(z, j, 0)
    for z in streams:
        score_dots(z, 0, 1)

    n_iota = lax.broadcasted_iota(jnp.int32, (nb, blk), 0)
    past = n_iota < j
    for z in streams:
        for hd in range(A_HEADS):
            sc = jnp.where(past, sel_scores[z][hd], NEG)
            sel = jnp.zeros((nb, blk), F32)
            for _ in range(MOBA_TOPK):
                mx = jnp.max(sc, axis=0, keepdims=True)
                first = jnp.min(jnp.where(sc == mx, n_iota, nb), axis=0, keepdims=True)
                pick = n_iota == first
                sel = jnp.where(pick, 1.0, sel)
                sc = jnp.where(pick, -jnp.inf, sc)
            sel_ref[z, hd, 0:nb, :] = jnp.where(past, sel, 0.0)
            sel_ref[z, hd, nb:nb + 1, :] = jnp.ones((1, blk), F32)

    k_iota = lax.broadcasted_iota(jnp.int32, (blk, blk), 0)
    q_iota = lax.broadcasted_iota(jnp.int32, (blk, blk), 1)
    causal_bias = jnp.where(k_iota <= q_iota, 0.0, NEG).astype(BF16)
    for z in streams:
        for hd in range(A_HEADS):
            s = raw_ref[z, 0, hd] + causal_bias
            m_new = jnp.max(s, axis=0, keepdims=True)
            p_ref[z, 0, hd] = jnp.exp2(s - m_new)
            m_ref[z, hd] = m_new.astype(F32)
            acc_ref[z, hd] = jnp.zeros(acc_ref.shape[2:], F32)

    def step(i, cur):
        nxt = 1 - cur
        n = i - 1
        prev = jnp.where(i == 1, j, i - 2)
        prev_sel = jnp.where(i == 1, nb, i - 2)
        row0 = pl.multiple_of(jnp.minimum(i, nb - 1) * blk, blk)
        ones = jnp.ones((PV_ONES_ROWS, blk), BF16)
        pvs = {}
        for hd in range(A_HEADS):
            pr = hd // 2
            for z in streams:
                kt = k_ref[z, pl.ds(row0, blk), pr * pw:(pr + 1) * pw]
                raw_ref[z, nxt, hd] = _dot(kt, qm_ref[z, hd]).astype(BF16)
                pvs[z, hd] = _dot(
                    jnp.concatenate([vT_ref[z, prev, hd * dh:(hd + 1) * dh, :], ones], axis=0),
                    p_ref[z, nxt, hd])
                s = raw_ref[z, cur, hd]
                selb = sel_ref[z, hd, pl.ds(n, 1), :] > 0.5
                smax = jnp.where(selb, jnp.max(s, axis=0, keepdims=True).astype(F32), NEG)
                m_old = m_ref[z, hd]
                m_new = jnp.maximum(m_old, smax)
                alpha_ref[z, hd] = jnp.exp2(m_old - m_new)
                m_ref[z, hd] = m_new
                p_ref[z, cur, hd] = jnp.exp2(s - m_new.astype(BF16))
        for hd in range(A_HEADS):
            for z in streams:
                keep = sel_ref[z, hd, pl.ds(prev_sel, 1), :] > 0.5
                acc_ref[z, hd] = alpha_ref[z, hd] * (acc_ref[z, hd] + jnp.where(keep, pvs[z, hd], 0.0))

    def body(ii, carry):
        step(2 * ii + 1, 1)
        step(2 * ii + 2, 0)
        return carry

    trips = (j + 1) // 2
    lax.fori_loop(0, trips, body, 0)

    last = jnp.where(j == 0, j, 2 * trips - 1)
    last_sel = jnp.where(j == 0, nb, 2 * trips - 1)
    last_pvs = [value_dots(z, last, 0) for z in streams]
    for z in streams:
        for pr in range(A_HEADS // 2):
            halves = []
            for hd in (2 * pr, 2 * pr + 1):
                keep = sel_ref[z, hd, pl.ds(last_sel, 1), :] > 0.5
                acc = acc_ref[z, hd] + jnp.where(keep, last_pvs[z][hd], 0.0)
                halves.append(acc[0:dh] / acc[dh:dh + 1])
            oT = jnp.concatenate(halves, axis=0)
            oT = oT * gT_ref[z, 0, pr * pw:(pr + 1) * pw, :].astype(F32)
            o_ref[z, :, pr * pw:(pr + 1) * pw] = oT.T.astype(BF16)


def _moba(qT, k, vT, gT):
    b, nb, aw, blk = qT.shape
    s = nb * blk
    pw = 2 * A_HEAD_DIM
    z = MOBA_NB
    assert b % z == 0
    q_spec = pl.BlockSpec((z, 1, aw, blk), lambda i, j: (i, j, 0, 0))
    row = lambda: pltpu.VMEM((z, A_HEADS, 1, blk), F32)
    return pl.pallas_call(
        functools.partial(_moba_kernel, nb=nb),
        grid=(b // z, nb),
        in_specs=[
            q_spec,
            pl.BlockSpec((z, s, aw), lambda i, j: (i, 0, 0)),
            pl.BlockSpec((z, nb, aw, blk), lambda i, j: (i, 0, 0, 0)),
            q_spec,
        ],
        out_specs=pl.BlockSpec((z, blk, aw), lambda i, j: (i, j, 0)),
        out_shape=jax.ShapeDtypeStruct((b, s, aw), BF16),
        scratch_shapes=[
            pltpu.VMEM((z, nb, aw), F32),
            pltpu.VMEM((z, A_HEADS, nb + V7X_SUBLANES, blk), F32),
            pltpu.VMEM((z, A_HEADS, pw, blk), BF16),
            pltpu.VMEM((z, 2, A_HEADS, blk, blk), BF16),
            pltpu.VMEM((z, 2, A_HEADS, blk, blk), BF16),
            row(), row(),
            pltpu.VMEM((z, A_HEADS, A_HEAD_DIM + PV_ONES_ROWS, blk), F32),
        ],
        compiler_params=pltpu.CompilerParams(
            dimension_semantics=("parallel", "arbitrary"), vmem_limit_bytes=MOBA_VMEM_LIMIT),
        name="moba",
    )(qT, k, vT, gT)


def _mlstm_kernel(bx_ref, sbz_ref, cw_ref, cb_ref, wqt_ref, wk_ref, wvt_ref, wgq_ref, wgk_ref,
                  wgv_ref, bg_ref, og_ref, skip_ref, o_ref, xbuf_ref, state_ref, m_ref):
    L = MLSTM_L
    dh = B_HEAD_DIM
    c = pl.program_id(1)

    @pl.when(c == 0)
    def _():
        xbuf_ref[:, 0:CONV_HALO, :] = jnp.zeros((MLSTM_NB, CONV_HALO, B_WIDTH), F32)
        state_ref[...] = jnp.zeros_like(state_ref)
        m_ref[...] = jnp.zeros_like(m_ref)

    cw = cw_ref[...]
    cb = cb_ref[...]
    kscale = dh ** -0.5
    s_iota = lax.broadcasted_iota(jnp.int32, (L, L), 0)
    t_iota = lax.broadcasted_iota(jnp.int32, (L, L), 1)
    tri = s_iota <= t_iota
    tri_b = jnp.where(tri, 1.0, 0.0).astype(BF16)
    ones = jnp.ones((dh, L), F32)

    def project(bi):
        bx_b = bx_ref[bi]
        xbuf_ref[bi, CONV_HALO:CONV_HALO + L, :] = bx_b.astype(F32)
        st = dict(xc=[], ks=[], vT=[], scores=[], inter=[], state=[])
        gT = bg_ref[...]
        for hd in range(B_HEADS):
            lo = hd * dh
            conv = cb[:, lo:lo + dh]
            for i in range(B_CONV):
                off = CONV_HALO - (B_CONV - 1) + i
                conv = conv + cw[i:i + 1, lo:lo + dh] * xbuf_ref[bi, off:off + L, lo:lo + dh]
            xc = _silu(conv)
            xc_b = xc.astype(BF16)
            qT_h = _dot_nt(wqt_ref[hd], xc_b).astype(BF16)
            vT_h = _dot_nt(wvt_ref[hd], bx_b[:, lo:lo + dh])
            k_h = _dot(xc_b, wk_ref[hd])
            ks_h = (k_h * kscale).astype(BF16)
            state = state_ref[bi, hd]
            st["scores"].append(_dot(ks_h, qT_h))
            st["inter"].append(_dot(state.astype(BF16), qT_h))
            gT = gT + (_dot(wgq_ref[:, lo:lo + dh], qT_h)
                       + _dot_nt(wgk_ref[:, lo:lo + dh], k_h.astype(BF16))
                       + _dot(wgv_ref[:, lo:lo + dh], vT_h.astype(BF16)))
            st["xc"].append(xc); st["ks"].append(ks_h); st["vT"].append(vT_h); st["state"].append(state)
        tail = xbuf_ref[bi, L:L + CONV_HALO, :]
        xbuf_ref[bi, 0:CONV_HALO, :] = tail
        st["gT"] = gT
        return st

    def gate_chain(st):
        gT = st["gT"]
        lf_hi, lf_lo = _split_bf16(_log_sigmoid(gT))
        cum = (_dot(lf_hi, tri_b) + _dot(lf_lo, tri_b))[B_HEADS:2 * B_HEADS]
        a = gT[0:B_HEADS] - cum
        st["cum"], st["a"] = cum, a
        pad = jnp.zeros((V7X_LANES - B_HEADS, L), F32)
        st["a_cols"] = jnp.concatenate([a, pad], axis=0).T

    def recur(bi, st):
        a, cum, a_cols = st["a"], st["cum"], st["a_cols"]
        for hd in range(B_HEADS):
            lo = hd * dh
            ks_h, state, inter = st["ks"][hd], st["state"][hd], st["inter"][hd]
            vT_aug = jnp.concatenate([st["vT"][hd], ones], axis=0)
            a_row = a[hd:hd + 1]
            a_col = a_cols[:, hd:hd + 1]
            cum_row = cum[hd:hd + 1]
            m_prev = m_ref[bi, hd][0:1, 0:1]
            amax = jnp.max(jnp.where(tri, a_col, NEG), axis=0, keepdims=True)
            b_row = jnp.maximum(m_prev, amax)
            dmat = jnp.exp(jnp.where(tri, a_col - b_row, NEG))
            w_intra = (st["scores"][hd] * dmat).astype(BF16)
            intra = _dot(vT_aug.astype(BF16), w_intra)
            w_inter = jnp.exp(m_prev - b_row)
            num = w_inter * inter[0:dh] + intra[0:dh]
            den = w_inter * inter[dh:dh + 1] + intra[dh:dh + 1]
            hc = num * (1.0 / jnp.maximum(jnp.abs(den), jnp.exp(-(cum_row + b_row))))
            mu = jnp.mean(hc, axis=0, keepdims=True)
            hcc = hc - mu
            var = jnp.mean(hcc * hcc, axis=0, keepdims=True)
            hb = (hcc * lax.rsqrt(var + 1e-5)).T * og_ref[:, lo:lo + dh]
            yb = (hb + skip_ref[:, lo:lo + dh] * st["xc"][hd]) * sbz_ref[bi, :, lo:lo + dh].astype(F32)
            o_ref[bi, :, lo:lo + dh] = yb.astype(BF16)
            b_end = b_row[:, L - 1:L]
            ws = jnp.exp(a_row - b_end)
            decay = jnp.exp(m_prev - b_end)
            state_ref[bi, hd] = decay * state + _dot((vT_aug * ws).astype(BF16), ks_h)
            m_ref[bi, hd] = jnp.broadcast_to(cum_row[:, L - 1:L] + b_end, m_ref.shape[2:])

    sts = []
    for bi in range(MLSTM_NB):
        sts.append(project(bi))
        gate_chain(sts[bi])
    for bi in range(MLSTM_NB):
        recur(bi, sts[bi])


def _block_diag_dense(w):
    nblk, blk, _ = w.shape
    n = nblk * blk
    idx = jnp.arange(n) // blk
    return jnp.where(idx[:, None] == idx[None, :], jnp.tile(w.reshape(n, blk), (1, nblk)), 0.0)


def _mlstm(bx, sbz, conv_w, conv_b, wq, wk, wv, w_gates, b_gates, out_g, skip):
    b, s, bw = bx.shape
    L = MLSTM_L
    nc = s // L
    nbb = MLSTM_NB
    assert b % nbb == 0 and s % L == 0
    per_head = wq.shape[0] // B_HEADS

    def head_blocks(w, transpose):
        blocks = [_block_diag_dense(w[h * per_head:(h + 1) * per_head]) for h in range(B_HEADS)]
        return jnp.stack([blk.T if transpose else blk for blk in blocks]).astype(BF16)

    wqt_d = head_blocks(wq, True)
    wk_d = head_blocks(wk, False)
    wvt_d = head_blocks(wv, True)
    hblk = (B_HEADS, B_HEAD_DIM, B_HEAD_DIM)
    wgq = w_gates[:bw].T.astype(BF16)
    wgv = w_gates[2 * bw:].T.astype(BF16)
    wgk = w_gates[bw:2 * bw].T.astype(BF16)
    const = lambda shape: pl.BlockSpec(shape, lambda i, j: (0,) * len(shape))
    tok = pl.BlockSpec((nbb, L, bw), lambda i, j: (i, j, 0))
    return pl.pallas_call(
        _mlstm_kernel,
        grid=(b // nbb, nc),
        in_specs=[tok, tok, const((B_CONV, bw)), const((1, bw)),
                  const(hblk), const(hblk), const(hblk),
                  const((GATE_ROWS, bw)), const((GATE_ROWS, bw)), const((GATE_ROWS, bw)),
                  const((GATE_ROWS, 1)), const((1, bw)), const((1, bw))],
        out_specs=tok,
        out_shape=jax.ShapeDtypeStruct((b, s, bw), BF16),
        scratch_shapes=[pltpu.VMEM((nbb, L + CONV_HALO, bw), F32),
                        pltpu.VMEM((nbb, B_HEADS, 2 * B_HEAD_DIM, B_HEAD_DIM), F32),
                        pltpu.VMEM((nbb, B_HEADS, V7X_SUBLANES, V7X_LANES), F32)],
        compiler_params=pltpu.CompilerParams(
            dimension_semantics=("parallel", "arbitrary"), vmem_limit_bytes=VMEM_LIMIT),
        name="mlstm",
    )(bx, sbz, conv_w, conv_b.reshape(1, bw), wqt_d, wk_d, wvt_d, wgq, wgk, wgv,
      b_gates.reshape(GATE_ROWS, 1), out_g.reshape(1, bw), skip.reshape(1, bw))


def _tail_kernel(x_ref, ya_ref, yb_ref, mod0_ref, mod1_ref, lng_ref, wo0_ref, wi1_ref, clg_ref,
                 clb_ref, ws_ref, bst_ref, wo1_ref, o_ref):
    tm = x_ref.shape[1]
    sub = TAIL_SUB_ROWS
    n_sub = tm // sub
    aw = A_WIDTH
    mod1 = mod1_ref[0]
    gate0 = mod0_ref[0][2:3]
    t_iota = lax.broadcasted_iota(jnp.int32, (C_CHUNK, C_CHUNK), 0)
    s_iota = lax.broadcasted_iota(jnp.int32, (C_CHUNK, C_CHUNK), 1)
    tril = s_iota <= t_iota
    gw = C_WIDTH // C_GROUPS
    bst = bst_ref[...]
    wms = [jnp.where(tril, ws_ref[g], 0.0).astype(BF16) for g in range(C_GROUPS)]

    def out_proj0(r):
        rows = slice(r * sub, (r + 1) * sub)
        y0 = _dot(ya_ref[0, rows, :], wo0_ref[0:aw, :]) + _dot(yb_ref[0, rows, :], wo0_ref[aw:, :])
        x1 = x_ref[0, rows, :] + gate0 * y0
        h = _adaln_rmsnorm(x1, lng_ref[...], mod1[1:2], mod1[0:1]).astype(BF16)
        return x1, h

    def mix_inputs(p):
        u = _gelu_tanh(p[:, :C_WIDTH])
        v = _gelu_tanh(p[:, C_WIDTH:2 * C_WIDTH])
        mu = jnp.mean(v, axis=-1, keepdims=True)
        vc = v - mu
        var = jnp.mean(vc * vc, axis=-1, keepdims=True)
        vn = ((vc * lax.rsqrt(var + 1e-5)) * clg_ref[...] + clb_ref[...]).astype(BF16)
        return vn, u * _silu(p[:, 2 * C_WIDTH:])

    def spatial_gate(vn, gate):
        cols = []
        for g in range(C_GROUPS):
            rows = []
            for ch in range(sub // C_CHUNK):
                vg = vn[ch * C_CHUNK:(ch + 1) * C_CHUNK, g * gw:(g + 1) * gw]
                rows.append(_dot(wms[g], vg) + bst[:, g:g + 1])
            cols.append(jnp.concatenate(rows, axis=0))
        return (gate * jnp.concatenate(cols, axis=1)).astype(BF16)

    def out_proj1(r, x1, y1):
        o_ref[0, r * sub:(r + 1) * sub, :] = x1 + mod1[2:3] * _dot(y1, wo1_ref[...])

    heads = [out_proj0(r) for r in range(n_sub)]
    projs = [_dot(heads[r][1], wi1_ref[...]) for r in range(n_sub)]
    y1_prev = None
    for r in range(n_sub):
        vn, gate = mix_inputs(projs[r])
        if y1_prev is not None:
            out_proj1(r - 1, heads[r - 1][0], y1_prev)
        y1_prev = spatial_gate(vn, gate)
    out_proj1(n_sub - 1, heads[n_sub - 1][0], y1_prev)


def _tail(x, ya, yb, mod0, mod1, ln_g1, w_out0, w_in1, w_out1, c_ln_g, c_ln_b, c_ws, c_bs):
    b, s, d = x.shape
    tm = TAIL_ROW_TILE
    nt = s // tm
    const = lambda shape: pl.BlockSpec(shape, lambda i, j: (0,) * len(shape),
                                       pipeline_mode=pl.Buffered(1))
    half = pl.BlockSpec((1, tm, A_WIDTH), lambda i, j: (i, j, 0))
    full = pl.BlockSpec((1, tm, d), lambda i, j: (i, j, 0))
    modspec = pl.BlockSpec((1, 3, d), lambda i, j: (i, 0, 0))
    return pl.pallas_call(
        _tail_kernel,
        grid=(b, nt),
        in_specs=[full, half, half, modspec, modspec, const((1, d)), const((d, d)),
                  const((d, 3 * C_WIDTH)), const((1, d)), const((1, d)),
                  const((C_GROUPS, C_CHUNK, C_CHUNK)), const((C_CHUNK, C_GROUPS)), const((d, d))],
        out_specs=full,
        out_shape=jax.ShapeDtypeStruct((b, s, d), F32),
        compiler_params=pltpu.CompilerParams(
            dimension_semantics=("parallel", "parallel"), vmem_limit_bytes=VMEM_LIMIT),
        name="tail",
    )(x, ya, yb, mod0, mod1, ln_g1.reshape(1, d), w_out0.astype(BF16), w_in1.astype(BF16),
      c_ln_g.reshape(1, d), c_ln_b.reshape(1, d), c_ws, c_bs.T, w_out1.astype(BF16))


def kernel(x, c, ln_g, ada_w, ada_b, w_in, w_out, a_q_g, a_k_g, b_conv_w, b_conv_b, b_wq, b_wk, b_wv,
           b_w_gates, b_b_gates, b_out_g, b_skip, c_ln_g, c_ln_b, c_ws, c_bs):
    mods = _ada_mods(c, ada_w, ada_b)
    qT, vT, gT, k, bx, sbz = _inproj0(x, mods[0], ln_g[0], w_in[0], a_q_g[0], a_k_g[0])
    ya = _moba(qT, k, vT, gT)
    yb = _mlstm(bx, sbz, b_conv_w[0], b_conv_b[0], b_wq[0], b_wk[0], b_wv[0], b_w_gates[0],
                b_b_gates[0], b_out_g[0], b_skip[0])
    return _tail(x, ya, yb, mods[0], mods[1], ln_g[1], w_out[0], w_in[1], w_out[1],
                 c_ln_g[0], c_ln_b[0], c_ws[0], c_bs[0])
```

```python
import functools

import jax
import jax.numpy as jnp
from jax import lax
from jax.experimental import pallas as pl
from jax.experimental.pallas import tpu as pltpu

F32 = jnp.float32
BF16 = jnp.bfloat16

D_MODEL = 1024
A_HEADS = 8
A_HEAD_DIM = 64
A_WIDTH = A_HEADS * A_HEAD_DIM
MOBA_BLOCK = 256
MOBA_TOPK = 3
B_HEADS = 4
B_HEAD_DIM = 128
B_WIDTH = B_HEADS * B_HEAD_DIM
B_CONV = 4
C_GROUPS = 8
C_CHUNK = 128
C_WIDTH = D_MODEL
NEG = -1e30
LOG2E = 1.4426950408889634

V7X_SUBLANES = 8
V7X_LANES = 128

MLSTM_L = 256
MLSTM_NB = 4
INPROJ_SUBTILES = 4
TAIL_ROW_TILE = 1024
TAIL_SUB_ROWS = 256
CONV_HALO = 8
GATE_ROWS = 2 * B_HEADS
PV_ONES_ROWS = 16
MOBA_NB = 2
VMEM_LIMIT = 48 * 1024 * 1024
MOBA_VMEM_LIMIT = 56 * 1024 * 1024


def _silu(x):
    return x * jax.nn.sigmoid(x)


def _gelu_tanh(x):
    return 0.5 * x * (1.0 + jnp.tanh(0.7978845608028654 * (x + 0.044715 * (x * x * x))))


def _log_sigmoid(x):
    return jnp.minimum(x, 0.0) - jnp.log(1.0 + jnp.exp(-jnp.abs(x)))


def _split_bf16(x):
    hi = x.astype(BF16)
    lo = (x - hi.astype(F32)).astype(BF16)
    return hi, lo


def _dot(a, b):
    return jnp.dot(a, b, preferred_element_type=F32)


def _dot_nt(a, b):
    return lax.dot_general(a, b, (((1,), (1,)), ((), ())), preferred_element_type=F32)


def _adaln_rmsnorm(x, ln_g, scale, shift):
    y = x * lax.rsqrt(jnp.mean(x * x, axis=-1, keepdims=True) + 1e-6)
    return (y * ln_g) * (1.0 + scale) + shift


def _ada_kernel(c_ref, w_ref, b_ref, o_ref):
    cs_hi, cs_lo = _split_bf16(_silu(c_ref[...]))
    w_hi, w_lo = _split_bf16(w_ref[0])
    o_ref[0] = _dot(cs_hi, w_hi) + _dot(cs_lo, w_hi) + _dot(cs_hi, w_lo) + b_ref[0]


def _ada_mods(c, ada_w, ada_b):
    depth, d, d3 = ada_w.shape
    b = c.shape[0]
    bp = V7X_SUBLANES * pl.cdiv(b, V7X_SUBLANES)
    cp =jnp.zeros((bp, d), F32).at[:b].set(c)
    nt = d3 // d
    out = pl.pallas_call(
        _ada_kernel,
        grid=(depth, nt),
        in_specs=[
            pl.BlockSpec((bp, d), lambda l, n: (0, 0)),
            pl.BlockSpec((1, d, d), lambda l, n: (l, 0, n)),
            pl.BlockSpec((1, 1, d), lambda l, n: (l, 0, n)),
        ],
        out_specs=pl.BlockSpec((1, bp, d), lambda l, n: (l, 0, n)),
        out_shape=jax.ShapeDtypeStruct((depth, bp, d3), F32),
        name="ada_mods",
    )(cp, ada_w, ada_b.reshape(depth, 1, d3))
    return out[:, :b].reshape(depth, b, 3, d)


def _inproj0_kernel(x_ref, mod_ref, lng_ref, wt_ref, w_ref, qg_ref, kg_ref,
                    qT_ref, vT_ref, gT_ref, k_ref, bx_ref, sbz_ref):
    tm = MOBA_BLOCK
    aw = A_WIDTH
    mod = mod_ref[0]
    qg = qg_ref[...] * (A_HEAD_DIM ** -0.5 * LOG2E)
    kg = kg_ref[...]
    hs = [_adaln_rmsnorm(x_ref[0, r * tm:(r + 1) * tm], lng_ref[...], mod[1:2], mod[0:1]).astype(BF16)
          for r in range(INPROJ_SUBTILES)]
    for r in range(INPROJ_SUBTILES):
        h = hs[r]
        rows = slice(r * tm, (r + 1) * tm)
        pt = _dot_nt(wt_ref[...], h)
        p = _dot(h, w_ref[...])
        kn = []
        for hd in range(A_HEADS):
            lo = hd * A_HEAD_DIM
            q = pt[lo:lo + A_HEAD_DIM]
            rq = lax.rsqrt(jnp.mean(q * q, axis=0, keepdims=True) + 1e-6)
            qT_ref[0, r, lo:lo + A_HEAD_DIM, :] = ((q * rq) * qg).astype(BF16)
            k = pt[aw + lo:aw + lo + A_HEAD_DIM]
            rk = lax.rsqrt(jnp.mean(k * k, axis=0, keepdims=True) + 1e-6)
            kn.append((k * rk) * kg)
        k_ref[0, rows, :] = jnp.concatenate(kn, axis=0).T.astype(BF16)
        vT_ref[0, r] = pt[2 * aw:3 * aw].astype(BF16)
        gT_ref[0, r] = _silu(pt[3 * aw:]).astype(BF16)
        bx_ref[0, rows, :] = p[:, :B_WIDTH].astype(BF16)
        sbz_ref[0, rows, :] = _silu(p[:, B_WIDTH:]).astype(BF16)


def _inproj0(x, mod, ln_g, w_in, q_g, k_g):
    b, s, d = x.shape
    blk = MOBA_BLOCK
    sub = INPROJ_SUBTILES
    tm = sub * blk
    nt = s // tm
    aw, bw = A_WIDTH, B_WIDTH
    assert w_in.shape == (d, 4 * aw + 2 * bw)
    wt = w_in[:, :4 * aw].T.astype(BF16)
    wn = w_in[:, 4 * aw:].astype(BF16)
    t_shape = jax.ShapeDtypeStruct((b, s // blk, aw, blk), BF16)
    n_shape = jax.ShapeDtypeStruct((b, s, aw), BF16)
    t_spec = pl.BlockSpec((1, sub, aw, blk), lambda i, j: (i, j, 0, 0))
    n_spec = pl.BlockSpec((1, tm, aw), lambda i, j: (i, j, 0))
    const = lambda shape: pl.BlockSpec(shape, lambda i, j: (0,) * len(shape),
                                       pipeline_mode=pl.Buffered(1))
    return pl.pallas_call(
        _inproj0_kernel,
        grid=(b, nt),
        in_specs=[
            pl.BlockSpec((1, tm, d), lambda i, j: (i, j, 0)),
            pl.BlockSpec((1, 3, d), lambda i, j: (i, 0, 0)),
            const((1, d)),
            const((4 * aw, d)),
            const((d, 2 * bw)),
            const((A_HEAD_DIM, 1)),
            const((A_HEAD_DIM, 1)),
        ],
        out_specs=[t_spec, t_spec, t_spec, n_spec, n_spec, n_spec],
        out_shape=[t_shape, t_shape, t_shape, n_shape, n_shape, n_shape],
        compiler_params=pltpu.CompilerParams(
            dimension_semantics=("parallel", "parallel"), vmem_limit_bytes=VMEM_LIMIT),
        name="inproj0",
    )(x, mod, ln_g.reshape(1, d), wt, wn, q_g.reshape(A_HEAD_DIM, 1), k_g.reshape(A_HEAD_DIM, 1))


def _moba_kernel(qT_ref, k_ref, vT_ref, gT_ref, o_ref, kmean_ref, sel_ref, qm_ref, raw_ref, p_ref,
                 m_ref, alpha_ref, acc_ref, *, nb):
    blk = MOBA_BLOCK
    dh = A_HEAD_DIM
    pw = 2 * dh
    streams = range(MOBA_NB)
    j = pl.program_id(1)

    @pl.when(j == 0)
    def _():
        for z in streams:
            for n in range(nb):
                kb = k_ref[z, n * blk:(n + 1) * blk, :].astype(F32)
                kmean_ref[z, n:n + 1, :] = jnp.mean(kb, axis=0, keepdims=True)

    d_iota = lax.broadcasted_iota(jnp.int32, (pw, blk), 0)
    for z in streams:
        for hd in range(A_HEADS):
            pr = hd // 2
            q2 = qT_ref[z, 0, pr * pw:(pr + 1) * pw, :]
            keep = (d_iota < dh) if hd % 2 == 0 else (d_iota >= dh)
            qm_ref[z, hd] = jnp.where(keep, q2, jnp.zeros_like(q2))

    def score_dots(z, n, slot):
        row0 = pl.multiple_of(n * blk, blk)
        for hd in range(A_HEADS):
            pr = hd // 2
            kt = k_ref[z, pl.ds(row0, blk), pr * pw:(pr + 1) * pw]
            raw_ref[z, slot, hd] = _dot(kt, qm_ref[z, hd]).astype(BF16)

    def value_dots(z, n, slot):
        ones = jnp.ones((PV_ONES_ROWS, blk), BF16)
        return [_dot(jnp.concatenate([vT_ref[z, n, hd * dh:(hd + 1) * dh, :], ones], axis=0),
                     p_ref[z, slot, hd]) for hd in range(A_HEADS)]

    sel_scores = []
    for z in streams:
        km_hi, km_lo = _split_bf16(kmean_ref[z])
        km2 = jnp.concatenate([km_hi, km_lo], axis=0)
        per_head = []
        for hd in range(A_HEADS):
            pr = hd // 2
            r2 = _dot(km2[:, pr * pw:(pr + 1) * pw], qm_ref[z, hd])
            per_head.append(r2[0:nb] + r2[nb:2 * nb])
        sel_scores.append(per_head)

    for z in streams:
        score_dots(z, j, 0)
    for z in streams:
        score_dots(z, 0, 1)

    n_iota = lax.broadcasted_iota(jnp.int32, (nb, blk), 0)
    past = n_iota < j
    for z in streams:
        for hd in range(A_HEADS):
            sc = jnp.where(past, sel_scores[z][hd], NEG)
            sel = jnp.zeros((nb, blk), F32)
            for _ in range(MOBA_TOPK):
                mx = jnp.max(sc, axis=0, keepdims=True)
                first = jnp.min(jnp.where(sc == mx, n_iota, nb), axis=0, keepdims=True)
                pick = n_iota == first
                sel = jnp.where(pick, 1.0, sel)
                sc = jnp.where(pick, -jnp.inf, sc)
            sel_ref[z, hd, 0:nb, :] = jnp.where(past, sel, 0.0)
            sel_ref[z, hd, nb:nb + 1, :] = jnp.ones((1, blk), F32)

    k_iota = lax.broadcasted_iota(jnp.int32, (blk, blk), 0)
    q_iota = lax.broadcasted_iota(jnp.int32, (blk, blk), 1)
    causal_bias = jnp.where(k_iota <= q_iota, 0.0, NEG).astype(BF16)
    for z in streams:
        for hd in range(A_HEADS):
            s = raw_ref[z, 0, hd] + causal_bias
            m_new = jnp.max(s, axis=0, keepdims=True)
            p_ref[z, 0, hd] = jnp.exp2(s - m_new)
            m_ref[z, hd] = m_new.astype(F32)
            acc_ref[z, hd] = jnp.zeros(acc_ref.shape[2:], F32)

    def step(z, i, cur):
        nxt = 1 - cur
        n = i - 1
        prev = jnp.where(i == 1, j, i - 2)
        prev_sel = jnp.where(i == 1, nb, i - 2)
        row0 = pl.multiple_of(jnp.minimum(i, nb - 1) * blk, blk)
        ones = jnp.ones((PV_ONES_ROWS, blk), BF16)
        pvs = []
        for hd in range(A_HEADS):
            pr = hd // 2
            kt = k_ref[z, pl.ds(row0, blk), pr * pw:(pr + 1) * pw]
            raw_ref[z, nxt, hd] = _dot(kt, qm_ref[z, hd]).astype(BF16)
            pvs.append(_dot(jnp.concatenate([vT_ref[z, prev, hd * dh:(hd + 1) * dh, :], ones], axis=0),
                            p_ref[z, nxt, hd]))
            s = raw_ref[z, cur, hd]
            selb = sel_ref[z, hd, pl.ds(n, 1), :] > 0.5
            smax = jnp.where(selb, jnp.max(s, axis=0, keepdims=True).astype(F32), NEG)
            m_old = m_ref[z, hd]
            m_new = jnp.maximum(m_old, smax)
            alpha_ref[z, hd] = jnp.exp2(m_old - m_new)
            m_ref[z, hd] = m_new
            p_ref[z, cur, hd] = jnp.exp2(s - m_new.astype(BF16))
        for hd in range(A_HEADS):
            keep = sel_ref[z, hd, pl.ds(prev_sel, 1), :] > 0.5
            acc_ref[z, hd] = alpha_ref[z, hd] * (acc_ref[z, hd] + jnp.where(keep, pvs[hd], 0.0))

    def body(ii, carry):
        for z in streams:
            step(z, 2 * ii + 1, 1)
        for z in streams:
            step(z, 2 * ii + 2, 0)
        return carry

    trips = (j + 1) // 2
    lax.fori_loop(0, trips, body, 0)

    last = jnp.where(j == 0, j, 2 * trips - 1)
    last_sel = jnp.where(j == 0, nb, 2 * trips - 1)
    last_pvs = [value_dots(z, last, 0) for z in streams]
    for z in streams:
        for pr in range(A_HEADS // 2):
            halves = []
            for hd in (2 * pr, 2 * pr + 1):
                keep = sel_ref[z, hd, pl.ds(last_sel, 1), :] > 0.5
                acc = acc_ref[z, hd] + jnp.where(keep, last_pvs[z][hd], 0.0)
                halves.append(acc[0:dh] / acc[dh:dh + 1])
            oT = jnp.concatenate(halves, axis=0)
            oT = oT * gT_ref[z, 0, pr * pw:(pr + 1) * pw, :].astype(F32)
            o_ref[z, :, pr * pw:(pr + 1) * pw] = oT.T.astype(BF16)


def _moba(qT, k, vT, gT):
    b, nb, aw, blk = qT.shape
    s = nb * blk
    pw = 2 * A_HEAD_DIM
    z = MOBA_NB
    assert b % z == 0
    q_spec = pl.BlockSpec((z, 1, aw, blk), lambda i, j: (i, j, 0, 0))
    row = lambda: pltpu.VMEM((z, A_HEADS, 1, blk), F32)
    return pl.pallas_call(
        functools.partial(_moba_kernel, nb=nb),
        grid=(b // z, nb),
        in_specs=[
            q_spec,
            pl.BlockSpec((z, s, aw), lambda i, j: (i, 0, 0)),
            pl.BlockSpec((z, nb, aw, blk), lambda i, j: (i, 0, 0, 0)),
            q_spec,
        ],
        out_specs=pl.BlockSpec((z, blk, aw), lambda i, j: (i, j, 0)),
        out_shape=jax.ShapeDtypeStruct((b, s, aw), BF16),
        scratch_shapes=[
            pltpu.VMEM((z, nb, aw), F32),
            pltpu.VMEM((z, A_HEADS, nb + V7X_SUBLANES, blk), F32),
            pltpu.VMEM((z, A_HEADS, pw, blk), BF16),
            pltpu.VMEM((z, 2, A_HEADS, blk, blk), BF16),
            pltpu.VMEM((z, 2, A_HEADS, blk, blk), BF16),
            row(), row(),
            pltpu.VMEM((z, A_HEADS, A_HEAD_DIM + PV_ONES_ROWS, blk), F32),
        ],
        compiler_params=pltpu.CompilerParams(
            dimension_semantics=("parallel", "arbitrary"), vmem_limit_bytes=MOBA_VMEM_LIMIT),
        name="moba",
    )(qT, k, vT, gT)


def _mlstm_kernel(bx_ref, sbz_ref, cw_ref, cb_ref, wqt_ref, wk_ref, wvt_ref, wgq_ref, wgk_ref,
                  wgv_ref, bg_ref, og_ref, skip_ref, o_ref, xbuf_ref, state_ref, m_ref):
    L = MLSTM_L
    dh = B_HEAD_DIM
    c = pl.program_id(1)

    @pl.when(c == 0)
    def _():
        xbuf_ref[:, 0:CONV_HALO, :] = jnp.zeros((MLSTM_NB, CONV_HALO, B_WIDTH), F32)
        state_ref[...] = jnp.zeros_like(state_ref)
        m_ref[...] = jnp.zeros_like(m_ref)

    cw = cw_ref[...]
    cb = cb_ref[...]
    kscale = dh ** -0.5
    s_iota = lax.broadcasted_iota(jnp.int32, (L, L), 0)
    t_iota = lax.broadcasted_iota(jnp.int32, (L, L), 1)
    tri = s_iota <= t_iota
    tri_b = jnp.where(tri, 1.0, 0.0).astype(BF16)
    ones = jnp.ones((dh, L), F32)

    def project(bi):
        bx_b = bx_ref[bi]
        xbuf_ref[bi, CONV_HALO:CONV_HALO + L, :] = bx_b.astype(F32)
        st = dict(xc=[], ks=[], vT=[], scores=[], inter=[], state=[])
        gT = bg_ref[...]
        for hd in range(B_HEADS):
            lo = hd * dh
            conv = cb[:, lo:lo + dh]
            for i in range(B_CONV):
                off = CONV_HALO - (B_CONV - 1) + i
                conv = conv + cw[i:i + 1, lo:lo + dh] * xbuf_ref[bi, off:off + L, lo:lo + dh]
            xc = _silu(conv)
            xc_b = xc.astype(BF16)
            qT_h = _dot_nt(wqt_ref[hd], xc_b).astype(BF16)
            vT_h = _dot_nt(wvt_ref[hd], bx_b[:, lo:lo + dh])
            k_h = _dot(xc_b, wk_ref[hd])
            ks_h = (k_h * kscale).astype(BF16)
            state = state_ref[bi, hd]
            st["scores"].append(_dot(ks_h, qT_h))
            st["inter"].append(_dot(state.astype(BF16), qT_h))
            gT = gT + (_dot(wgq_ref[:, lo:lo + dh], qT_h)
                       + _dot_nt(wgk_ref[:, lo:lo + dh], k_h.astype(BF16))
                       + _dot(wgv_ref[:, lo:lo + dh], vT_h.astype(BF16)))
            st["xc"].append(xc); st["ks"].append(ks_h); st["vT"].append(vT_h); st["state"].append(state)
        tail = xbuf_ref[bi, L:L + CONV_HALO, :]
        xbuf_ref[bi, 0:CONV_HALO, :] = tail
        st["gT"] = gT
        return st

    def gate_chain(st):
        gT = st["gT"]
        lf_hi, lf_lo = _split_bf16(_log_sigmoid(gT))
        cum = (_dot(lf_hi, tri_b) + _dot(lf_lo, tri_b))[B_HEADS:2 * B_HEADS]
        a = gT[0:B_HEADS] - cum
        st["cum"], st["a"] = cum, a
        pad = jnp.zeros((V7X_LANES - B_HEADS, L), F32)
        st["a_cols"] = jnp.concatenate([a, pad], axis=0).T

    def recur(bi, st):
        a, cum, a_cols = st["a"], st["cum"], st["a_cols"]
        for hd in range(B_HEADS):
            lo = hd * dh
            ks_h, state, inter = st["ks"][hd], st["state"][hd], st["inter"][hd]
            vT_aug = jnp.concatenate([st["vT"][hd], ones], axis=0)
            a_row = a[hd:hd + 1]
            a_col = a_cols[:, hd:hd + 1]
            cum_row = cum[hd:hd + 1]
            m_prev = m_ref[bi, hd][0:1, 0:1]
            amax = jnp.max(jnp.where(tri, a_col, NEG), axis=0, keepdims=True)
            b_row = jnp.maximum(m_prev, amax)
            dmat = jnp.exp(jnp.where(tri, a_col - b_row, NEG))
            w_intra = (st["scores"][hd] * dmat).astype(BF16)
            intra = _dot(vT_aug.astype(BF16), w_intra)
            w_inter = jnp.exp(m_prev - b_row)
            num = w_inter * inter[0:dh] + intra[0:dh]
            den = w_inter * inter[dh:dh + 1] + intra[dh:dh + 1]
            hc = num * (1.0 / jnp.maximum(jnp.abs(den), jnp.exp(-(cum_row + b_row))))
            mu = jnp.mean(hc, axis=0, keepdims=True)
            hcc = hc - mu
            var = jnp.mean(hcc * hcc, axis=0, keepdims=True)
            hb = (hcc * lax.rsqrt(var + 1e-5)).T * og_ref[:, lo:lo + dh]
            yb = (hb + skip_ref[:, lo:lo + dh] * st["xc"][hd]) * sbz_ref[bi, :, lo:lo + dh].astype(F32)
            o_ref[bi, :, lo:lo + dh] = yb.astype(BF16)
            b_end = b_row[:, L - 1:L]
            ws = jnp.exp(a_row - b_end)
            decay = jnp.exp(m_prev - b_end)
            state_ref[bi, hd] = decay * state + _dot((vT_aug * ws).astype(BF16), ks_h)
            m_ref[bi, hd] = jnp.broadcast_to(cum_row[:, L - 1:L] + b_end, m_ref.shape[2:])

    sts = []
    for bi in range(MLSTM_NB):
        sts.append(project(bi))
        gate_chain(sts[bi])
    for bi in range(MLSTM_NB):
        recur(bi, sts[bi])


def _block_diag_dense(w):
    nblk, blk, _ = w.shape
    n = nblk * blk
    idx = jnp.arange(n) // blk
    return jnp.where(idx[:, None] == idx[None, :], jnp.tile(w.reshape(n, blk), (1, nblk)), 0.0)


def _mlstm(bx, sbz, conv_w, conv_b, wq, wk, wv, w_gates, b_gates, out_g, skip):
    b, s, bw = bx.shape
    L = MLSTM_L
    nc = s // L
    nbb = MLSTM_NB
    assert b % nbb == 0 and s % L == 0
    per_head = wq.shape[0] // B_HEADS

    def head_blocks(w, transpose):
        blocks = [_block_diag_dense(w[h * per_head:(h + 1) * per_head]) for h in range(B_HEADS)]
        return jnp.stack([blk.T if transpose else blk for blk in blocks]).astype(BF16)

    wqt_d = head_blocks(wq, True)
    wk_d = head_blocks(wk, False)
    wvt_d = head_blocks(wv, True)
    hblk = (B_HEADS, B_HEAD_DIM, B_HEAD_DIM)
    wgq = w_gates[:bw].T.astype(BF16)
    wgv = w_gates[2 * bw:].T.astype(BF16)
    wgk = w_gates[bw:2 * bw].T.astype(BF16)
    const = lambda shape: pl.BlockSpec(shape, lambda i, j: (0,) * len(shape))
    tok = pl.BlockSpec((nbb, L, bw), lambda i, j: (i, j, 0))
    return pl.pallas_call(
        _mlstm_kernel,
        grid=(b // nbb, nc),
        in_specs=[tok, tok, const((B_CONV, bw)), const((1, bw)),
                  const(hblk), const(hblk), const(hblk),
                  const((GATE_ROWS, bw)), const((GATE_ROWS, bw)), const((GATE_ROWS, bw)),
                  const((GATE_ROWS, 1)), const((1, bw)), const((1, bw))],
        out_specs=tok,
        out_shape=jax.ShapeDtypeStruct((b, s, bw), BF16),
        scratch_shapes=[pltpu.VMEM((nbb, L + CONV_HALO, bw), F32),
                        pltpu.VMEM((nbb, B_HEADS, 2 * B_HEAD_DIM, B_HEAD_DIM), F32),
                        pltpu.VMEM((nbb, B_HEADS, V7X_SUBLANES, V7X_LANES), F32)],
        compiler_params=pltpu.CompilerParams(
            dimension_semantics=("parallel", "arbitrary"), vmem_limit_bytes=VMEM_LIMIT),
        name="mlstm",
    )(bx, sbz, conv_w, conv_b.reshape(1, bw), wqt_d, wk_d, wvt_d, wgq, wgk, wgv,
      b_gates.reshape(GATE_ROWS, 1), out_g.reshape(1, bw), skip.reshape(1, bw))


def _tail_kernel(x_ref, ya_ref, yb_ref, mod0_ref, mod1_ref, lng_ref, wo0_ref, wi1_ref, clg_ref,
                 clb_ref, ws_ref, bst_ref, wo1_ref, o_ref):
    tm = x_ref.shape[1]
    sub = TAIL_SUB_ROWS
    n_sub = tm // sub
    aw = A_WIDTH
    mod1 = mod1_ref[0]
    gate0 = mod0_ref[0][2:3]
    t_iota = lax.broadcasted_iota(jnp.int32, (C_CHUNK, C_CHUNK), 0)
    s_iota = lax.broadcasted_iota(jnp.int32, (C_CHUNK, C_CHUNK), 1)
    tril = s_iota <= t_iota
    gw = C_WIDTH // C_GROUPS
    bst = bst_ref[...]
    wms = [jnp.where(tril, ws_ref[g], 0.0).astype(BF16) for g in range(C_GROUPS)]

    def out_proj0(r):
        rows = slice(r * sub, (r + 1) * sub)
        y0 = _dot(ya_ref[0, rows, :], wo0_ref[0:aw, :]) + _dot(yb_ref[0, rows, :], wo0_ref[aw:, :])
        x1 = x_ref[0, rows, :] + gate0 * y0
        h = _adaln_rmsnorm(x1, lng_ref[...], mod1[1:2], mod1[0:1]).astype(BF16)
        return x1, h

    def mix_inputs(p):
        u = _gelu_tanh(p[:, :C_WIDTH])
        v = _gelu_tanh(p[:, C_WIDTH:2 * C_WIDTH])
        mu = jnp.mean(v, axis=-1, keepdims=True)
        vc = v - mu
        var = jnp.mean(vc * vc, axis=-1, keepdims=True)
        vn = ((vc * lax.rsqrt(var + 1e-5)) * clg_ref[...] + clb_ref[...]).astype(BF16)
        return vn, u * _silu(p[:, 2 * C_WIDTH:])

    def spatial_gate(vn, gate):
        cols = []
        for g in range(C_GROUPS):
            rows = []
            for ch in range(sub // C_CHUNK):
                vg = vn[ch * C_CHUNK:(ch + 1) * C_CHUNK, g * gw:(g + 1) * gw]
                rows.append(_dot(wms[g], vg) + bst[:, g:g + 1])
            cols.append(jnp.concatenate(rows, axis=0))
        return (gate * jnp.concatenate(cols, axis=1)).astype(BF16)

    def out_proj1(r, x1, y1):
        o_ref[0, r * sub:(r + 1) * sub, :] = x1 + mod1[2:3] * _dot(y1, wo1_ref[...])

    heads = [out_proj0(r) for r in range(n_sub)]
    projs = [_dot(heads[r][1], wi1_ref[...]) for r in range(n_sub)]
    y1_prev = None
    for r in range(n_sub):
        vn, gate = mix_inputs(projs[r])
        if y1_prev is not None:
            out_proj1(r - 1, heads[r - 1][0], y1_prev)
        y1_prev = spatial_gate(vn, gate)
    out_proj1(n_sub - 1, heads[n_sub - 1][0], y1_prev)


def _tail(x, ya, yb, mod0, mod1, ln_g1, w_out0, w_in1, w_out1, c_ln_g, c_ln_b, c_ws, c_bs):
    b, s, d = x.shape
    tm = TAIL_ROW_TILE
    nt = s // tm
    const = lambda shape: pl.BlockSpec(shape, lambda i, j: (0,) * len(shape),
                                       pipeline_mode=pl.Buffered(1))
    half = pl.BlockSpec((1, tm, A_WIDTH), lambda i, j: (i, j, 0))
    full = pl.BlockSpec((1, tm, d), lambda i, j: (i, j, 0))
    modspec = pl.BlockSpec((1, 3, d), lambda i, j: (i, 0, 0))
    return pl.pallas_call(
        _tail_kernel,
        grid=(b, nt),
        in_specs=[full, half, half, modspec, modspec, const((1, d)), const((d, d)),
                  const((d, 3 * C_WIDTH)), const((1, d)), const((1, d)),
                  const((C_GROUPS, C_CHUNK, C_CHUNK)), const((C_CHUNK, C_GROUPS)), const((d, d))],
        out_specs=full,
        out_shape=jax.ShapeDtypeStruct((b, s, d), F32),
        compiler_params=pltpu.CompilerParams(
            dimension_semantics=("parallel", "parallel"), vmem_limit_bytes=VMEM_LIMIT),
        name="tail",
    )(x, ya, yb, mod0, mod1, ln_g1.reshape(1, d), w_out0.astype(BF16), w_in1.astype(BF16),
      c_ln_g.reshape(1, d), c_ln_b.reshape(1, d), c_ws, c_bs.T, w_out1.astype(BF16))


def kernel(x, c, ln_g, ada_w, ada_b, w_in, w_out, a_q_g, a_k_g, b_conv_w, b_conv_b, b_wq, b_wk, b_wv,
           b_w_gates, b_b_gates, b_out_g, b_skip, c_ln_g, c_ln_b, c_ws, c_bs):
    mods = _ada_mods(c, ada_w, ada_b)
    qT, vT, gT, k, bx, sbz = _inproj0(x, mods[0], ln_g[0], w_in[0], a_q_g[0], a_k_g[0])
    ya = _moba(qT, k, vT, gT)
    yb = _mlstm(bx, sbz, b_conv_w[0], b_conv_b[0], b_wq[0], b_wk[0], b_wv[0], b_w_gates[0],
                b_b_gates[0], b_out_g[0], b_skip[0])
    return _tail(x, ya, yb, mods[0], mods[1], ln_g[1], w_out[0], w_in[1], w_out[1],
                 c_ln_g[0], c_ln_b[0], c_ws[0], c_bs[0])
```

```python
import functools

import jax
import jax.numpy as jnp
from jax import lax
from jax.experimental import pallas as pl
from jax.experimental.pallas import tpu as pltpu

F32 = jnp.float32
BF16 = jnp.bfloat16

D_MODEL = 1024
A_HEADS = 8
A_HEAD_DIM = 64
A_WIDTH = A_HEADS * A_HEAD_DIM
MOBA_BLOCK = 256
MOBA_TOPK = 3
B_HEADS = 4
B_HEAD_DIM = 128
B_WIDTH = B_HEADS * B_HEAD_DIM
B_CONV = 4
C_GROUPS = 8
C_CHUNK = 128
C_WIDTH = D_MODEL
NEG = -1e30
LOG2E = 1.4426950408889634

V7X_SUBLANES = 8
V7X_LANES = 128

MLSTM_L = 256
MLSTM_NB = 4
INPROJ_SUBTILES = 4
TAIL_ROW_TILE = 1024
TAIL_SUB_ROWS = 256
CONV_HALO = 8
GATE_ROWS = 2 * B_HEADS
PV_ONES_ROWS = 16
MOBA_NB = 2
VMEM_LIMIT = 48 * 1024 * 1024
MOBA_VMEM_LIMIT = 56 * 1024 * 1024


def _silu(x):
    return x * jax.nn.sigmoid(x)


def _gelu_tanh(x):
    return 0.5 * x * (1.0 + jnp.tanh(0.7978845608028654 * (x + 0.044715 * (x * x * x))))


def _log_sigmoid(x):
    return jnp.minimum(x, 0.0) - jnp.log(1.0 + jnp.exp(-jnp.abs(x)))


def _split_bf16(x):
    hi = x.astype(BF16)
    lo = (x - hi.astype(F32)).astype(BF16)
    return hi, lo


def _dot(a, b):
    return jnp.dot(a, b, preferred_element_type=F32)


def _dot_nt(a, b):
    return lax.dot_general(a, b, (((1,), (1,)), ((), ())), preferred_element_type=F32)


def _adaln_rmsnorm(x, ln_g, scale, shift):
    y = x * lax.rsqrt(jnp.mean(x * x, axis=-1, keepdims=True) + 1e-6)
    return (y * ln_g) * (1.0 + scale) + shift


def _ada_kernel(c_ref, w_ref, b_ref, o_ref):
    cs_hi, cs_lo = _split_bf16(_silu(c_ref[...]))
    w_hi, w_lo = _split_bf16(w_ref[0])
    o_ref[0] = _dot(cs_hi, w_hi) + _dot(cs_lo, w_hi) + _dot(cs_hi, w_lo) + b_ref[0]


def _ada_mods(c, ada_w, ada_b):
    depth, d, d3 = ada_w.shape
    b = c.shape[0]
    bp = V7X_SUBLANES * pl.cdiv(b, V7X_SUBLANES)
    cp =jnp.zeros((bp, d), F32).at[:b].set(c)
    nt = d3 // d
    out = pl.pallas_call(
        _ada_kernel,
        grid=(depth, nt),
        in_specs=[
            pl.BlockSpec((bp, d), lambda l, n: (0, 0)),
            pl.BlockSpec((1, d, d), lambda l, n: (l, 0, n)),
            pl.BlockSpec((1, 1, d), lambda l, n: (l, 0, n)),
        ],
        out_specs=pl.BlockSpec((1, bp, d), lambda l, n: (l, 0, n)),
        out_shape=jax.ShapeDtypeStruct((depth, bp, d3), F32),
        name="ada_mods",
    )(cp, ada_w, ada_b.reshape(depth, 1, d3))
    return out[:, :b].reshape(depth, b, 3, d)


def _inproj0_kernel(x_ref, mod_ref, lng_ref, wt_ref, w_ref, qg_ref, kg_ref,
                    qT_ref, vT_ref, gT_ref, k_ref, bx_ref, sbz_ref):
    tm = MOBA_BLOCK
    aw = A_WIDTH
    mod = mod_ref[0]
    qg = qg_ref[...] * (A_HEAD_DIM ** -0.5 * LOG2E)
    kg = kg_ref[...]
    hs = [_adaln_rmsnorm(x_ref[0, r * tm:(r + 1) * tm], lng_ref[...], mod[1:2], mod[0:1]).astype(BF16)
          for r in range(INPROJ_SUBTILES)]
    for r in range(INPROJ_SUBTILES):
        h = hs[r]
        rows = slice(r * tm, (r + 1) * tm)
        pt = _dot_nt(wt_ref[...], h)
        p = _dot(h, w_ref[...])
        kn = []
        for hd in range(A_HEADS):
            lo = hd * A_HEAD_DIM
            q = pt[lo:lo + A_HEAD_DIM]
            rq = lax.rsqrt(jnp.mean(q * q, axis=0, keepdims=True) + 1e-6)
            qT_ref[0, r, lo:lo + A_HEAD_DIM, :] = ((q * rq) * qg).astype(BF16)
            k = pt[aw + lo:aw + lo + A_HEAD_DIM]
            rk = lax.rsqrt(jnp.mean(k * k, axis=0, keepdims=True) + 1e-6)
            kn.append((k * rk) * kg)
        k_ref[0, rows, :] = jnp.concatenate(kn, axis=0).T.astype(BF16)
        vT_ref[0, r] = pt[2 * aw:3 * aw].astype(BF16)
        gT_ref[0, r] = _silu(pt[3 * aw:]).astype(BF16)
        bx_ref[0, rows, :] = p[:, :B_WIDTH].astype(BF16)
        sbz_ref[0, rows, :] = _silu(p[:, B_WIDTH:]).astype(BF16)


def _inproj0(x, mod, ln_g, w_in, q_g, k_g):
    b, s, d = x.shape
    blk = MOBA_BLOCK
    sub = INPROJ_SUBTILES
    tm = sub * blk
    nt = s // tm
    aw, bw = A_WIDTH, B_WIDTH
    assert w_in.shape == (d, 4 * aw + 2 * bw)
    wt = w_in[:, :4 * aw].T.astype(BF16)
    wn = w_in[:, 4 * aw:].astype(BF16)
    t_shape = jax.ShapeDtypeStruct((b, s // blk, aw, blk), BF16)
    n_shape = jax.ShapeDtypeStruct((b, s, aw), BF16)
    t_spec = pl.BlockSpec((1, sub, aw, blk), lambda i, j: (i, j, 0, 0))
    n_spec = pl.BlockSpec((1, tm, aw), lambda i, j: (i, j, 0))
    const = lambda shape: pl.BlockSpec(shape, lambda i, j: (0,) * len(shape),
                                       pipeline_mode=pl.Buffered(1))
    return pl.pallas_call(
        _inproj0_kernel,
        grid=(b, nt),
        in_specs=[
            pl.BlockSpec((1, tm, d), lambda i, j: (i, j, 0)),
            pl.BlockSpec((1, 3, d), lambda i, j: (i, 0, 0)),
            const((1, d)),
            const((4 * aw, d)),
            const((d, 2 * bw)),
            const((A_HEAD_DIM, 1)),
            const((A_HEAD_DIM, 1)),
        ],
        out_specs=[t_spec, t_spec, t_spec, n_spec, n_spec, n_spec],
        out_shape=[t_shape, t_shape, t_shape, n_shape, n_shape, n_shape],
        compiler_params=pltpu.CompilerParams(
            dimension_semantics=("parallel", "parallel"), vmem_limit_bytes=VMEM_LIMIT),
        name="inproj0",
    )(x, mod, ln_g.reshape(1, d), wt, wn, q_g.reshape(A_HEAD_DIM, 1), k_g.reshape(A_HEAD_DIM, 1))


def _moba_kernel(qT_ref, k_ref, vT_ref, gT_ref, o_ref, kmean_ref, sel_ref, qm_ref, raw_ref, p_ref,
                 m_ref, alpha_ref, acc_ref, *, nb):
    blk = MOBA_BLOCK
    dh = A_HEAD_DIM
    pw = 2 * dh
    streams = range(MOBA_NB)
    j = pl.program_id(1)

    @pl.when(j == 0)
    def _():
        for z in streams:
            for n in range(nb):
                kb = k_ref[z, n * blk:(n + 1) * blk, :].astype(F32)
                kmean_ref[z, n:n + 1, :] = jnp.mean(kb, axis=0, keepdims=True)

    d_iota = lax.broadcasted_iota(jnp.int32, (pw, blk), 0)
    for z in streams:
        for hd in range(A_HEADS):
            pr = hd // 2
            q2 = qT_ref[z, 0, pr * pw:(pr + 1) * pw, :]
            keep = (d_iota < dh) if hd % 2 == 0 else (d_iota >= dh)
            qm_ref[z, hd] = jnp.where(keep, q2, jnp.zeros_like(q2))

    def score_dots(z, n, slot):
        row0 = pl.multiple_of(n * blk, blk)
        for hd in range(A_HEADS):
            pr = hd // 2
            kt = k_ref[z, pl.ds(row0, blk), pr * pw:(pr + 1) * pw]
            raw_ref[z, slot, hd] = _dot(kt, qm_ref[z, hd]).astype(BF16)

    def value_dots(z, n, slot):
        ones = jnp.ones((PV_ONES_ROWS, blk), BF16)
        return [_dot(jnp.concatenate([vT_ref[z, n, hd * dh:(hd + 1) * dh, :], ones], axis=0),
                     p_ref[z, slot, hd]) for hd in range(A_HEADS)]

    sel_scores = []
    for z in streams:
        km_hi, km_lo = _split_bf16(kmean_ref[z])
        km2 = jnp.concatenate([km_hi, km_lo], axis=0)
        per_head = []
        for hd in range(A_HEADS):
            pr = hd // 2
            r2 = _dot(km2[:, pr * pw:(pr + 1) * pw], qm_ref[z, hd])
            per_head.append(r2[0:nb] + r2[nb:2 * nb])
        sel_scores.append(per_head)

    for z in streams:
        score_dots(z, 0, 0)

    n_iota = lax.broadcasted_iota(jnp.int32, (nb, blk), 0)
    past = n_iota < j
    for z in streams:
        for hd in range(A_HEADS):
            sc = jnp.where(past, sel_scores[z][hd], NEG)
            sel = jnp.zeros((nb, blk), F32)
            for _ in range(MOBA_TOPK):
                mx = jnp.max(sc, axis=0, keepdims=True)
                first = jnp.min(jnp.where(sc == mx, n_iota, nb), axis=0, keepdims=True)
                pick = n_iota == first
                sel = jnp.where(pick, 1.0, sel)
                sc = jnp.where(pick, -jnp.inf, sc)
            sel_ref[z, hd, 0:nb, :] = jnp.where(past, sel, 0.0)
            sel_ref[z, hd, nb:nb + 1, :] = jnp.ones((1, blk), F32)
            sel_ref[z, hd, nb + 1:nb + 2, :] = jnp.zeros((1, blk), F32)
            m_ref[z, hd] = jnp.full((1, blk), NEG, F32)
            acc_ref[z, hd] = jnp.zeros(acc_ref.shape[2:], F32)

    def step(z, i, cur):
        nxt = 1 - cur
        n = i
        prev = jnp.maximum(i - 1, 0)
        prev_sel = jnp.where(i == 0, nb + 1, i - 1)
        row0 = pl.multiple_of(jnp.minimum(i + 1, nb - 1) * blk, blk)
        ones = jnp.ones((PV_ONES_ROWS, blk), BF16)
        pvs = []
        for hd in range(A_HEADS):
            pr = hd // 2
            kt = k_ref[z, pl.ds(row0, blk), pr * pw:(pr + 1) * pw]
            raw_ref[z, nxt, hd] = _dot(kt, qm_ref[z, hd]).astype(BF16)
            pvs.append(_dot(jnp.concatenate([vT_ref[z, prev, hd * dh:(hd + 1) * dh, :], ones], axis=0),
                            p_ref[z, nxt, hd]))
            s = raw_ref[z, cur, hd]
            selb = sel_ref[z, hd, pl.ds(n, 1), :] > 0.5
            smax = jnp.where(selb, jnp.max(s, axis=0, keepdims=True).astype(F32), NEG)
            m_old = m_ref[z, hd]
            m_new = jnp.maximum(m_old, smax)
            alpha_ref[z, hd] = jnp.exp2(m_old - m_new)
            m_ref[z, hd] = m_new
            p_ref[z, cur, hd] = jnp.exp2(s - m_new.astype(BF16))
        for hd in range(A_HEADS):
            keep = sel_ref[z, hd, pl.ds(prev_sel, 1), :] > 0.5
            acc_ref[z, hd] = alpha_ref[z, hd] * (acc_ref[z, hd] + jnp.where(keep, pvs[hd], 0.0))

    def body(ii, carry):
        for z in streams:
            step(z, 2 * ii, 0)
        for z in streams:
            step(z, 2 * ii + 1, 1)
        return carry

    trips = (j + 1) // 2
    lax.fori_loop(0, trips, body, 0)
    n_steps = 2 * trips

    last = jnp.maximum(n_steps - 1, 0)
    last_sel = jnp.where(n_steps == 0, nb + 1, n_steps - 1)
    last_pvs = [value_dots(z, last, 1) for z in streams]
    own_slot = j & 1
    k_iota = lax.broadcasted_iota(jnp.int32, (blk, blk), 0)
    q_iota = lax.broadcasted_iota(jnp.int32, (blk, blk), 1)
    causal_bias = jnp.where(k_iota <= q_iota, 0.0, NEG).astype(BF16)
    for z in streams:
        for hd in range(A_HEADS):
            s = raw_ref[z, own_slot, hd] + causal_bias
            m_old = m_ref[z, hd]
            m_new = jnp.maximum(m_old, jnp.max(s, axis=0, keepdims=True).astype(F32))
            p_ref[z, 0, hd] = jnp.exp2(s - m_new.astype(BF16))
            keep = sel_ref[z, hd, pl.ds(last_sel, 1), :] > 0.5
            acc_ref[z, hd] = jnp.exp2(m_old - m_new) * (
                acc_ref[z, hd] + jnp.where(keep, last_pvs[z][hd], 0.0))
    own_pvs = [value_dots(z, j, 0) for z in streams]
    for z in streams:
        for pr in range(A_HEADS // 2):
            halves = []
            for hd in (2 * pr, 2 * pr + 1):
                acc = acc_ref[z, hd] + own_pvs[z][hd]
                halves.append(acc[0:dh] / acc[dh:dh + 1])
            oT = jnp.concatenate(halves, axis=0)
            oT = oT * gT_ref[z, 0, pr * pw:(pr + 1) * pw, :].astype(F32)
            o_ref[z, :, pr * pw:(pr + 1) * pw] = oT.T.astype(BF16)


def _moba(qT, k, vT, gT):
    b, nb, aw, blk = qT.shape
    s = nb * blk
    pw = 2 * A_HEAD_DIM
    z = MOBA_NB
    assert b % z == 0
    q_spec = pl.BlockSpec((z, 1, aw, blk), lambda i, j: (i, j, 0, 0))
    row = lambda: pltpu.VMEM((z, A_HEADS, 1, blk), F32)
    return pl.pallas_call(
        functools.partial(_moba_kernel, nb=nb),
        grid=(b // z, nb),
        in_specs=[
            q_spec,
            pl.BlockSpec((z, s, aw), lambda i, j: (i, 0, 0)),
            pl.BlockSpec((z, nb, aw, blk), lambda i, j: (i, 0, 0, 0)),
            q_spec,
        ],
        out_specs=pl.BlockSpec((z, blk, aw), lambda i, j: (i, j, 0)),
        out_shape=jax.ShapeDtypeStruct((b, s, aw), BF16),
        scratch_shapes=[
            pltpu.VMEM((z, nb, aw), F32),
            pltpu.VMEM((z, A_HEADS, nb + V7X_SUBLANES, blk), F32),
            pltpu.VMEM((z, A_HEADS, pw, blk), BF16),
            pltpu.VMEM((z, 2, A_HEADS, blk, blk), BF16),
            pltpu.VMEM((z, 2, A_HEADS, blk, blk), BF16),
            row(), row(),
            pltpu.VMEM((z, A_HEADS, A_HEAD_DIM + PV_ONES_ROWS, blk), F32),
        ],
        compiler_params=pltpu.CompilerParams(
            dimension_semantics=("parallel", "arbitrary"), vmem_limit_bytes=MOBA_VMEM_LIMIT),
        name="moba",
    )(qT, k, vT, gT)


def _mlstm_kernel(bx_ref, sbz_ref, cw_ref, cb_ref, wqt_ref, wk_ref, wvt_ref, wgq_ref, wgk_ref,
                  wgv_ref, bg_ref, og_ref, skip_ref, o_ref, xbuf_ref, state_ref, m_ref):
    L = MLSTM_L
    dh = B_HEAD_DIM
    c = pl.program_id(1)

    @pl.when(c == 0)
    def _():
        xbuf_ref[:, 0:CONV_HALO, :] = jnp.zeros((MLSTM_NB, CONV_HALO, B_WIDTH), F32)
        state_ref[...] = jnp.zeros_like(state_ref)
        m_ref[...] = jnp.zeros_like(m_ref)

    cw = cw_ref[...]
    cb = cb_ref[...]
    kscale = dh ** -0.5
    s_iota = lax.broadcasted_iota(jnp.int32, (L, L), 0)
    t_iota = lax.broadcasted_iota(jnp.int32, (L, L), 1)
    tri = s_iota <= t_iota
    tri_b = jnp.where(tri, 1.0, 0.0).astype(BF16)
    ones = jnp.ones((dh, L), F32)

    def project(bi):
        bx_b = bx_ref[bi]
        xbuf_ref[bi, CONV_HALO:CONV_HALO + L, :] = bx_b.astype(F32)
        st = dict(xc=[], ks=[], vT=[], scores=[], inter=[], state=[])
        gT = bg_ref[...]
        for hd in range(B_HEADS):
            lo = hd * dh
            conv = cb[:, lo:lo + dh]
            for i in range(B_CONV):
                off = CONV_HALO - (B_CONV - 1) + i
                conv = conv + cw[i:i + 1, lo:lo + dh] * xbuf_ref[bi, off:off + L, lo:lo + dh]
            xc = _silu(conv)
            xc_b = xc.astype(BF16)
            qT_h = _dot_nt(wqt_ref[hd], xc_b).astype(BF16)
            vT_h = _dot_nt(wvt_ref[hd], bx_b[:, lo:lo + dh])
            k_h = _dot(xc_b, wk_ref[hd])
            ks_h = (k_h * kscale).astype(BF16)
            state = state_ref[bi, hd]
            st["scores"].append(_dot(ks_h, qT_h))
            st["inter"].append(_dot(state.astype(BF16), qT_h))
            gT = gT + (_dot(wgq_ref[:, lo:lo + dh], qT_h)
                       + _dot_nt(wgk_ref[:, lo:lo + dh], k_h.astype(BF16))
                       + _dot(wgv_ref[:, lo:lo + dh], vT_h.astype(BF16)))
            st["xc"].append(xc); st["ks"].append(ks_h); st["vT"].append(vT_h); st["state"].append(state)
        tail = xbuf_ref[bi, L:L + CONV_HALO, :]
        xbuf_ref[bi, 0:CONV_HALO, :] = tail
        st["gT"] = gT
        return st

    def gate_chain(st):
        gT = st["gT"]
        lf_hi, lf_lo = _split_bf16(_log_sigmoid(gT))
        cum = (_dot(lf_hi, tri_b) + _dot(lf_lo, tri_b))[B_HEADS:2 * B_HEADS]
        a = gT[0:B_HEADS] - cum
        st["cum"], st["a"] = cum, a
        pad = jnp.zeros((V7X_LANES - B_HEADS, L), F32)
        st["a_cols"] = jnp.concatenate([a, pad], axis=0).T

    def recur(bi, st):
        a, cum, a_cols = st["a"], st["cum"], st["a_cols"]
        for hd in range(B_HEADS):
            lo = hd * dh
            ks_h, state, inter = st["ks"][hd], st["state"][hd], st["inter"][hd]
            vT_aug = jnp.concatenate([st["vT"][hd], ones], axis=0)
            a_row = a[hd:hd + 1]
            a_col = a_cols[:, hd:hd + 1]
            cum_row = cum[hd:hd + 1]
            m_prev = m_ref[bi, hd][0:1, 0:1]
            amax = jnp.max(jnp.where(tri, a_col, NEG), axis=0, keepdims=True)
            b_row = jnp.maximum(m_prev, amax)
            dmat = jnp.exp(jnp.where(tri, a_col - b_row, NEG))
            w_intra = (st["scores"][hd] * dmat).astype(BF16)
            intra = _dot(vT_aug.astype(BF16), w_intra)
            w_inter = jnp.exp(m_prev - b_row)
            num = w_inter * inter[0:dh] + intra[0:dh]
            den = w_inter * inter[dh:dh + 1] + intra[dh:dh + 1]
            hc = num * (1.0 / jnp.maximum(jnp.abs(den), jnp.exp(-(cum_row + b_row))))
            mu = jnp.mean(hc, axis=0, keepdims=True)
            hcc = hc - mu
            var = jnp.mean(hcc * hcc, axis=0, keepdims=True)
            hb = (hcc * lax.rsqrt(var + 1e-5)).T * og_ref[:, lo:lo + dh]
            yb = (hb + skip_ref[:, lo:lo + dh] * st["xc"][hd]) * sbz_ref[bi, :, lo:lo + dh].astype(F32)
            o_ref[bi, :, lo:lo + dh] = yb.astype(BF16)
            b_end = b_row[:, L - 1:L]
            ws = jnp.exp(a_row - b_end)
            decay = jnp.exp(m_prev - b_end)
            state_ref[bi, hd] = decay * state + _dot((vT_aug * ws).astype(BF16), ks_h)
            m_ref[bi, hd] = jnp.broadcast_to(cum_row[:, L - 1:L] + b_end, m_ref.shape[2:])

    sts = []
    for bi in range(MLSTM_NB):
        sts.append(project(bi))
        gate_chain(sts[bi])
    for bi in range(MLSTM_NB):
        recur(bi, sts[bi])


def _block_diag_dense(w):
    nblk, blk, _ = w.shape
    n = nblk * blk
    idx = jnp.arange(n) // blk
    return jnp.where(idx[:, None] == idx[None, :], jnp.tile(w.reshape(n, blk), (1, nblk)), 0.0)


def _mlstm(bx, sbz, conv_w, conv_b, wq, wk, wv, w_gates, b_gates, out_g, skip):
    b, s, bw = bx.shape
    L = MLSTM_L
    nc = s // L
    nbb = MLSTM_NB
    assert b % nbb == 0 and s % L == 0
    per_head = wq.shape[0] // B_HEADS

    def head_blocks(w, transpose):
        blocks = [_block_diag_dense(w[h * per_head:(h + 1) * per_head]) for h in range(B_HEADS)]
        return jnp.stack([blk.T if transpose else blk for blk in blocks]).astype(BF16)

    wqt_d = head_blocks(wq, True)
    wk_d = head_blocks(wk, False)
    wvt_d = head_blocks(wv, True)
    hblk = (B_HEADS, B_HEAD_DIM, B_HEAD_DIM)
    wgq = w_gates[:bw].T.astype(BF16)
    wgv = w_gates[2 * bw:].T.astype(BF16)
    wgk = w_gates[bw:2 * bw].T.astype(BF16)
    const = lambda shape: pl.BlockSpec(shape, lambda i, j: (0,) * len(shape))
    tok = pl.BlockSpec((nbb, L, bw), lambda i, j: (i, j, 0))
    return pl.pallas_call(
        _mlstm_kernel,
        grid=(b // nbb, nc),
        in_specs=[tok, tok, const((B_CONV, bw)), const((1, bw)),
                  const(hblk), const(hblk), const(hblk),
                  const((GATE_ROWS, bw)), const((GATE_ROWS, bw)), const((GATE_ROWS, bw)),
                  const((GATE_ROWS, 1)), const((1, bw)), const((1, bw))],
        out_specs=tok,
        out_shape=jax.ShapeDtypeStruct((b, s, bw), BF16),
        scratch_shapes=[pltpu.VMEM((nbb, L + CONV_HALO, bw), F32),
                        pltpu.VMEM((nbb, B_HEADS, 2 * B_HEAD_DIM, B_HEAD_DIM), F32),
                        pltpu.VMEM((nbb, B_HEADS, V7X_SUBLANES, V7X_LANES), F32)],
        compiler_params=pltpu.CompilerParams(
            dimension_semantics=("parallel", "arbitrary"), vmem_limit_bytes=VMEM_LIMIT),
        name="mlstm",
    )(bx, sbz, conv_w, conv_b.reshape(1, bw), wqt_d, wk_d, wvt_d, wgq, wgk, wgv,
      b_gates.reshape(GATE_ROWS, 1), out_g.reshape(1, bw), skip.reshape(1, bw))


def _tail_kernel(x_ref, ya_ref, yb_ref, mod0_ref, mod1_ref, lng_ref, wo0_ref, wi1_ref, clg_ref,
                 clb_ref, ws_ref, bst_ref, wo1_ref, o_ref):
    tm = x_ref.shape[1]
    sub = TAIL_SUB_ROWS
    n_sub = tm // sub
    aw = A_WIDTH
    mod1 = mod1_ref[0]
    gate0 = mod0_ref[0][2:3]
    t_iota = lax.broadcasted_iota(jnp.int32, (C_CHUNK, C_CHUNK), 0)
    s_iota = lax.broadcasted_iota(jnp.int32, (C_CHUNK, C_CHUNK), 1)
    tril = s_iota <= t_iota
    gw = C_WIDTH // C_GROUPS
    bst = bst_ref[...]
    wms = [jnp.where(tril, ws_ref[g], 0.0).astype(BF16) for g in range(C_GROUPS)]

    def out_proj0(r):
        rows = slice(r * sub, (r + 1) * sub)
        y0 = _dot(ya_ref[0, rows, :], wo0_ref[0:aw, :]) + _dot(yb_ref[0, rows, :], wo0_ref[aw:, :])
        x1 = x_ref[0, rows, :] + gate0 * y0
        h = _adaln_rmsnorm(x1, lng_ref[...], mod1[1:2], mod1[0:1]).astype(BF16)
        return x1, h

    def mix_inputs(p):
        u = _gelu_tanh(p[:, :C_WIDTH])
        v = _gelu_tanh(p[:, C_WIDTH:2 * C_WIDTH])
        mu = jnp.mean(v, axis=-1, keepdims=True)
        vc = v - mu
        var = jnp.mean(vc * vc, axis=-1, keepdims=True)
        vn = ((vc * lax.rsqrt(var + 1e-5)) * clg_ref[...] + clb_ref[...]).astype(BF16)
        return vn, u * _silu(p[:, 2 * C_WIDTH:])

    def spatial_gate(vn, gate):
        cols = []
        for g in range(C_GROUPS):
            rows = []
            for ch in range(sub // C_CHUNK):
                vg = vn[ch * C_CHUNK:(ch + 1) * C_CHUNK, g * gw:(g + 1) * gw]
                rows.append(_dot(wms[g], vg) + bst[:, g:g + 1])
            cols.append(jnp.concatenate(rows, axis=0))
        return (gate * jnp.concatenate(cols, axis=1)).astype(BF16)

    def out_proj1(r, x1, y1):
        o_ref[0, r * sub:(r + 1) * sub, :] = x1 + mod1[2:3] * _dot(y1, wo1_ref[...])

    heads = [out_proj0(r) for r in range(n_sub)]
    projs = [_dot(heads[r][1], wi1_ref[...]) for r in range(n_sub)]
    y1_prev = None
    for r in range(n_sub):
        vn, gate = mix_inputs(projs[r])
        if y1_prev is not None:
            out_proj1(r - 1, heads[r - 1][0], y1_prev)
        y1_prev = spatial_gate(vn, gate)
    out_proj1(n_sub - 1, heads[n_sub - 1][0], y1_prev)


def _tail(x, ya, yb, mod0, mod1, ln_g1, w_out0, w_in1, w_out1, c_ln_g, c_ln_b, c_ws, c_bs):
    b, s, d = x.shape
    tm = TAIL_ROW_TILE
    nt = s // tm
    const = lambda shape: pl.BlockSpec(shape, lambda i, j: (0,) * len(shape),
                                       pipeline_mode=pl.Buffered(1))
    half = pl.BlockSpec((1, tm, A_WIDTH), lambda i, j: (i, j, 0))
    full = pl.BlockSpec((1, tm, d), lambda i, j: (i, j, 0))
    modspec = pl.BlockSpec((1, 3, d), lambda i, j: (i, 0, 0))
    return pl.pallas_call(
        _tail_kernel,
        grid=(b, nt),
        in_specs=[full, half, half, modspec, modspec, const((1, d)), const((d, d)),
                  const((d, 3 * C_WIDTH)), const((1, d)), const((1, d)),
                  const((C_GROUPS, C_CHUNK, C_CHUNK)), const((C_CHUNK, C_GROUPS)), const((d, d))],
        out_specs=full,
        out_shape=jax.ShapeDtypeStruct((b, s, d), F32),
        compiler_params=pltpu.CompilerParams(
            dimension_semantics=("parallel", "parallel"), vmem_limit_bytes=VMEM_LIMIT),
        name="tail",
    )(x, ya, yb, mod0, mod1, ln_g1.reshape(1, d), w_out0.astype(BF16), w_in1.astype(BF16),
      c_ln_g.reshape(1, d), c_ln_b.reshape(1, d), c_ws, c_bs.T, w_out1.astype(BF16))


def kernel(x, c, ln_g, ada_w, ada_b, w_in, w_out, a_q_g, a_k_g, b_conv_w, b_conv_b, b_wq, b_wk, b_wv,
           b_w_gates, b_b_gates, b_out_g, b_skip, c_ln_g, c_ln_b, c_ws, c_bs):
    mods = _ada_mods(c, ada_w, ada_b)
    qT, vT, gT, k, bx, sbz = _inproj0(x, mods[0], ln_g[0], w_in[0], a_q_g[0], a_k_g[0])
    ya = _moba(qT, k, vT, gT)
    yb = _mlstm(bx, sbz, b_conv_w[0], b_conv_b[0], b_wq[0], b_wk[0], b_wv[0], b_w_gates[0],
                b_b_gates[0], b_out_g[0], b_skip[0])
    return _tail(x, ya, yb, mods[0], mods[1], ln_g[1], w_out[0], w_in[1], w_out[1],
                 c_ln_g[0], c_ln_b[0], c_ws[0], c_bs[0])
```

```python
import functools

import jax
import jax.numpy as jnp
from jax import lax
from jax.experimental import pallas as pl
from jax.experimental.pallas import tpu as pltpu

F32 = jnp.float32
BF16 = jnp.bfloat16

D_MODEL = 1024
A_HEADS = 8
A_HEAD_DIM = 64
A_WIDTH = A_HEADS * A_HEAD_DIM
MOBA_BLOCK = 256
MOBA_TOPK = 3
B_HEADS = 4
B_HEAD_DIM = 128
B_WIDTH = B_HEADS * B_HEAD_DIM
B_CONV = 4
C_GROUPS = 8
C_CHUNK = 128
C_WIDTH = D_MODEL
NEG = -1e30
LOG2E = 1.4426950408889634

V7X_SUBLANES = 8
V7X_LANES = 128

MLSTM_L = 256
MLSTM_NB = 4
INPROJ_SUBTILES = 4
TAIL_ROW_TILE = 1024
TAIL_SUB_ROWS = 512
CONV_HALO = 8
GATE_ROWS = 2 * B_HEADS
PV_ONES_ROWS = 16
MOBA_NB = 2
VMEM_LIMIT = 48 * 1024 * 1024
MOBA_VMEM_LIMIT = 56 * 1024 * 1024


def _silu(x):
    return x * jax.nn.sigmoid(x)


def _gelu_tanh(x):
    return 0.5 * x * (1.0 + jnp.tanh(0.7978845608028654 * (x + 0.044715 * (x * x * x))))


def _log_sigmoid(x):
    return jnp.minimum(x, 0.0) - jnp.log(1.0 + jnp.exp(-jnp.abs(x)))


def _split_bf16(x):
    hi = x.astype(BF16)
    lo = (x - hi.astype(F32)).astype(BF16)
    return hi, lo


def _dot(a, b):
    return jnp.dot(a, b, preferred_element_type=F32)


def _dot_nt(a, b):
    return lax.dot_general(a, b, (((1,), (1,)), ((), ())), preferred_element_type=F32)


def _adaln_rmsnorm(x, ln_g, scale, shift):
    y = x * lax.rsqrt(jnp.mean(x * x, axis=-1, keepdims=True) + 1e-6)
    return (y * ln_g) * (1.0 + scale) + shift


def _ada_kernel(c_ref, w_ref, b_ref, o_ref):
    cs_hi, cs_lo = _split_bf16(_silu(c_ref[...]))
    w_hi, w_lo = _split_bf16(w_ref[0])
    o_ref[0] = _dot(cs_hi, w_hi) + _dot(cs_lo, w_hi) + _dot(cs_hi, w_lo) + b_ref[0]


def _ada_mods(c, ada_w, ada_b):
    depth, d, d3 = ada_w.shape
    b = c.shape[0]
    bp = V7X_SUBLANES * pl.cdiv(b, V7X_SUBLANES)
    cp =jnp.zeros((bp, d), F32).at[:b].set(c)
    nt = d3 // d
    out = pl.pallas_call(
        _ada_kernel,
        grid=(depth, nt),
        in_specs=[
            pl.BlockSpec((bp, d), lambda l, n: (0, 0)),
            pl.BlockSpec((1, d, d), lambda l, n: (l, 0, n)),
            pl.BlockSpec((1, 1, d), lambda l, n: (l, 0, n)),
        ],
        out_specs=pl.BlockSpec((1, bp, d), lambda l, n: (l, 0, n)),
        out_shape=jax.ShapeDtypeStruct((depth, bp, d3), F32),
        name="ada_mods",
    )(cp, ada_w, ada_b.reshape(depth, 1, d3))
    return out[:, :b].reshape(depth, b, 3, d)


def _inproj0_kernel(x_ref, mod_ref, lng_ref, wt_ref, w_ref, qg_ref, kg_ref,
                    qT_ref, vT_ref, gT_ref, k_ref, bx_ref, sbz_ref):
    tm = MOBA_BLOCK
    aw = A_WIDTH
    mod = mod_ref[0]
    qg = qg_ref[...] * (A_HEAD_DIM ** -0.5 * LOG2E)
    kg = kg_ref[...]
    hs = [_adaln_rmsnorm(x_ref[0, r * tm:(r + 1) * tm], lng_ref[...], mod[1:2], mod[0:1]).astype(BF16)
          for r in range(INPROJ_SUBTILES)]
    for r in range(INPROJ_SUBTILES):
        h = hs[r]
        rows = slice(r * tm, (r + 1) * tm)
        pt = _dot_nt(wt_ref[...], h)
        p = _dot(h, w_ref[...])
        kn = []
        for hd in range(A_HEADS):
            lo = hd * A_HEAD_DIM
            q = pt[lo:lo + A_HEAD_DIM]
            rq = lax.rsqrt(jnp.mean(q * q, axis=0, keepdims=True) + 1e-6)
            qT_ref[0, r, lo:lo + A_HEAD_DIM, :] = ((q * rq) * qg).astype(BF16)
            k = pt[aw + lo:aw + lo + A_HEAD_DIM]
            rk = lax.rsqrt(jnp.mean(k * k, axis=0, keepdims=True) + 1e-6)
            kn.append((k * rk) * kg)
        k_ref[0, rows, :] = jnp.concatenate(kn, axis=0).T.astype(BF16)
        vT_ref[0, r] = pt[2 * aw:3 * aw].astype(BF16)
        gT_ref[0, r] = _silu(pt[3 * aw:]).astype(BF16)
        bx_ref[0, rows, :] = p[:, :B_WIDTH].astype(BF16)
        sbz_ref[0, rows, :] = _silu(p[:, B_WIDTH:]).astype(BF16)


def _inproj0(x, mod, ln_g, w_in, q_g, k_g):
    b, s, d = x.shape
    blk = MOBA_BLOCK
    sub = INPROJ_SUBTILES
    tm = sub * blk
    nt = s // tm
    aw, bw = A_WIDTH, B_WIDTH
    assert w_in.shape == (d, 4 * aw + 2 * bw)
    wt = w_in[:, :4 * aw].T.astype(BF16)
    wn = w_in[:, 4 * aw:].astype(BF16)
    t_shape = jax.ShapeDtypeStruct((b, s // blk, aw, blk), BF16)
    n_shape = jax.ShapeDtypeStruct((b, s, aw), BF16)
    t_spec = pl.BlockSpec((1, sub, aw, blk), lambda i, j: (i, j, 0, 0))
    n_spec = pl.BlockSpec((1, tm, aw), lambda i, j: (i, j, 0))
    const = lambda shape: pl.BlockSpec(shape, lambda i, j: (0,) * len(shape),
                                       pipeline_mode=pl.Buffered(1))
    return pl.pallas_call(
        _inproj0_kernel,
        grid=(b, nt),
        in_specs=[
            pl.BlockSpec((1, tm, d), lambda i, j: (i, j, 0)),
            pl.BlockSpec((1, 3, d), lambda i, j: (i, 0, 0)),
            const((1, d)),
            const((4 * aw, d)),
            const((d, 2 * bw)),
            const((A_HEAD_DIM, 1)),
            const((A_HEAD_DIM, 1)),
        ],
        out_specs=[t_spec, t_spec, t_spec, n_spec, n_spec, n_spec],
        out_shape=[t_shape, t_shape, t_shape, n_shape, n_shape, n_shape],
        compiler_params=pltpu.CompilerParams(
            dimension_semantics=("parallel", "parallel"), vmem_limit_bytes=VMEM_LIMIT),
        name="inproj0",
    )(x, mod, ln_g.reshape(1, d), wt, wn, q_g.reshape(A_HEAD_DIM, 1), k_g.reshape(A_HEAD_DIM, 1))


def _moba_kernel(qT_ref, k_ref, vT_ref, gT_ref, o_ref, kmean_ref, sel_ref, qm_ref, raw_ref, p_ref,
                 m_ref, alpha_ref, acc_ref, *, nb):
    blk = MOBA_BLOCK
    dh = A_HEAD_DIM
    pw = 2 * dh
    streams = range(MOBA_NB)
    j = pl.program_id(1)

    @pl.when(j == 0)
    def _():
        for z in streams:
            for n in range(nb):
                kb = k_ref[z, n * blk:(n + 1) * blk, :].astype(F32)
                kmean_ref[z, n:n + 1, :] = jnp.mean(kb, axis=0, keepdims=True)

    d_iota = lax.broadcasted_iota(jnp.int32, (pw, blk), 0)
    for z in streams:
        for hd in range(A_HEADS):
            pr = hd // 2
            q2 = qT_ref[z, 0, pr * pw:(pr + 1) * pw, :]
            keep = (d_iota < dh) if hd % 2 == 0 else (d_iota >= dh)
            qm_ref[z, hd] = jnp.where(keep, q2, jnp.zeros_like(q2))

    def score_dots(z, n, slot):
        row0 = pl.multiple_of(n * blk, blk)
        for hd in range(A_HEADS):
            pr = hd // 2
            kt = k_ref[z, pl.ds(row0, blk), pr * pw:(pr + 1) * pw]
            raw_ref[z, slot, hd] = _dot(kt, qm_ref[z, hd]).astype(BF16)

    def value_dots(z, n, slot):
        ones = jnp.ones((PV_ONES_ROWS, blk), BF16)
        return [_dot(jnp.concatenate([vT_ref[z, n, hd * dh:(hd + 1) * dh, :], ones], axis=0),
                     p_ref[z, slot, hd]) for hd in range(A_HEADS)]

    sel_scores = []
    for z in streams:
        km_hi, km_lo = _split_bf16(kmean_ref[z])
        km2 = jnp.concatenate([km_hi, km_lo], axis=0)
        per_head = []
        for hd in range(A_HEADS):
            pr = hd // 2
            r2 = _dot(km2[:, pr * pw:(pr + 1) * pw], qm_ref[z, hd])
            per_head.append(r2[0:nb] + r2[nb:2 * nb])
        sel_scores.append(per_head)

    for z in streams:
        score_dots(z, j, 0)
    for z in streams:
        score_dots(z, 0, 1)

    n_iota = lax.broadcasted_iota(jnp.int32, (nb, blk), 0)
    past = n_iota < j
    for z in streams:
        for hd in range(A_HEADS):
            sc = jnp.where(past, sel_scores[z][hd], NEG)
            sel = jnp.zeros((nb, blk), F32)
            for _ in range(MOBA_TOPK):
                mx = jnp.max(sc, axis=0, keepdims=True)
                first = jnp.min(jnp.where(sc == mx, n_iota, nb), axis=0, keepdims=True)
                pick = n_iota == first
                sel = jnp.where(pick, 1.0, sel)
                sc = jnp.where(pick, -jnp.inf, sc)
            sel_ref[z, hd, 0:nb, :] = jnp.where(past, sel, 0.0)
            sel_ref[z, hd, nb:nb + 1, :] = jnp.ones((1, blk), F32)

    k_iota = lax.broadcasted_iota(jnp.int32, (blk, blk), 0)
    q_iota = lax.broadcasted_iota(jnp.int32, (blk, blk), 1)
    causal_bias = jnp.where(k_iota <= q_iota, 0.0, NEG).astype(BF16)
    for z in streams:
        for hd in range(A_HEADS):
            s = raw_ref[z, 0, hd] + causal_bias
            m_new = jnp.max(s, axis=0, keepdims=True)
            p_ref[z, 0, hd] = jnp.exp2(s - m_new)
            m_ref[z, hd] = m_new.astype(F32)
            acc_ref[z, hd] = jnp.zeros(acc_ref.shape[2:], F32)

    def step(z, i, cur):
        nxt = 1 - cur
        n = i - 1
        prev = jnp.where(i == 1, j, i - 2)
        prev_sel = jnp.where(i == 1, nb, i - 2)
        row0 = pl.multiple_of(jnp.minimum(i, nb - 1) * blk, blk)
        ones = jnp.ones((PV_ONES_ROWS, blk), BF16)
        pvs = []
        for hd in range(A_HEADS):
            pr = hd // 2
            kt = k_ref[z, pl.ds(row0, blk), pr * pw:(pr + 1) * pw]
            raw_ref[z, nxt, hd] = _dot(kt, qm_ref[z, hd]).astype(BF16)
            pvs.append(_dot(jnp.concatenate([vT_ref[z, prev, hd * dh:(hd + 1) * dh, :], ones], axis=0),
                            p_ref[z, nxt, hd]))
            s = raw_ref[z, cur, hd]
            selb = sel_ref[z, hd, pl.ds(n, 1), :] > 0.5
            smax = jnp.where(selb, jnp.max(s, axis=0, keepdims=True).astype(F32), NEG)
            m_old = m_ref[z, hd]
            m_new = jnp.maximum(m_old, smax)
            alpha_ref[z, hd] = jnp.exp2(m_old - m_new)
            m_ref[z, hd] = m_new
            p_ref[z, cur, hd] = jnp.exp2(s - m_new.astype(BF16))
        for hd in range(A_HEADS):
            keep = sel_ref[z, hd, pl.ds(prev_sel, 1), :] > 0.5
            acc_ref[z, hd] = alpha_ref[z, hd] * (acc_ref[z, hd] + jnp.where(keep, pvs[hd], 0.0))

    def body(ii, carry):
        for z in streams:
            step(z, 2 * ii + 1, 1)
        for z in streams:
            step(z, 2 * ii + 2, 0)
        return carry

    trips = (j + 1) // 2
    lax.fori_loop(0, trips, body, 0)

    last = jnp.where(j == 0, j, 2 * trips - 1)
    last_sel = jnp.where(j == 0, nb, 2 * trips - 1)
    last_pvs = [value_dots(z, last, 0) for z in streams]
    for z in streams:
        for pr in range(A_HEADS // 2):
            halves = []
            for hd in (2 * pr, 2 * pr + 1):
                keep = sel_ref[z, hd, pl.ds(last_sel, 1), :] > 0.5
                acc = acc_ref[z, hd] + jnp.where(keep, last_pvs[z][hd], 0.0)
                halves.append(acc[0:dh] / acc[dh:dh + 1])
            oT = jnp.concatenate(halves, axis=0)
            oT = oT * gT_ref[z, 0, pr * pw:(pr + 1) * pw, :].astype(F32)
            o_ref[z, :, pr * pw:(pr + 1) * pw] = oT.T.astype(BF16)


def _moba(qT, k, vT, gT):
    b, nb, aw, blk = qT.shape
    s = nb * blk
    pw = 2 * A_HEAD_DIM
    z = MOBA_NB
    assert b % z == 0
    q_spec = pl.BlockSpec((z, 1, aw, blk), lambda i, j: (i, j, 0, 0))
    row = lambda: pltpu.VMEM((z, A_HEADS, 1, blk), F32)
    return pl.pallas_call(
        functools.partial(_moba_kernel, nb=nb),
        grid=(b // z, nb),
        in_specs=[
            q_spec,
            pl.BlockSpec((z, s, aw), lambda i, j: (i, 0, 0)),
            pl.BlockSpec((z, nb, aw, blk), lambda i, j: (i, 0, 0, 0)),
            q_spec,
        ],
        out_specs=pl.BlockSpec((z, blk, aw), lambda i, j: (i, j, 0)),
        out_shape=jax.ShapeDtypeStruct((b, s, aw), BF16),
        scratch_shapes=[
            pltpu.VMEM((z, nb, aw), F32),
            pltpu.VMEM((z, A_HEADS, nb + V7X_SUBLANES, blk), F32),
            pltpu.VMEM((z, A_HEADS, pw, blk), BF16),
            pltpu.VMEM((z, 2, A_HEADS, blk, blk), BF16),
            pltpu.VMEM((z, 2, A_HEADS, blk, blk), BF16),
            row(), row(),
            pltpu.VMEM((z, A_HEADS, A_HEAD_DIM + PV_ONES_ROWS, blk), F32),
        ],
        compiler_params=pltpu.CompilerParams(
            dimension_semantics=("parallel", "arbitrary"), vmem_limit_bytes=MOBA_VMEM_LIMIT),
        name="moba",
    )(qT, k, vT, gT)


def _mlstm_kernel(bx_ref, sbz_ref, cw_ref, cb_ref, wqt_ref, wk_ref, wvt_ref, wgq_ref, wgk_ref,
                  wgv_ref, bg_ref, og_ref, skip_ref, o_ref, xbuf_ref, state_ref, m_ref):
    L = MLSTM_L
    dh = B_HEAD_DIM
    c = pl.program_id(1)

    @pl.when(c == 0)
    def _():
        xbuf_ref[:, 0:CONV_HALO, :] = jnp.zeros((MLSTM_NB, CONV_HALO, B_WIDTH), F32)
        state_ref[...] = jnp.zeros_like(state_ref)
        m_ref[...] = jnp.zeros_like(m_ref)

    cw = cw_ref[...]
    cb = cb_ref[...]
    kscale = dh ** -0.5
    s_iota = lax.broadcasted_iota(jnp.int32, (L, L), 0)
    t_iota = lax.broadcasted_iota(jnp.int32, (L, L), 1)
    tri = s_iota <= t_iota
    tri_b = jnp.where(tri, 1.0, 0.0).astype(BF16)
    ones = jnp.ones((dh, L), F32)

    def project(bi):
        bx_b = bx_ref[bi]
        xbuf_ref[bi, CONV_HALO:CONV_HALO + L, :] = bx_b.astype(F32)
        st = dict(xc=[], ks=[], vT=[], scores=[], inter=[], state=[])
        gT = bg_ref[...]
        for hd in range(B_HEADS):
            lo = hd * dh
            conv = cb[:, lo:lo + dh]
            for i in range(B_CONV):
                off = CONV_HALO - (B_CONV - 1) + i
                conv = conv + cw[i:i + 1, lo:lo + dh] * xbuf_ref[bi, off:off + L, lo:lo + dh]
            xc = _silu(conv)
            xc_b = xc.astype(BF16)
            qT_h = _dot_nt(wqt_ref[hd], xc_b).astype(BF16)
            vT_h = _dot_nt(wvt_ref[hd], bx_b[:, lo:lo + dh])
            k_h = _dot(xc_b, wk_ref[hd])
            ks_h = (k_h * kscale).astype(BF16)
            state = state_ref[bi, hd]
            st["scores"].append(_dot(ks_h, qT_h))
            st["inter"].append(_dot(state.astype(BF16), qT_h))
            gT = gT + (_dot(wgq_ref[:, lo:lo + dh], qT_h)
                       + _dot_nt(wgk_ref[:, lo:lo + dh], k_h.astype(BF16))
                       + _dot(wgv_ref[:, lo:lo + dh], vT_h.astype(BF16)))
            st["xc"].append(xc); st["ks"].append(ks_h); st["vT"].append(vT_h); st["state"].append(state)
        tail = xbuf_ref[bi, L:L + CONV_HALO, :]
        xbuf_ref[bi, 0:CONV_HALO, :] = tail
        st["gT"] = gT
        return st

    def gate_chain(st):
        gT = st["gT"]
        lf_hi, lf_lo = _split_bf16(_log_sigmoid(gT))
        cum = (_dot(lf_hi, tri_b) + _dot(lf_lo, tri_b))[B_HEADS:2 * B_HEADS]
        a = gT[0:B_HEADS] - cum
        st["cum"], st["a"] = cum, a
        pad = jnp.zeros((V7X_LANES - B_HEADS, L), F32)
        st["a_cols"] = jnp.concatenate([a, pad], axis=0).T

    def recur(bi, st):
        a, cum, a_cols = st["a"], st["cum"], st["a_cols"]
        for hd in range(B_HEADS):
            lo = hd * dh
            ks_h, state, inter = st["ks"][hd], st["state"][hd], st["inter"][hd]
            vT_aug = jnp.concatenate([st["vT"][hd], ones], axis=0)
            a_row = a[hd:hd + 1]
            a_col = a_cols[:, hd:hd + 1]
            cum_row = cum[hd:hd + 1]
            m_prev = m_ref[bi, hd][0:1, 0:1]
            amax = jnp.max(jnp.where(tri, a_col, NEG), axis=0, keepdims=True)
            b_row = jnp.maximum(m_prev, amax)
            dmat = jnp.exp(jnp.where(tri, a_col - b_row, NEG))
            w_intra = (st["scores"][hd] * dmat).astype(BF16)
            intra = _dot(vT_aug.astype(BF16), w_intra)
            w_inter = jnp.exp(m_prev - b_row)
            num = w_inter * inter[0:dh] + intra[0:dh]
            den = w_inter * inter[dh:dh + 1] + intra[dh:dh + 1]
            hc = num * (1.0 / jnp.maximum(jnp.abs(den), jnp.exp(-(cum_row + b_row))))
            mu = jnp.mean(hc, axis=0, keepdims=True)
            hcc = hc - mu
            var = jnp.mean(hcc * hcc, axis=0, keepdims=True)
            hb = (hcc * lax.rsqrt(var + 1e-5)).T * og_ref[:, lo:lo + dh]
            yb = (hb + skip_ref[:, lo:lo + dh] * st["xc"][hd]) * sbz_ref[bi, :, lo:lo + dh].astype(F32)
            o_ref[bi, :, lo:lo + dh] = yb.astype(BF16)
            b_end = b_row[:, L - 1:L]
            ws = jnp.exp(a_row - b_end)
            decay = jnp.exp(m_prev - b_end)
            state_ref[bi, hd] = decay * state + _dot((vT_aug * ws).astype(BF16), ks_h)
            m_ref[bi, hd] = jnp.broadcast_to(cum_row[:, L - 1:L] + b_end, m_ref.shape[2:])

    sts = []
    for bi in range(MLSTM_NB):
        sts.append(project(bi))
        gate_chain(sts[bi])
    for bi in range(MLSTM_NB):
        recur(bi, sts[bi])


def _block_diag_dense(w):
    nblk, blk, _ = w.shape
    n = nblk * blk
    idx = jnp.arange(n) // blk
    return jnp.where(idx[:, None] == idx[None, :], jnp.tile(w.reshape(n, blk), (1, nblk)), 0.0)


def _mlstm(bx, sbz, conv_w, conv_b, wq, wk, wv, w_gates, b_gates, out_g, skip):
    b, s, bw = bx.shape
    L = MLSTM_L
    nc = s // L
    nbb = MLSTM_NB
    assert b % nbb == 0 and s % L == 0
    per_head = wq.shape[0] // B_HEADS

    def head_blocks(w, transpose):
        blocks = [_block_diag_dense(w[h * per_head:(h + 1) * per_head]) for h in range(B_HEADS)]
        return jnp.stack([blk.T if transpose else blk for blk in blocks]).astype(BF16)

    wqt_d = head_blocks(wq, True)
    wk_d = head_blocks(wk, False)
    wvt_d = head_blocks(wv, True)
    hblk = (B_HEADS, B_HEAD_DIM, B_HEAD_DIM)
    wgq = w_gates[:bw].T.astype(BF16)
    wgv = w_gates[2 * bw:].T.astype(BF16)
    wgk = w_gates[bw:2 * bw].T.astype(BF16)
    const = lambda shape: pl.BlockSpec(shape, lambda i, j: (0,) * len(shape))
    tok = pl.BlockSpec((nbb, L, bw), lambda i, j: (i, j, 0))
    return pl.pallas_call(
        _mlstm_kernel,
        grid=(b // nbb, nc),
        in_specs=[tok, tok, const((B_CONV, bw)), const((1, bw)),
                  const(hblk), const(hblk), const(hblk),
                  const((GATE_ROWS, bw)), const((GATE_ROWS, bw)), const((GATE_ROWS, bw)),
                  const((GATE_ROWS, 1)), const((1, bw)), const((1, bw))],
        out_specs=tok,
        out_shape=jax.ShapeDtypeStruct((b, s, bw), BF16),
        scratch_shapes=[pltpu.VMEM((nbb, L + CONV_HALO, bw), F32),
                        pltpu.VMEM((nbb, B_HEADS, 2 * B_HEAD_DIM, B_HEAD_DIM), F32),
                        pltpu.VMEM((nbb, B_HEADS, V7X_SUBLANES, V7X_LANES), F32)],
        compiler_params=pltpu.CompilerParams(
            dimension_semantics=("parallel", "arbitrary"), vmem_limit_bytes=VMEM_LIMIT),
        name="mlstm",
    )(bx, sbz, conv_w, conv_b.reshape(1, bw), wqt_d, wk_d, wvt_d, wgq, wgk, wgv,
      b_gates.reshape(GATE_ROWS, 1), out_g.reshape(1, bw), skip.reshape(1, bw))


def _tail_kernel(x_ref, ya_ref, yb_ref, mod0_ref, mod1_ref, lng_ref, wo0_ref, wi1_ref, clg_ref,
                 clb_ref, ws_ref, bst_ref, wo1_ref, o_ref):
    tm = x_ref.shape[1]
    sub = TAIL_SUB_ROWS
    n_sub = tm // sub
    aw = A_WIDTH
    mod1 = mod1_ref[0]
    gate0 = mod0_ref[0][2:3]
    t_iota = lax.broadcasted_iota(jnp.int32, (C_CHUNK, C_CHUNK), 0)
    s_iota = lax.broadcasted_iota(jnp.int32, (C_CHUNK, C_CHUNK), 1)
    tril = s_iota <= t_iota
    gw = C_WIDTH // C_GROUPS
    bst = bst_ref[...]
    wms = [jnp.where(tril, ws_ref[g], 0.0).astype(BF16) for g in range(C_GROUPS)]

    def out_proj0(r):
        rows = slice(r * sub, (r + 1) * sub)
        y0 = _dot(ya_ref[0, rows, :], wo0_ref[0:aw, :]) + _dot(yb_ref[0, rows, :], wo0_ref[aw:, :])
        x1 = x_ref[0, rows, :] + gate0 * y0
        h = _adaln_rmsnorm(x1, lng_ref[...], mod1[1:2], mod1[0:1]).astype(BF16)
        return x1, h

    def mix_inputs(p):
        u = _gelu_tanh(p[:, :C_WIDTH])
        v = _gelu_tanh(p[:, C_WIDTH:2 * C_WIDTH])
        mu = jnp.mean(v, axis=-1, keepdims=True)
        vc = v - mu
        var = jnp.mean(vc * vc, axis=-1, keepdims=True)
        vn = ((vc * lax.rsqrt(var + 1e-5)) * clg_ref[...] + clb_ref[...]).astype(BF16)
        return vn, u * _silu(p[:, 2 * C_WIDTH:])

    def spatial_gate(vn, gate):
        cols = []
        for g in range(C_GROUPS):
            rows = []
            for ch in range(sub // C_CHUNK):
                vg = vn[ch * C_CHUNK:(ch + 1) * C_CHUNK, g * gw:(g + 1) * gw]
                rows.append(_dot(wms[g], vg) + bst[:, g:g + 1])
            cols.append(jnp.concatenate(rows, axis=0))
        return (gate * jnp.concatenate(cols, axis=1)).astype(BF16)

    def out_proj1(r, x1, y1):
        o_ref[0, r * sub:(r + 1) * sub, :] = x1 + mod1[2:3] * _dot(y1, wo1_ref[...])

    heads = [out_proj0(r) for r in range(n_sub)]
    projs = [_dot(heads[r][1], wi1_ref[...]) for r in range(n_sub)]
    y1_prev = None
    for r in range(n_sub):
        vn, gate = mix_inputs(projs[r])
        if y1_prev is not None:
            out_proj1(r - 1, heads[r - 1][0], y1_prev)
        y1_prev = spatial_gate(vn, gate)
    out_proj1(n_sub - 1, heads[n_sub - 1][0], y1_prev)


def _tail(x, ya, yb, mod0, mod1, ln_g1, w_out0, w_in1, w_out1, c_ln_g, c_ln_b, c_ws, c_bs):
    b, s, d = x.shape
    tm = TAIL_ROW_TILE
    nt = s // tm
    const = lambda shape: pl.BlockSpec(shape, lambda i, j: (0,) * len(shape),
                                       pipeline_mode=pl.Buffered(1))
    half = pl.BlockSpec((1, tm, A_WIDTH), lambda i, j: (i, j, 0))
    full = pl.BlockSpec((1, tm, d), lambda i, j: (i, j, 0))
    modspec = pl.BlockSpec((1, 3, d), lambda i, j: (i, 0, 0))
    return pl.pallas_call(
        _tail_kernel,
        grid=(b, nt),
        in_specs=[full, half, half, modspec, modspec, const((1, d)), const((d, d)),
                  const((d, 3 * C_WIDTH)), const((1, d)), const((1, d)),
                  const((C_GROUPS, C_CHUNK, C_CHUNK)), const((C_CHUNK, C_GROUPS)), const((d, d))],
        out_specs=full,
        out_shape=jax.ShapeDtypeStruct((b, s, d), F32),
        compiler_params=pltpu.CompilerParams(
            dimension_semantics=("parallel", "parallel"), vmem_limit_bytes=VMEM_LIMIT),
        name="tail",
    )(x, ya, yb, mod0, mod1, ln_g1.reshape(1, d), w_out0.astype(BF16), w_in1.astype(BF16),
      c_ln_g.reshape(1, d), c_ln_b.reshape(1, d), c_ws, c_bs.T, w_out1.astype(BF16))


def kernel(x, c, ln_g, ada_w, ada_b, w_in, w_out, a_q_g, a_k_g, b_conv_w, b_conv_b, b_wq, b_wk, b_wv,
           b_w_gates, b_b_gates, b_out_g, b_skip, c_ln_g, c_ln_b, c_ws, c_bs):
    mods = _ada_mods(c, ada_w, ada_b)
    qT, vT, gT, k, bx, sbz = _inproj0(x, mods[0], ln_g[0], w_in[0], a_q_g[0], a_k_g[0])
    ya = _moba(qT, k, vT, gT)
    yb = _mlstm(bx, sbz, b_conv_w[0], b_conv_b[0], b_wq[0], b_wk[0], b_wv[0], b_w_gates[0],
                b_b_gates[0], b_out_g[0], b_skip[0])
    return _tail(x, ya, yb, mods[0], mods[1], ln_g[1], w_out[0], w_in[1], w_out[1],
                 c_ln_g[0], c_ln_b[0], c_ws[0], c_bs[0])
```

```python
import functools

import jax
import jax.numpy as jnp
from jax import lax
from jax.experimental import pallas as pl
from jax.experimental.pallas import tpu as pltpu

F32 = jnp.float32
BF16 = jnp.bfloat16

D_MODEL = 1024
A_HEADS = 8
A_HEAD_DIM = 64
A_WIDTH = A_HEADS * A_HEAD_DIM
MOBA_BLOCK = 256
MOBA_TOPK = 3
B_HEADS = 4
B_HEAD_DIM = 128
B_WIDTH = B_HEADS * B_HEAD_DIM
B_CONV = 4
C_GROUPS = 8
C_CHUNK = 128
C_WIDTH = D_MODEL
NEG = -1e30
LOG2E = 1.4426950408889634

V7X_SUBLANES = 8
V7X_LANES = 128

MLSTM_L = 256
MLSTM_NB = 4
INPROJ_SUBTILES = 4
TAIL_ROW_TILE = 1024
TAIL_SUB_ROWS = 512
CONV_HALO = 8
GATE_ROWS = 2 * B_HEADS
PV_ONES_ROWS = 16
MOBA_NB = 2
VMEM_LIMIT = 48 * 1024 * 1024
MOBA_VMEM_LIMIT = 56 * 1024 * 1024


def _silu(x):
    return x * jax.nn.sigmoid(x)


def _gelu_tanh(x):
    return 0.5 * x * (1.0 + jnp.tanh(0.7978845608028654 * (x + 0.044715 * (x * x * x))))


def _log_sigmoid(x):
    return jnp.minimum(x, 0.0) - jnp.log(1.0 + jnp.exp(-jnp.abs(x)))


def _split_bf16(x):
    hi = x.astype(BF16)
    lo = (x - hi.astype(F32)).astype(BF16)
    return hi, lo


def _dot(a, b):
    return jnp.dot(a, b, preferred_element_type=F32)


def _dot_nt(a, b):
    return lax.dot_general(a, b, (((1,), (1,)), ((), ())), preferred_element_type=F32)


def _adaln_rmsnorm(x, ln_g, scale, shift):
    y = x * lax.rsqrt(jnp.mean(x * x, axis=-1, keepdims=True) + 1e-6)
    return (y * ln_g) * (1.0 + scale) + shift


def _ada_kernel(c_ref, w_ref, b_ref, o_ref):
    cs_hi, cs_lo = _split_bf16(_silu(c_ref[...]))
    w_hi, w_lo = _split_bf16(w_ref[0])
    o_ref[0] = _dot(cs_hi, w_hi) + _dot(cs_lo, w_hi) + _dot(cs_hi, w_lo) + b_ref[0]


def _ada_mods(c, ada_w, ada_b):
    depth, d, d3 = ada_w.shape
    b = c.shape[0]
    bp = V7X_SUBLANES * pl.cdiv(b, V7X_SUBLANES)
    cp =jnp.zeros((bp, d), F32).at[:b].set(c)
    nt = d3 // d
    out = pl.pallas_call(
        _ada_kernel,
        grid=(depth, nt),
        in_specs=[
            pl.BlockSpec((bp, d), lambda l, n: (0, 0)),
            pl.BlockSpec((1, d, d), lambda l, n: (l, 0, n)),
            pl.BlockSpec((1, 1, d), lambda l, n: (l, 0, n)),
        ],
        out_specs=pl.BlockSpec((1, bp, d), lambda l, n: (l, 0, n)),
        out_shape=jax.ShapeDtypeStruct((depth, bp, d3), F32),
        name="ada_mods",
    )(cp, ada_w, ada_b.reshape(depth, 1, d3))
    return out[:, :b].reshape(depth, b, 3, d)


def _inproj0_kernel(x_ref, mod_ref, lng_ref, wt_ref, w_ref, qg_ref, kg_ref,
                    qT_ref, vT_ref, gT_ref, k_ref, bx_ref, sbz_ref):
    tm = MOBA_BLOCK
    aw = A_WIDTH
    mod = mod_ref[0]
    qg = qg_ref[...] * (A_HEAD_DIM ** -0.5 * LOG2E)
    kg = kg_ref[...]
    hs = [_adaln_rmsnorm(x_ref[0, r * tm:(r + 1) * tm], lng_ref[...], mod[1:2], mod[0:1]).astype(BF16)
          for r in range(INPROJ_SUBTILES)]
    for r in range(INPROJ_SUBTILES):
        h = hs[r]
        rows = slice(r * tm, (r + 1) * tm)
        pt = _dot_nt(wt_ref[...], h)
        p = _dot(h, w_ref[...])
        kn = []
        for hd in range(A_HEADS):
            lo = hd * A_HEAD_DIM
            q = pt[lo:lo + A_HEAD_DIM]
            rq = lax.rsqrt(jnp.mean(q * q, axis=0, keepdims=True) + 1e-6)
            qT_ref[0, r, lo:lo + A_HEAD_DIM, :] = ((q * rq) * qg).astype(BF16)
            k = pt[aw + lo:aw + lo + A_HEAD_DIM]
            rk = lax.rsqrt(jnp.mean(k * k, axis=0, keepdims=True) + 1e-6)
            kn.append((k * rk) * kg)
        k_ref[0, rows, :] = jnp.concatenate(kn, axis=0).T.astype(BF16)
        vT_ref[0, r] = pt[2 * aw:3 * aw].astype(BF16)
        gT_ref[0, r] = _silu(pt[3 * aw:]).astype(BF16)
        bx_ref[0, rows, :] = p[:, :B_WIDTH].astype(BF16)
        sbz_ref[0, rows, :] = _silu(p[:, B_WIDTH:]).astype(BF16)


def _inproj0(x, mod, ln_g, w_in, q_g, k_g):
    b, s, d = x.shape
    blk = MOBA_BLOCK
    sub = INPROJ_SUBTILES
    tm = sub * blk
    nt = s // tm
    aw, bw = A_WIDTH, B_WIDTH
    assert w_in.shape == (d, 4 * aw + 2 * bw)
    wt = w_in[:, :4 * aw].T.astype(BF16)
    wn = w_in[:, 4 * aw:].astype(BF16)
    t_shape = jax.ShapeDtypeStruct((b, s // blk, aw, blk), BF16)
    n_shape = jax.ShapeDtypeStruct((b, s, aw), BF16)
    t_spec = pl.BlockSpec((1, sub, aw, blk), lambda i, j: (i, j, 0, 0))
    n_spec = pl.BlockSpec((1, tm, aw), lambda i, j: (i, j, 0))
    const = lambda shape: pl.BlockSpec(shape, lambda i, j: (0,) * len(shape),
                                       pipeline_mode=pl.Buffered(1))
    return pl.pallas_call(
        _inproj0_kernel,
        grid=(b, nt),
        in_specs=[
            pl.BlockSpec((1, tm, d), lambda i, j: (i, j, 0)),
            pl.BlockSpec((1, 3, d), lambda i, j: (i, 0, 0)),
            const((1, d)),
            const((4 * aw, d)),
            const((d, 2 * bw)),
            const((A_HEAD_DIM, 1)),
            const((A_HEAD_DIM, 1)),
        ],
        out_specs=[t_spec, t_spec, t_spec, n_spec, n_spec, n_spec],
        out_shape=[t_shape, t_shape, t_shape, n_shape, n_shape, n_shape],
        compiler_params=pltpu.CompilerParams(
            dimension_semantics=("parallel", "parallel"), vmem_limit_bytes=VMEM_LIMIT),
        name="inproj0",
    )(x, mod, ln_g.reshape(1, d), wt, wn, q_g.reshape(A_HEAD_DIM, 1), k_g.reshape(A_HEAD_DIM, 1))


def _moba_kernel(qT_ref, k_ref, vT_ref, gT_ref, o_ref, kmean_ref, sel_ref, qm_ref, raw_ref, p_ref,
                 m_ref, alpha_ref, acc_ref, *, nb):
    blk = MOBA_BLOCK
    dh = A_HEAD_DIM
    pw = 2 * dh
    streams = range(MOBA_NB)
    j = pl.program_id(1)

    @pl.when(j == 0)
    def _():
        for z in streams:
            for n in range(nb):
                kb = k_ref[z, n * blk:(n + 1) * blk, :].astype(F32)
                kmean_ref[z, n:n + 1, :] = jnp.mean(kb, axis=0, keepdims=True)

    d_iota = lax.broadcasted_iota(jnp.int32, (pw, blk), 0)

    def value_dots(z, n, slot):
        ones = jnp.ones((PV_ONES_ROWS, blk), BF16)
        return [_dot(jnp.concatenate([vT_ref[z, n, hd * dh:(hd + 1) * dh, :], ones], axis=0),
                     p_ref[z, slot, hd]) for hd in range(A_HEADS)]

    sel_scores = []
    own0 = pl.multiple_of(j * blk, blk)
    for z in streams:
        km_hi, km_lo = _split_bf16(kmean_ref[z])
        km2 = jnp.concatenate([km_hi, km_lo], axis=0)
        per_head = []
        for hd in range(A_HEADS):
            pr = hd // 2
            lanes = slice(pr * pw, (pr + 1) * pw)
            q2 = qT_ref[z, 0, lanes, :]
            keep = (d_iota < dh) if hd % 2 == 0 else (d_iota >= dh)
            qm = jnp.where(keep, q2, jnp.zeros_like(q2))
            qm_ref[z, hd] = qm
            r2 = _dot(km2[:, lanes], qm)
            per_head.append(r2[0:nb] + r2[nb:2 * nb])
            raw_ref[z, 0, hd] = _dot(k_ref[z, pl.ds(own0, blk), lanes], qm).astype(BF16)
            raw_ref[z, 1, hd] = _dot(k_ref[z, 0:blk, lanes], qm).astype(BF16)
        sel_scores.append(per_head)

    n_iota = lax.broadcasted_iota(jnp.int32, (nb, blk), 0)
    past = n_iota < j
    for z in streams:
        for hd in range(A_HEADS):
            sc = jnp.where(past, sel_scores[z][hd], NEG)
            sel = jnp.zeros((nb, blk), F32)
            for _ in range(MOBA_TOPK):
                mx = jnp.max(sc, axis=0, keepdims=True)
                first = jnp.min(jnp.where(sc == mx, n_iota, nb), axis=0, keepdims=True)
                pick = n_iota == first
                sel = jnp.where(pick, 1.0, sel)
                sc = jnp.where(pick, -jnp.inf, sc)
            sel_ref[z, hd, 0:nb, :] = jnp.where(past, sel, 0.0)
            sel_ref[z, hd, nb:nb + 1, :] = jnp.ones((1, blk), F32)

    k_iota = lax.broadcasted_iota(jnp.int32, (blk, blk), 0)
    q_iota = lax.broadcasted_iota(jnp.int32, (blk, blk), 1)
    causal_bias = jnp.where(k_iota <= q_iota, 0.0, NEG).astype(BF16)
    for z in streams:
        for hd in range(A_HEADS):
            s = raw_ref[z, 0, hd] + causal_bias
            m_new = jnp.max(s, axis=0, keepdims=True)
            p_ref[z, 0, hd] = jnp.exp2(s - m_new)
            m_ref[z, hd] = m_new.astype(F32)
            acc_ref[z, hd] = jnp.zeros(acc_ref.shape[2:], F32)

    def step(z, i, cur):
        nxt = 1 - cur
        n = i - 1
        prev = jnp.where(i == 1, j, i - 2)
        prev_sel = jnp.where(i == 1, nb, i - 2)
        row0 = pl.multiple_of(jnp.minimum(i, nb - 1) * blk, blk)
        ones = jnp.ones((PV_ONES_ROWS, blk), BF16)
        pvs = []
        for hd in range(A_HEADS):
            pr = hd // 2
            kt = k_ref[z, pl.ds(row0, blk), pr * pw:(pr + 1) * pw]
            raw_ref[z, nxt, hd] = _dot(kt, qm_ref[z, hd]).astype(BF16)
            pvs.append(_dot(jnp.concatenate([vT_ref[z, prev, hd * dh:(hd + 1) * dh, :], ones], axis=0),
                            p_ref[z, nxt, hd]))
            s = raw_ref[z, cur, hd]
            selb = sel_ref[z, hd, pl.ds(n, 1), :] > 0.5
            smax = jnp.where(selb, jnp.max(s, axis=0, keepdims=True).astype(F32), NEG)
            m_old = m_ref[z, hd]
            m_new = jnp.maximum(m_old, smax)
            alpha_ref[z, hd] = jnp.exp2(m_old - m_new)
            m_ref[z, hd] = m_new
            p_ref[z, cur, hd] = jnp.exp2(s - m_new.astype(BF16))
        for hd in range(A_HEADS):
            keep = sel_ref[z, hd, pl.ds(prev_sel, 1), :] > 0.5
            acc_ref[z, hd] = alpha_ref[z, hd] * (acc_ref[z, hd] + jnp.where(keep, pvs[hd], 0.0))

    def body(ii, carry):
        for z in streams:
            step(z, 2 * ii + 1, 1)
        for z in streams:
            step(z, 2 * ii + 2, 0)
        return carry

    trips = (j + 1) // 2
    lax.fori_loop(0, trips, body, 0)

    last = jnp.where(j == 0, j, 2 * trips - 1)
    last_sel = jnp.where(j == 0, nb, 2 * trips - 1)
    last_pvs = [value_dots(z, last, 0) for z in streams]
    for z in streams:
        for pr in range(A_HEADS // 2):
            halves = []
            for hd in (2 * pr, 2 * pr + 1):
                keep = sel_ref[z, hd, pl.ds(last_sel, 1), :] > 0.5
                acc = acc_ref[z, hd] + jnp.where(keep, last_pvs[z][hd], 0.0)
                halves.append(acc[0:dh] / acc[dh:dh + 1])
            oT = jnp.concatenate(halves, axis=0)
            oT = oT * gT_ref[z, 0, pr * pw:(pr + 1) * pw, :].astype(F32)
            o_ref[z, :, pr * pw:(pr + 1) * pw] = oT.T.astype(BF16)


def _moba(qT, k, vT, gT):
    b, nb, aw, blk = qT.shape
    s = nb * blk
    pw = 2 * A_HEAD_DIM
    z = MOBA_NB
    assert b % z == 0
    q_spec = pl.BlockSpec((z, 1, aw, blk), lambda i, j: (i, j, 0, 0))
    row = lambda: pltpu.VMEM((z, A_HEADS, 1, blk), F32)
    return pl.pallas_call(
        functools.partial(_moba_kernel, nb=nb),
        grid=(b // z, nb),
        in_specs=[
            q_spec,
            pl.BlockSpec((z, s, aw), lambda i, j: (i, 0, 0)),
            pl.BlockSpec((z, nb, aw, blk), lambda i, j: (i, 0, 0, 0)),
            q_spec,
        ],
        out_specs=pl.BlockSpec((z, blk, aw), lambda i, j: (i, j, 0)),
        out_shape=jax.ShapeDtypeStruct((b, s, aw), BF16),
        scratch_shapes=[
            pltpu.VMEM((z, nb, aw), F32),
            pltpu.VMEM((z, A_HEADS, nb + V7X_SUBLANES, blk), F32),
            pltpu.VMEM((z, A_HEADS, pw, blk), BF16),
            pltpu.VMEM((z, 2, A_HEADS, blk, blk), BF16),
            pltpu.VMEM((z, 2, A_HEADS, blk, blk), BF16),
            row(), row(),
            pltpu.VMEM((z, A_HEADS, A_HEAD_DIM + PV_ONES_ROWS, blk), F32),
        ],
        compiler_params=pltpu.CompilerParams(
            dimension_semantics=("parallel", "arbitrary"), vmem_limit_bytes=MOBA_VMEM_LIMIT),
        name="moba",
    )(qT, k, vT, gT)


def _mlstm_kernel(bx_ref, sbz_ref, cw_ref, cb_ref, wqt_ref, wk_ref, wvt_ref, wgq_ref, wgk_ref,
                  wgv_ref, bg_ref, og_ref, skip_ref, o_ref, xbuf_ref, state_ref, m_ref):
    L = MLSTM_L
    dh = B_HEAD_DIM
    c = pl.program_id(1)

    @pl.when(c == 0)
    def _():
        xbuf_ref[:, 0:CONV_HALO, :] = jnp.zeros((MLSTM_NB, CONV_HALO, B_WIDTH), F32)
        state_ref[...] = jnp.zeros_like(state_ref)
        m_ref[...] = jnp.zeros_like(m_ref)

    cw = cw_ref[...]
    cb = cb_ref[...]
    kscale = dh ** -0.5
    s_iota = lax.broadcasted_iota(jnp.int32, (L, L), 0)
    t_iota = lax.broadcasted_iota(jnp.int32, (L, L), 1)
    tri = s_iota <= t_iota
    tri_b = jnp.where(tri, 1.0, 0.0).astype(BF16)
    ones = jnp.ones((dh, L), F32)

    def project(bi):
        bx_b = bx_ref[bi]
        xbuf_ref[bi, CONV_HALO:CONV_HALO + L, :] = bx_b.astype(F32)
        st = dict(xc=[], ks=[], vT=[], scores=[], inter=[], state=[])
        gT = bg_ref[...]
        for hd in range(B_HEADS):
            lo = hd * dh
            conv = cb[:, lo:lo + dh]
            for i in range(B_CONV):
                off = CONV_HALO - (B_CONV - 1) + i
                conv = conv + cw[i:i + 1, lo:lo + dh] * xbuf_ref[bi, off:off + L, lo:lo + dh]
            xc = _silu(conv)
            xc_b = xc.astype(BF16)
            qT_h = _dot_nt(wqt_ref[hd], xc_b).astype(BF16)
            vT_h = _dot_nt(wvt_ref[hd], bx_b[:, lo:lo + dh])
            k_h = _dot(xc_b, wk_ref[hd])
            ks_h = (k_h * kscale).astype(BF16)
            state = state_ref[bi, hd]
            st["scores"].append(_dot(ks_h, qT_h))
            st["inter"].append(_dot(state.astype(BF16), qT_h))
            gT = gT + (_dot(wgq_ref[:, lo:lo + dh], qT_h)
                       + _dot_nt(wgk_ref[:, lo:lo + dh], k_h.astype(BF16))
                       + _dot(wgv_ref[:, lo:lo + dh], vT_h.astype(BF16)))
            st["xc"].append(xc); st["ks"].append(ks_h); st["vT"].append(vT_h); st["state"].append(state)
        tail = xbuf_ref[bi, L:L + CONV_HALO, :]
        xbuf_ref[bi, 0:CONV_HALO, :] = tail
        st["gT"] = gT
        return st

    def gate_chain(st):
        gT = st["gT"]
        lf_hi, lf_lo = _split_bf16(_log_sigmoid(gT))
        cum = (_dot(lf_hi, tri_b) + _dot(lf_lo, tri_b))[B_HEADS:2 * B_HEADS]
        a = gT[0:B_HEADS] - cum
        st["cum"], st["a"] = cum, a
        pad = jnp.zeros((V7X_LANES - B_HEADS, L), F32)
        st["a_cols"] = jnp.concatenate([a, pad], axis=0).T

    def recur(bi, st):
        a, cum, a_cols = st["a"], st["cum"], st["a_cols"]
        for hd in range(B_HEADS):
            lo = hd * dh
            ks_h, state, inter = st["ks"][hd], st["state"][hd], st["inter"][hd]
            vT_aug = jnp.concatenate([st["vT"][hd], ones], axis=0)
            a_row = a[hd:hd + 1]
            a_col = a_cols[:, hd:hd + 1]
            cum_row = cum[hd:hd + 1]
            m_prev = m_ref[bi, hd][0:1, 0:1]
            amax = jnp.max(jnp.where(tri, a_col, NEG), axis=0, keepdims=True)
            b_row = jnp.maximum(m_prev, amax)
            dmat = jnp.exp(jnp.where(tri, a_col - b_row, NEG))
            w_intra = (st["scores"][hd] * dmat).astype(BF16)
            intra = _dot(vT_aug.astype(BF16), w_intra)
            w_inter = jnp.exp(m_prev - b_row)
            num = w_inter * inter[0:dh] + intra[0:dh]
            den = w_inter * inter[dh:dh + 1] + intra[dh:dh + 1]
            hc = num * (1.0 / jnp.maximum(jnp.abs(den), jnp.exp(-(cum_row + b_row))))
            mu = jnp.mean(hc, axis=0, keepdims=True)
            hcc = hc - mu
            var = jnp.mean(hcc * hcc, axis=0, keepdims=True)
            hb = (hcc * lax.rsqrt(var + 1e-5)).T * og_ref[:, lo:lo + dh]
            yb = (hb + skip_ref[:, lo:lo + dh] * st["xc"][hd]) * sbz_ref[bi, :, lo:lo + dh].astype(F32)
            o_ref[bi, :, lo:lo + dh] = yb.astype(BF16)
            b_end = b_row[:, L - 1:L]
            ws = jnp.exp(a_row - b_end)
            decay = jnp.exp(m_prev - b_end)
            state_ref[bi, hd] = decay * state + _dot((vT_aug * ws).astype(BF16), ks_h)
            m_ref[bi, hd] = jnp.broadcast_to(cum_row[:, L - 1:L] + b_end, m_ref.shape[2:])

    sts = []
    for bi in range(MLSTM_NB):
        sts.append(project(bi))
        gate_chain(sts[bi])
    for bi in range(MLSTM_NB):
        recur(bi, sts[bi])


def _block_diag_dense(w):
    nblk, blk, _ = w.shape
    n = nblk * blk
    idx = jnp.arange(n) // blk
    return jnp.where(idx[:, None] == idx[None, :], jnp.tile(w.reshape(n, blk), (1, nblk)), 0.0)


def _mlstm(bx, sbz, conv_w, conv_b, wq, wk, wv, w_gates, b_gates, out_g, skip):
    b, s, bw = bx.shape
    L = MLSTM_L
    nc = s // L
    nbb = MLSTM_NB
    assert b % nbb == 0 and s % L == 0
    per_head = wq.shape[0] // B_HEADS

    def head_blocks(w, transpose):
        blocks = [_block_diag_dense(w[h * per_head:(h + 1) * per_head]) for h in range(B_HEADS)]
        return jnp.stack([blk.T if transpose else blk for blk in blocks]).astype(BF16)

    wqt_d = head_blocks(wq, True)
    wk_d = head_blocks(wk, False)
    wvt_d = head_blocks(wv, True)
    hblk = (B_HEADS, B_HEAD_DIM, B_HEAD_DIM)
    wgq = w_gates[:bw].T.astype(BF16)
    wgv = w_gates[2 * bw:].T.astype(BF16)
    wgk = w_gates[bw:2 * bw].T.astype(BF16)
    const = lambda shape: pl.BlockSpec(shape, lambda i, j: (0,) * len(shape))
    tok = pl.BlockSpec((nbb, L, bw), lambda i, j: (i, j, 0))
    return pl.pallas_call(
        _mlstm_kernel,
        grid=(b // nbb, nc),
        in_specs=[tok, tok, const((B_CONV, bw)), const((1, bw)),
                  const(hblk), const(hblk), const(hblk),
                  const((GATE_ROWS, bw)), const((GATE_ROWS, bw)), const((GATE_ROWS, bw)),
                  const((GATE_ROWS, 1)), const((1, bw)), const((1, bw))],
        out_specs=tok,
        out_shape=jax.ShapeDtypeStruct((b, s, bw), BF16),
        scratch_shapes=[pltpu.VMEM((nbb, L + CONV_HALO, bw), F32),
                        pltpu.VMEM((nbb, B_HEADS, 2 * B_HEAD_DIM, B_HEAD_DIM), F32),
                        pltpu.VMEM((nbb, B_HEADS, V7X_SUBLANES, V7X_LANES), F32)],
        compiler_params=pltpu.CompilerParams(
            dimension_semantics=("parallel", "arbitrary"), vmem_limit_bytes=VMEM_LIMIT),
        name="mlstm",
    )(bx, sbz, conv_w, conv_b.reshape(1, bw), wqt_d, wk_d, wvt_d, wgq, wgk, wgv,
      b_gates.reshape(GATE_ROWS, 1), out_g.reshape(1, bw), skip.reshape(1, bw))


def _tail_kernel(x_ref, ya_ref, yb_ref, mod0_ref, mod1_ref, lng_ref, wo0_ref, wi1_ref, clg_ref,
                 clb_ref, ws_ref, bst_ref, wo1_ref, o_ref):
    tm = x_ref.shape[1]
    sub = TAIL_SUB_ROWS
    n_sub = tm // sub
    aw = A_WIDTH
    mod1 = mod1_ref[0]
    gate0 = mod0_ref[0][2:3]
    t_iota = lax.broadcasted_iota(jnp.int32, (C_CHUNK, C_CHUNK), 0)
    s_iota = lax.broadcasted_iota(jnp.int32, (C_CHUNK, C_CHUNK), 1)
    tril = s_iota <= t_iota
    gw = C_WIDTH // C_GROUPS
    bst = bst_ref[...]
    wms = [jnp.where(tril, ws_ref[g], 0.0).astype(BF16) for g in range(C_GROUPS)]

    def out_proj0(r):
        rows = slice(r * sub, (r + 1) * sub)
        y0 = _dot(ya_ref[0, rows, :], wo0_ref[0:aw, :]) + _dot(yb_ref[0, rows, :], wo0_ref[aw:, :])
        x1 = x_ref[0, rows, :] + gate0 * y0
        h = _adaln_rmsnorm(x1, lng_ref[...], mod1[1:2], mod1[0:1]).astype(BF16)
        return x1, h

    def mix_inputs(p):
        u = _gelu_tanh(p[:, :C_WIDTH])
        v = _gelu_tanh(p[:, C_WIDTH:2 * C_WIDTH])
        mu = jnp.mean(v, axis=-1, keepdims=True)
        vc = v - mu
        var = jnp.mean(vc * vc, axis=-1, keepdims=True)
        vn = ((vc * lax.rsqrt(var + 1e-5)) * clg_ref[...] + clb_ref[...]).astype(BF16)
        return vn, u * _silu(p[:, 2 * C_WIDTH:])

    def spatial_gate(vn, gate):
        cols = []
        for g in range(C_GROUPS):
            rows = []
            for ch in range(sub // C_CHUNK):
                vg = vn[ch * C_CHUNK:(ch + 1) * C_CHUNK, g * gw:(g + 1) * gw]
                rows.append(_dot(wms[g], vg) + bst[:, g:g + 1])
            cols.append(jnp.concatenate(rows, axis=0))
        return (gate * jnp.concatenate(cols, axis=1)).astype(BF16)

    def out_proj1(r, x1, y1):
        o_ref[0, r * sub:(r + 1) * sub, :] = x1 + mod1[2:3] * _dot(y1, wo1_ref[...])

    heads = [out_proj0(r) for r in range(n_sub)]
    projs = [_dot(heads[r][1], wi1_ref[...]) for r in range(n_sub)]
    y1_prev = None
    for r in range(n_sub):
        vn, gate = mix_inputs(projs[r])
        if y1_prev is not None:
            out_proj1(r - 1, heads[r - 1][0], y1_prev)
        y1_prev = spatial_gate(vn, gate)
    out_proj1(n_sub - 1, heads[n_sub - 1][0], y1_prev)


def _tail(x, ya, yb, mod0, mod1, ln_g1, w_out0, w_in1, w_out1, c_ln_g, c_ln_b, c_ws, c_bs):
    b, s, d = x.shape
    tm = TAIL_ROW_TILE
    nt = s // tm
    const = lambda shape: pl.BlockSpec(shape, lambda i, j: (0,) * len(shape),
                                       pipeline_mode=pl.Buffered(1))
    half = pl.BlockSpec((1, tm, A_WIDTH), lambda i, j: (i, j, 0))
    full = pl.BlockSpec((1, tm, d), lambda i, j: (i, j, 0))
    modspec = pl.BlockSpec((1, 3, d), lambda i, j: (i, 0, 0))
    return pl.pallas_call(
        _tail_kernel,
        grid=(b, nt),
        in_specs=[full, half, half, modspec, modspec, const((1, d)), const((d, d)),
                  const((d, 3 * C_WIDTH)), const((1, d)), const((1, d)),
                  const((C_GROUPS, C_CHUNK, C_CHUNK)), const((C_CHUNK, C_GROUPS)), const((d, d))],
        out_specs=full,
        out_shape=jax.ShapeDtypeStruct((b, s, d), F32),
        compiler_params=pltpu.CompilerParams(
            dimension_semantics=("parallel", "parallel"), vmem_limit_bytes=VMEM_LIMIT),
        name="tail",
    )(x, ya, yb, mod0, mod1, ln_g1.reshape(1, d), w_out0.astype(BF16), w_in1.astype(BF16),
      c_ln_g.reshape(1, d), c_ln_b.reshape(1, d), c_ws, c_bs.T, w_out1.astype(BF16))


def kernel(x, c, ln_g, ada_w, ada_b, w_in, w_out, a_q_g, a_k_g, b_conv_w, b_conv_b, b_wq, b_wk, b_wv,
           b_w_gates, b_b_gates, b_out_g, b_skip, c_ln_g, c_ln_b, c_ws, c_bs):
    mods = _ada_mods(c, ada_w, ada_b)
    qT, vT, gT, k, bx, sbz = _inproj0(x, mods[0], ln_g[0], w_in[0], a_q_g[0], a_k_g[0])
    ya = _moba(qT, k, vT, gT)
    yb = _mlstm(bx, sbz, b_conv_w[0], b_conv_b[0], b_wq[0], b_wk[0], b_wv[0], b_w_gates[0],
                b_b_gates[0], b_out_g[0], b_skip[0])
    return _tail(x, ya, yb, mods[0], mods[1], ln_g[1], w_out[0], w_in[1], w_out[1],
                 c_ln_g[0], c_ln_b[0], c_ws[0], c_bs[0])
```

```python
import functools

import jax
import jax.numpy as jnp
from jax import lax
from jax.experimental import pallas as pl
from jax.experimental.pallas import tpu as pltpu

F32 = jnp.float32
BF16 = jnp.bfloat16

D_MODEL = 1024
A_HEADS = 8
A_HEAD_DIM = 64
A_WIDTH = A_HEADS * A_HEAD_DIM
MOBA_BLOCK = 256
MOBA_TOPK = 3
B_HEADS = 4
B_HEAD_DIM = 128
B_WIDTH = B_HEADS * B_HEAD_DIM
B_CONV = 4
C_GROUPS = 8
C_CHUNK = 128
C_WIDTH = D_MODEL
NEG = -1e30
LOG2E = 1.4426950408889634

V7X_SUBLANES = 8
V7X_LANES = 128

MLSTM_L = 256
MLSTM_NB = 4
INPROJ_SUBTILES = 4
TAIL_ROW_TILE = 1024
TAIL_SUB_ROWS = 512
CONV_HALO = 8
GATE_ROWS = 2 * B_HEADS
PV_ONES_ROWS = 16
MOBA_NB = 2
VMEM_LIMIT = 48 * 1024 * 1024
MOBA_VMEM_LIMIT = 56 * 1024 * 1024


def _silu(x):
    return x * jax.nn.sigmoid(x)


def _gelu_tanh(x):
    return 0.5 * x * (1.0 + jnp.tanh(0.7978845608028654 * (x + 0.044715 * (x * x * x))))


def _log_sigmoid(x):
    return jnp.minimum(x, 0.0) - jnp.log(1.0 + jnp.exp(-jnp.abs(x)))


def _split_bf16(x):
    hi = x.astype(BF16)
    lo = (x - hi.astype(F32)).astype(BF16)
    return hi, lo


def _dot(a, b):
    return jnp.dot(a, b, preferred_element_type=F32)


def _dot_nt(a, b):
    return lax.dot_general(a, b, (((1,), (1,)), ((), ())), preferred_element_type=F32)


def _adaln_rmsnorm(x, ln_g, scale, shift):
    y = x * lax.rsqrt(jnp.mean(x * x, axis=-1, keepdims=True) + 1e-6)
    return (y * ln_g) * (1.0 + scale) + shift


def _ada_kernel(c_ref, w_ref, b_ref, o_ref):
    cs_hi, cs_lo = _split_bf16(_silu(c_ref[...]))
    w_hi, w_lo = _split_bf16(w_ref[0])
    o_ref[0] = _dot(cs_hi, w_hi) + _dot(cs_lo, w_hi) + _dot(cs_hi, w_lo) + b_ref[0]


def _ada_mods(c, ada_w, ada_b):
    depth, d, d3 = ada_w.shape
    b = c.shape[0]
    bp = V7X_SUBLANES * pl.cdiv(b, V7X_SUBLANES)
    cp =jnp.zeros((bp, d), F32).at[:b].set(c)
    nt = d3 // d
    out = pl.pallas_call(
        _ada_kernel,
        grid=(depth, nt),
        in_specs=[
            pl.BlockSpec((bp, d), lambda l, n: (0, 0)),
            pl.BlockSpec((1, d, d), lambda l, n: (l, 0, n)),
            pl.BlockSpec((1, 1, d), lambda l, n: (l, 0, n)),
        ],
        out_specs=pl.BlockSpec((1, bp, d), lambda l, n: (l, 0, n)),
        out_shape=jax.ShapeDtypeStruct((depth, bp, d3), F32),
        name="ada_mods",
    )(cp, ada_w, ada_b.reshape(depth, 1, d3))
    return out[:, :b].reshape(depth, b, 3, d)


def _inproj0_kernel(x_ref, mod_ref, lng_ref, wt_ref, w_ref, qg_ref, kg_ref,
                    qT_ref, vT_ref, gT_ref, k_ref, bx_ref, sbz_ref):
    tm = MOBA_BLOCK
    aw = A_WIDTH
    mod = mod_ref[0]
    qg = qg_ref[...] * (A_HEAD_DIM ** -0.5 * LOG2E)
    kg = kg_ref[...]
    hs = [_adaln_rmsnorm(x_ref[0, r * tm:(r + 1) * tm], lng_ref[...], mod[1:2], mod[0:1]).astype(BF16)
          for r in range(INPROJ_SUBTILES)]
    for r in range(INPROJ_SUBTILES):
        h = hs[r]
        rows = slice(r * tm, (r + 1) * tm)
        pt = _dot_nt(wt_ref[...], h)
        p = _dot(h, w_ref[...])
        kn = []
        for hd in range(A_HEADS):
            lo = hd * A_HEAD_DIM
            q = pt[lo:lo + A_HEAD_DIM]
            rq = lax.rsqrt(jnp.mean(q * q, axis=0, keepdims=True) + 1e-6)
            qT_ref[0, r, lo:lo + A_HEAD_DIM, :] = ((q * rq) * qg).astype(BF16)
            k = pt[aw + lo:aw + lo + A_HEAD_DIM]
            rk = lax.rsqrt(jnp.mean(k * k, axis=0, keepdims=True) + 1e-6)
            kn.append((k * rk) * kg)
        k_ref[0, rows, :] = jnp.concatenate(kn, axis=0).T.astype(BF16)
        vT_ref[0, r] = pt[2 * aw:3 * aw].astype(BF16)
        gT_ref[0, r] = _silu(pt[3 * aw:]).astype(BF16)
        bx_ref[0, rows, :] = p[:, :B_WIDTH].astype(BF16)
        sbz_ref[0, rows, :] = _silu(p[:, B_WIDTH:]).astype(BF16)


def _inproj0(x, mod, ln_g, w_in, q_g, k_g):
    b, s, d = x.shape
    blk = MOBA_BLOCK
    sub = INPROJ_SUBTILES
    tm = sub * blk
    nt = s // tm
    aw, bw = A_WIDTH, B_WIDTH
    assert w_in.shape == (d, 4 * aw + 2 * bw)
    wt = w_in[:, :4 * aw].T.astype(BF16)
    wn = w_in[:, 4 * aw:].astype(BF16)
    t_shape = jax.ShapeDtypeStruct((b, s // blk, aw, blk), BF16)
    n_shape = jax.ShapeDtypeStruct((b, s, aw), BF16)
    t_spec = pl.BlockSpec((1, sub, aw, blk), lambda i, j: (i, j, 0, 0))
    n_spec = pl.BlockSpec((1, tm, aw), lambda i, j: (i, j, 0))
    const = lambda shape: pl.BlockSpec(shape, lambda i, j: (0,) * len(shape),
                                       pipeline_mode=pl.Buffered(1))
    return pl.pallas_call(
        _inproj0_kernel,
        grid=(b, nt),
        in_specs=[
            pl.BlockSpec((1, tm, d), lambda i, j: (i, j, 0)),
            pl.BlockSpec((1, 3, d), lambda i, j: (i, 0, 0)),
            const((1, d)),
            const((4 * aw, d)),
            const((d, 2 * bw)),
            const((A_HEAD_DIM, 1)),
            const((A_HEAD_DIM, 1)),
        ],
        out_specs=[t_spec, t_spec, t_spec, n_spec, n_spec, n_spec],
        out_shape=[t_shape, t_shape, t_shape, n_shape, n_shape, n_shape],
        compiler_params=pltpu.CompilerParams(
            dimension_semantics=("parallel", "parallel"), vmem_limit_bytes=VMEM_LIMIT),
        name="inproj0",
    )(x, mod, ln_g.reshape(1, d), wt, wn, q_g.reshape(A_HEAD_DIM, 1), k_g.reshape(A_HEAD_DIM, 1))


def _moba_kernel(qT_ref, k_ref, vT_ref, gT_ref, o_ref, kmean_ref, sel_ref, qm_ref, raw_ref, p_ref,
                 m_ref, acc_ref, *, nb):
    blk = MOBA_BLOCK
    dh = A_HEAD_DIM
    pw = 2 * dh
    streams = range(MOBA_NB)
    j = pl.program_id(1)

    @pl.when(j == 0)
    def _():
        for z in streams:
            for n in range(nb):
                kb = k_ref[z, n * blk:(n + 1) * blk, :].astype(F32)
                kmean_ref[z, n:n + 1, :] = jnp.mean(kb, axis=0, keepdims=True)

    d_iota = lax.broadcasted_iota(jnp.int32, (pw, blk), 0)

    def value_dots(z, n, slot):
        ones = jnp.ones((PV_ONES_ROWS, blk), BF16)
        return [_dot(jnp.concatenate([vT_ref[z, n, hd * dh:(hd + 1) * dh, :], ones], axis=0),
                     p_ref[z, slot, hd]) for hd in range(A_HEADS)]

    sel_scores = []
    own0 = pl.multiple_of(j * blk, blk)
    k_iota = lax.broadcasted_iota(jnp.int32, (blk, blk), 0)
    q_iota = lax.broadcasted_iota(jnp.int32, (blk, blk), 1)
    causal_bias = jnp.where(k_iota <= q_iota, 0.0, NEG).astype(BF16)
    for z in streams:
        km_hi, km_lo = _split_bf16(kmean_ref[z])
        km2 = jnp.concatenate([km_hi, km_lo], axis=0)
        per_head = []
        for hd in range(A_HEADS):
            pr = hd // 2
            lanes = slice(pr * pw, (pr + 1) * pw)
            q2 = qT_ref[z, 0, lanes, :]
            keep = (d_iota < dh) if hd % 2 == 0 else (d_iota >= dh)
            qm = jnp.where(keep, q2, jnp.zeros_like(q2))
            qm_ref[z, hd] = qm
            r2 = _dot(km2[:, lanes], qm)
            per_head.append(r2[0:nb] + r2[nb:2 * nb])
            own = _dot(k_ref[z, pl.ds(own0, blk), lanes], qm).astype(BF16)
            raw_ref[z, 1, hd] = _dot(k_ref[z, 0:blk, lanes], qm).astype(BF16)
            s = own + causal_bias
            m_new = jnp.max(s, axis=0, keepdims=True)
            p_ref[z, 0, hd] = jnp.exp2(s - m_new)
            m_ref[z, hd] = m_new.astype(F32)
            acc_ref[z, hd] = jnp.zeros(acc_ref.shape[2:], F32)
        sel_scores.append(per_head)

    n_iota = lax.broadcasted_iota(jnp.int32, (nb, blk), 0)
    past = n_iota < j
    for z in streams:
        for hd in range(A_HEADS):
            sc = jnp.where(past, sel_scores[z][hd], NEG)
            sel = jnp.zeros((nb, blk), F32)
            for _ in range(MOBA_TOPK):
                mx = jnp.max(sc, axis=0, keepdims=True)
                first = jnp.min(jnp.where(sc == mx, n_iota, nb), axis=0, keepdims=True)
                pick = n_iota == first
                sel = jnp.where(pick, 1.0, sel)
                sc = jnp.where(pick, -jnp.inf, sc)
            sel_ref[z, hd, 0:nb, :] = jnp.where(past, sel, 0.0)
            sel_ref[z, hd, nb:nb + 1, :] = jnp.ones((1, blk), F32)

    def step(z, i, cur):
        nxt = 1 - cur
        n = i - 1
        prev = jnp.where(i == 1, j, i - 2)
        prev_sel = jnp.where(i == 1, nb, i - 2)
        row0 = pl.multiple_of(jnp.minimum(i, nb - 1) * blk, blk)
        ones = jnp.ones((PV_ONES_ROWS, blk), BF16)
        pvs, alphas = [], []
        for hd in range(A_HEADS):
            pr = hd // 2
            kt = k_ref[z, pl.ds(row0, blk), pr * pw:(pr + 1) * pw]
            raw_ref[z, nxt, hd] = _dot(kt, qm_ref[z, hd]).astype(BF16)
            pvs.append(_dot(jnp.concatenate([vT_ref[z, prev, hd * dh:(hd + 1) * dh, :], ones], axis=0),
                            p_ref[z, nxt, hd]))
            s = raw_ref[z, cur, hd]
            selb = sel_ref[z, hd, pl.ds(n, 1), :] > 0.5
            smax = jnp.where(selb, jnp.max(s, axis=0, keepdims=True).astype(F32), NEG)
            m_old = m_ref[z, hd]
            m_new = jnp.maximum(m_old, smax)
            alphas.append(jnp.exp2(m_old - m_new))
            m_ref[z, hd] = m_new
            p_ref[z, cur, hd] = jnp.exp2(s - m_new.astype(BF16))
        for hd in range(A_HEADS):
            keep = sel_ref[z, hd, pl.ds(prev_sel, 1), :] > 0.5
            acc_ref[z, hd] = alphas[hd] * (acc_ref[z, hd] + jnp.where(keep, pvs[hd], 0.0))

    def body(ii, carry):
        for z in streams:
            step(z, 2 * ii + 1, 1)
        for z in streams:
            step(z, 2 * ii + 2, 0)
        return carry

    trips = (j + 1) // 2
    lax.fori_loop(0, trips, body, 0)

    last = jnp.where(j == 0, j, 2 * trips - 1)
    last_sel = jnp.where(j == 0, nb, 2 * trips - 1)
    last_pvs = [value_dots(z, last, 0) for z in streams]
    for z in streams:
        for pr in range(A_HEADS // 2):
            halves = []
            for hd in (2 * pr, 2 * pr + 1):
                keep = sel_ref[z, hd, pl.ds(last_sel, 1), :] > 0.5
                acc = acc_ref[z, hd] + jnp.where(keep, last_pvs[z][hd], 0.0)
                halves.append(acc[0:dh] / acc[dh:dh + 1])
            oT = jnp.concatenate(halves, axis=0)
            oT = oT * gT_ref[z, 0, pr * pw:(pr + 1) * pw, :].astype(F32)
            o_ref[z, :, pr * pw:(pr + 1) * pw] = oT.T.astype(BF16)


def _moba(qT, k, vT, gT):
    b, nb, aw, blk = qT.shape
    s = nb * blk
    pw = 2 * A_HEAD_DIM
    z = MOBA_NB
    assert b % z == 0
    q_spec = pl.BlockSpec((z, 1, aw, blk), lambda i, j: (i, j, 0, 0))
    row = lambda: pltpu.VMEM((z, A_HEADS, 1, blk), F32)
    return pl.pallas_call(
        functools.partial(_moba_kernel, nb=nb),
        grid=(b // z, nb),
        in_specs=[
            q_spec,
            pl.BlockSpec((z, s, aw), lambda i, j: (i, 0, 0)),
            pl.BlockSpec((z, nb, aw, blk), lambda i, j: (i, 0, 0, 0)),
            q_spec,
        ],
        out_specs=pl.BlockSpec((z, blk, aw), lambda i, j: (i, j, 0)),
        out_shape=jax.ShapeDtypeStruct((b, s, aw), BF16),
        scratch_shapes=[
            pltpu.VMEM((z, nb, aw), F32),
            pltpu.VMEM((z, A_HEADS, nb + V7X_SUBLANES, blk), F32),
            pltpu.VMEM((z, A_HEADS, pw, blk), BF16),
            pltpu.VMEM((z, 2, A_HEADS, blk, blk), BF16),
            pltpu.VMEM((z, 2, A_HEADS, blk, blk), BF16),
            row(),
            pltpu.VMEM((z, A_HEADS, A_HEAD_DIM + PV_ONES_ROWS, blk), F32),
        ],
        compiler_params=pltpu.CompilerParams(
            dimension_semantics=("parallel", "arbitrary"), vmem_limit_bytes=MOBA_VMEM_LIMIT),
        name="moba",
    )(qT, k, vT, gT)


def _mlstm_kernel(bx_ref, sbz_ref, cw_ref, cb_ref, wqt_ref, wk_ref, wvt_ref, wgq_ref, wgk_ref,
                  wgv_ref, bg_ref, og_ref, skip_ref, o_ref, xbuf_ref, state_ref, m_ref):
    L = MLSTM_L
    dh = B_HEAD_DIM
    c = pl.program_id(1)

    @pl.when(c == 0)
    def _():
        xbuf_ref[:, 0:CONV_HALO, :] = jnp.zeros((MLSTM_NB, CONV_HALO, B_WIDTH), F32)
        state_ref[...] = jnp.zeros_like(state_ref)
        m_ref[...] = jnp.zeros_like(m_ref)

    cw = cw_ref[...]
    cb = cb_ref[...]
    kscale = dh ** -0.5
    s_iota = lax.broadcasted_iota(jnp.int32, (L, L), 0)
    t_iota = lax.broadcasted_iota(jnp.int32, (L, L), 1)
    tri = s_iota <= t_iota
    tri_b = jnp.where(tri, 1.0, 0.0).astype(BF16)
    ones = jnp.ones((dh, L), F32)

    def project(bi):
        bx_b = bx_ref[bi]
        xbuf_ref[bi, CONV_HALO:CONV_HALO + L, :] = bx_b.astype(F32)
        st = dict(xc=[], ks=[], vT=[], scores=[], inter=[], state=[])
        gT = bg_ref[...]
        for hd in range(B_HEADS):
            lo = hd * dh
            conv = cb[:, lo:lo + dh]
            for i in range(B_CONV):
                off = CONV_HALO - (B_CONV - 1) + i
                conv = conv + cw[i:i + 1, lo:lo + dh] * xbuf_ref[bi, off:off + L, lo:lo + dh]
            xc = _silu(conv)
            xc_b = xc.astype(BF16)
            qT_h = _dot_nt(wqt_ref[hd], xc_b).astype(BF16)
            vT_h = _dot_nt(wvt_ref[hd], bx_b[:, lo:lo + dh])
            k_h = _dot(xc_b, wk_ref[hd])
            ks_h = (k_h * kscale).astype(BF16)
            state = state_ref[bi, hd]
            st["scores"].append(_dot(ks_h, qT_h))
            st["inter"].append(_dot(state.astype(BF16), qT_h))
            gT = gT + (_dot(wgq_ref[:, lo:lo + dh], qT_h)
                       + _dot_nt(wgk_ref[:, lo:lo + dh], k_h.astype(BF16))
                       + _dot(wgv_ref[:, lo:lo + dh], vT_h.astype(BF16)))
            st["xc"].append(xc); st["ks"].append(ks_h); st["vT"].append(vT_h); st["state"].append(state)
        tail = xbuf_ref[bi, L:L + CONV_HALO, :]
        xbuf_ref[bi, 0:CONV_HALO, :] = tail
        st["gT"] = gT
        return st

    def gate_chain(st):
        gT = st["gT"]
        lf_hi, lf_lo = _split_bf16(_log_sigmoid(gT))
        cum = (_dot(lf_hi, tri_b) + _dot(lf_lo, tri_b))[B_HEADS:2 * B_HEADS]
        a = gT[0:B_HEADS] - cum
        st["cum"], st["a"] = cum, a
        pad = jnp.zeros((V7X_LANES - B_HEADS, L), F32)
        st["a_cols"] = jnp.concatenate([a, pad], axis=0).T

    def recur(bi, st):
        a, cum, a_cols = st["a"], st["cum"], st["a_cols"]
        for hd in range(B_HEADS):
            lo = hd * dh
            ks_h, state, inter = st["ks"][hd], st["state"][hd], st["inter"][hd]
            vT_aug = jnp.concatenate([st["vT"][hd], ones], axis=0)
            a_row = a[hd:hd + 1]
            a_col = a_cols[:, hd:hd + 1]
            cum_row = cum[hd:hd + 1]
            m_prev = m_ref[bi, hd][0:1, 0:1]
            amax = jnp.max(jnp.where(tri, a_col, NEG), axis=0, keepdims=True)
            b_row = jnp.maximum(m_prev, amax)
            dmat = jnp.exp(jnp.where(tri, a_col - b_row, NEG))
            w_intra = (st["scores"][hd] * dmat).astype(BF16)
            intra = _dot(vT_aug.astype(BF16), w_intra)
            w_inter = jnp.exp(m_prev - b_row)
            num = w_inter * inter[0:dh] + intra[0:dh]
            den = w_inter * inter[dh:dh + 1] + intra[dh:dh + 1]
            hc = num * (1.0 / jnp.maximum(jnp.abs(den), jnp.exp(-(cum_row + b_row))))
            mu = jnp.mean(hc, axis=0, keepdims=True)
            hcc = hc - mu
            var = jnp.mean(hcc * hcc, axis=0, keepdims=True)
            hb = (hcc * lax.rsqrt(var + 1e-5)).T * og_ref[:, lo:lo + dh]
            yb = (hb + skip_ref[:, lo:lo + dh] * st["xc"][hd]) * sbz_ref[bi, :, lo:lo + dh].astype(F32)
            o_ref[bi, :, lo:lo + dh] = yb.astype(BF16)
            b_end = b_row[:, L - 1:L]
            ws = jnp.exp(a_row - b_end)
            decay = jnp.exp(m_prev - b_end)
            state_ref[bi, hd] = decay * state + _dot((vT_aug * ws).astype(BF16), ks_h)
            m_ref[bi, hd] = jnp.broadcast_to(cum_row[:, L - 1:L] + b_end, m_ref.shape[2:])

    sts = []
    for bi in range(MLSTM_NB):
        sts.append(project(bi))
        gate_chain(sts[bi])
    for bi in range(MLSTM_NB):
        recur(bi, sts[bi])


def _block_diag_dense(w):
    nblk, blk, _ = w.shape
    n = nblk * blk
    idx = jnp.arange(n) // blk
    return jnp.where(idx[:, None] == idx[None, :], jnp.tile(w.reshape(n, blk), (1, nblk)), 0.0)


def _mlstm(bx, sbz, conv_w, conv_b, wq, wk, wv, w_gates, b_gates, out_g, skip):
    b, s, bw = bx.shape
    L = MLSTM_L
    nc = s // L
    nbb = MLSTM_NB
    assert b % nbb == 0 and s % L == 0
    per_head = wq.shape[0] // B_HEADS

    def head_blocks(w, transpose):
        blocks = [_block_diag_dense(w[h * per_head:(h + 1) * per_head]) for h in range(B_HEADS)]
        return jnp.stack([blk.T if transpose else blk for blk in blocks]).astype(BF16)

    wqt_d = head_blocks(wq, True)
    wk_d = head_blocks(wk, False)
    wvt_d = head_blocks(wv, True)
    hblk = (B_HEADS, B_HEAD_DIM, B_HEAD_DIM)
    wgq = w_gates[:bw].T.astype(BF16)
    wgv = w_gates[2 * bw:].T.astype(BF16)
    wgk = w_gates[bw:2 * bw].T.astype(BF16)
    const = lambda shape: pl.BlockSpec(shape, lambda i, j: (0,) * len(shape))
    tok = pl.BlockSpec((nbb, L, bw), lambda i, j: (i, j, 0))
    return pl.pallas_call(
        _mlstm_kernel,
        grid=(b // nbb, nc),
        in_specs=[tok, tok, const((B_CONV, bw)), const((1, bw)),
                  const(hblk), const(hblk), const(hblk),
                  const((GATE_ROWS, bw)), const((GATE_ROWS, bw)), const((GATE_ROWS, bw)),
                  const((GATE_ROWS, 1)), const((1, bw)), const((1, bw))],
        out_specs=tok,
        out_shape=jax.ShapeDtypeStruct((b, s, bw), BF16),
        scratch_shapes=[pltpu.VMEM((nbb, L + CONV_HALO, bw), F32),
                        pltpu.VMEM((nbb, B_HEADS, 2 * B_HEAD_DIM, B_HEAD_DIM), F32),
                        pltpu.VMEM((nbb, B_HEADS, V7X_SUBLANES, V7X_LANES), F32)],
        compiler_params=pltpu.CompilerParams(
            dimension_semantics=("parallel", "arbitrary"), vmem_limit_bytes=VMEM_LIMIT),
        name="mlstm",
    )(bx, sbz, conv_w, conv_b.reshape(1, bw), wqt_d, wk_d, wvt_d, wgq, wgk, wgv,
      b_gates.reshape(GATE_ROWS, 1), out_g.reshape(1, bw), skip.reshape(1, bw))


def _tail_kernel(x_ref, ya_ref, yb_ref, mod0_ref, mod1_ref, lng_ref, wo0_ref, wi1_ref, clg_ref,
                 clb_ref, ws_ref, bst_ref, wo1_ref, o_ref):
    tm = x_ref.shape[1]
    sub = TAIL_SUB_ROWS
    n_sub = tm // sub
    aw = A_WIDTH
    mod1 = mod1_ref[0]
    gate0 = mod0_ref[0][2:3]
    t_iota = lax.broadcasted_iota(jnp.int32, (C_CHUNK, C_CHUNK), 0)
    s_iota = lax.broadcasted_iota(jnp.int32, (C_CHUNK, C_CHUNK), 1)
    tril = s_iota <= t_iota
    gw = C_WIDTH // C_GROUPS
    bst = bst_ref[...]
    wms = [jnp.where(tril, ws_ref[g], 0.0).astype(BF16) for g in range(C_GROUPS)]

    def out_proj0(r):
        rows = slice(r * sub, (r + 1) * sub)
        y0 = _dot(ya_ref[0, rows, :], wo0_ref[0:aw, :]) + _dot(yb_ref[0, rows, :], wo0_ref[aw:, :])
        x1 = x_ref[0, rows, :] + gate0 * y0
        h = _adaln_rmsnorm(x1, lng_ref[...], mod1[1:2], mod1[0:1]).astype(BF16)
        return x1, h

    def mix_inputs(p):
        u = _gelu_tanh(p[:, :C_WIDTH])
        v = _gelu_tanh(p[:, C_WIDTH:2 * C_WIDTH])
        mu = jnp.mean(v, axis=-1, keepdims=True)
        vc = v - mu
        var = jnp.mean(vc * vc, axis=-1, keepdims=True)
        vn = ((vc * lax.rsqrt(var + 1e-5)) * clg_ref[...] + clb_ref[...]).astype(BF16)
        return vn, u * _silu(p[:, 2 * C_WIDTH:])

    def spatial_gate(vn, gate):
        cols = []
        for g in range(C_GROUPS):
            rows = []
            for ch in range(sub // C_CHUNK):
                vg = vn[ch * C_CHUNK:(ch + 1) * C_CHUNK, g * gw:(g + 1) * gw]
                rows.append(_dot(wms[g], vg) + bst[:, g:g + 1])
            cols.append(jnp.concatenate(rows, axis=0))
        return (gate * jnp.concatenate(cols, axis=1)).astype(BF16)

    def out_proj1(r, x1, y1):
        o_ref[0, r * sub:(r + 1) * sub, :] = x1 + mod1[2:3] * _dot(y1, wo1_ref[...])

    heads = [out_proj0(r) for r in range(n_sub)]
    projs = [_dot(heads[r][1], wi1_ref[...]) for r in range(n_sub)]
    y1_prev = None
    for r in range(n_sub):
        vn, gate = mix_inputs(projs[r])
        if y1_prev is not None:
            out_proj1(r - 1, heads[r - 1][0], y1_prev)
        y1_prev = spatial_gate(vn, gate)
    out_proj1(n_sub - 1, heads[n_sub - 1][0], y1_prev)


def _tail(x, ya, yb, mod0, mod1, ln_g1, w_out0, w_in1, w_out1, c_ln_g, c_ln_b, c_ws, c_bs):
    b, s, d = x.shape
    tm = TAIL_ROW_TILE
    nt = s // tm
    const = lambda shape: pl.BlockSpec(shape, lambda i, j: (0,) * len(shape),
                                       pipeline_mode=pl.Buffered(1))
    half = pl.BlockSpec((1, tm, A_WIDTH), lambda i, j: (i, j, 0))
    full = pl.BlockSpec((1, tm, d), lambda i, j: (i, j, 0))
    modspec = pl.BlockSpec((1, 3, d), lambda i, j: (i, 0, 0))
    return pl.pallas_call(
        _tail_kernel,
        grid=(b, nt),
        in_specs=[full, half, half, modspec, modspec, const((1, d)), const((d, d)),
                  const((d, 3 * C_WIDTH)), const((1, d)), const((1, d)),
                  const((C_GROUPS, C_CHUNK, C_CHUNK)), const((C_CHUNK, C_GROUPS)), const((d, d))],
        out_specs=full,
        out_shape=jax.ShapeDtypeStruct((b, s, d), F32),
        compiler_params=pltpu.CompilerParams(
            dimension_semantics=("parallel", "parallel"), vmem_limit_bytes=VMEM_LIMIT),
        name="tail",
    )(x, ya, yb, mod0, mod1, ln_g1.reshape(1, d), w_out0.astype(BF16), w_in1.astype(BF16),
      c_ln_g.reshape(1, d), c_ln_b.reshape(1, d), c_ws, c_bs.T, w_out1.astype(BF16))


def kernel(x, c, ln_g, ada_w, ada_b, w_in, w_out, a_q_g, a_k_g, b_conv_w, b_conv_b, b_wq, b_wk, b_wv,
           b_w_gates, b_b_gates, b_out_g, b_skip, c_ln_g, c_ln_b, c_ws, c_bs):
    mods = _ada_mods(c, ada_w, ada_b)
    qT, vT, gT, k, bx, sbz = _inproj0(x, mods[0], ln_g[0], w_in[0], a_q_g[0], a_k_g[0])
    ya = _moba(qT, k, vT, gT)
    yb = _mlstm(bx, sbz, b_conv_w[0], b_conv_b[0], b_wq[0], b_wk[0], b_wv[0], b_w_gates[0],
                b_b_gates[0], b_out_g[0], b_skip[0])
    return _tail(x, ya, yb, mods[0], mods[1], ln_g[1], w_out[0], w_in[1], w_out[1],
                 c_ln_g[0], c_ln_b[0], c_ws[0], c_bs[0])
```

```python
import functools

import jax
import jax.numpy as jnp
from jax import lax
from jax.experimental import pallas as pl
from jax.experimental.pallas import tpu as pltpu

F32 = jnp.float32
BF16 = jnp.bfloat16

D_MODEL = 1024
A_HEADS = 8
A_HEAD_DIM = 64
A_WIDTH = A_HEADS * A_HEAD_DIM
MOBA_BLOCK = 256
MOBA_TOPK = 3
B_HEADS = 4
B_HEAD_DIM = 128
B_WIDTH = B_HEADS * B_HEAD_DIM
B_CONV = 4
C_GROUPS = 8
C_CHUNK = 128
C_WIDTH = D_MODEL
NEG = -1e30
LOG2E = 1.4426950408889634

V7X_SUBLANES = 8
V7X_LANES = 128

MLSTM_L = 256
MLSTM_NB = 4
INPROJ_SUBTILES = 4
TAIL_ROW_TILE = 1024
TAIL_SUB_ROWS = 512
CONV_HALO = 8
GATE_ROWS = 2 * B_HEADS
PV_ONES_ROWS = 16
MOBA_NB = 2
VMEM_LIMIT = 48 * 1024 * 1024
MOBA_VMEM_LIMIT = 56 * 1024 * 1024


def _silu(x):
    return x * jax.nn.sigmoid(x)


def _gelu_tanh(x):
    return 0.5 * x * (1.0 + jnp.tanh(0.7978845608028654 * (x + 0.044715 * (x * x * x))))


def _log_sigmoid(x):
    return jnp.minimum(x, 0.0) - jnp.log(1.0 + jnp.exp(-jnp.abs(x)))


def _split_bf16(x):
    hi = x.astype(BF16)
    lo = (x - hi.astype(F32)).astype(BF16)
    return hi, lo


def _dot(a, b):
    return jnp.dot(a, b, preferred_element_type=F32)


def _dot_nt(a, b):
    return lax.dot_general(a, b, (((1,), (1,)), ((), ())), preferred_element_type=F32)


def _adaln_rmsnorm(x, ln_g, scale, shift):
    y = x * lax.rsqrt(jnp.mean(x * x, axis=-1, keepdims=True) + 1e-6)
    return (y * ln_g) * (1.0 + scale) + shift


def _ada_kernel(c_ref, w_ref, b_ref, o_ref):
    cs_hi, cs_lo = _split_bf16(_silu(c_ref[...]))
    w_hi, w_lo = _split_bf16(w_ref[0])
    o_ref[0] = _dot(cs_hi, w_hi) + _dot(cs_lo, w_hi) + _dot(cs_hi, w_lo) + b_ref[0]


def _ada_mods(c, ada_w, ada_b):
    depth, d, d3 = ada_w.shape
    b = c.shape[0]
    bp = V7X_SUBLANES * pl.cdiv(b, V7X_SUBLANES)
    cp =jnp.zeros((bp, d), F32).at[:b].set(c)
    nt = d3 // d
    out = pl.pallas_call(
        _ada_kernel,
        grid=(depth, nt),
        in_specs=[
            pl.BlockSpec((bp, d), lambda l, n: (0, 0)),
            pl.BlockSpec((1, d, d), lambda l, n: (l, 0, n)),
            pl.BlockSpec((1, 1, d), lambda l, n: (l, 0, n)),
        ],
        out_specs=pl.BlockSpec((1, bp, d), lambda l, n: (l, 0, n)),
        out_shape=jax.ShapeDtypeStruct((depth, bp, d3), F32),
        name="ada_mods",
    )(cp, ada_w, ada_b.reshape(depth, 1, d3))
    return out[:, :b].reshape(depth, b, 3, d)


def _inproj0_kernel(x_ref, mod_ref, lng_ref, wt_ref, w_ref, qg_ref, kg_ref,
                    qT_ref, vT_ref, gT_ref, k_ref, bx_ref, sbz_ref):
    tm = MOBA_BLOCK
    aw = A_WIDTH
    mod = mod_ref[0]
    qg = qg_ref[...] * (A_HEAD_DIM ** -0.5 * LOG2E)
    kg = kg_ref[...]
    hs = [_adaln_rmsnorm(x_ref[0, r * tm:(r + 1) * tm], lng_ref[...], mod[1:2], mod[0:1]).astype(BF16)
          for r in range(INPROJ_SUBTILES)]
    for r in range(INPROJ_SUBTILES):
        h = hs[r]
        rows = slice(r * tm, (r + 1) * tm)
        pt = _dot_nt(wt_ref[...], h)
        p = _dot(h, w_ref[...])
        kn = []
        for hd in range(A_HEADS):
            lo = hd * A_HEAD_DIM
            q = pt[lo:lo + A_HEAD_DIM]
            rq = lax.rsqrt(jnp.mean(q * q, axis=0, keepdims=True) + 1e-6)
            qT_ref[0, r, lo:lo + A_HEAD_DIM, :] = ((q * rq) * qg).astype(BF16)
            k = pt[aw + lo:aw + lo + A_HEAD_DIM]
            rk = lax.rsqrt(jnp.mean(k * k, axis=0, keepdims=True) + 1e-6)
            kn.append((k * rk) * kg)
        k_ref[0, rows, :] = jnp.concatenate(kn, axis=0).T.astype(BF16)
        vT_ref[0, r] = pt[2 * aw:3 * aw].astype(BF16)
        gT_ref[0, r] = _silu(pt[3 * aw:]).astype(BF16)
        bx_ref[0, rows, :] = p[:, :B_WIDTH].astype(BF16)
        sbz_ref[0, rows, :] = _silu(p[:, B_WIDTH:]).astype(BF16)


def _inproj0(x, mod, ln_g, w_in, q_g, k_g):
    b, s, d = x.shape
    blk = MOBA_BLOCK
    sub = INPROJ_SUBTILES
    tm = sub * blk
    nt = s // tm
    aw, bw = A_WIDTH, B_WIDTH
    assert w_in.shape == (d, 4 * aw + 2 * bw)
    wt = w_in[:, :4 * aw].T.astype(BF16)
    wn = w_in[:, 4 * aw:].astype(BF16)
    t_shape = jax.ShapeDtypeStruct((b, s // blk, aw, blk), BF16)
    n_shape = jax.ShapeDtypeStruct((b, s, aw), BF16)
    t_spec = pl.BlockSpec((1, sub, aw, blk), lambda i, j: (i, j, 0, 0))
    n_spec = pl.BlockSpec((1, tm, aw), lambda i, j: (i, j, 0))
    const = lambda shape: pl.BlockSpec(shape, lambda i, j: (0,) * len(shape),
                                       pipeline_mode=pl.Buffered(1))
    return pl.pallas_call(
        _inproj0_kernel,
        grid=(b, nt),
        in_specs=[
            pl.BlockSpec((1, tm, d), lambda i, j: (i, j, 0)),
            pl.BlockSpec((1, 3, d), lambda i, j: (i, 0, 0)),
            const((1, d)),
            const((4 * aw, d)),
            const((d, 2 * bw)),
            const((A_HEAD_DIM, 1)),
            const((A_HEAD_DIM, 1)),
        ],
        out_specs=[t_spec, t_spec, t_spec, n_spec, n_spec, n_spec],
        out_shape=[t_shape, t_shape, t_shape, n_shape, n_shape, n_shape],
        compiler_params=pltpu.CompilerParams(
            dimension_semantics=("parallel", "parallel"), vmem_limit_bytes=VMEM_LIMIT),
        name="inproj0",
    )(x, mod, ln_g.reshape(1, d), wt, wn, q_g.reshape(A_HEAD_DIM, 1), k_g.reshape(A_HEAD_DIM, 1))


def _moba_kernel(qT_ref, k_ref, vT_ref, gT_ref, o_ref, kmean_ref, sel_ref, qm_ref, raw_ref, p_ref,
                 m_ref, acc_ref, *, nb):
    blk = MOBA_BLOCK
    dh = A_HEAD_DIM
    pw = 2 * dh
    streams = range(MOBA_NB)
    j = pl.program_id(1)

    @pl.when(j == 0)
    def _():
        for z in streams:
            for n in range(nb):
                kb = k_ref[z, n * blk:(n + 1) * blk, :].astype(F32)
                kmean_ref[z, n:n + 1, :] = jnp.mean(kb, axis=0, keepdims=True)

    d_iota = lax.broadcasted_iota(jnp.int32, (pw, blk), 0)

    def value_dots(z, n, slot):
        ones = jnp.ones((PV_ONES_ROWS, blk), BF16)
        return [_dot(jnp.concatenate([vT_ref[z, n, hd * dh:(hd + 1) * dh, :], ones], axis=0),
                     p_ref[z, slot, hd]) for hd in range(A_HEADS)]

    sel_scores = []
    own0 = pl.multiple_of(j * blk, blk)
    k_iota = lax.broadcasted_iota(jnp.int32, (blk, blk), 0)
    q_iota = lax.broadcasted_iota(jnp.int32, (blk, blk), 1)
    causal_bias = jnp.where(k_iota <= q_iota, 0.0, NEG).astype(BF16)
    for z in streams:
        km_hi, km_lo = _split_bf16(kmean_ref[z])
        km2 = jnp.concatenate([km_hi, km_lo], axis=0)
        per_head = []
        for hd in range(A_HEADS):
            pr = hd // 2
            lanes = slice(pr * pw, (pr + 1) * pw)
            q2 = qT_ref[z, 0, lanes, :]
            keep = (d_iota < dh) if hd % 2 == 0 else (d_iota >= dh)
            qm = jnp.where(keep, q2, jnp.zeros_like(q2))
            qm_ref[z, hd] = qm
            r2 = _dot(km2[:, lanes], qm)
            per_head.append(r2[0:nb] + r2[nb:2 * nb])
            own = _dot(k_ref[z, pl.ds(own0, blk), lanes], qm).astype(BF16)
            raw_ref[z, 1, hd] = _dot(k_ref[z, 0:blk, lanes], qm).astype(BF16)
            s = own + causal_bias
            m_new = jnp.max(s, axis=0, keepdims=True)
            p_ref[z, 0, hd] = jnp.exp2(s - m_new)
            m_ref[z, hd] = m_new.astype(F32)
            acc_ref[z, hd] = jnp.zeros(acc_ref.shape[2:], F32)
        sel_scores.append(per_head)

    n_iota = lax.broadcasted_iota(jnp.int32, (nb, blk), 0)
    past = n_iota < j
    for z in streams:
        for hd in range(A_HEADS):
            sc = jnp.where(past, sel_scores[z][hd], NEG)
            sel = jnp.zeros((nb, blk), F32)
            for _ in range(MOBA_TOPK):
                mx = jnp.max(sc, axis=0, keepdims=True)
                first = jnp.min(jnp.where(sc == mx, n_iota, nb), axis=0, keepdims=True)
                pick = n_iota == first
                sel = jnp.where(pick, 1.0, sel)
                sc = jnp.where(pick, -jnp.inf, sc)
            sel_ref[z, hd, 0:nb, :] = jnp.where(past, sel, 0.0)
            sel_ref[z, hd, nb:nb + 1, :] = jnp.ones((1, blk), F32)

    def step(z, i, cur, m_in):
        nxt = 1 - cur
        n = i - 1
        prev = jnp.where(i == 1, j, i - 2)
        prev_sel = jnp.where(i == 1, nb, i - 2)
        row0 = pl.multiple_of(jnp.minimum(i, nb - 1) * blk, blk)
        ones = jnp.ones((PV_ONES_ROWS, blk), BF16)
        pvs, alphas, m_out = [], [], []
        for hd in range(A_HEADS):
            pr = hd // 2
            kt = k_ref[z, pl.ds(row0, blk), pr * pw:(pr + 1) * pw]
            raw_ref[z, nxt, hd] = _dot(kt, qm_ref[z, hd]).astype(BF16)
            pvs.append(_dot(jnp.concatenate([vT_ref[z, prev, hd * dh:(hd + 1) * dh, :], ones], axis=0),
                            p_ref[z, nxt, hd]))
            s = raw_ref[z, cur, hd]
            selb = sel_ref[z, hd, pl.ds(n, 1), :] > 0.5
            smax = jnp.where(selb, jnp.max(s, axis=0, keepdims=True).astype(F32), NEG)
            m_old = m_in[hd]
            m_new = jnp.maximum(m_old, smax)
            alphas.append(jnp.exp2(m_old - m_new))
            m_out.append(m_new)
            p_ref[z, cur, hd] = jnp.exp2(s - m_new.astype(BF16))
        for hd in range(A_HEADS):
            keep = sel_ref[z, hd, pl.ds(prev_sel, 1), :] > 0.5
            acc_ref[z, hd] = alphas[hd] * (acc_ref[z, hd] + jnp.where(keep, pvs[hd], 0.0))
        return m_out

    def body(ii, carry):
        ms = [[m_ref[z, hd] for hd in range(A_HEADS)] for z in streams]
        for z in streams:
            ms[z] = step(z, 2 * ii + 1, 1, ms[z])
        for z in streams:
            ms[z] = step(z, 2 * ii + 2, 0, ms[z])
        for z in streams:
            for hd in range(A_HEADS):
                m_ref[z, hd] = ms[z][hd]
        return carry

    trips = (j + 1) // 2
    lax.fori_loop(0, trips, body, 0)

    last = jnp.where(j == 0, j, 2 * trips - 1)
    last_sel = jnp.where(j == 0, nb, 2 * trips - 1)
    last_pvs = [value_dots(z, last, 0) for z in streams]
    for z in streams:
        for pr in range(A_HEADS // 2):
            halves = []
            for hd in (2 * pr, 2 * pr + 1):
                keep = sel_ref[z, hd, pl.ds(last_sel, 1), :] > 0.5
                acc = acc_ref[z, hd] + jnp.where(keep, last_pvs[z][hd], 0.0)
                halves.append(acc[0:dh] / acc[dh:dh + 1])
            oT = jnp.concatenate(halves, axis=0)
            oT = oT * gT_ref[z, 0, pr * pw:(pr + 1) * pw, :].astype(F32)
            o_ref[z, :, pr * pw:(pr + 1) * pw] = oT.T.astype(BF16)


def _moba(qT, k, vT, gT):
    b, nb, aw, blk = qT.shape
    s = nb * blk
    pw = 2 * A_HEAD_DIM
    z = MOBA_NB
    assert b % z == 0
    q_spec = pl.BlockSpec((z, 1, aw, blk), lambda i, j: (i, j, 0, 0))
    row = lambda: pltpu.VMEM((z, A_HEADS, 1, blk), F32)
    return pl.pallas_call(
        functools.partial(_moba_kernel, nb=nb),
        grid=(b // z, nb),
        in_specs=[
            q_spec,
            pl.BlockSpec((z, s, aw), lambda i, j: (i, 0, 0)),
            pl.BlockSpec((z, nb, aw, blk), lambda i, j: (i, 0, 0, 0)),
            q_spec,
        ],
        out_specs=pl.BlockSpec((z, blk, aw), lambda i, j: (i, j, 0)),
        out_shape=jax.ShapeDtypeStruct((b, s, aw), BF16),
        scratch_shapes=[
            pltpu.VMEM((z, nb, aw), F32),
            pltpu.VMEM((z, A_HEADS, nb + V7X_SUBLANES, blk), F32),
            pltpu.VMEM((z, A_HEADS, pw, blk), BF16),
            pltpu.VMEM((z, 2, A_HEADS, blk, blk), BF16),
            pltpu.VMEM((z, 2, A_HEADS, blk, blk), BF16),
            row(),
            pltpu.VMEM((z, A_HEADS, A_HEAD_DIM + PV_ONES_ROWS, blk), F32),
        ],
        compiler_params=pltpu.CompilerParams(
            dimension_semantics=("parallel", "arbitrary"), vmem_limit_bytes=MOBA_VMEM_LIMIT),
        name="moba",
    )(qT, k, vT, gT)


def _mlstm_kernel(bx_ref, sbz_ref, cw_ref, cb_ref, wqt_ref, wk_ref, wvt_ref, wgq_ref, wgk_ref,
                  wgv_ref, bg_ref, og_ref, skip_ref, o_ref, xbuf_ref, state_ref, m_ref):
    L = MLSTM_L
    dh = B_HEAD_DIM
    c = pl.program_id(1)

    @pl.when(c == 0)
    def _():
        xbuf_ref[:, 0:CONV_HALO, :] = jnp.zeros((MLSTM_NB, CONV_HALO, B_WIDTH), F32)
        state_ref[...] = jnp.zeros_like(state_ref)
        m_ref[...] = jnp.zeros_like(m_ref)

    cw = cw_ref[...]
    cb = cb_ref[...]
    kscale = dh ** -0.5
    s_iota = lax.broadcasted_iota(jnp.int32, (L, L), 0)
    t_iota = lax.broadcasted_iota(jnp.int32, (L, L), 1)
    tri = s_iota <= t_iota
    tri_b = jnp.where(tri, 1.0, 0.0).astype(BF16)
    ones = jnp.ones((dh, L), F32)

    def project(bi):
        bx_b = bx_ref[bi]
        xbuf_ref[bi, CONV_HALO:CONV_HALO + L, :] = bx_b.astype(F32)
        st = dict(xc=[], ks=[], vT=[], scores=[], inter=[], state=[])
        gT = bg_ref[...]
        for hd in range(B_HEADS):
            lo = hd * dh
            conv = cb[:, lo:lo + dh]
            for i in range(B_CONV):
                off = CONV_HALO - (B_CONV - 1) + i
                conv = conv + cw[i:i + 1, lo:lo + dh] * xbuf_ref[bi, off:off + L, lo:lo + dh]
            xc = _silu(conv)
            xc_b = xc.astype(BF16)
            qT_h = _dot_nt(wqt_ref[hd], xc_b).astype(BF16)
            vT_h = _dot_nt(wvt_ref[hd], bx_b[:, lo:lo + dh])
            k_h = _dot(xc_b, wk_ref[hd])
            ks_h = (k_h * kscale).astype(BF16)
            state = state_ref[bi, hd]
            st["scores"].append(_dot(ks_h, qT_h))
            st["inter"].append(_dot(state.astype(BF16), qT_h))
            gT = gT + (_dot(wgq_ref[:, lo:lo + dh], qT_h)
                       + _dot_nt(wgk_ref[:, lo:lo + dh], k_h.astype(BF16))
                       + _dot(wgv_ref[:, lo:lo + dh], vT_h.astype(BF16)))
            st["xc"].append(xc); st["ks"].append(ks_h); st["vT"].append(vT_h); st["state"].append(state)
        tail = xbuf_ref[bi, L:L + CONV_HALO, :]
        xbuf_ref[bi, 0:CONV_HALO, :] = tail
        st["gT"] = gT
        return st

    def gate_chain(st):
        gT = st["gT"]
        lf_hi, lf_lo = _split_bf16(_log_sigmoid(gT))
        cum = (_dot(lf_hi, tri_b) + _dot(lf_lo, tri_b))[B_HEADS:2 * B_HEADS]
        a = gT[0:B_HEADS] - cum
        st["cum"], st["a"] = cum, a
        pad = jnp.zeros((V7X_LANES - B_HEADS, L), F32)
        st["a_cols"] = jnp.concatenate([a, pad], axis=0).T

    def recur(bi, st):
        a, cum, a_cols = st["a"], st["cum"], st["a_cols"]
        for hd in range(B_HEADS):
            lo = hd * dh
            ks_h, state, inter = st["ks"][hd], st["state"][hd], st["inter"][hd]
            vT_aug = jnp.concatenate([st["vT"][hd], ones], axis=0)
            a_row = a[hd:hd + 1]
            a_col = a_cols[:, hd:hd + 1]
            cum_row = cum[hd:hd + 1]
            m_prev = m_ref[bi, hd][0:1, 0:1]
            amax = jnp.max(jnp.where(tri, a_col, NEG), axis=0, keepdims=True)
            b_row = jnp.maximum(m_prev, amax)
            dmat = jnp.exp(jnp.where(tri, a_col - b_row, NEG))
            w_intra = (st["scores"][hd] * dmat).astype(BF16)
            intra = _dot(vT_aug.astype(BF16), w_intra)
            w_inter = jnp.exp(m_prev - b_row)
            num = w_inter * inter[0:dh] + intra[0:dh]
            den = w_inter * inter[dh:dh + 1] + intra[dh:dh + 1]
            hc = num * (1.0 / jnp.maximum(jnp.abs(den), jnp.exp(-(cum_row + b_row))))
            mu = jnp.mean(hc, axis=0, keepdims=True)
            hcc = hc - mu
            var = jnp.mean(hcc * hcc, axis=0, keepdims=True)
            hb = (hcc * lax.rsqrt(var + 1e-5)).T * og_ref[:, lo:lo + dh]
            yb = (hb + skip_ref[:, lo:lo + dh] * st["xc"][hd]) * sbz_ref[bi, :, lo:lo + dh].astype(F32)
            o_ref[bi, :, lo:lo + dh] = yb.astype(BF16)
            b_end = b_row[:, L - 1:L]
            ws = jnp.exp(a_row - b_end)
            decay = jnp.exp(m_prev - b_end)
            state_ref[bi, hd] = decay * state + _dot((vT_aug * ws).astype(BF16), ks_h)
            m_ref[bi, hd] = jnp.broadcast_to(cum_row[:, L - 1:L] + b_end, m_ref.shape[2:])

    sts = []
    for bi in range(MLSTM_NB):
        sts.append(project(bi))
        gate_chain(sts[bi])
    for bi in range(MLSTM_NB):
        recur(bi, sts[bi])


def _block_diag_dense(w):
    nblk, blk, _ = w.shape
    n = nblk * blk
    idx = jnp.arange(n) // blk
    return jnp.where(idx[:, None] == idx[None, :], jnp.tile(w.reshape(n, blk), (1, nblk)), 0.0)


def _mlstm(bx, sbz, conv_w, conv_b, wq, wk, wv, w_gates, b_gates, out_g, skip):
    b, s, bw = bx.shape
    L = MLSTM_L
    nc = s // L
    nbb = MLSTM_NB
    assert b % nbb == 0 and s % L == 0
    per_head = wq.shape[0] // B_HEADS

    def head_blocks(w, transpose):
        blocks = [_block_diag_dense(w[h * per_head:(h + 1) * per_head]) for h in range(B_HEADS)]
        return jnp.stack([blk.T if transpose else blk for blk in blocks]).astype(BF16)

    wqt_d = head_blocks(wq, True)
    wk_d = head_blocks(wk, False)
    wvt_d = head_blocks(wv, True)
    hblk = (B_HEADS, B_HEAD_DIM, B_HEAD_DIM)
    wgq = w_gates[:bw].T.astype(BF16)
    wgv = w_gates[2 * bw:].T.astype(BF16)
    wgk = w_gates[bw:2 * bw].T.astype(BF16)
    const = lambda shape: pl.BlockSpec(shape, lambda i, j: (0,) * len(shape))
    tok = pl.BlockSpec((nbb, L, bw), lambda i, j: (i, j, 0))
    return pl.pallas_call(
        _mlstm_kernel,
        grid=(b // nbb, nc),
        in_specs=[tok, tok, const((B_CONV, bw)), const((1, bw)),
                  const(hblk), const(hblk), const(hblk),
                  const((GATE_ROWS, bw)), const((GATE_ROWS, bw)), const((GATE_ROWS, bw)),
                  const((GATE_ROWS, 1)), const((1, bw)), const((1, bw))],
        out_specs=tok,
        out_shape=jax.ShapeDtypeStruct((b, s, bw), BF16),
        scratch_shapes=[pltpu.VMEM((nbb, L + CONV_HALO, bw), F32),
                        pltpu.VMEM((nbb, B_HEADS, 2 * B_HEAD_DIM, B_HEAD_DIM), F32),
                        pltpu.VMEM((nbb, B_HEADS, V7X_SUBLANES, V7X_LANES), F32)],
        compiler_params=pltpu.CompilerParams(
            dimension_semantics=("parallel", "arbitrary"), vmem_limit_bytes=VMEM_LIMIT),
        name="mlstm",
    )(bx, sbz, conv_w, conv_b.reshape(1, bw), wqt_d, wk_d, wvt_d, wgq, wgk, wgv,
      b_gates.reshape(GATE_ROWS, 1), out_g.reshape(1, bw), skip.reshape(1, bw))


def _tail_kernel(x_ref, ya_ref, yb_ref, mod0_ref, mod1_ref, lng_ref, wo0_ref, wi1_ref, clg_ref,
                 clb_ref, ws_ref, bst_ref, wo1_ref, o_ref):
    tm = x_ref.shape[1]
    sub = TAIL_SUB_ROWS
    n_sub = tm // sub
    aw = A_WIDTH
    mod1 = mod1_ref[0]
    gate0 = mod0_ref[0][2:3]
    t_iota = lax.broadcasted_iota(jnp.int32, (C_CHUNK, C_CHUNK), 0)
    s_iota = lax.broadcasted_iota(jnp.int32, (C_CHUNK, C_CHUNK), 1)
    tril = s_iota <= t_iota
    gw = C_WIDTH // C_GROUPS
    bst = bst_ref[...]
    wms = [jnp.where(tril, ws_ref[g], 0.0).astype(BF16) for g in range(C_GROUPS)]

    def out_proj0(r):
        rows = slice(r * sub, (r + 1) * sub)
        y0 = _dot(ya_ref[0, rows, :], wo0_ref[0:aw, :]) + _dot(yb_ref[0, rows, :], wo0_ref[aw:, :])
        x1 = x_ref[0, rows, :] + gate0 * y0
        h = _adaln_rmsnorm(x1, lng_ref[...], mod1[1:2], mod1[0:1]).astype(BF16)
        return x1, h

    def mix_inputs(p):
        u = _gelu_tanh(p[:, :C_WIDTH])
        v = _gelu_tanh(p[:, C_WIDTH:2 * C_WIDTH])
        mu = jnp.mean(v, axis=-1, keepdims=True)
        vc = v - mu
        var = jnp.mean(vc * vc, axis=-1, keepdims=True)
        vn = ((vc * lax.rsqrt(var + 1e-5)) * clg_ref[...] + clb_ref[...]).astype(BF16)
        return vn, u * _silu(p[:, 2 * C_WIDTH:])

    def spatial_gate(vn, gate):
        cols = []
        for g in range(C_GROUPS):
            rows = []
            for ch in range(sub // C_CHUNK):
                vg = vn[ch * C_CHUNK:(ch + 1) * C_CHUNK, g * gw:(g + 1) * gw]
                rows.append(_dot(wms[g], vg) + bst[:, g:g + 1])
            cols.append(jnp.concatenate(rows, axis=0))
        return (gate * jnp.concatenate(cols, axis=1)).astype(BF16)

    def out_proj1(r, x1, y1):
        o_ref[0, r * sub:(r + 1) * sub, :] = x1 + mod1[2:3] * _dot(y1, wo1_ref[...])

    heads = [out_proj0(r) for r in range(n_sub)]
    projs = [_dot(heads[r][1], wi1_ref[...]) for r in range(n_sub)]
    y1_prev = None
    for r in range(n_sub):
        vn, gate = mix_inputs(projs[r])
        if y1_prev is not None:
            out_proj1(r - 1, heads[r - 1][0], y1_prev)
        y1_prev = spatial_gate(vn, gate)
    out_proj1(n_sub - 1, heads[n_sub - 1][0], y1_prev)


def _tail(x, ya, yb, mod0, mod1, ln_g1, w_out0, w_in1, w_out1, c_ln_g, c_ln_b, c_ws, c_bs):
    b, s, d = x.shape
    tm = TAIL_ROW_TILE
    nt = s // tm
    const = lambda shape: pl.BlockSpec(shape, lambda i, j: (0,) * len(shape),
                                       pipeline_mode=pl.Buffered(1))
    half = pl.BlockSpec((1, tm, A_WIDTH), lambda i, j: (i, j, 0))
    full = pl.BlockSpec((1, tm, d), lambda i, j: (i, j, 0))
    modspec = pl.BlockSpec((1, 3, d), lambda i, j: (i, 0, 0))
    return pl.pallas_call(
        _tail_kernel,
        grid=(b, nt),
        in_specs=[full, half, half, modspec, modspec, const((1, d)), const((d, d)),
                  const((d, 3 * C_WIDTH)), const((1, d)), const((1, d)),
                  const((C_GROUPS, C_CHUNK, C_CHUNK)), const((C_CHUNK, C_GROUPS)), const((d, d))],
        out_specs=full,
        out_shape=jax.ShapeDtypeStruct((b, s, d), F32),
        compiler_params=pltpu.CompilerParams(
            dimension_semantics=("parallel", "parallel"), vmem_limit_bytes=VMEM_LIMIT),
        name="tail",
    )(x, ya, yb, mod0, mod1, ln_g1.reshape(1, d), w_out0.astype(BF16), w_in1.astype(BF16),
      c_ln_g.reshape(1, d), c_ln_b.reshape(1, d), c_ws, c_bs.T, w_out1.astype(BF16))


def kernel(x, c, ln_g, ada_w, ada_b, w_in, w_out, a_q_g, a_k_g, b_conv_w, b_conv_b, b_wq, b_wk, b_wv,
           b_w_gates, b_b_gates, b_out_g, b_skip, c_ln_g, c_ln_b, c_ws, c_bs):
    mods = _ada_mods(c, ada_w, ada_b)
    qT, vT, gT, k, bx, sbz = _inproj0(x, mods[0], ln_g[0], w_in[0], a_q_g[0], a_k_g[0])
    ya = _moba(qT, k, vT, gT)
    yb = _mlstm(bx, sbz, b_conv_w[0], b_conv_b[0], b_wq[0], b_wk[0], b_wv[0], b_w_gates[0],
                b_b_gates[0], b_out_g[0], b_skip[0])
    return _tail(x, ya, yb, mods[0], mods[1], ln_g[1], w_out[0], w_in[1], w_out[1],
                 c_ln_g[0], c_ln_b[0], c_ws[0], c_bs[0])
```

```python
import functools

import jax
import jax.numpy as jnp
from jax import lax
from jax.experimental import pallas as pl
from jax.experimental.pallas import tpu as pltpu

F32 = jnp.float32
BF16 = jnp.bfloat16

D_MODEL = 1024
A_HEADS = 8
A_HEAD_DIM = 64
A_WIDTH = A_HEADS * A_HEAD_DIM
MOBA_BLOCK = 256
MOBA_TOPK = 3
B_HEADS = 4
B_HEAD_DIM = 128
B_WIDTH = B_HEADS * B_HEAD_DIM
B_CONV = 4
C_GROUPS = 8
C_CHUNK = 128
C_WIDTH = D_MODEL
NEG = -1e30
LOG2E = 1.4426950408889634

V7X_SUBLANES = 8
V7X_LANES = 128

MLSTM_L = 256
MLSTM_NB = 4
INPROJ_SUBTILES = 4
TAIL_ROW_TILE = 1024
TAIL_SUB_ROWS = 512
CONV_HALO = 8
GATE_ROWS = 2 * B_HEADS
PV_ONES_ROWS = 16
MOBA_NB = 2
VMEM_LIMIT = 48 * 1024 * 1024
MOBA_VMEM_LIMIT = 56 * 1024 * 1024


def _silu(x):
    return x * jax.nn.sigmoid(x)


def _gelu_tanh(x):
    return 0.5 * x * (1.0 + jnp.tanh(0.7978845608028654 * (x + 0.044715 * (x * x * x))))


def _log_sigmoid(x):
    return jnp.minimum(x, 0.0) - jnp.log(1.0 + jnp.exp(-jnp.abs(x)))


def _split_bf16(x):
    hi = x.astype(BF16)
    lo = (x - hi.astype(F32)).astype(BF16)
    return hi, lo


def _dot(a, b):
    return jnp.dot(a, b, preferred_element_type=F32)


def _dot_nt(a, b):
    return lax.dot_general(a, b, (((1,), (1,)), ((), ())), preferred_element_type=F32)


def _adaln_rmsnorm(x, ln_g, scale, shift):
    y = x * lax.rsqrt(jnp.mean(x * x, axis=-1, keepdims=True) + 1e-6)
    return (y * ln_g) * (1.0 + scale) + shift


def _ada_kernel(c_ref, w_ref, b_ref, o_ref):
    cs_hi, cs_lo = _split_bf16(_silu(c_ref[...]))
    w_hi, w_lo = _split_bf16(w_ref[0])
    o_ref[0] = _dot(cs_hi, w_hi) + _dot(cs_lo, w_hi) + _dot(cs_hi, w_lo) + b_ref[0]


def _ada_mods(c, ada_w, ada_b):
    depth, d, d3 = ada_w.shape
    b = c.shape[0]
    bp = V7X_SUBLANES * pl.cdiv(b, V7X_SUBLANES)
    cp =jnp.zeros((bp, d), F32).at[:b].set(c)
    nt = d3 // d
    out = pl.pallas_call(
        _ada_kernel,
        grid=(depth, nt),
        in_specs=[
            pl.BlockSpec((bp, d), lambda l, n: (0, 0)),
            pl.BlockSpec((1, d, d), lambda l, n: (l, 0, n)),
            pl.BlockSpec((1, 1, d), lambda l, n: (l, 0, n)),
        ],
        out_specs=pl.BlockSpec((1, bp, d), lambda l, n: (l, 0, n)),
        out_shape=jax.ShapeDtypeStruct((depth, bp, d3), F32),
        name="ada_mods",
    )(cp, ada_w, ada_b.reshape(depth, 1, d3))
    return out[:, :b].reshape(depth, b, 3, d)


def _inproj0_kernel(x_ref, mod_ref, lng_ref, wt_ref, w_ref, qg_ref, kg_ref,
                    qT_ref, vT_ref, gT_ref, k_ref, bx_ref, sbz_ref):
    tm = MOBA_BLOCK
    aw = A_WIDTH
    mod = mod_ref[0]
    qg = qg_ref[...] * (A_HEAD_DIM ** -0.5 * LOG2E)
    kg = kg_ref[...]
    hs = [_adaln_rmsnorm(x_ref[0, r * tm:(r + 1) * tm], lng_ref[...], mod[1:2], mod[0:1]).astype(BF16)
          for r in range(INPROJ_SUBTILES)]
    for r in range(INPROJ_SUBTILES):
        h = hs[r]
        rows = slice(r * tm, (r + 1) * tm)
        pt = _dot_nt(wt_ref[...], h)
        p = _dot(h, w_ref[...])
        kn = []
        for hd in range(A_HEADS):
            lo = hd * A_HEAD_DIM
            q = pt[lo:lo + A_HEAD_DIM]
            rq = lax.rsqrt(jnp.mean(q * q, axis=0, keepdims=True) + 1e-6)
            qT_ref[0, r, lo:lo + A_HEAD_DIM, :] = ((q * rq) * qg).astype(BF16)
            k = pt[aw + lo:aw + lo + A_HEAD_DIM]
            rk = lax.rsqrt(jnp.mean(k * k, axis=0, keepdims=True) + 1e-6)
            kn.append((k * rk) * kg)
        k_ref[0, rows, :] = jnp.concatenate(kn, axis=0).T.astype(BF16)
        vT_ref[0, r] = pt[2 * aw:3 * aw].astype(BF16)
        gT_ref[0, r] = _silu(pt[3 * aw:]).astype(BF16)
        bx_ref[0, rows, :] = p[:, :B_WIDTH].astype(BF16)
        sbz_ref[0, rows, :] = _silu(p[:, B_WIDTH:]).astype(BF16)


def _inproj0(x, mod, ln_g, w_in, q_g, k_g):
    b, s, d = x.shape
    blk = MOBA_BLOCK
    sub = INPROJ_SUBTILES
    tm = sub * blk
    nt = s // tm
    aw, bw = A_WIDTH, B_WIDTH
    assert w_in.shape == (d, 4 * aw + 2 * bw)
    wt = w_in[:, :4 * aw].astype(BF16).T
    wn = w_in[:, 4 * aw:].astype(BF16)
    t_shape = jax.ShapeDtypeStruct((b, s // blk, aw, blk), BF16)
    n_shape = jax.ShapeDtypeStruct((b, s, aw), BF16)
    t_spec = pl.BlockSpec((1, sub, aw, blk), lambda i, j: (i, j, 0, 0))
    n_spec = pl.BlockSpec((1, tm, aw), lambda i, j: (i, j, 0))
    const = lambda shape: pl.BlockSpec(shape, lambda i, j: (0,) * len(shape),
                                       pipeline_mode=pl.Buffered(1))
    return pl.pallas_call(
        _inproj0_kernel,
        grid=(b, nt),
        in_specs=[
            pl.BlockSpec((1, tm, d), lambda i, j: (i, j, 0)),
            pl.BlockSpec((1, 3, d), lambda i, j: (i, 0, 0)),
            const((1, d)),
            const((4 * aw, d)),
            const((d, 2 * bw)),
            const((A_HEAD_DIM, 1)),
            const((A_HEAD_DIM, 1)),
        ],
        out_specs=[t_spec, t_spec, t_spec, n_spec, n_spec, n_spec],
        out_shape=[t_shape, t_shape, t_shape, n_shape, n_shape, n_shape],
        compiler_params=pltpu.CompilerParams(
            dimension_semantics=("parallel", "parallel"), vmem_limit_bytes=VMEM_LIMIT),
        name="inproj0",
    )(x, mod, ln_g.reshape(1, d), wt, wn, q_g.reshape(A_HEAD_DIM, 1), k_g.reshape(A_HEAD_DIM, 1))


def _moba_kernel(qT_ref, k_ref, vT_ref, gT_ref, o_ref, kmean_ref, sel_ref, qm_ref, raw_ref, p_ref,
                 m_ref, acc_ref, *, nb):
    blk = MOBA_BLOCK
    dh = A_HEAD_DIM
    pw = 2 * dh
    streams = range(MOBA_NB)
    j = pl.program_id(1)

    @pl.when(j == 0)
    def _():
        for z in streams:
            for n in range(nb):
                kb = k_ref[z, n * blk:(n + 1) * blk, :].astype(F32)
                kmean_ref[z, n:n + 1, :] = jnp.mean(kb, axis=0, keepdims=True)

    d_iota = lax.broadcasted_iota(jnp.int32, (pw, blk), 0)

    def value_dots(z, n, slot):
        ones = jnp.ones((PV_ONES_ROWS, blk), BF16)
        return [_dot(jnp.concatenate([vT_ref[z, n, hd * dh:(hd + 1) * dh, :], ones], axis=0),
                     p_ref[z, slot, hd]) for hd in range(A_HEADS)]

    sel_scores = []
    own0 = pl.multiple_of(j * blk, blk)
    k_iota = lax.broadcasted_iota(jnp.int32, (blk, blk), 0)
    q_iota = lax.broadcasted_iota(jnp.int32, (blk, blk), 1)
    causal_bias = jnp.where(k_iota <= q_iota, 0.0, NEG).astype(BF16)
    for z in streams:
        km_hi, km_lo = _split_bf16(kmean_ref[z])
        km2 = jnp.concatenate([km_hi, km_lo], axis=0)
        per_head = []
        for hd in range(A_HEADS):
            pr = hd // 2
            lanes = slice(pr * pw, (pr + 1) * pw)
            q2 = qT_ref[z, 0, lanes, :]
            keep = (d_iota < dh) if hd % 2 == 0 else (d_iota >= dh)
            qm = jnp.where(keep, q2, jnp.zeros_like(q2))
            qm_ref[z, hd] = qm
            r2 = _dot(km2[:, lanes], qm)
            per_head.append(r2[0:nb] + r2[nb:2 * nb])
            own = _dot(k_ref[z, pl.ds(own0, blk), lanes], qm).astype(BF16)
            raw_ref[z, 1, hd] = _dot(k_ref[z, 0:blk, lanes], qm).astype(BF16)
            s = own + causal_bias
            m_new = jnp.max(s, axis=0, keepdims=True)
            p_ref[z, 0, hd] = jnp.exp2(s - m_new)
            m_ref[z, hd] = m_new.astype(F32)
            acc_ref[z, hd] = jnp.zeros(acc_ref.shape[2:], F32)
        sel_scores.append(per_head)

    n_iota = lax.broadcasted_iota(jnp.int32, (nb, blk), 0)
    past = n_iota < j
    for z in streams:
        for hd in range(A_HEADS):
            sc = jnp.where(past, sel_scores[z][hd], NEG)
            sel = jnp.zeros((nb, blk), F32)
            for _ in range(MOBA_TOPK):
                mx = jnp.max(sc, axis=0, keepdims=True)
                first = jnp.min(jnp.where(sc == mx, n_iota, nb), axis=0, keepdims=True)
                pick = n_iota == first
                sel = jnp.where(pick, 1.0, sel)
                sc = jnp.where(pick, -jnp.inf, sc)
            sel_ref[z, hd, 0:nb, :] = jnp.where(past, sel, 0.0)
            sel_ref[z, hd, nb:nb + 1, :] = jnp.ones((1, blk), F32)

    def step(z, i, cur, m_in):
        nxt = 1 - cur
        n = i - 1
        prev = jnp.where(i == 1, j, i - 2)
        prev_sel = jnp.where(i == 1, nb, i - 2)
        row0 = pl.multiple_of(jnp.minimum(i, nb - 1) * blk, blk)
        ones = jnp.ones((PV_ONES_ROWS, blk), BF16)
        pvs, alphas, m_out = [], [], []
        for hd in range(A_HEADS):
            pr = hd // 2
            kt = k_ref[z, pl.ds(row0, blk), pr * pw:(pr + 1) * pw]
            raw_ref[z, nxt, hd] = _dot(kt, qm_ref[z, hd]).astype(BF16)
            pvs.append(_dot(jnp.concatenate([vT_ref[z, prev, hd * dh:(hd + 1) * dh, :], ones], axis=0),
                            p_ref[z, nxt, hd]))
            s = raw_ref[z, cur, hd]
            selb = sel_ref[z, hd, pl.ds(n, 1), :] > 0.5
            smax = jnp.where(selb, jnp.max(s, axis=0, keepdims=True).astype(F32), NEG)
            m_old = m_in[hd]
            m_new = jnp.maximum(m_old, smax)
            alphas.append(jnp.exp2(m_old - m_new))
            m_out.append(m_new)
            p_ref[z, cur, hd] = jnp.exp2(s - m_new.astype(BF16))
        for hd in range(A_HEADS):
            keep = sel_ref[z, hd, pl.ds(prev_sel, 1), :] > 0.5
            acc_ref[z, hd] = alphas[hd] * (acc_ref[z, hd] + jnp.where(keep, pvs[hd], 0.0))
        return m_out

    def body(ii, carry):
        ms = [[m_ref[z, hd] for hd in range(A_HEADS)] for z in streams]
        for z in streams:
            ms[z] = step(z, 2 * ii + 1, 1, ms[z])
        for z in streams:
            ms[z] = step(z, 2 * ii + 2, 0, ms[z])
        for z in streams:
            for hd in range(A_HEADS):
                m_ref[z, hd] = ms[z][hd]
        return carry

    trips = (j + 1) // 2
    lax.fori_loop(0, trips, body, 0)

    last = jnp.where(j == 0, j, 2 * trips - 1)
    last_sel = jnp.where(j == 0, nb, 2 * trips - 1)
    last_pvs = [value_dots(z, last, 0) for z in streams]
    for z in streams:
        for pr in range(A_HEADS // 2):
            halves = []
            for hd in (2 * pr, 2 * pr + 1):
                keep = sel_ref[z, hd, pl.ds(last_sel, 1), :] > 0.5
                acc = acc_ref[z, hd] + jnp.where(keep, last_pvs[z][hd], 0.0)
                halves.append(acc[0:dh] / acc[dh:dh + 1])
            oT = jnp.concatenate(halves, axis=0)
            oT = oT * gT_ref[z, 0, pr * pw:(pr + 1) * pw, :].astype(F32)
            o_ref[z, :, pr * pw:(pr + 1) * pw] = oT.astype(BF16).T


def _moba(qT, k, vT, gT):
    b, nb, aw, blk = qT.shape
    s = nb * blk
    pw = 2 * A_HEAD_DIM
    z = MOBA_NB
    assert b % z == 0
    q_spec = pl.BlockSpec((z, 1, aw, blk), lambda i, j: (i, j, 0, 0))
    row = lambda: pltpu.VMEM((z, A_HEADS, 1, blk), F32)
    return pl.pallas_call(
        functools.partial(_moba_kernel, nb=nb),
        grid=(b // z, nb),
        in_specs=[
            q_spec,
            pl.BlockSpec((z, s, aw), lambda i, j: (i, 0, 0)),
            pl.BlockSpec((z, nb, aw, blk), lambda i, j: (i, 0, 0, 0)),
            q_spec,
        ],
        out_specs=pl.BlockSpec((z, blk, aw), lambda i, j: (i, j, 0)),
        out_shape=jax.ShapeDtypeStruct((b, s, aw), BF16),
        scratch_shapes=[
            pltpu.VMEM((z, nb, aw), F32),
            pltpu.VMEM((z, A_HEADS, nb + V7X_SUBLANES, blk), F32),
            pltpu.VMEM((z, A_HEADS, pw, blk), BF16),
            pltpu.VMEM((z, 2, A_HEADS, blk, blk), BF16),
            pltpu.VMEM((z, 2, A_HEADS, blk, blk), BF16),
            row(),
            pltpu.VMEM((z, A_HEADS, A_HEAD_DIM + PV_ONES_ROWS, blk), F32),
        ],
        compiler_params=pltpu.CompilerParams(
            dimension_semantics=("parallel", "arbitrary"), vmem_limit_bytes=MOBA_VMEM_LIMIT),
        name="moba",
    )(qT, k, vT, gT)


def _mlstm_kernel(bx_ref, sbz_ref, cw_ref, cb_ref, wqt_ref, wk_ref, wvt_ref, wgq_ref, wgk_ref,
                  wgv_ref, bg_ref, og_ref, skip_ref, o_ref, xbuf_ref, state_ref, m_ref):
    L = MLSTM_L
    dh = B_HEAD_DIM
    c = pl.program_id(1)

    @pl.when(c == 0)
    def _():
        xbuf_ref[:, 0:CONV_HALO, :] = jnp.zeros((MLSTM_NB, CONV_HALO, B_WIDTH), F32)
        state_ref[...] = jnp.zeros_like(state_ref)
        m_ref[...] = jnp.zeros_like(m_ref)

    cw = cw_ref[...]
    cb = cb_ref[...]
    kscale = dh ** -0.5
    s_iota = lax.broadcasted_iota(jnp.int32, (L, L), 0)
    t_iota = lax.broadcasted_iota(jnp.int32, (L, L), 1)
    tri = s_iota <= t_iota
    tri_b = jnp.where(tri, 1.0, 0.0).astype(BF16)
    ones = jnp.ones((dh, L), F32)

    def project(bi):
        bx_b = bx_ref[bi]
        xbuf_ref[bi, CONV_HALO:CONV_HALO + L, :] = bx_b.astype(F32)
        st = dict(xc=[], ks=[], vT=[], scores=[], inter=[], state=[])
        gT = bg_ref[...]
        for hd in range(B_HEADS):
            lo = hd * dh
            conv = cb[:, lo:lo + dh]
            for i in range(B_CONV):
                off = CONV_HALO - (B_CONV - 1) + i
                conv = conv + cw[i:i + 1, lo:lo + dh] * xbuf_ref[bi, off:off + L, lo:lo + dh]
            xc = _silu(conv)
            xc_b = xc.astype(BF16)
            qT_h = _dot_nt(wqt_ref[hd], xc_b).astype(BF16)
            vT_h = _dot_nt(wvt_ref[hd], bx_b[:, lo:lo + dh])
            k_h = _dot(xc_b, wk_ref[hd])
            ks_h = (k_h * kscale).astype(BF16)
            state = state_ref[bi, hd]
            st["scores"].append(_dot(ks_h, qT_h))
            st["inter"].append(_dot(state.astype(BF16), qT_h))
            gT = gT + (_dot(wgq_ref[:, lo:lo + dh], qT_h)
                       + _dot_nt(wgk_ref[:, lo:lo + dh], k_h.astype(BF16))
                       + _dot(wgv_ref[:, lo:lo + dh], vT_h.astype(BF16)))
            st["xc"].append(xc); st["ks"].append(ks_h); st["vT"].append(vT_h); st["state"].append(state)
        tail = xbuf_ref[bi, L:L + CONV_HALO, :]
        xbuf_ref[bi, 0:CONV_HALO, :] = tail
        st["gT"] = gT
        return st

    def gate_chain(st):
        gT = st["gT"]
        lf_hi, lf_lo = _split_bf16(_log_sigmoid(gT))
        cum = (_dot(lf_hi, tri_b) + _dot(lf_lo, tri_b))[B_HEADS:2 * B_HEADS]
        a = gT[0:B_HEADS] - cum
        st["cum"], st["a"] = cum, a
        pad = jnp.zeros((V7X_LANES - B_HEADS, L), F32)
        st["a_cols"] = jnp.concatenate([a, pad], axis=0).T

    def recur(bi, st):
        a, cum, a_cols = st["a"], st["cum"], st["a_cols"]
        for hd in range(B_HEADS):
            lo = hd * dh
            ks_h, state, inter = st["ks"][hd], st["state"][hd], st["inter"][hd]
            vT_aug = jnp.concatenate([st["vT"][hd], ones], axis=0)
            a_row = a[hd:hd + 1]
            a_col = a_cols[:, hd:hd + 1]
            cum_row = cum[hd:hd + 1]
            m_prev = m_ref[bi, hd][0:1, 0:1]
            amax = jnp.max(jnp.where(tri, a_col, NEG), axis=0, keepdims=True)
            b_row = jnp.maximum(m_prev, amax)
            dmat = jnp.exp(jnp.where(tri, a_col - b_row, NEG))
            w_intra = (st["scores"][hd] * dmat).astype(BF16)
            intra = _dot(vT_aug.astype(BF16), w_intra)
            w_inter = jnp.exp(m_prev - b_row)
            num = w_inter * inter[0:dh] + intra[0:dh]
            den = w_inter * inter[dh:dh + 1] + intra[dh:dh + 1]
            hc = num * (1.0 / jnp.maximum(jnp.abs(den), jnp.exp(-(cum_row + b_row))))
            mu = jnp.mean(hc, axis=0, keepdims=True)
            hcc = hc - mu
            var = jnp.mean(hcc * hcc, axis=0, keepdims=True)
            hb = (hcc * lax.rsqrt(var + 1e-5)).T * og_ref[:, lo:lo + dh]
            yb = (hb + skip_ref[:, lo:lo + dh] * st["xc"][hd]) * sbz_ref[bi, :, lo:lo + dh].astype(F32)
            o_ref[bi, :, lo:lo + dh] = yb.astype(BF16)
            b_end = b_row[:, L - 1:L]
            ws = jnp.exp(a_row - b_end)
            decay = jnp.exp(m_prev - b_end)
            state_ref[bi, hd] = decay * state + _dot((vT_aug * ws).astype(BF16), ks_h)
            m_ref[bi, hd] = jnp.broadcast_to(cum_row[:, L - 1:L] + b_end, m_ref.shape[2:])

    sts = []
    for bi in range(MLSTM_NB):
        sts.append(project(bi))
        gate_chain(sts[bi])
    for bi in range(MLSTM_NB):
        recur(bi, sts[bi])


def _block_diag_dense(w):
    nblk, blk, _ = w.shape
    n = nblk * blk
    idx = jnp.arange(n) // blk
    return jnp.where(idx[:, None] == idx[None, :], jnp.tile(w.reshape(n, blk), (1, nblk)), 0.0)


def _mlstm(bx, sbz, conv_w, conv_b, wq, wk, wv, w_gates, b_gates, out_g, skip):
    b, s, bw = bx.shape
    L = MLSTM_L
    nc = s // L
    nbb = MLSTM_NB
    assert b % nbb == 0 and s % L == 0
    per_head = wq.shape[0] // B_HEADS

    def head_blocks(w, transpose):
        blocks = [_block_diag_dense(w[h * per_head:(h + 1) * per_head]) for h in range(B_HEADS)]
        return jnp.stack([blk.T if transpose else blk for blk in blocks]).astype(BF16)

    wqt_d = head_blocks(wq, True)
    wk_d = head_blocks(wk, False)
    wvt_d = head_blocks(wv, True)
    hblk = (B_HEADS, B_HEAD_DIM, B_HEAD_DIM)
    wgq = w_gates[:bw].T.astype(BF16)
    wgv = w_gates[2 * bw:].T.astype(BF16)
    wgk = w_gates[bw:2 * bw].T.astype(BF16)
    const = lambda shape: pl.BlockSpec(shape, lambda i, j: (0,) * len(shape))
    tok = pl.BlockSpec((nbb, L, bw), lambda i, j: (i, j, 0))
    return pl.pallas_call(
        _mlstm_kernel,
        grid=(b // nbb, nc),
        in_specs=[tok, tok, const((B_CONV, bw)), const((1, bw)),
                  const(hblk), const(hblk), const(hblk),
                  const((GATE_ROWS, bw)), const((GATE_ROWS, bw)), const((GATE_ROWS, bw)),
                  const((GATE_ROWS, 1)), const((1, bw)), const((1, bw))],
        out_specs=tok,
        out_shape=jax.ShapeDtypeStruct((b, s, bw), BF16),
        scratch_shapes=[pltpu.VMEM((nbb, L + CONV_HALO, bw), F32),
                        pltpu.VMEM((nbb, B_HEADS, 2 * B_HEAD_DIM, B_HEAD_DIM), F32),
                        pltpu.VMEM((nbb, B_HEADS, V7X_SUBLANES, V7X_LANES), F32)],
        compiler_params=pltpu.CompilerParams(
            dimension_semantics=("parallel", "arbitrary"), vmem_limit_bytes=VMEM_LIMIT),
        name="mlstm",
    )(bx, sbz, conv_w, conv_b.reshape(1, bw), wqt_d, wk_d, wvt_d, wgq, wgk, wgv,
      b_gates.reshape(GATE_ROWS, 1), out_g.reshape(1, bw), skip.reshape(1, bw))


def _tail_kernel(x_ref, ya_ref, yb_ref, mod0_ref, mod1_ref, lng_ref, wo0_ref, wi1_ref, clg_ref,
                 clb_ref, ws_ref, bst_ref, wo1_ref, o_ref):
    tm = x_ref.shape[1]
    sub = TAIL_SUB_ROWS
    n_sub = tm // sub
    aw = A_WIDTH
    mod1 = mod1_ref[0]
    gate0 = mod0_ref[0][2:3]
    t_iota = lax.broadcasted_iota(jnp.int32, (C_CHUNK, C_CHUNK), 0)
    s_iota = lax.broadcasted_iota(jnp.int32, (C_CHUNK, C_CHUNK), 1)
    tril = s_iota <= t_iota
    gw = C_WIDTH // C_GROUPS
    bst = bst_ref[...]
    wms = [jnp.where(tril, ws_ref[g], 0.0).astype(BF16) for g in range(C_GROUPS)]

    def out_proj0(r):
        rows = slice(r * sub, (r + 1) * sub)
        y0 = _dot(ya_ref[0, rows, :], wo0_ref[0:aw, :]) + _dot(yb_ref[0, rows, :], wo0_ref[aw:, :])
        x1 = x_ref[0, rows, :] + gate0 * y0
        h = _adaln_rmsnorm(x1, lng_ref[...], mod1[1:2], mod1[0:1]).astype(BF16)
        return x1, h

    def mix_inputs(p):
        u = _gelu_tanh(p[:, :C_WIDTH])
        v = _gelu_tanh(p[:, C_WIDTH:2 * C_WIDTH])
        mu = jnp.mean(v, axis=-1, keepdims=True)
        vc = v - mu
        var = jnp.mean(vc * vc, axis=-1, keepdims=True)
        vn = ((vc * lax.rsqrt(var + 1e-5)) * clg_ref[...] + clb_ref[...]).astype(BF16)
        return vn, u * _silu(p[:, 2 * C_WIDTH:])

    def spatial_gate(vn, gate):
        cols = []
        for g in range(C_GROUPS):
            rows = []
            for ch in range(sub // C_CHUNK):
                vg = vn[ch * C_CHUNK:(ch + 1) * C_CHUNK, g * gw:(g + 1) * gw]
                rows.append(_dot(wms[g], vg) + bst[:, g:g + 1])
            cols.append(jnp.concatenate(rows, axis=0))
        return (gate * jnp.concatenate(cols, axis=1)).astype(BF16)

    def out_proj1(r, x1, y1):
        o_ref[0, r * sub:(r + 1) * sub, :] = x1 + mod1[2:3] * _dot(y1, wo1_ref[...])

    heads = [out_proj0(r) for r in range(n_sub)]
    projs = [_dot(heads[r][1], wi1_ref[...]) for r in range(n_sub)]
    y1_prev = None
    for r in range(n_sub):
        vn, gate = mix_inputs(projs[r])
        if y1_prev is not None:
            out_proj1(r - 1, heads[r - 1][0], y1_prev)
        y1_prev = spatial_gate(vn, gate)
    out_proj1(n_sub - 1, heads[n_sub - 1][0], y1_prev)


def _tail(x, ya, yb, mod0, mod1, ln_g1, w_out0, w_in1, w_out1, c_ln_g, c_ln_b, c_ws, c_bs):
    b, s, d = x.shape
    tm = TAIL_ROW_TILE
    nt = s // tm
    const = lambda shape: pl.BlockSpec(shape, lambda i, j: (0,) * len(shape),
                                       pipeline_mode=pl.Buffered(1))
    half = pl.BlockSpec((1, tm, A_WIDTH), lambda i, j: (i, j, 0))
    full = pl.BlockSpec((1, tm, d), lambda i, j: (i, j, 0))
    modspec = pl.BlockSpec((1, 3, d), lambda i, j: (i, 0, 0))
    return pl.pallas_call(
        _tail_kernel,
        grid=(b, nt),
        in_specs=[full, half, half, modspec, modspec, const((1, d)), const((d, d)),
                  const((d, 3 * C_WIDTH)), const((1, d)), const((1, d)),
                  const((C_GROUPS, C_CHUNK, C_CHUNK)), const((C_CHUNK, C_GROUPS)), const((d, d))],
        out_specs=full,
        out_shape=jax.ShapeDtypeStruct((b, s, d), F32),
        compiler_params=pltpu.CompilerParams(
            dimension_semantics=("parallel", "parallel"), vmem_limit_bytes=VMEM_LIMIT),
        name="tail",
    )(x, ya, yb, mod0, mod1, ln_g1.reshape(1, d), w_out0.astype(BF16), w_in1.astype(BF16),
      c_ln_g.reshape(1, d), c_ln_b.reshape(1, d), c_ws, c_bs.T, w_out1.astype(BF16))


def kernel(x, c, ln_g, ada_w, ada_b, w_in, w_out, a_q_g, a_k_g, b_conv_w, b_conv_b, b_wq, b_wk, b_wv,
           b_w_gates, b_b_gates, b_out_g, b_skip, c_ln_g, c_ln_b, c_ws, c_bs):
    mods = _ada_mods(c, ada_w, ada_b)
    qT, vT, gT, k, bx, sbz = _inproj0(x, mods[0], ln_g[0], w_in[0], a_q_g[0], a_k_g[0])
    ya = _moba(qT, k, vT, gT)
    yb = _mlstm(bx, sbz, b_conv_w[0], b_conv_b[0], b_wq[0], b_wk[0], b_wv[0], b_w_gates[0],
                b_b_gates[0], b_out_g[0], b_skip[0])
    return _tail(x, ya, yb, mods[0], mods[1], ln_g[1], w_out[0], w_in[1], w_out[1],
                 c_ln_g[0], c_ln_b[0], c_ws[0], c_bs[0])
```

```python
import functools

import jax
import jax.numpy as jnp
from jax import lax
from jax.experimental import pallas as pl
from jax.experimental.pallas import tpu as pltpu

F32 = jnp.float32
BF16 = jnp.bfloat16

D_MODEL = 1024
A_HEADS = 8
A_HEAD_DIM = 64
A_WIDTH = A_HEADS * A_HEAD_DIM
MOBA_BLOCK = 256
MOBA_TOPK = 3
B_HEADS = 4
B_HEAD_DIM = 128
B_WIDTH = B_HEADS * B_HEAD_DIM
B_CONV = 4
C_GROUPS = 8
C_CHUNK = 128
C_WIDTH = D_MODEL
NEG = -1e30
LOG2E = 1.4426950408889634

V7X_SUBLANES = 8
V7X_LANES = 128

MLSTM_L = 256
MLSTM_NB = 4
INPROJ_SUBTILES = 4
TAIL_ROW_TILE = 1024
TAIL_SUB_ROWS = 512
CONV_HALO = 8
GATE_ROWS = 2 * B_HEADS
PV_ONES_ROWS = 16
MOBA_NB = 2
VMEM_LIMIT = 48 * 1024 * 1024
MOBA_VMEM_LIMIT = 56 * 1024 * 1024


def _silu(x):
    return x * jax.nn.sigmoid(x)


def _gelu_tanh(x):
    return 0.5 * x * (1.0 + jnp.tanh(0.7978845608028654 * (x + 0.044715 * (x * x * x))))


def _log_sigmoid(x):
    return jnp.minimum(x, 0.0) - jnp.log(1.0 + jnp.exp(-jnp.abs(x)))


def _split_bf16(x):
    hi = x.astype(BF16)
    lo = (x - hi.astype(F32)).astype(BF16)
    return hi, lo


def _dot(a, b):
    return jnp.dot(a, b, preferred_element_type=F32)


def _dot_nt(a, b):
    return lax.dot_general(a, b, (((1,), (1,)), ((), ())), preferred_element_type=F32)


def _adaln_rmsnorm(x, ln_g, scale, shift):
    y = x * lax.rsqrt(jnp.mean(x * x, axis=-1, keepdims=True) + 1e-6)
    return (y * ln_g) * (1.0 + scale) + shift


def _ada_kernel(c_ref, w_ref, b_ref, o_ref):
    cs_hi, cs_lo = _split_bf16(_silu(c_ref[...]))
    w_hi, w_lo = _split_bf16(w_ref[0])
    o_ref[0] = _dot(cs_hi, w_hi) + _dot(cs_lo, w_hi) + _dot(cs_hi, w_lo) + b_ref[0]


def _ada_mods(c, ada_w, ada_b):
    depth, d, d3 = ada_w.shape
    b = c.shape[0]
    bp = V7X_SUBLANES * pl.cdiv(b, V7X_SUBLANES)
    cp =jnp.zeros((bp, d), F32).at[:b].set(c)
    nt = d3 // d
    out = pl.pallas_call(
        _ada_kernel,
        grid=(depth, nt),
        in_specs=[
            pl.BlockSpec((bp, d), lambda l, n: (0, 0)),
            pl.BlockSpec((1, d, d), lambda l, n: (l, 0, n)),
            pl.BlockSpec((1, 1, d), lambda l, n: (l, 0, n)),
        ],
        out_specs=pl.BlockSpec((1, bp, d), lambda l, n: (l, 0, n)),
        out_shape=jax.ShapeDtypeStruct((depth, bp, d3), F32),
        name="ada_mods",
    )(cp, ada_w, ada_b.reshape(depth, 1, d3))
    return out[:, :b].reshape(depth, b, 3, d)


def _inproj0_kernel(x_ref, mod_ref, lng_ref, wt_ref, w_ref, qg_ref, kg_ref,
                    qT_ref, vT_ref, gT_ref, k_ref, bx_ref, sbz_ref):
    tm = MOBA_BLOCK
    aw = A_WIDTH
    mod = mod_ref[0]
    qg = qg_ref[...] * (A_HEAD_DIM ** -0.5 * LOG2E)
    kg = kg_ref[...]
    hs = [_adaln_rmsnorm(x_ref[0, r * tm:(r + 1) * tm], lng_ref[...], mod[1:2], mod[0:1]).astype(BF16)
          for r in range(INPROJ_SUBTILES)]
    for r in range(INPROJ_SUBTILES):
        h = hs[r]
        rows = slice(r * tm, (r + 1) * tm)
        pt = _dot_nt(wt_ref[...], h)
        p = _dot(h, w_ref[...])
        kn = []
        for hd in range(A_HEADS):
            lo = hd * A_HEAD_DIM
            q = pt[lo:lo + A_HEAD_DIM]
            rq = lax.rsqrt(jnp.mean(q * q, axis=0, keepdims=True) + 1e-6)
            qT_ref[0, r, lo:lo + A_HEAD_DIM, :] = ((q * rq) * qg).astype(BF16)
            k = pt[aw + lo:aw + lo + A_HEAD_DIM]
            rk = lax.rsqrt(jnp.mean(k * k, axis=0, keepdims=True) + 1e-6)
            kn.append((k * rk) * kg)
        k_ref[0, rows, :] = jnp.concatenate(kn, axis=0).T.astype(BF16)
        vT_ref[0, r] = pt[2 * aw:3 * aw].astype(BF16)
        gT_ref[0, r] = _silu(pt[3 * aw:]).astype(BF16)
        bx_ref[0, rows, :] = p[:, :B_WIDTH].astype(BF16)
        sbz_ref[0, rows, :] = _silu(p[:, B_WIDTH:]).astype(BF16)


def _inproj0(x, mod, ln_g, w_in, q_g, k_g):
    b, s, d = x.shape
    blk = MOBA_BLOCK
    sub = INPROJ_SUBTILES
    tm = sub * blk
    nt = s // tm
    aw, bw = A_WIDTH, B_WIDTH
    assert w_in.shape == (d, 4 * aw + 2 * bw)
    wt = w_in[:, :4 * aw].astype(BF16).T
    wn = w_in[:, 4 * aw:].astype(BF16)
    t_shape = jax.ShapeDtypeStruct((b, s // blk, aw, blk), BF16)
    n_shape = jax.ShapeDtypeStruct((b, s, aw), BF16)
    t_spec = pl.BlockSpec((1, sub, aw, blk), lambda i, j: (i, j, 0, 0))
    n_spec = pl.BlockSpec((1, tm, aw), lambda i, j: (i, j, 0))
    const = lambda shape: pl.BlockSpec(shape, lambda i, j: (0,) * len(shape),
                                       pipeline_mode=pl.Buffered(1))
    return pl.pallas_call(
        _inproj0_kernel,
        grid=(b, nt),
        in_specs=[
            pl.BlockSpec((1, tm, d), lambda i, j: (i, j, 0)),
            pl.BlockSpec((1, 3, d), lambda i, j: (i, 0, 0)),
            const((1, d)),
            const((4 * aw, d)),
            const((d, 2 * bw)),
            const((A_HEAD_DIM, 1)),
            const((A_HEAD_DIM, 1)),
        ],
        out_specs=[t_spec, t_spec, t_spec, n_spec, n_spec, n_spec],
        out_shape=[t_shape, t_shape, t_shape, n_shape, n_shape, n_shape],
        compiler_params=pltpu.CompilerParams(
            dimension_semantics=("parallel", "parallel"), vmem_limit_bytes=VMEM_LIMIT),
        name="inproj0",
    )(x, mod, ln_g.reshape(1, d), wt, wn, q_g.reshape(A_HEAD_DIM, 1), k_g.reshape(A_HEAD_DIM, 1))


def _moba_kernel(qT_ref, k_ref, vT_ref, gT_ref, o_ref, kmean_ref, sel_ref, qm_ref, raw_ref, p_ref,
                 m_ref, acc_ref, *, nb):
    blk = MOBA_BLOCK
    dh = A_HEAD_DIM
    pw = 2 * dh
    streams = range(MOBA_NB)
    j = pl.program_id(1)

    @pl.when(j == 0)
    def _():
        for z in streams:
            for n in range(nb):
                kb = k_ref[z, n * blk:(n + 1) * blk, :].astype(F32)
                kmean_ref[z, n:n + 1, :] = jnp.mean(kb, axis=0, keepdims=True)

    d_iota = lax.broadcasted_iota(jnp.int32, (pw, blk), 0)

    def value_dots(z, n, slot):
        ones = jnp.ones((PV_ONES_ROWS, blk), BF16)
        return [_dot(jnp.concatenate([vT_ref[z, n, hd * dh:(hd + 1) * dh, :], ones], axis=0),
                     p_ref[z, slot, hd]) for hd in range(A_HEADS)]

    sel_scores = []
    own0 = pl.multiple_of(j * blk, blk)
    k_iota = lax.broadcasted_iota(jnp.int32, (blk, blk), 0)
    q_iota = lax.broadcasted_iota(jnp.int32, (blk, blk), 1)
    causal_bias = jnp.where(k_iota <= q_iota, 0.0, NEG).astype(BF16)
    for z in streams:
        km_hi, km_lo = _split_bf16(kmean_ref[z])
        km2 = jnp.concatenate([km_hi, km_lo], axis=0)
        per_head = []
        for hd in range(A_HEADS):
            pr = hd // 2
            lanes = slice(pr * pw, (pr + 1) * pw)
            q2 = qT_ref[z, 0, lanes, :]
            keep = (d_iota < dh) if hd % 2 == 0 else (d_iota >= dh)
            qm = jnp.where(keep, q2, jnp.zeros_like(q2))
            qm_ref[z, hd] = qm
            r2 = _dot(km2[:, lanes], qm)
            per_head.append(r2[0:nb] + r2[nb:2 * nb])
            own = _dot(k_ref[z, pl.ds(own0, blk), lanes], qm).astype(BF16)
            raw_ref[z, 1, hd] = _dot(k_ref[z, 0:blk, lanes], qm).astype(BF16)
            s = own + causal_bias
            m_new = jnp.max(s, axis=0, keepdims=True)
            p_ref[z, 0, hd] = jnp.exp2(s - m_new)
            m_ref[z, hd] = m_new.astype(F32)
            acc_ref[z, hd] = jnp.zeros(acc_ref.shape[2:], F32)
        sel_scores.append(per_head)

    n_iota = lax.broadcasted_iota(jnp.int32, (nb, blk), 0)
    past = n_iota < j
    for z in streams:
        for hd in range(A_HEADS):
            sc = jnp.where(past, sel_scores[z][hd], NEG)
            sel = jnp.zeros((nb, blk), F32)
            for _ in range(MOBA_TOPK):
                mx = jnp.max(sc, axis=0, keepdims=True)
                first = jnp.min(jnp.where(sc == mx, n_iota, nb), axis=0, keepdims=True)
                pick = n_iota == first
                sel = jnp.where(pick, 1.0, sel)
                sc = jnp.where(pick, -jnp.inf, sc)
            sel_ref[z, hd, 0:nb, :] = jnp.where(past, sel, 0.0)
            sel_ref[z, hd, nb:nb + 1, :] = jnp.ones((1, blk), F32)

    def step(z, i, cur, m_in):
        nxt = 1 - cur
        n = i - 1
        prev = jnp.where(i == 1, j, i - 2)
        prev_sel = jnp.where(i == 1, nb, i - 2)
        row0 = pl.multiple_of(jnp.minimum(i, nb - 1) * blk, blk)
        ones = jnp.ones((PV_ONES_ROWS, blk), BF16)
        pvs, alphas, m_out = [], [], []
        for hd in range(A_HEADS):
            pr = hd // 2
            kt = k_ref[z, pl.ds(row0, blk), pr * pw:(pr + 1) * pw]
            raw_ref[z, nxt, hd] = _dot(kt, qm_ref[z, hd]).astype(BF16)
            pvs.append(_dot(jnp.concatenate([vT_ref[z, prev, hd * dh:(hd + 1) * dh, :], ones], axis=0),
                            p_ref[z, nxt, hd]))
            s = raw_ref[z, cur, hd]
            selb = sel_ref[z, hd, pl.ds(n, 1), :] > 0.5
            smax = jnp.where(selb, jnp.max(s, axis=0, keepdims=True).astype(F32), NEG)
            m_old = m_in[hd]
            m_new = jnp.maximum(m_old, smax)
            alphas.append(jnp.exp2(m_old - m_new))
            m_out.append(m_new)
            p_ref[z, cur, hd] = jnp.exp2(s - m_new.astype(BF16))
        for hd in range(A_HEADS):
            keep = sel_ref[z, hd, pl.ds(prev_sel, 1), :] > 0.5
            acc_ref[z, hd] = alphas[hd] * (acc_ref[z, hd] + jnp.where(keep, pvs[hd], 0.0))
        return m_out

    def body(ii, carry):
        ms = [[m_ref[z, hd] for hd in range(A_HEADS)] for z in streams]
        for z in streams:
            ms[z] = step(z, 2 * ii + 1, 1, ms[z])
        for z in streams:
            ms[z] = step(z, 2 * ii + 2, 0, ms[z])
        for z in streams:
            for hd in range(A_HEADS):
                m_ref[z, hd] = ms[z][hd]
        return carry

    trips = (j + 1) // 2
    lax.fori_loop(0, trips, body, 0)

    last = jnp.where(j == 0, j, 2 * trips - 1)
    last_sel = jnp.where(j == 0, nb, 2 * trips - 1)
    last_pvs = [value_dots(z, last, 0) for z in streams]
    for z in streams:
        for pr in range(A_HEADS // 2):
            halves = []
            for hd in (2 * pr, 2 * pr + 1):
                keep = sel_ref[z, hd, pl.ds(last_sel, 1), :] > 0.5
                acc = acc_ref[z, hd] + jnp.where(keep, last_pvs[z][hd], 0.0)
                halves.append(acc[0:dh] / acc[dh:dh + 1])
            oT = jnp.concatenate(halves, axis=0)
            oT = oT * gT_ref[z, 0, pr * pw:(pr + 1) * pw, :].astype(F32)
            o_ref[z, :, pr * pw:(pr + 1) * pw] = oT.astype(BF16).T


def _moba(qT, k, vT, gT):
    b, nb, aw, blk = qT.shape
    s = nb * blk
    pw = 2 * A_HEAD_DIM
    z = MOBA_NB
    assert b % z == 0
    q_spec = pl.BlockSpec((z, 1, aw, blk), lambda i, j: (i, j, 0, 0))
    row = lambda: pltpu.VMEM((z, A_HEADS, 1, blk), F32)
    return pl.pallas_call(
        functools.partial(_moba_kernel, nb=nb),
        grid=(b // z, nb),
        in_specs=[
            q_spec,
            pl.BlockSpec((z, s, aw), lambda i, j: (i, 0, 0)),
            pl.BlockSpec((z, nb, aw, blk), lambda i, j: (i, 0, 0, 0)),
            q_spec,
        ],
        out_specs=pl.BlockSpec((z, blk, aw), lambda i, j: (i, j, 0)),
        out_shape=jax.ShapeDtypeStruct((b, s, aw), BF16),
        scratch_shapes=[
            pltpu.VMEM((z, nb, aw), F32),
            pltpu.VMEM((z, A_HEADS, nb + V7X_SUBLANES, blk), F32),
            pltpu.VMEM((z, A_HEADS, pw, blk), BF16),
            pltpu.VMEM((z, 2, A_HEADS, blk, blk), BF16),
            pltpu.VMEM((z, 2, A_HEADS, blk, blk), BF16),
            row(),
            pltpu.VMEM((z, A_HEADS, A_HEAD_DIM + PV_ONES_ROWS, blk), F32),
        ],
        compiler_params=pltpu.CompilerParams(
            dimension_semantics=("parallel", "arbitrary"), vmem_limit_bytes=MOBA_VMEM_LIMIT),
        name="moba",
    )(qT, k, vT, gT)


def _mlstm_kernel(bx_ref, sbz_ref, cw_ref, cb_ref, wqt_ref, wk_ref, wvt_ref, wgq_ref, wgk_ref,
                  wgv_ref, bg_ref, og_ref, skip_ref, o_ref, xbuf_ref, state_ref, m_ref):
    L = MLSTM_L
    dh = B_HEAD_DIM
    c = pl.program_id(1)

    @pl.when(c == 0)
    def _():
        xbuf_ref[:, 0:CONV_HALO, :] = jnp.zeros((MLSTM_NB, CONV_HALO, B_WIDTH), F32)
        state_ref[...] = jnp.zeros_like(state_ref)
        m_ref[...] = jnp.zeros_like(m_ref)

    cw = cw_ref[...]
    cb = cb_ref[...]
    kscale = dh ** -0.5
    s_iota = lax.broadcasted_iota(jnp.int32, (L, L), 0)
    t_iota = lax.broadcasted_iota(jnp.int32, (L, L), 1)
    tri = s_iota <= t_iota
    tri_b = jnp.where(tri, 1.0, 0.0).astype(BF16)
    ones = jnp.ones((dh, L), F32)

    def project(bi):
        bx_b = bx_ref[bi]
        xbuf_ref[bi, CONV_HALO:CONV_HALO + L, :] = bx_b.astype(F32)
        st = dict(xc=[], ks=[], vT=[], scores=[], inter=[], state=[])
        gT = bg_ref[...]
        for hd in range(B_HEADS):
            lo = hd * dh
            conv = cb[:, lo:lo + dh]
            for i in range(B_CONV):
                off = CONV_HALO - (B_CONV - 1) + i
                conv = conv + cw[i:i + 1, lo:lo + dh] * xbuf_ref[bi, off:off + L, lo:lo + dh]
            xc = _silu(conv)
            xc_b = xc.astype(BF16)
            qT_h = _dot_nt(wqt_ref[hd], xc_b).astype(BF16)
            vT_h = _dot_nt(wvt_ref[hd], bx_b[:, lo:lo + dh])
            k_h = _dot(xc_b, wk_ref[hd])
            ks_h = (k_h * kscale).astype(BF16)
            state = state_ref[bi, hd]
            st["scores"].append(_dot(ks_h, qT_h))
            st["inter"].append(_dot(state.astype(BF16), qT_h))
            gT = gT + (_dot(wgq_ref[:, lo:lo + dh], qT_h)
                       + _dot_nt(wgk_ref[:, lo:lo + dh], k_h.astype(BF16))
                       + _dot(wgv_ref[:, lo:lo + dh], vT_h.astype(BF16)))
            st["xc"].append(xc); st["ks"].append(ks_h); st["vT"].append(vT_h); st["state"].append(state)
        tail = xbuf_ref[bi, L:L + CONV_HALO, :]
        xbuf_ref[bi, 0:CONV_HALO, :] = tail
        st["gT"] = gT
        return st

    def gate_chain(st):
        gT = st["gT"]
        lf_hi, lf_lo = _split_bf16(_log_sigmoid(gT))
        cum = (_dot(lf_hi, tri_b) + _dot(lf_lo, tri_b))[B_HEADS:2 * B_HEADS]
        a = gT[0:B_HEADS] - cum
        st["cum"], st["a"] = cum, a
        pad = jnp.zeros((V7X_LANES - B_HEADS, L), F32)
        st["a_cols"] = jnp.concatenate([a, pad], axis=0).T

    def recur(bi, st):
        a, cum, a_cols = st["a"], st["cum"], st["a_cols"]
        for hd in range(B_HEADS):
            lo = hd * dh
            ks_h, state, inter = st["ks"][hd], st["state"][hd], st["inter"][hd]
            vT_aug = jnp.concatenate([st["vT"][hd], ones], axis=0)
            a_row = a[hd:hd + 1]
            a_col = a_cols[:, hd:hd + 1]
            cum_row = cum[hd:hd + 1]
            m_prev = m_ref[bi, hd][0:1, 0:1]
            amax = jnp.max(jnp.where(tri, a_col, NEG), axis=0, keepdims=True)
            b_row = jnp.maximum(m_prev, amax)
            dmat = jnp.exp(jnp.where(tri, a_col - b_row, NEG))
            w_intra = (st["scores"][hd] * dmat).astype(BF16)
            intra = _dot(vT_aug.astype(BF16), w_intra)
            w_inter = jnp.exp(m_prev - b_row)
            num = w_inter * inter[0:dh] + intra[0:dh]
            den = w_inter * inter[dh:dh + 1] + intra[dh:dh + 1]
            hc = num * (1.0 / jnp.maximum(jnp.abs(den), jnp.exp(-(cum_row + b_row))))
            mu = jnp.mean(hc, axis=0, keepdims=True)
            hcc = hc - mu
            var = jnp.mean(hcc * hcc, axis=0, keepdims=True)
            hb = (hcc * lax.rsqrt(var + 1e-5)).T * og_ref[:, lo:lo + dh]
            yb = (hb + skip_ref[:, lo:lo + dh] * st["xc"][hd]) * sbz_ref[bi, :, lo:lo + dh].astype(F32)
            o_ref[bi, :, lo:lo + dh] = yb.astype(BF16)
            b_end = b_row[:, L - 1:L]
            ws = jnp.exp(a_row - b_end)
            decay = jnp.exp(m_prev - b_end)
            state_ref[bi, hd] = decay * state + _dot((vT_aug * ws).astype(BF16), ks_h)
            m_ref[bi, hd] = jnp.broadcast_to(cum_row[:, L - 1:L] + b_end, m_ref.shape[2:])

    sts = []
    for bi in range(MLSTM_NB):
        sts.append(project(bi))
        gate_chain(sts[bi])
    for bi in range(MLSTM_NB):
        recur(bi, sts[bi])


def _block_diag_dense(w):
    nblk, blk, _ = w.shape
    n = nblk * blk
    idx = jnp.arange(n) // blk
    return jnp.where(idx[:, None] == idx[None, :], jnp.tile(w.reshape(n, blk), (1, nblk)), 0.0)


def _mlstm(bx, sbz, conv_w, conv_b, wq, wk, wv, w_gates, b_gates, out_g, skip):
    b, s, bw = bx.shape
    L = MLSTM_L
    nc = s // L
    nbb = MLSTM_NB
    assert b % nbb == 0 and s % L == 0
    per_head = wq.shape[0] // B_HEADS

    def head_blocks(w, transpose):
        blocks = [_block_diag_dense(w[h * per_head:(h + 1) * per_head]) for h in range(B_HEADS)]
        return jnp.stack([blk.T if transpose else blk for blk in blocks]).astype(BF16)

    wqt_d = head_blocks(wq, True)
    wk_d = head_blocks(wk, False)
    wvt_d = head_blocks(wv, True)
    hblk = (B_HEADS, B_HEAD_DIM, B_HEAD_DIM)
    wgq = w_gates[:bw].T.astype(BF16)
    wgv = w_gates[2 * bw:].T.astype(BF16)
    wgk = w_gates[bw:2 * bw].T.astype(BF16)
    const = lambda shape: pl.BlockSpec(shape, lambda i, j: (0,) * len(shape))
    tok = pl.BlockSpec((nbb, L, bw), lambda i, j: (i, j, 0))
    return pl.pallas_call(
        _mlstm_kernel,
        grid=(b // nbb, nc),
        in_specs=[tok, tok, const((B_CONV, bw)), const((1, bw)),
                  const(hblk), const(hblk), const(hblk),
                  const((GATE_ROWS, bw)), const((GATE_ROWS, bw)), const((GATE_ROWS, bw)),
                  const((GATE_ROWS, 1)), const((1, bw)), const((1, bw))],
        out_specs=tok,
        out_shape=jax.ShapeDtypeStruct((b, s, bw), BF16),
        scratch_shapes=[pltpu.VMEM((nbb, L + CONV_HALO, bw), F32),
                        pltpu.VMEM((nbb, B_HEADS, 2 * B_HEAD_DIM, B_HEAD_DIM), F32),
                        pltpu.VMEM((nbb, B_HEADS, V7X_SUBLANES, V7X_LANES), F32)],
        compiler_params=pltpu.CompilerParams(
            dimension_semantics=("parallel", "arbitrary"), vmem_limit_bytes=VMEM_LIMIT),
        name="mlstm",
    )(bx, sbz, conv_w, conv_b.reshape(1, bw), wqt_d, wk_d, wvt_d, wgq, wgk, wgv,
      b_gates.reshape(GATE_ROWS, 1), out_g.reshape(1, bw), skip.reshape(1, bw))


def _tail_kernel(x_ref, ya_ref, yb_ref, mod0_ref, mod1_ref, lng_ref, wo0_ref, wi1_ref, clg_ref,
                 clb_ref, ws_ref, bst_ref, wo1_ref, o_ref):
    tm = x_ref.shape[1]
    sub = TAIL_SUB_ROWS
    n_sub = tm // sub
    aw = A_WIDTH
    mod1 = mod1_ref[0]
    gate0 = mod0_ref[0][2:3]
    t_iota = lax.broadcasted_iota(jnp.int32, (C_CHUNK, C_CHUNK), 0)
    s_iota = lax.broadcasted_iota(jnp.int32, (C_CHUNK, C_CHUNK), 1)
    tril = s_iota <= t_iota
    gw = C_WIDTH // C_GROUPS
    bst = bst_ref[...]
    wms = [jnp.where(tril, ws_ref[g], 0.0).astype(BF16) for g in range(C_GROUPS)]

    def out_proj0(r):
        rows = slice(r * sub, (r + 1) * sub)
        y0 = _dot(ya_ref[0, rows, :], wo0_ref[0:aw, :]) + _dot(yb_ref[0, rows, :], wo0_ref[aw:, :])
        x1 = x_ref[0, rows, :] + gate0 * y0
        h = _adaln_rmsnorm(x1, lng_ref[...], mod1[1:2], mod1[0:1]).astype(BF16)
        return x1, h

    def mix_inputs(p):
        u = _gelu_tanh(p[:, :C_WIDTH])
        v = _gelu_tanh(p[:, C_WIDTH:2 * C_WIDTH])
        mu = jnp.mean(v, axis=-1, keepdims=True)
        vc = v - mu
        var = jnp.mean(vc * vc, axis=-1, keepdims=True)
        vn = ((vc * lax.rsqrt(var + 1e-5)) * clg_ref[...] + clb_ref[...]).astype(BF16)
        return vn, u * _silu(p[:, 2 * C_WIDTH:])

    def spatial_gate(vn, gate):
        cols = []
        for g in range(C_GROUPS):
            rows = []
            for ch in range(sub // C_CHUNK):
                vg = vn[ch * C_CHUNK:(ch + 1) * C_CHUNK, g * gw:(g + 1) * gw]
                rows.append(_dot(wms[g], vg) + bst[:, g:g + 1])
            cols.append(jnp.concatenate(rows, axis=0))
        return (gate * jnp.concatenate(cols, axis=1)).astype(BF16)

    def out_proj1(r, x1, y1):
        o_ref[0, r * sub:(r + 1) * sub, :] = x1 + mod1[2:3] * _dot(y1, wo1_ref[...])

    heads = [out_proj0(r) for r in range(n_sub)]
    projs = [_dot(heads[r][1], wi1_ref[...]) for r in range(n_sub)]
    y1_prev = None
    for r in range(n_sub):
        vn, gate = mix_inputs(projs[r])
        y1 = spatial_gate(vn, gate)
        if y1_prev is not None:
            out_proj1(r - 1, heads[r - 1][0], y1_prev)
        y1_prev = y1
    out_proj1(n_sub - 1, heads[n_sub - 1][0], y1_prev)


def _tail(x, ya, yb, mod0, mod1, ln_g1, w_out0, w_in1, w_out1, c_ln_g, c_ln_b, c_ws, c_bs):
    b, s, d = x.shape
    tm = TAIL_ROW_TILE
    nt = s // tm
    const = lambda shape: pl.BlockSpec(shape, lambda i, j: (0,) * len(shape),
                                       pipeline_mode=pl.Buffered(1))
    half = pl.BlockSpec((1, tm, A_WIDTH), lambda i, j: (i, j, 0))
    full = pl.BlockSpec((1, tm, d), lambda i, j: (i, j, 0))
    modspec = pl.BlockSpec((1, 3, d), lambda i, j: (i, 0, 0))
    return pl.pallas_call(
        _tail_kernel,
        grid=(b, nt),
        in_specs=[full, half, half, modspec, modspec, const((1, d)), const((d, d)),
                  const((d, 3 * C_WIDTH)), const((1, d)), const((1, d)),
                  const((C_GROUPS, C_CHUNK, C_CHUNK)), const((C_CHUNK, C_GROUPS)), const((d, d))],
        out_specs=full,
        out_shape=jax.ShapeDtypeStruct((b, s, d), F32),
        compiler_params=pltpu.CompilerParams(
            dimension_semantics=("parallel", "parallel"), vmem_limit_bytes=VMEM_LIMIT),
        name="tail",
    )(x, ya, yb, mod0, mod1, ln_g1.reshape(1, d), w_out0.astype(BF16), w_in1.astype(BF16),
      c_ln_g.reshape(1, d), c_ln_b.reshape(1, d), c_ws, c_bs.T, w_out1.astype(BF16))


def kernel(x, c, ln_g, ada_w, ada_b, w_in, w_out, a_q_g, a_k_g, b_conv_w, b_conv_b, b_wq, b_wk, b_wv,
           b_w_gates, b_b_gates, b_out_g, b_skip, c_ln_g, c_ln_b, c_ws, c_bs):
    mods = _ada_mods(c, ada_w, ada_b)
    qT, vT, gT, k, bx, sbz = _inproj0(x, mods[0], ln_g[0], w_in[0], a_q_g[0], a_k_g[0])
    ya = _moba(qT, k, vT, gT)
    yb = _mlstm(bx, sbz, b_conv_w[0], b_conv_b[0], b_wq[0], b_wk[0], b_wv[0], b_w_gates[0],
                b_b_gates[0], b_out_g[0], b_skip[0])
    return _tail(x, ya, yb, mods[0], mods[1], ln_g[1], w_out[0], w_in[1], w_out[1],
                 c_ln_g[0], c_ln_b[0], c_ws[0], c_bs[0])
```

```python
import functools

import jax
import jax.numpy as jnp
from jax import lax
from jax.experimental import pallas as pl
from jax.experimental.pallas import tpu as pltpu

F32 = jnp.float32
BF16 = jnp.bfloat16

D_MODEL = 1024
A_HEADS = 8
A_HEAD_DIM = 64
A_WIDTH = A_HEADS * A_HEAD_DIM
MOBA_BLOCK = 256
MOBA_TOPK = 3
B_HEADS = 4
B_HEAD_DIM = 128
B_WIDTH = B_HEADS * B_HEAD_DIM
B_CONV = 4
C_GROUPS = 8
C_CHUNK = 128
C_WIDTH = D_MODEL
NEG = -1e30
LOG2E = 1.4426950408889634

V7X_SUBLANES = 8
V7X_LANES = 128

MLSTM_L = 256
MLSTM_NB = 4
INPROJ_SUBTILES = 4
TAIL_ROW_TILE = 1024
TAIL_SUB_ROWS = 512
CONV_HALO = 8
GATE_ROWS = 2 * B_HEADS
PV_ONES_ROWS = 16
MOBA_NB = 2
VMEM_LIMIT = 48 * 1024 * 1024
MOBA_VMEM_LIMIT = 56 * 1024 * 1024


def _silu(x):
    return x * jax.nn.sigmoid(x)


def _gelu_tanh(x):
    return 0.5 * x * (1.0 + jnp.tanh(0.7978845608028654 * (x + 0.044715 * (x * x * x))))


def _log_sigmoid(x):
    return jnp.minimum(x, 0.0) - jnp.log(1.0 + jnp.exp(-jnp.abs(x)))


def _split_bf16(x):
    hi = x.astype(BF16)
    lo = (x - hi.astype(F32)).astype(BF16)
    return hi, lo


def _dot(a, b):
    return jnp.dot(a, b, preferred_element_type=F32)


def _dot_nt(a, b):
    return lax.dot_general(a, b, (((1,), (1,)), ((), ())), preferred_element_type=F32)


def _adaln_rmsnorm(x, ln_g, scale, shift):
    y = x * lax.rsqrt(jnp.mean(x * x, axis=-1, keepdims=True) + 1e-6)
    return (y * ln_g) * (1.0 + scale) + shift


def _ada_kernel(c_ref, w_ref, b_ref, o_ref):
    cs_hi, cs_lo = _split_bf16(_silu(c_ref[...]))
    w_hi, w_lo = _split_bf16(w_ref[0])
    o_ref[0] = _dot(cs_hi, w_hi) + _dot(cs_lo, w_hi) + _dot(cs_hi, w_lo) + b_ref[0]


def _ada_mods(c, ada_w, ada_b):
    depth, d, d3 = ada_w.shape
    b = c.shape[0]
    bp = V7X_SUBLANES * pl.cdiv(b, V7X_SUBLANES)
    cp =jnp.zeros((bp, d), F32).at[:b].set(c)
    nt = d3 // d
    out = pl.pallas_call(
        _ada_kernel,
        grid=(depth, nt),
        in_specs=[
            pl.BlockSpec((bp, d), lambda l, n: (0, 0)),
            pl.BlockSpec((1, d, d), lambda l, n: (l, 0, n)),
            pl.BlockSpec((1, 1, d), lambda l, n: (l, 0, n)),
        ],
        out_specs=pl.BlockSpec((1, bp, d), lambda l, n: (l, 0, n)),
        out_shape=jax.ShapeDtypeStruct((depth, bp, d3), F32),
        name="ada_mods",
    )(cp, ada_w, ada_b.reshape(depth, 1, d3))
    return out[:, :b].reshape(depth, b, 3, d)


def _inproj0_kernel(x_ref, mod_ref, lng_ref, wt_ref, w_ref, qg_ref, kg_ref,
                    qT_ref, vT_ref, gT_ref, k_ref, bx_ref, sbz_ref):
    tm = MOBA_BLOCK
    aw = A_WIDTH
    mod = mod_ref[0]
    qg = qg_ref[...] * (A_HEAD_DIM ** -0.5 * LOG2E)
    kg = kg_ref[...]
    hs = [_adaln_rmsnorm(x_ref[0, r * tm:(r + 1) * tm], lng_ref[...], mod[1:2], mod[0:1]).astype(BF16)
          for r in range(INPROJ_SUBTILES)]
    for r in range(INPROJ_SUBTILES):
        h = hs[r]
        rows = slice(r * tm, (r + 1) * tm)
        pt = _dot_nt(wt_ref[...], h)
        p = _dot(h, w_ref[...])
        kn = []
        for hd in range(A_HEADS):
            lo = hd * A_HEAD_DIM
            q = pt[lo:lo + A_HEAD_DIM]
            rq = lax.rsqrt(jnp.mean(q * q, axis=0, keepdims=True) + 1e-6)
            qT_ref[0, r, lo:lo + A_HEAD_DIM, :] = ((q * rq) * qg).astype(BF16)
            k = pt[aw + lo:aw + lo + A_HEAD_DIM]
            rk = lax.rsqrt(jnp.mean(k * k, axis=0, keepdims=True) + 1e-6)
            kn.append((k * rk) * kg)
        k_ref[0, rows, :] = jnp.concatenate(kn, axis=0).astype(BF16).T
        vT_ref[0, r] = pt[2 * aw:3 * aw].astype(BF16)
        gT_ref[0, r] = _silu(pt[3 * aw:]).astype(BF16)
        bx_ref[0, rows, :] = p[:, :B_WIDTH].astype(BF16)
        sbz_ref[0, rows, :] = _silu(p[:, B_WIDTH:]).astype(BF16)


def _inproj0(x, mod, ln_g, w_in, q_g, k_g):
    b, s, d = x.shape
    blk = MOBA_BLOCK
    sub = INPROJ_SUBTILES
    tm = sub * blk
    nt = s // tm
    aw, bw = A_WIDTH, B_WIDTH
    assert w_in.shape == (d, 4 * aw + 2 * bw)
    wt = w_in[:, :4 * aw].astype(BF16).T
    wn = w_in[:, 4 * aw:].astype(BF16)
    t_shape = jax.ShapeDtypeStruct((b, s // blk, aw, blk), BF16)
    n_shape = jax.ShapeDtypeStruct((b, s, aw), BF16)
    t_spec = pl.BlockSpec((1, sub, aw, blk), lambda i, j: (i, j, 0, 0))
    n_spec = pl.BlockSpec((1, tm, aw), lambda i, j: (i, j, 0))
    const = lambda shape: pl.BlockSpec(shape, lambda i, j: (0,) * len(shape),
                                       pipeline_mode=pl.Buffered(1))
    return pl.pallas_call(
        _inproj0_kernel,
        grid=(b, nt),
        in_specs=[
            pl.BlockSpec((1, tm, d), lambda i, j: (i, j, 0)),
            pl.BlockSpec((1, 3, d), lambda i, j: (i, 0, 0)),
            const((1, d)),
            const((4 * aw, d)),
            const((d, 2 * bw)),
            const((A_HEAD_DIM, 1)),
            const((A_HEAD_DIM, 1)),
        ],
        out_specs=[t_spec, t_spec, t_spec, n_spec, n_spec, n_spec],
        out_shape=[t_shape, t_shape, t_shape, n_shape, n_shape, n_shape],
        compiler_params=pltpu.CompilerParams(
            dimension_semantics=("parallel", "parallel"), vmem_limit_bytes=VMEM_LIMIT),
        name="inproj0",
    )(x, mod, ln_g.reshape(1, d), wt, wn, q_g.reshape(A_HEAD_DIM, 1), k_g.reshape(A_HEAD_DIM, 1))


def _moba_kernel(qT_ref, k_ref, vT_ref, gT_ref, o_ref, kmean_ref, sel_ref, qm_ref, raw_ref, p_ref,
                 m_ref, acc_ref, *, nb):
    blk = MOBA_BLOCK
    dh = A_HEAD_DIM
    pw = 2 * dh
    streams = range(MOBA_NB)
    j = pl.program_id(1)

    @pl.when(j == 0)
    def _():
        for z in streams:
            for n in range(nb):
                kb = k_ref[z, n * blk:(n + 1) * blk, :].astype(F32)
                kmean_ref[z, n:n + 1, :] = jnp.mean(kb, axis=0, keepdims=True)

    d_iota = lax.broadcasted_iota(jnp.int32, (pw, blk), 0)

    def value_dots(z, n, slot):
        ones = jnp.ones((PV_ONES_ROWS, blk), BF16)
        return [_dot(jnp.concatenate([vT_ref[z, n, hd * dh:(hd + 1) * dh, :], ones], axis=0),
                     p_ref[z, slot, hd]) for hd in range(A_HEADS)]

    sel_scores = []
    own0 = pl.multiple_of(j * blk, blk)
    k_iota = lax.broadcasted_iota(jnp.int32, (blk, blk), 0)
    q_iota = lax.broadcasted_iota(jnp.int32, (blk, blk), 1)
    causal_bias = jnp.where(k_iota <= q_iota, 0.0, NEG).astype(BF16)
    for z in streams:
        km_hi, km_lo = _split_bf16(kmean_ref[z])
        km2 = jnp.concatenate([km_hi, km_lo], axis=0)
        per_head = []
        for hd in range(A_HEADS):
            pr = hd // 2
            lanes = slice(pr * pw, (pr + 1) * pw)
            q2 = qT_ref[z, 0, lanes, :]
            keep = (d_iota < dh) if hd % 2 == 0 else (d_iota >= dh)
            qm = jnp.where(keep, q2, jnp.zeros_like(q2))
            qm_ref[z, hd] = qm
            r2 = _dot(km2[:, lanes], qm)
            per_head.append(r2[0:nb] + r2[nb:2 * nb])
            own = _dot(k_ref[z, pl.ds(own0, blk), lanes], qm).astype(BF16)
            raw_ref[z, 1, hd] = _dot(k_ref[z, 0:blk, lanes], qm).astype(BF16)
            s = own + causal_bias
            m_new = jnp.max(s, axis=0, keepdims=True)
            p_ref[z, 0, hd] = jnp.exp2(s - m_new)
            m_ref[z, hd] = m_new.astype(F32)
            acc_ref[z, hd] = jnp.zeros(acc_ref.shape[2:], F32)
        sel_scores.append(per_head)

    n_iota = lax.broadcasted_iota(jnp.int32, (nb, blk), 0)
    past = n_iota < j
    for z in streams:
        for hd in range(A_HEADS):
            sc = jnp.where(past, sel_scores[z][hd], NEG)
            sel = jnp.zeros((nb, blk), F32)
            for _ in range(MOBA_TOPK):
                mx = jnp.max(sc, axis=0, keepdims=True)
                first = jnp.min(jnp.where(sc == mx, n_iota, nb), axis=0, keepdims=True)
                pick = n_iota == first
                sel = jnp.where(pick, 1.0, sel)
                sc = jnp.where(pick, -jnp.inf, sc)
            sel_ref[z, hd, 0:nb, :] = jnp.where(past, sel, 0.0)
            sel_ref[z, hd, nb:nb + 1, :] = jnp.ones((1, blk), F32)

    def step(z, i, cur, m_in):
        nxt = 1 - cur
        n = i - 1
        prev = jnp.where(i == 1, j, i - 2)
        prev_sel = jnp.where(i == 1, nb, i - 2)
        row0 = pl.multiple_of(jnp.minimum(i, nb - 1) * blk, blk)
        ones = jnp.ones((PV_ONES_ROWS, blk), BF16)
        pvs, alphas, m_out = [], [], []
        for hd in range(A_HEADS):
            pr = hd // 2
            kt = k_ref[z, pl.ds(row0, blk), pr * pw:(pr + 1) * pw]
            raw_ref[z, nxt, hd] = _dot(kt, qm_ref[z, hd]).astype(BF16)
            pvs.append(_dot(jnp.concatenate([vT_ref[z, prev, hd * dh:(hd + 1) * dh, :], ones], axis=0),
                            p_ref[z, nxt, hd]))
            s = raw_ref[z, cur, hd]
            selb = sel_ref[z, hd, pl.ds(n, 1), :] > 0.5
            smax = jnp.where(selb, jnp.max(s, axis=0, keepdims=True).astype(F32), NEG)
            m_old = m_in[hd]
            m_new = jnp.maximum(m_old, smax)
            alphas.append(jnp.exp2(m_old - m_new))
            m_out.append(m_new)
            p_ref[z, cur, hd] = jnp.exp2(s - m_new.astype(BF16))
        for hd in range(A_HEADS):
            keep = sel_ref[z, hd, pl.ds(prev_sel, 1), :] > 0.5
            acc_ref[z, hd] = alphas[hd] * (acc_ref[z, hd] + jnp.where(keep, pvs[hd], 0.0))
        return m_out

    def body(ii, carry):
        ms = [[m_ref[z, hd] for hd in range(A_HEADS)] for z in streams]
        for z in streams:
            ms[z] = step(z, 2 * ii + 1, 1, ms[z])
        for z in streams:
            ms[z] = step(z, 2 * ii + 2, 0, ms[z])
        for z in streams:
            for hd in range(A_HEADS):
                m_ref[z, hd] = ms[z][hd]
        return carry

    trips = (j + 1) // 2
    lax.fori_loop(0, trips, body, 0)

    last = jnp.where(j == 0, j, 2 * trips - 1)
    last_sel = jnp.where(j == 0, nb, 2 * trips - 1)
    last_pvs = [value_dots(z, last, 0) for z in streams]
    for z in streams:
        for pr in range(A_HEADS // 2):
            halves = []
            for hd in (2 * pr, 2 * pr + 1):
                keep = sel_ref[z, hd, pl.ds(last_sel, 1), :] > 0.5
                acc = acc_ref[z, hd] + jnp.where(keep, last_pvs[z][hd], 0.0)
                halves.append(acc[0:dh] / acc[dh:dh + 1])
            oT = jnp.concatenate(halves, axis=0)
            oT = oT * gT_ref[z, 0, pr * pw:(pr + 1) * pw, :].astype(F32)
            o_ref[z, :, pr * pw:(pr + 1) * pw] = oT.astype(BF16).T


def _moba(qT, k, vT, gT):
    b, nb, aw, blk = qT.shape
    s = nb * blk
    pw = 2 * A_HEAD_DIM
    z = MOBA_NB
    assert b % z == 0
    q_spec = pl.BlockSpec((z, 1, aw, blk), lambda i, j: (i, j, 0, 0))
    row = lambda: pltpu.VMEM((z, A_HEADS, 1, blk), F32)
    return pl.pallas_call(
        functools.partial(_moba_kernel, nb=nb),
        grid=(b // z, nb),
        in_specs=[
            q_spec,
            pl.BlockSpec((z, s, aw), lambda i, j: (i, 0, 0)),
            pl.BlockSpec((z, nb, aw, blk), lambda i, j: (i, 0, 0, 0)),
            q_spec,
        ],
        out_specs=pl.BlockSpec((z, blk, aw), lambda i, j: (i, j, 0)),
        out_shape=jax.ShapeDtypeStruct((b, s, aw), BF16),
        scratch_shapes=[
            pltpu.VMEM((z, nb, aw), F32),
            pltpu.VMEM((z, A_HEADS, nb + V7X_SUBLANES, blk), F32),
            pltpu.VMEM((z, A_HEADS, pw, blk), BF16),
            pltpu.VMEM((z, 2, A_HEADS, blk, blk), BF16),
            pltpu.VMEM((z, 2, A_HEADS, blk, blk), BF16),
            row(),
            pltpu.VMEM((z, A_HEADS, A_HEAD_DIM + PV_ONES_ROWS, blk), F32),
        ],
        compiler_params=pltpu.CompilerParams(
            dimension_semantics=("parallel", "arbitrary"), vmem_limit_bytes=MOBA_VMEM_LIMIT),
        name="moba",
    )(qT, k, vT, gT)


def _mlstm_kernel(bx_ref, sbz_ref, cw_ref, cb_ref, wqt_ref, wk_ref, wvt_ref, wgq_ref, wgk_ref,
                  wgv_ref, bg_ref, og_ref, skip_ref, o_ref, xbuf_ref, state_ref, m_ref):
    L = MLSTM_L
    dh = B_HEAD_DIM
    c = pl.program_id(1)

    @pl.when(c == 0)
    def _():
        xbuf_ref[:, 0:CONV_HALO, :] = jnp.zeros((MLSTM_NB, CONV_HALO, B_WIDTH), F32)
        state_ref[...] = jnp.zeros_like(state_ref)
        m_ref[...] = jnp.zeros_like(m_ref)

    cw = cw_ref[...]
    cb = cb_ref[...]
    kscale = dh ** -0.5
    s_iota = lax.broadcasted_iota(jnp.int32, (L, L), 0)
    t_iota = lax.broadcasted_iota(jnp.int32, (L, L), 1)
    tri = s_iota <= t_iota
    tri_b = jnp.where(tri, 1.0, 0.0).astype(BF16)
    ones = jnp.ones((dh, L), F32)

    def project(bi):
        bx_b = bx_ref[bi]
        xbuf_ref[bi, CONV_HALO:CONV_HALO + L, :] = bx_b.astype(F32)
        st = dict(xc=[], ks=[], vT=[], scores=[], inter=[], state=[])
        gT = bg_ref[...]
        for hd in range(B_HEADS):
            lo = hd * dh
            conv = cb[:, lo:lo + dh]
            for i in range(B_CONV):
                off = CONV_HALO - (B_CONV - 1) + i
                conv = conv + cw[i:i + 1, lo:lo + dh] * xbuf_ref[bi, off:off + L, lo:lo + dh]
            xc = _silu(conv)
            xc_b = xc.astype(BF16)
            qT_h = _dot_nt(wqt_ref[hd], xc_b).astype(BF16)
            vT_h = _dot_nt(wvt_ref[hd], bx_b[:, lo:lo + dh])
            k_h = _dot(xc_b, wk_ref[hd])
            ks_h = (k_h * kscale).astype(BF16)
            state = state_ref[bi, hd]
            st["scores"].append(_dot(ks_h, qT_h))
            st["inter"].append(_dot(state.astype(BF16), qT_h))
            gT = gT + (_dot(wgq_ref[:, lo:lo + dh], qT_h)
                       + _dot_nt(wgk_ref[:, lo:lo + dh], k_h.astype(BF16))
                       + _dot(wgv_ref[:, lo:lo + dh], vT_h.astype(BF16)))
            st["xc"].append(xc); st["ks"].append(ks_h); st["vT"].append(vT_h); st["state"].append(state)
        tail = xbuf_ref[bi, L:L + CONV_HALO, :]
        xbuf_ref[bi, 0:CONV_HALO, :] = tail
        st["gT"] = gT
        return st

    def gate_chain(st):
        gT = st["gT"]
        lf_hi, lf_lo = _split_bf16(_log_sigmoid(gT))
        cum = (_dot(lf_hi, tri_b) + _dot(lf_lo, tri_b))[B_HEADS:2 * B_HEADS]
        a = gT[0:B_HEADS] - cum
        st["cum"], st["a"] = cum, a
        pad = jnp.zeros((V7X_LANES - B_HEADS, L), F32)
        st["a_cols"] = jnp.concatenate([a, pad], axis=0).T

    def recur(bi, st):
        a, cum, a_cols = st["a"], st["cum"], st["a_cols"]
        for hd in range(B_HEADS):
            lo = hd * dh
            ks_h, state, inter = st["ks"][hd], st["state"][hd], st["inter"][hd]
            vT_aug = jnp.concatenate([st["vT"][hd], ones], axis=0)
            a_row = a[hd:hd + 1]
            a_col = a_cols[:, hd:hd + 1]
            cum_row = cum[hd:hd + 1]
            m_prev = m_ref[bi, hd][0:1, 0:1]
            amax = jnp.max(jnp.where(tri, a_col, NEG), axis=0, keepdims=True)
            b_row = jnp.maximum(m_prev, amax)
            dmat = jnp.exp(jnp.where(tri, a_col - b_row, NEG))
            w_intra = (st["scores"][hd] * dmat).astype(BF16)
            intra = _dot(vT_aug.astype(BF16), w_intra)
            w_inter = jnp.exp(m_prev - b_row)
            num = w_inter * inter[0:dh] + intra[0:dh]
            den = w_inter * inter[dh:dh + 1] + intra[dh:dh + 1]
            hc = num * (1.0 / jnp.maximum(jnp.abs(den), jnp.exp(-(cum_row + b_row))))
            mu = jnp.mean(hc, axis=0, keepdims=True)
            hcc = hc - mu
            var = jnp.mean(hcc * hcc, axis=0, keepdims=True)
            hb = (hcc * lax.rsqrt(var + 1e-5)).T * og_ref[:, lo:lo + dh]
            yb = (hb + skip_ref[:, lo:lo + dh] * st["xc"][hd]) * sbz_ref[bi, :, lo:lo + dh].astype(F32)
            o_ref[bi, :, lo:lo + dh] = yb.astype(BF16)
            b_end = b_row[:, L - 1:L]
            ws = jnp.exp(a_row - b_end)
            decay = jnp.exp(m_prev - b_end)
            state_ref[bi, hd] = decay * state + _dot((vT_aug * ws).astype(BF16), ks_h)
            m_ref[bi, hd] = jnp.broadcast_to(cum_row[:, L - 1:L] + b_end, m_ref.shape[2:])

    sts = []
    for bi in range(MLSTM_NB):
        sts.append(project(bi))
        gate_chain(sts[bi])
    for bi in range(MLSTM_NB):
        recur(bi, sts[bi])


def _block_diag_dense(w):
    nblk, blk, _ = w.shape
    n = nblk * blk
    idx = jnp.arange(n) // blk
    return jnp.where(idx[:, None] == idx[None, :], jnp.tile(w.reshape(n, blk), (1, nblk)), 0.0)


def _mlstm(bx, sbz, conv_w, conv_b, wq, wk, wv, w_gates, b_gates, out_g, skip):
    b, s, bw = bx.shape
    L = MLSTM_L
    nc = s // L
    nbb = MLSTM_NB
    assert b % nbb == 0 and s % L == 0
    per_head = wq.shape[0] // B_HEADS

    def head_blocks(w, transpose):
        blocks = [_block_diag_dense(w[h * per_head:(h + 1) * per_head]) for h in range(B_HEADS)]
        return jnp.stack([blk.T if transpose else blk for blk in blocks]).astype(BF16)

    wqt_d = head_blocks(wq, True)
    wk_d = head_blocks(wk, False)
    wvt_d = head_blocks(wv, True)
    hblk = (B_HEADS, B_HEAD_DIM, B_HEAD_DIM)
    wgq = w_gates[:bw].T.astype(BF16)
    wgv = w_gates[2 * bw:].T.astype(BF16)
    wgk = w_gates[bw:2 * bw].T.astype(BF16)
    const = lambda shape: pl.BlockSpec(shape, lambda i, j: (0,) * len(shape))
    tok = pl.BlockSpec((nbb, L, bw), lambda i, j: (i, j, 0))
    return pl.pallas_call(
        _mlstm_kernel,
        grid=(b // nbb, nc),
        in_specs=[tok, tok, const((B_CONV, bw)), const((1, bw)),
                  const(hblk), const(hblk), const(hblk),
                  const((GATE_ROWS, bw)), const((GATE_ROWS, bw)), const((GATE_ROWS, bw)),
                  const((GATE_ROWS, 1)), const((1, bw)), const((1, bw))],
        out_specs=tok,
        out_shape=jax.ShapeDtypeStruct((b, s, bw), BF16),
        scratch_shapes=[pltpu.VMEM((nbb, L + CONV_HALO, bw), F32),
                        pltpu.VMEM((nbb, B_HEADS, 2 * B_HEAD_DIM, B_HEAD_DIM), F32),
                        pltpu.VMEM((nbb, B_HEADS, V7X_SUBLANES, V7X_LANES), F32)],
        compiler_params=pltpu.CompilerParams(
            dimension_semantics=("parallel", "arbitrary"), vmem_limit_bytes=VMEM_LIMIT),
        name="mlstm",
    )(bx, sbz, conv_w, conv_b.reshape(1, bw), wqt_d, wk_d, wvt_d, wgq, wgk, wgv,
      b_gates.reshape(GATE_ROWS, 1), out_g.reshape(1, bw), skip.reshape(1, bw))


def _tail_kernel(x_ref, ya_ref, yb_ref, mod0_ref, mod1_ref, lng_ref, wo0_ref, wi1_ref, clg_ref,
                 clb_ref, ws_ref, bst_ref, wo1_ref, o_ref):
    tm = x_ref.shape[1]
    sub = TAIL_SUB_ROWS
    n_sub = tm // sub
    aw = A_WIDTH
    mod1 = mod1_ref[0]
    gate0 = mod0_ref[0][2:3]
    t_iota = lax.broadcasted_iota(jnp.int32, (C_CHUNK, C_CHUNK), 0)
    s_iota = lax.broadcasted_iota(jnp.int32, (C_CHUNK, C_CHUNK), 1)
    tril = s_iota <= t_iota
    gw = C_WIDTH // C_GROUPS
    bst = bst_ref[...]
    wms = [jnp.where(tril, ws_ref[g], 0.0).astype(BF16) for g in range(C_GROUPS)]

    def out_proj0(r):
        rows = slice(r * sub, (r + 1) * sub)
        y0 = _dot(ya_ref[0, rows, :], wo0_ref[0:aw, :]) + _dot(yb_ref[0, rows, :], wo0_ref[aw:, :])
        x1 = x_ref[0, rows, :] + gate0 * y0
        h = _adaln_rmsnorm(x1, lng_ref[...], mod1[1:2], mod1[0:1]).astype(BF16)
        return x1, h

    def mix_inputs(p):
        u = _gelu_tanh(p[:, :C_WIDTH])
        v = _gelu_tanh(p[:, C_WIDTH:2 * C_WIDTH])
        mu = jnp.mean(v, axis=-1, keepdims=True)
        vc = v - mu
        var = jnp.mean(vc * vc, axis=-1, keepdims=True)
        vn = ((vc * lax.rsqrt(var + 1e-5)) * clg_ref[...] + clb_ref[...]).astype(BF16)
        return vn, u * _silu(p[:, 2 * C_WIDTH:])

    def spatial_gate(vn, gate):
        cols = []
        for g in range(C_GROUPS):
            rows = []
            for ch in range(sub // C_CHUNK):
                vg = vn[ch * C_CHUNK:(ch + 1) * C_CHUNK, g * gw:(g + 1) * gw]
                rows.append(_dot(wms[g], vg) + bst[:, g:g + 1])
            cols.append(jnp.concatenate(rows, axis=0))
        return (gate * jnp.concatenate(cols, axis=1)).astype(BF16)

    def out_proj1(r, x1, y1):
        o_ref[0, r * sub:(r + 1) * sub, :] = x1 + mod1[2:3] * _dot(y1, wo1_ref[...])

    heads = [out_proj0(r) for r in range(n_sub)]
    projs = [_dot(heads[r][1], wi1_ref[...]) for r in range(n_sub)]
    y1_prev = None
    for r in range(n_sub):
        vn, gate = mix_inputs(projs[r])
        y1 = spatial_gate(vn, gate)
        if y1_prev is not None:
            out_proj1(r - 1, heads[r - 1][0], y1_prev)
        y1_prev = y1
    out_proj1(n_sub - 1, heads[n_sub - 1][0], y1_prev)


def _tail(x, ya, yb, mod0, mod1, ln_g1, w_out0, w_in1, w_out1, c_ln_g, c_ln_b, c_ws, c_bs):
    b, s, d = x.shape
    tm = TAIL_ROW_TILE
    nt = s // tm
    const = lambda shape: pl.BlockSpec(shape, lambda i, j: (0,) * len(shape),
                                       pipeline_mode=pl.Buffered(1))
    half = pl.BlockSpec((1, tm, A_WIDTH), lambda i, j: (i, j, 0))
    full = pl.BlockSpec((1, tm, d), lambda i, j: (i, j, 0))
    modspec = pl.BlockSpec((1, 3, d), lambda i, j: (i, 0, 0))
    return pl.pallas_call(
        _tail_kernel,
        grid=(b, nt),
        in_specs=[full, half, half, modspec, modspec, const((1, d)), const((d, d)),
                  const((d, 3 * C_WIDTH)), const((1, d)), const((1, d)),
                  const((C_GROUPS, C_CHUNK, C_CHUNK)), const((C_CHUNK, C_GROUPS)), const((d, d))],
        out_specs=full,
        out_shape=jax.ShapeDtypeStruct((b, s, d), F32),
        compiler_params=pltpu.CompilerParams(
            dimension_semantics=("parallel", "parallel"), vmem_limit_bytes=VMEM_LIMIT),
        name="tail",
    )(x, ya, yb, mod0, mod1, ln_g1.reshape(1, d), w_out0.astype(BF16), w_in1.astype(BF16),
      c_ln_g.reshape(1, d), c_ln_b.reshape(1, d), c_ws, c_bs.T, w_out1.astype(BF16))


def kernel(x, c, ln_g, ada_w, ada_b, w_in, w_out, a_q_g, a_k_g, b_conv_w, b_conv_b, b_wq, b_wk, b_wv,
           b_w_gates, b_b_gates, b_out_g, b_skip, c_ln_g, c_ln_b, c_ws, c_bs):
    mods = _ada_mods(c, ada_w, ada_b)
    qT, vT, gT, k, bx, sbz = _inproj0(x, mods[0], ln_g[0], w_in[0], a_q_g[0], a_k_g[0])
    ya = _moba(qT, k, vT, gT)
    yb = _mlstm(bx, sbz, b_conv_w[0], b_conv_b[0], b_wq[0], b_wk[0], b_wv[0], b_w_gates[0],
                b_b_gates[0], b_out_g[0], b_skip[0])
    return _tail(x, ya, yb, mods[0], mods[1], ln_g[1], w_out[0], w_in[1], w_out[1],
                 c_ln_g[0], c_ln_b[0], c_ws[0], c_bs[0])
```

```python
import functools

import jax
import jax.numpy as jnp
from jax import lax
from jax.experimental import pallas as pl
from jax.experimental.pallas import tpu as pltpu

F32 = jnp.float32
BF16 = jnp.bfloat16

D_MODEL = 1024
A_HEADS = 8
A_HEAD_DIM = 64
A_WIDTH = A_HEADS * A_HEAD_DIM
MOBA_BLOCK = 256
MOBA_TOPK = 3
B_HEADS = 4
B_HEAD_DIM = 128
B_WIDTH = B_HEADS * B_HEAD_DIM
B_CONV = 4
C_GROUPS = 8
C_CHUNK = 128
C_WIDTH = D_MODEL
NEG = -1e30
LOG2E = 1.4426950408889634

V7X_SUBLANES = 8
V7X_LANES = 128

MLSTM_L = 256
MLSTM_NB = 4
INPROJ_SUBTILES = 4
TAIL_ROW_TILE = 1024
TAIL_SUB_ROWS = 512
CONV_HALO = 8
GATE_ROWS = 2 * B_HEADS
PV_ONES_ROWS = 16
MOBA_NB = 2
VMEM_LIMIT = 48 * 1024 * 1024
MOBA_VMEM_LIMIT = 56 * 1024 * 1024


def _silu(x):
    return x * jax.nn.sigmoid(x)


def _gelu_tanh(x):
    return 0.5 * x * (1.0 + jnp.tanh(0.7978845608028654 * (x + 0.044715 * (x * x * x))))


def _log_sigmoid(x):
    return jnp.minimum(x, 0.0) - jnp.log(1.0 + jnp.exp(-jnp.abs(x)))


def _split_bf16(x):
    hi = x.astype(BF16)
    lo = (x - hi.astype(F32)).astype(BF16)
    return hi, lo


def _dot(a, b):
    return jnp.dot(a, b, preferred_element_type=F32)


def _dot_nt(a, b):
    return lax.dot_general(a, b, (((1,), (1,)), ((), ())), preferred_element_type=F32)


def _adaln_rmsnorm(x, ln_g, scale, shift):
    y = x * lax.rsqrt(jnp.mean(x * x, axis=-1, keepdims=True) + 1e-6)
    return (y * ln_g) * (1.0 + scale) + shift


def _ada_kernel(c_ref, w_ref, b_ref, o_ref, *, b):
    bp = c_ref.shape[0]
    cs_hi, cs_lo = _split_bf16(_silu(c_ref[...]))
    w_hi, w_lo = _split_bf16(w_ref[0])
    row = lax.broadcasted_iota(jnp.int32, cs_hi.shape, 0)
    both = _dot(jnp.where(row < b, cs_hi, cs_lo), w_hi)
    lo_hi = pltpu.roll(both, bp - b, 0)
    o_ref[0] = both + lo_hi + _dot(cs_hi, w_lo) + b_ref[0]


def _ada_mods(c, ada_w, ada_b):
    depth, d, d3 = ada_w.shape
    b = c.shape[0]
    bp = V7X_SUBLANES * pl.cdiv(2 * b, V7X_SUBLANES)
    cp = jnp.zeros((bp, d), F32).at[:b].set(c).at[b:2 * b].set(c)
    nt = d3 // d
    out = pl.pallas_call(
        functools.partial(_ada_kernel, b=b),
        grid=(depth, nt),
        in_specs=[
            pl.BlockSpec((bp, d), lambda l, n: (0, 0)),
            pl.BlockSpec((1, d, d), lambda l, n: (l, 0, n)),
            pl.BlockSpec((1, 1, d), lambda l, n: (l, 0, n)),
        ],
        out_specs=pl.BlockSpec((1, bp, d), lambda l, n: (l, 0, n)),
        out_shape=jax.ShapeDtypeStruct((depth, bp, d3), F32),
        name="ada_mods",
    )(cp, ada_w, ada_b.reshape(depth, 1, d3))
    return out[:, :b].reshape(depth, b, 3, d)


def _inproj0_kernel(x_ref, mod_ref, lng_ref, wt_ref, w_ref, qg_ref, kg_ref,
                    qT_ref, vT_ref, gT_ref, k_ref, bx_ref, sbz_ref):
    tm = MOBA_BLOCK
    aw = A_WIDTH
    mod = mod_ref[0]
    qg = qg_ref[...] * (A_HEAD_DIM ** -0.5 * LOG2E)
    kg = kg_ref[...]
    hs = [_adaln_rmsnorm(x_ref[0, r * tm:(r + 1) * tm], lng_ref[...], mod[1:2], mod[0:1]).astype(BF16)
          for r in range(INPROJ_SUBTILES)]
    for r in range(INPROJ_SUBTILES):
        h = hs[r]
        rows = slice(r * tm, (r + 1) * tm)
        pt = _dot_nt(wt_ref[...], h)
        p = _dot(h, w_ref[...])
        kn = []
        for hd in range(A_HEADS):
            lo = hd * A_HEAD_DIM
            q = pt[lo:lo + A_HEAD_DIM]
            rq = lax.rsqrt(jnp.mean(q * q, axis=0, keepdims=True) + 1e-6)
            qT_ref[0, r, lo:lo + A_HEAD_DIM, :] = ((q * rq) * qg).astype(BF16)
            k = pt[aw + lo:aw + lo + A_HEAD_DIM]
            rk = lax.rsqrt(jnp.mean(k * k, axis=0, keepdims=True) + 1e-6)
            kn.append((k * rk) * kg)
        k_ref[0, rows, :] = jnp.concatenate(kn, axis=0).astype(BF16).T
        vT_ref[0, r] = pt[2 * aw:3 * aw].astype(BF16)
        gT_ref[0, r] = _silu(pt[3 * aw:]).astype(BF16)
        bx_ref[0, rows, :] = p[:, :B_WIDTH].astype(BF16)
        sbz_ref[0, rows, :] = _silu(p[:, B_WIDTH:]).astype(BF16)


def _inproj0(x, mod, ln_g, w_in, q_g, k_g):
    b, s, d = x.shape
    blk = MOBA_BLOCK
    sub = INPROJ_SUBTILES
    tm = sub * blk
    nt = s // tm
    aw, bw = A_WIDTH, B_WIDTH
    assert w_in.shape == (d, 4 * aw + 2 * bw)
    wt = w_in[:, :4 * aw].astype(BF16).T
    wn = w_in[:, 4 * aw:].astype(BF16)
    t_shape = jax.ShapeDtypeStruct((b, s // blk, aw, blk), BF16)
    n_shape = jax.ShapeDtypeStruct((b, s, aw), BF16)
    t_spec = pl.BlockSpec((1, sub, aw, blk), lambda i, j: (i, j, 0, 0))
    n_spec = pl.BlockSpec((1, tm, aw), lambda i, j: (i, j, 0))
    const = lambda shape: pl.BlockSpec(shape, lambda i, j: (0,) * len(shape),
                                       pipeline_mode=pl.Buffered(1))
    return pl.pallas_call(
        _inproj0_kernel,
        grid=(b, nt),
        in_specs=[
            pl.BlockSpec((1, tm, d), lambda i, j: (i, j, 0)),
            pl.BlockSpec((1, 3, d), lambda i, j: (i, 0, 0)),
            const((1, d)),
            const((4 * aw, d)),
            const((d, 2 * bw)),
            const((A_HEAD_DIM, 1)),
            const((A_HEAD_DIM, 1)),
        ],
        out_specs=[t_spec, t_spec, t_spec, n_spec, n_spec, n_spec],
        out_shape=[t_shape, t_shape, t_shape, n_shape, n_shape, n_shape],
        compiler_params=pltpu.CompilerParams(
            dimension_semantics=("parallel", "parallel"), vmem_limit_bytes=VMEM_LIMIT),
        name="inproj0",
    )(x, mod, ln_g.reshape(1, d), wt, wn, q_g.reshape(A_HEAD_DIM, 1), k_g.reshape(A_HEAD_DIM, 1))


def _moba_kernel(qT_ref, k_ref, vT_ref, gT_ref, o_ref, kmean_ref, sel_ref, qm_ref, raw_ref, p_ref,
                 m_ref, acc_ref, *, nb):
    blk = MOBA_BLOCK
    dh = A_HEAD_DIM
    pw = 2 * dh
    streams = range(MOBA_NB)
    j = pl.program_id(1)

    @pl.when(j == 0)
    def _():
        for z in streams:
            for n in range(nb):
                kb = k_ref[z, n * blk:(n + 1) * blk, :].astype(F32)
                kmean_ref[z, n:n + 1, :] = jnp.mean(kb, axis=0, keepdims=True)

    d_iota = lax.broadcasted_iota(jnp.int32, (pw, blk), 0)

    def value_dots(z, n, slot):
        ones = jnp.ones((PV_ONES_ROWS, blk), BF16)
        return [_dot(jnp.concatenate([vT_ref[z, n, hd * dh:(hd + 1) * dh, :], ones], axis=0),
                     p_ref[z, slot, hd]) for hd in range(A_HEADS)]

    sel_scores = []
    own0 = pl.multiple_of(j * blk, blk)
    k_iota = lax.broadcasted_iota(jnp.int32, (blk, blk), 0)
    q_iota = lax.broadcasted_iota(jnp.int32, (blk, blk), 1)
    causal_bias = jnp.where(k_iota <= q_iota, 0.0, NEG).astype(BF16)
    for z in streams:
        km_hi, km_lo = _split_bf16(kmean_ref[z])
        km2 = jnp.concatenate([km_hi, km_lo], axis=0)
        per_head = []
        for hd in range(A_HEADS):
            pr = hd // 2
            lanes = slice(pr * pw, (pr + 1) * pw)
            q2 = qT_ref[z, 0, lanes, :]
            keep = (d_iota < dh) if hd % 2 == 0 else (d_iota >= dh)
            qm = jnp.where(keep, q2, jnp.zeros_like(q2))
            qm_ref[z, hd] = qm
            r2 = _dot(km2[:, lanes], qm)
            per_head.append(r2[0:nb] + r2[nb:2 * nb])
            own = _dot(k_ref[z, pl.ds(own0, blk), lanes], qm).astype(BF16)
            raw_ref[z, 1, hd] = _dot(k_ref[z, 0:blk, lanes], qm).astype(BF16)
            s = own + causal_bias
            m_new = jnp.max(s, axis=0, keepdims=True)
            p_ref[z, 0, hd] = jnp.exp2(s - m_new)
            m_ref[z, hd] = m_new.astype(F32)
            acc_ref[z, hd] = jnp.zeros(acc_ref.shape[2:], F32)
        sel_scores.append(per_head)

    n_iota = lax.broadcasted_iota(jnp.int32, (nb, blk), 0)
    past = n_iota < j
    for z in streams:
        for hd in range(A_HEADS):
            sc = jnp.where(past, sel_scores[z][hd], NEG)
            sel = jnp.zeros((nb, blk), F32)
            for _ in range(MOBA_TOPK):
                mx = jnp.max(sc, axis=0, keepdims=True)
                first = jnp.min(jnp.where(sc == mx, n_iota, nb), axis=0, keepdims=True)
                pick = n_iota == first
                sel = jnp.where(pick, 1.0, sel)
                sc = jnp.where(pick, -jnp.inf, sc)
            sel_ref[z, hd, 0:nb, :] = jnp.where(past, sel, 0.0)
            sel_ref[z, hd, nb:nb + 1, :] = jnp.ones((1, blk), F32)

    def step(z, i, cur, m_in):
        nxt = 1 - cur
        n = i - 1
        prev = jnp.where(i == 1, j, i - 2)
        prev_sel = jnp.where(i == 1, nb, i - 2)
        row0 = pl.multiple_of(jnp.minimum(i, nb - 1) * blk, blk)
        ones = jnp.ones((PV_ONES_ROWS, blk), BF16)
        pvs, alphas, m_out = [], [], []
        for hd in range(A_HEADS):
            pr = hd // 2
            kt = k_ref[z, pl.ds(row0, blk), pr * pw:(pr + 1) * pw]
            raw_ref[z, nxt, hd] = _dot(kt, qm_ref[z, hd]).astype(BF16)
            pvs.append(_dot(jnp.concatenate([vT_ref[z, prev, hd * dh:(hd + 1) * dh, :], ones], axis=0),
                            p_ref[z, nxt, hd]))
            s = raw_ref[z, cur, hd]
            selb = sel_ref[z, hd, pl.ds(n, 1), :] > 0.5
            smax = jnp.where(selb, jnp.max(s, axis=0, keepdims=True).astype(F32), NEG)
            m_old = m_in[hd]
            m_new = jnp.maximum(m_old, smax)
            alphas.append(jnp.exp2(m_old - m_new))
            m_out.append(m_new)
            p_ref[z, cur, hd] = jnp.exp2(s - m_new.astype(BF16))
        for hd in range(A_HEADS):
            keep = sel_ref[z, hd, pl.ds(prev_sel, 1), :] > 0.5
            acc_ref[z, hd] = alphas[hd] * (acc_ref[z, hd] + jnp.where(keep, pvs[hd], 0.0))
        return m_out

    def body(ii, carry):
        ms = [[m_ref[z, hd] for hd in range(A_HEADS)] for z in streams]
        for z in streams:
            ms[z] = step(z, 2 * ii + 1, 1, ms[z])
        for z in streams:
            ms[z] = step(z, 2 * ii + 2, 0, ms[z])
        for z in streams:
            for hd in range(A_HEADS):
                m_ref[z, hd] = ms[z][hd]
        return carry

    trips = (j + 1) // 2
    lax.fori_loop(0, trips, body, 0)

    last = jnp.where(j == 0, j, 2 * trips - 1)
    last_sel = jnp.where(j == 0, nb, 2 * trips - 1)
    last_pvs = [value_dots(z, last, 0) for z in streams]
    for z in streams:
        for pr in range(A_HEADS // 2):
            halves = []
            for hd in (2 * pr, 2 * pr + 1):
                keep = sel_ref[z, hd, pl.ds(last_sel, 1), :] > 0.5
                acc = acc_ref[z, hd] + jnp.where(keep, last_pvs[z][hd], 0.0)
                halves.append(acc[0:dh] / acc[dh:dh + 1])
            oT = jnp.concatenate(halves, axis=0)
            oT = oT * gT_ref[z, 0, pr * pw:(pr + 1) * pw, :].astype(F32)
            o_ref[z, :, pr * pw:(pr + 1) * pw] = oT.astype(BF16).T


def _moba(qT, k, vT, gT):
    b, nb, aw, blk = qT.shape
    s = nb * blk
    pw = 2 * A_HEAD_DIM
    z = MOBA_NB
    assert b % z == 0
    q_spec = pl.BlockSpec((z, 1, aw, blk), lambda i, j: (i, j, 0, 0))
    row = lambda: pltpu.VMEM((z, A_HEADS, 1, blk), F32)
    return pl.pallas_call(
        functools.partial(_moba_kernel, nb=nb),
        grid=(b // z, nb),
        in_specs=[
            q_spec,
            pl.BlockSpec((z, s, aw), lambda i, j: (i, 0, 0)),
            pl.BlockSpec((z, nb, aw, blk), lambda i, j: (i, 0, 0, 0)),
            q_spec,
        ],
        out_specs=pl.BlockSpec((z, blk, aw), lambda i, j: (i, j, 0)),
        out_shape=jax.ShapeDtypeStruct((b, s, aw), BF16),
        scratch_shapes=[
            pltpu.VMEM((z, nb, aw), F32),
            pltpu.VMEM((z, A_HEADS, nb + V7X_SUBLANES, blk), F32),
            pltpu.VMEM((z, A_HEADS, pw, blk), BF16),
            pltpu.VMEM((z, 2, A_HEADS, blk, blk), BF16),
            pltpu.VMEM((z, 2, A_HEADS, blk, blk), BF16),
            row(),
            pltpu.VMEM((z, A_HEADS, A_HEAD_DIM + PV_ONES_ROWS, blk), F32),
        ],
        compiler_params=pltpu.CompilerParams(
            dimension_semantics=("parallel", "arbitrary"), vmem_limit_bytes=MOBA_VMEM_LIMIT),
        name="moba",
    )(qT, k, vT, gT)


def _mlstm_kernel(bx_ref, sbz_ref, cw_ref, cb_ref, wqt_ref, wk_ref, wvt_ref, wgq_ref, wgk_ref,
                  wgv_ref, bg_ref, og_ref, skip_ref, o_ref, xbuf_ref, state_ref, m_ref):
    L = MLSTM_L
    dh = B_HEAD_DIM
    c = pl.program_id(1)

    @pl.when(c == 0)
    def _():
        xbuf_ref[:, 0:CONV_HALO, :] = jnp.zeros((MLSTM_NB, CONV_HALO, B_WIDTH), F32)
        state_ref[...] = jnp.zeros_like(state_ref)
        m_ref[...] = jnp.zeros_like(m_ref)

    cw = cw_ref[...]
    cb = cb_ref[...]
    kscale = dh ** -0.5
    s_iota = lax.broadcasted_iota(jnp.int32, (L, L), 0)
    t_iota = lax.broadcasted_iota(jnp.int32, (L, L), 1)
    tri = s_iota <= t_iota
    tri_b = jnp.where(tri, 1.0, 0.0).astype(BF16)
    ones = jnp.ones((dh, L), F32)

    def project(bi):
        bx_b = bx_ref[bi]
        xbuf_ref[bi, CONV_HALO:CONV_HALO + L, :] = bx_b.astype(F32)
        st = dict(xc=[], ks=[], vT=[], scores=[], inter=[], state=[])
        gT = bg_ref[...]
        for hd in range(B_HEADS):
            lo = hd * dh
            conv = cb[:, lo:lo + dh]
            for i in range(B_CONV):
                off = CONV_HALO - (B_CONV - 1) + i
                conv = conv + cw[i:i + 1, lo:lo + dh] * xbuf_ref[bi, off:off + L, lo:lo + dh]
            xc = _silu(conv)
            xc_b = xc.astype(BF16)
            qT_h = _dot_nt(wqt_ref[hd], xc_b).astype(BF16)
            vT_h = _dot_nt(wvt_ref[hd], bx_b[:, lo:lo + dh])
            k_h = _dot(xc_b, wk_ref[hd])
            ks_h = (k_h * kscale).astype(BF16)
            state = state_ref[bi, hd]
            st["scores"].append(_dot(ks_h, qT_h))
            st["inter"].append(_dot(state.astype(BF16), qT_h))
            gT = gT + (_dot(wgq_ref[:, lo:lo + dh], qT_h)
                       + _dot_nt(wgk_ref[:, lo:lo + dh], k_h.astype(BF16))
                       + _dot(wgv_ref[:, lo:lo + dh], vT_h.astype(BF16)))
            st["xc"].append(xc); st["ks"].append(ks_h); st["vT"].append(vT_h); st["state"].append(state)
        tail = xbuf_ref[bi, L:L + CONV_HALO, :]
        xbuf_ref[bi, 0:CONV_HALO, :] = tail
        st["gT"] = gT
        return st

    def gate_chain(st):
        gT = st["gT"]
        lf_hi, lf_lo = _split_bf16(_log_sigmoid(gT))
        cum = (_dot(lf_hi, tri_b) + _dot(lf_lo, tri_b))[B_HEADS:2 * B_HEADS]
        a = gT[0:B_HEADS] - cum
        st["cum"], st["a"] = cum, a
        pad = jnp.zeros((V7X_LANES - B_HEADS, L), F32)
        st["a_cols"] = jnp.concatenate([a, pad], axis=0).T

    def recur(bi, st):
        a, cum, a_cols = st["a"], st["cum"], st["a_cols"]
        for hd in range(B_HEADS):
            lo = hd * dh
            ks_h, state, inter = st["ks"][hd], st["state"][hd], st["inter"][hd]
            vT_aug = jnp.concatenate([st["vT"][hd], ones], axis=0)
            a_row = a[hd:hd + 1]
            a_col = a_cols[:, hd:hd + 1]
            cum_row = cum[hd:hd + 1]
            m_prev = m_ref[bi, hd][0:1, 0:1]
            amax = jnp.max(jnp.where(tri, a_col, NEG), axis=0, keepdims=True)
            b_row = jnp.maximum(m_prev, amax)
            dmat = jnp.exp(jnp.where(tri, a_col - b_row, NEG))
            w_intra = (st["scores"][hd] * dmat).astype(BF16)
            intra = _dot(vT_aug.astype(BF16), w_intra)
            w_inter = jnp.exp(m_prev - b_row)
            num = w_inter * inter[0:dh] + intra[0:dh]
            den = w_inter * inter[dh:dh + 1] + intra[dh:dh + 1]
            hc = num * (1.0 / jnp.maximum(jnp.abs(den), jnp.exp(-(cum_row + b_row))))
            mu = jnp.mean(hc, axis=0, keepdims=True)
            hcc = hc - mu
            var = jnp.mean(hcc * hcc, axis=0, keepdims=True)
            hb = (hcc * lax.rsqrt(var + 1e-5)).T * og_ref[:, lo:lo + dh]
            yb = (hb + skip_ref[:, lo:lo + dh] * st["xc"][hd]) * sbz_ref[bi, :, lo:lo + dh].astype(F32)
            o_ref[bi, :, lo:lo + dh] = yb.astype(BF16)
            b_end = b_row[:, L - 1:L]
            ws = jnp.exp(a_row - b_end)
            decay = jnp.exp(m_prev - b_end)
            state_ref[bi, hd] = decay * state + _dot((vT_aug * ws).astype(BF16), ks_h)
            m_ref[bi, hd] = jnp.broadcast_to(cum_row[:, L - 1:L] + b_end, m_ref.shape[2:])

    sts = []
    for bi in range(MLSTM_NB):
        sts.append(project(bi))
        gate_chain(sts[bi])
    for bi in range(MLSTM_NB):
        recur(bi, sts[bi])


def _block_diag_dense(w):
    nblk, blk, _ = w.shape
    n = nblk * blk
    idx = jnp.arange(n) // blk
    return jnp.where(idx[:, None] == idx[None, :], jnp.tile(w.reshape(n, blk), (1, nblk)), 0.0)


def _mlstm(bx, sbz, conv_w, conv_b, wq, wk, wv, w_gates, b_gates, out_g, skip):
    b, s, bw = bx.shape
    L = MLSTM_L
    nc = s // L
    nbb = MLSTM_NB
    assert b % nbb == 0 and s % L == 0
    per_head = wq.shape[0] // B_HEADS

    def head_blocks(w, transpose):
        blocks = [_block_diag_dense(w[h * per_head:(h + 1) * per_head]) for h in range(B_HEADS)]
        return jnp.stack([blk.T if transpose else blk for blk in blocks]).astype(BF16)

    wqt_d = head_blocks(wq, True)
    wk_d = head_blocks(wk, False)
    wvt_d = head_blocks(wv, True)
    hblk = (B_HEADS, B_HEAD_DIM, B_HEAD_DIM)
    wgq = w_gates[:bw].T.astype(BF16)
    wgv = w_gates[2 * bw:].T.astype(BF16)
    wgk = w_gates[bw:2 * bw].T.astype(BF16)
    const = lambda shape: pl.BlockSpec(shape, lambda i, j: (0,) * len(shape))
    tok = pl.BlockSpec((nbb, L, bw), lambda i, j: (i, j, 0))
    return pl.pallas_call(
        _mlstm_kernel,
        grid=(b // nbb, nc),
        in_specs=[tok, tok, const((B_CONV, bw)), const((1, bw)),
                  const(hblk), const(hblk), const(hblk),
                  const((GATE_ROWS, bw)), const((GATE_ROWS, bw)), const((GATE_ROWS, bw)),
                  const((GATE_ROWS, 1)), const((1, bw)), const((1, bw))],
        out_specs=tok,
        out_shape=jax.ShapeDtypeStruct((b, s, bw), BF16),
        scratch_shapes=[pltpu.VMEM((nbb, L + CONV_HALO, bw), F32),
                        pltpu.VMEM((nbb, B_HEADS, 2 * B_HEAD_DIM, B_HEAD_DIM), F32),
                        pltpu.VMEM((nbb, B_HEADS, V7X_SUBLANES, V7X_LANES), F32)],
        compiler_params=pltpu.CompilerParams(
            dimension_semantics=("parallel", "arbitrary"), vmem_limit_bytes=VMEM_LIMIT),
        name="mlstm",
    )(bx, sbz, conv_w, conv_b.reshape(1, bw), wqt_d, wk_d, wvt_d, wgq, wgk, wgv,
      b_gates.reshape(GATE_ROWS, 1), out_g.reshape(1, bw), skip.reshape(1, bw))


def _tail_kernel(x_ref, ya_ref, yb_ref, mod0_ref, mod1_ref, lng_ref, wo0_ref, wi1_ref, clg_ref,
                 clb_ref, ws_ref, bst_ref, wo1_ref, o_ref):
    tm = x_ref.shape[1]
    sub = TAIL_SUB_ROWS
    n_sub = tm // sub
    aw = A_WIDTH
    mod1 = mod1_ref[0]
    gate0 = mod0_ref[0][2:3]
    t_iota = lax.broadcasted_iota(jnp.int32, (C_CHUNK, C_CHUNK), 0)
    s_iota = lax.broadcasted_iota(jnp.int32, (C_CHUNK, C_CHUNK), 1)
    tril = s_iota <= t_iota
    gw = C_WIDTH // C_GROUPS
    bst = bst_ref[...]
    wms = [jnp.where(tril, ws_ref[g], 0.0).astype(BF16) for g in range(C_GROUPS)]

    def out_proj0(r):
        rows = slice(r * sub, (r + 1) * sub)
        y0 = _dot(ya_ref[0, rows, :], wo0_ref[0:aw, :]) + _dot(yb_ref[0, rows, :], wo0_ref[aw:, :])
        x1 = x_ref[0, rows, :] + gate0 * y0
        h = _adaln_rmsnorm(x1, lng_ref[...], mod1[1:2], mod1[0:1]).astype(BF16)
        return x1, h

    def mix_inputs(p):
        u = _gelu_tanh(p[:, :C_WIDTH])
        v = _gelu_tanh(p[:, C_WIDTH:2 * C_WIDTH])
        mu = jnp.mean(v, axis=-1, keepdims=True)
        vc = v - mu
        var = jnp.mean(vc * vc, axis=-1, keepdims=True)
        vn = ((vc * lax.rsqrt(var + 1e-5)) * clg_ref[...] + clb_ref[...]).astype(BF16)
        return vn, u * _silu(p[:, 2 * C_WIDTH:])

    def spatial_gate(vn, gate):
        cols = []
        for g in range(C_GROUPS):
            rows = []
            for ch in range(sub // C_CHUNK):
                vg = vn[ch * C_CHUNK:(ch + 1) * C_CHUNK, g * gw:(g + 1) * gw]
                rows.append(_dot(wms[g], vg) + bst[:, g:g + 1])
            cols.append(jnp.concatenate(rows, axis=0))
        return (gate * jnp.concatenate(cols, axis=1)).astype(BF16)

    def out_proj1(r, x1, y1):
        o_ref[0, r * sub:(r + 1) * sub, :] = x1 + mod1[2:3] * _dot(y1, wo1_ref[...])

    heads = [out_proj0(r) for r in range(n_sub)]
    projs = [_dot(heads[r][1], wi1_ref[...]) for r in range(n_sub)]
    y1_prev = None
    for r in range(n_sub):
        vn, gate = mix_inputs(projs[r])
        y1 = spatial_gate(vn, gate)
        if y1_prev is not None:
            out_proj1(r - 1, heads[r - 1][0], y1_prev)
        y1_prev = y1
    out_proj1(n_sub - 1, heads[n_sub - 1][0], y1_prev)


def _tail(x, ya, yb, mod0, mod1, ln_g1, w_out0, w_in1, w_out1, c_ln_g, c_ln_b, c_ws, c_bs):
    b, s, d = x.shape
    tm = TAIL_ROW_TILE
    nt = s // tm
    const = lambda shape: pl.BlockSpec(shape, lambda i, j: (0,) * len(shape),
                                       pipeline_mode=pl.Buffered(1))
    half = pl.BlockSpec((1, tm, A_WIDTH), lambda i, j: (i, j, 0))
    full = pl.BlockSpec((1, tm, d), lambda i, j: (i, j, 0))
    modspec = pl.BlockSpec((1, 3, d), lambda i, j: (i, 0, 0))
    return pl.pallas_call(
        _tail_kernel,
        grid=(b, nt),
        in_specs=[full, half, half, modspec, modspec, const((1, d)), const((d, d)),
                  const((d, 3 * C_WIDTH)), const((1, d)), const((1, d)),
                  const((C_GROUPS, C_CHUNK, C_CHUNK)), const((C_CHUNK, C_GROUPS)), const((d, d))],
        out_specs=full,
        out_shape=jax.ShapeDtypeStruct((b, s, d), F32),
        compiler_params=pltpu.CompilerParams(
            dimension_semantics=("parallel", "parallel"), vmem_limit_bytes=VMEM_LIMIT),
        name="tail",
    )(x, ya, yb, mod0, mod1, ln_g1.reshape(1, d), w_out0.astype(BF16), w_in1.astype(BF16),
      c_ln_g.reshape(1, d), c_ln_b.reshape(1, d), c_ws, c_bs.T, w_out1.astype(BF16))


def kernel(x, c, ln_g, ada_w, ada_b, w_in, w_out, a_q_g, a_k_g, b_conv_w, b_conv_b, b_wq, b_wk, b_wv,
           b_w_gates, b_b_gates, b_out_g, b_skip, c_ln_g, c_ln_b, c_ws, c_bs):
    mods = _ada_mods(c, ada_w, ada_b)
    qT, vT, gT, k, bx, sbz = _inproj0(x, mods[0], ln_g[0], w_in[0], a_q_g[0], a_k_g[0])
    ya = _moba(qT, k, vT, gT)
    yb = _mlstm(bx, sbz, b_conv_w[0], b_conv_b[0], b_wq[0], b_wk[0], b_wv[0], b_w_gates[0],
                b_b_gates[0], b_out_g[0], b_skip[0])
    return _tail(x, ya, yb, mods[0], mods[1], ln_g[1], w_out[0], w_in[1], w_out[1],
                 c_ln_g[0], c_ln_b[0], c_ws[0], c_bs[0])
```
